```python
import math
import jax, jax.numpy as jnp
from jax import lax
import numpy as np

D_MODEL = 1024
BATCH = 16
SEQ = 2048
DEPTH = 1

D_CONV = D_MODEL // 2
CONV_WIDTH = 3
D_SSM = D_MODEL // 2
SSM_GROUP = 16
N_SSM_GROUPS = D_SSM // SSM_GROUP
SSM_STATE = 64
DT_MIN = 0.001
DT_MAX = 0.1
D_IN_PROJ = 3 * D_CONV + D_SSM + 2 * D_MODEL
N_EXPERTS = 32
TOP_K = 4
D_FF = D_MODEL
SWIGLU_LIMIT = 7.0
SWIGLU_ALPHA = 1.702
MOE_BLOCK = 512
RMS_EPS = 1e-6

kernel_name = "hybrid_conv_s5_gated_moe"


def rmsnorm(x, g):
    xf = x.astype(jnp.float32)
    y = xf * lax.rsqrt(jnp.mean(xf * xf, axis=-1, keepdims=True) + RMS_EPS)
    return (y * g.astype(jnp.float32)).astype(x.dtype)


def short_conv_mixer(b_gate, c_gate, v, conv_w, w_conv_out):
    seq = v.shape[1]
    h = c_gate * v
    hp = jnp.pad(h, ((0, 0), (CONV_WIDTH - 1, 0), (0, 0)))
    z = conv_w[0] * hp[:, 0:seq]
    for k in range(1, CONV_WIDTH):
        z = z + conv_w[k] * hp[:, k:k + seq]
    return (b_gate * z) @ w_conv_out


def s5_mixer(u, lam_re, lam_im, log_dt, b_re, b_im, c_re, c_im, d_skip, w_glu):
    bsz, seq, _ = u.shape
    uf = u.astype(jnp.float32).reshape(bsz, seq, N_SSM_GROUPS, SSM_GROUP)
    lam = lax.complex(lam_re.astype(jnp.float32), lam_im.astype(jnp.float32))
    dt = jnp.exp(log_dt.astype(jnp.float32))[:, None]
    lam_bar = jnp.exp(lam * dt)
    b_c = lax.complex(b_re.astype(jnp.float32), b_im.astype(jnp.float32))
    b_bar = ((lam_bar - 1.0) / lam)[..., None] * b_c
    c_c = lax.complex(c_re.astype(jnp.float32), c_im.astype(jnp.float32))
    bu = jnp.einsum('bsgc,gpc->bsgp', uf.astype(jnp.complex64), b_bar)
    a = jnp.broadcast_to(lam_bar, (1, seq, N_SSM_GROUPS, SSM_STATE))

    def combine(e_i, e_j):
        a_i, b_i = e_i
        a_j, b_j = e_j
        return a_j * a_i, a_j * b_i + b_j

    _, states = lax.associative_scan(combine, (a, bu), axis=1)
    y = jnp.real(jnp.einsum('bsgp,gcp->bsgc', states, c_c))
    y = y + d_skip.astype(jnp.float32).reshape(N_SSM_GROUPS, SSM_GROUP) * uf
    y = jax.nn.gelu(y.reshape(bsz, seq, D_SSM)).astype(u.dtype)
    val, gate = jnp.split(y @ w_glu, 2, axis=-1)
    return val * jax.nn.sigmoid(gate)


def moe_ffn(x, w_router, b_router, w_gate_up, b_gate_up, w_down, b_down):
    bsz, seq, d = x.shape
    n_tok = bsz * seq
    xf = x.reshape(n_tok, d)
    logits = xf.astype(jnp.float32) @ w_router.astype(jnp.float32) + b_router.astype(jnp.float32)
    top_v, top_i = lax.top_k(logits, TOP_K)
    gates = jax.nn.softmax(top_v, axis=-1).astype(x.dtype)
    n_assign = n_tok * TOP_K
    flat_e = top_i.reshape(-1)
    order = jnp.argsort(flat_e)
    sorted_e = flat_e[order]
    sorted_tok = (order // TOP_K).astype(jnp.int32)
    counts = jnp.bincount(flat_e, length=N_EXPERTS)
    padded = ((counts + MOE_BLOCK - 1) // MOE_BLOCK) * MOE_BLOCK
    pad_end = jnp.cumsum(padded)
    pad_start = pad_end - padded
    un_start = jnp.cumsum(counts) - counts
    dest = (pad_start[sorted_e] + (jnp.arange(n_assign) - un_start[sorted_e])).astype(jnp.int32)
    n_rows = n_assign + N_EXPERTS * MOE_BLOCK
    n_blocks = n_rows // MOE_BLOCK
    row_tok = jnp.full((n_rows,), n_tok, jnp.int32).at[dest].set(sorted_tok)
    block_e = jnp.minimum(
        jnp.searchsorted(pad_end, jnp.arange(n_blocks) * MOE_BLOCK, side='right'), N_EXPERTS - 1)
    x_pad = jnp.concatenate([xf, jnp.zeros((1, d), xf.dtype)], axis=0)
    x_rows = x_pad[row_tok].reshape(n_blocks, MOE_BLOCK, d)

    def expert_block(args):
        xb, e = args
        h = xb @ w_gate_up[e] + b_gate_up[e]
        g, up = jnp.split(h, 2, axis=-1)
        g = jnp.minimum(g, SWIGLU_LIMIT)
        up = jnp.clip(up, -SWIGLU_LIMIT, SWIGLU_LIMIT)
        act = (up + 1.0) * (g * jax.nn.sigmoid(SWIGLU_ALPHA * g))
        return act @ w_down[e] + b_down[e]

    y_rows = lax.map(expert_block, (x_rows, block_e)).reshape(n_rows, d)
    dest_flat = jnp.zeros((n_assign,), jnp.int32).at[order].set(dest)
    y_assign = y_rows[dest_flat].reshape(n_tok, TOP_K, d)
    y = jnp.einsum('tk,tkd->td', gates, y_assign)
    return y.reshape(bsz, seq, d)


def setup_inputs(seed: int = 0) -> dict:
    key = jax.random.key(seed)
    ks = jax.random.split(key, 24)
    f32 = jnp.float32
    nrm = lambda k, shape, s: jax.random.normal(k, shape, f32) * s
    x = jax.random.normal(ks[0], (BATCH, SEQ, D_MODEL), f32)
    norm_mix_g = 1.0 + nrm(ks[1], (DEPTH, D_MODEL), 0.02)
    w_in = nrm(ks[2], (DEPTH, D_MODEL, D_IN_PROJ), D_MODEL ** -0.5)
    conv_w = nrm(ks[3], (DEPTH, CONV_WIDTH, D_CONV), CONV_WIDTH ** -0.5)
    w_conv_out = nrm(ks[4], (DEPTH, D_CONV, D_MODEL), D_CONV ** -0.5)
    ssm_lam_re = -0.5 + nrm(ks[5], (DEPTH, N_SSM_GROUPS, SSM_STATE), 0.01)
    ssm_lam_im = (jnp.pi * jnp.arange(SSM_STATE, dtype=f32)) + nrm(ks[6], (DEPTH, N_SSM_GROUPS, SSM_STATE), 0.01)
    ssm_log_dt = jax.random.uniform(ks[7], (DEPTH, N_SSM_GROUPS), f32,
                                    minval=math.log(DT_MIN), maxval=math.log(DT_MAX))
    ssm_b_re = nrm(ks[8], (DEPTH, N_SSM_GROUPS, SSM_STATE, SSM_GROUP), (2 * SSM_GROUP) ** -0.5)
    ssm_b_im = nrm(ks[9], (DEPTH, N_SSM_GROUPS, SSM_STATE, SSM_GROUP), (2 * SSM_GROUP) ** -0.5)
    ssm_c_re = nrm(ks[10], (DEPTH, N_SSM_GROUPS, SSM_GROUP, SSM_STATE), (2 * SSM_STATE) ** -0.5)
    ssm_c_im = nrm(ks[11], (DEPTH, N_SSM_GROUPS, SSM_GROUP, SSM_STATE), (2 * SSM_STATE) ** -0.5)
    ssm_d = nrm(ks[12], (DEPTH, D_SSM), 1.0)
    w_glu = nrm(ks[13], (DEPTH, D_SSM, 2 * D_MODEL), D_SSM ** -0.5)
    w_out = nrm(ks[14], (DEPTH, D_MODEL, D_MODEL), D_MODEL ** -0.5)
    norm_ffn_g = 1.0 + nrm(ks[15], (DEPTH, D_MODEL), 0.02)
    w_router = nrm(ks[16], (DEPTH, D_MODEL, N_EXPERTS), D_MODEL ** -0.5)
    b_router = nrm(ks[17], (DEPTH, N_EXPERTS), 0.01)
    w_gate_up = nrm(ks[18], (DEPTH, N_EXPERTS, D_MODEL, 2 * D_FF), D_MODEL ** -0.5)
    b_gate_up = nrm(ks[19], (DEPTH, N_EXPERTS, 2 * D_FF), 0.01)
    w_down = nrm(ks[20], (DEPTH, N_EXPERTS, D_FF, D_MODEL), D_FF ** -0.5)
    b_down = nrm(ks[21], (DEPTH, N_EXPERTS, D_MODEL), 0.01)
    norm_f_g = 1.0 + nrm(ks[22], (D_MODEL,), 0.02)
    return {"x": x, "norm_mix_g": norm_mix_g, "w_in": w_in, "conv_w": conv_w,
            "w_conv_out": w_conv_out, "ssm_lam_re": ssm_lam_re, "ssm_lam_im": ssm_lam_im,
            "ssm_log_dt": ssm_log_dt, "ssm_b_re": ssm_b_re, "ssm_b_im": ssm_b_im,
            "ssm_c_re": ssm_c_re, "ssm_c_im": ssm_c_im, "ssm_d": ssm_d, "w_glu": w_glu,
            "w_out": w_out, "norm_ffn_g": norm_ffn_g, "w_router": w_router,
            "b_router": b_router, "w_gate_up": w_gate_up, "b_gate_up": b_gate_up,
            "w_down": w_down, "b_down": b_down, "norm_f_g": norm_f_g}


def reference(x, norm_mix_g, w_in, conv_w, w_conv_out, ssm_lam_re, ssm_lam_im, ssm_log_dt,
              ssm_b_re, ssm_b_im, ssm_c_re, ssm_c_im, ssm_d, w_glu, w_out, norm_ffn_g,
              w_router, b_router, w_gate_up, b_gate_up, w_down, b_down, norm_f_g):
    splits = [D_CONV, 2 * D_CONV, 3 * D_CONV, 3 * D_CONV + D_SSM, 3 * D_CONV + D_SSM + D_MODEL]
    for l in range(DEPTH):
        h = rmsnorm(x, norm_mix_g[l])
        proj = h @ w_in[l]
        b_gate, c_gate, v, u, gate_a, gate_b = jnp.split(proj, splits, axis=-1)
        y_a = short_conv_mixer(b_gate, c_gate, v, conv_w[l], w_conv_out[l])
        y_b = s5_mixer(u, ssm_lam_re[l], ssm_lam_im[l], ssm_log_dt[l], ssm_b_re[l], ssm_b_im[l],
                       ssm_c_re[l], ssm_c_im[l], ssm_d[l], w_glu[l])
        merged = jax.nn.sigmoid(gate_a) * y_a + jax.nn.sigmoid(gate_b) * y_b
        x = x + merged @ w_out[l]
        h = rmsnorm(x, norm_ffn_g[l])
        x = x + moe_ffn(h, w_router[l], b_router[l], w_gate_up[l], b_gate_up[l],
                        w_down[l], b_down[l])
    return rmsnorm(x, norm_f_g)
```

```python
import functools
import math

import jax
import jax.numpy as jnp
from jax import lax
from jax.experimental import pallas as pl
from jax.experimental.pallas import tpu as pltpu

D_MODEL = 1024
D_CONV = 512
CONV_WIDTH = 3
D_SSM = 512
SSM_GROUP = 16
N_SSM_GROUPS = 32
SSM_STATE = 64
N_EXPERTS = 32
TOP_K = 4
D_FF = 1024
SWIGLU_LIMIT = 7.0
SWIGLU_ALPHA = 1.702
MOE_BLOCK = 512
RMS_EPS = 1e-6

CHUNK = 16
GROUPS_PER_STEP = 2
LANES = 128
TOKEN_TILE = 512
VMEM_LIMIT_BYTES = 56 * 1024 * 1024

_BF16 = jnp.bfloat16
_F32 = jnp.float32


def _rmsnorm(xf, g):
    return xf * lax.rsqrt(jnp.mean(xf * xf, axis=-1, keepdims=True) + RMS_EPS) * g


def _sigmoid(v):
    return 1.0 / (1.0 + jnp.exp(-v))


def _in_proj_kernel(tiles_per_seq, x_ref, g_ref, w_ref, cw_ref, bz_ref, u_ref, hbuf):
    tm = x_ref.shape[0]
    i = pl.program_id(0)

    @pl.when(i % tiles_per_seq == 0)
    def _():
        hbuf[0:8, :] = jnp.zeros((8, D_CONV), _F32)

    xn = _rmsnorm(x_ref[...], g_ref[...]).astype(_BF16)
    proj = jnp.dot(xn, w_ref[...], preferred_element_type=_F32)
    b_gate = proj[:, 0:D_CONV]
    c_gate = proj[:, D_CONV:2 * D_CONV]
    v = proj[:, 2 * D_CONV:3 * D_CONV]
    u_ref[...] = proj[:, 3 * D_CONV:].astype(_BF16)

    hbuf[8:8 + tm, :] = c_gate * v
    cw = cw_ref[...]
    z = (cw[0:1, :] * hbuf[6:6 + tm, :] + cw[1:2, :] * hbuf[7:7 + tm, :]
         + cw[2:3, :] * hbuf[8:8 + tm, :])
    bz_ref[...] = (b_gate * z).astype(_BF16)
    hbuf[0:8, :] = hbuf[tm:tm + 8, :]


def _in_proj(x2, g, w_bcvu, conv_w, seq):
    t = x2.shape[0]
    tm = TOKEN_TILE
    return pl.pallas_call(
        functools.partial(_in_proj_kernel, seq // tm),
        out_shape=(jax.ShapeDtypeStruct((t, D_CONV), _BF16),
                   jax.ShapeDtypeStruct((t, D_SSM), _BF16)),
        grid=(t // tm,),
        in_specs=[pl.BlockSpec((tm, D_MODEL), lambda i: (i, 0)),
                  pl.BlockSpec((1, D_MODEL), lambda i: (0, 0)),
                  pl.BlockSpec((D_MODEL, 3 * D_CONV + D_SSM), lambda i: (0, 0)),
                  pl.BlockSpec((CONV_WIDTH, D_CONV), lambda i: (0, 0))],
        out_specs=(pl.BlockSpec((tm, D_CONV), lambda i: (i, 0)),
                   pl.BlockSpec((tm, D_SSM), lambda i: (i, 0))),
        scratch_shapes=[pltpu.VMEM((tm + 8, D_CONV), _F32)],
        compiler_params=pltpu.CompilerParams(
            dimension_semantics=("arbitrary",), vmem_limit_bytes=VMEM_LIMIT_BYTES),
        name="in_proj",
    )(x2, g, w_bcvu, conv_w)


def _ssm_tables(lam_re, lam_im, log_dt, b_re, b_im, c_re, c_im, d_skip):
    hi = lax.Precision.HIGHEST
    dt = jnp.exp(log_dt)[:, None]
    k = jnp.arange(CHUNK + 1, dtype=_F32)[:, None, None]
    mag = jnp.exp(k * (lam_re * dt))
    ang = k * (lam_im * dt)
    pw_re, pw_im = mag * jnp.cos(ang), mag * jnp.sin(ang)
    a_re, a_im = pw_re[1], pw_im[1]
    den = lam_re * lam_re + lam_im * lam_im
    q_re = ((a_re - 1.0) * lam_re + a_im * lam_im) / den
    q_im = (a_im * lam_re - (a_re - 1.0) * lam_im) / den
    bb_re = q_re[..., None] * b_re - q_im[..., None] * b_im
    bb_im = q_re[..., None] * b_im + q_im[..., None] * b_re
    pk_re, pk_im = pw_re[:CHUNK], pw_im[:CHUNK]
    ab_re = pk_re[..., None] * bb_re[None] - pk_im[..., None] * bb_im[None]
    ab_im = pk_re[..., None] * bb_im[None] + pk_im[..., None] * bb_re[None]
    kern = (jnp.einsum('gcp,lgpd->lgcd', c_re, ab_re, precision=hi)
            - jnp.einsum('gcp,lgpd->lgcd', c_im, ab_im, precision=hi))
    eye = jnp.eye(SSM_GROUP, dtype=_F32)
    kern = kern.at[0].add(d_skip.reshape(N_SSM_GROUPS, SSM_GROUP)[:, :, None] * eye)
    lag = jnp.arange(CHUNK)[None, :] - jnp.arange(CHUNK)[:, None]
    toep = jnp.where((lag >= 0)[:, :, None, None, None],
                     kern[jnp.clip(lag, 0, CHUNK - 1)], 0.0)
    toep = toep.transpose(2, 0, 4, 1, 3).reshape(N_SSM_GROUPS, CHUNK * SSM_GROUP, CHUNK * SSM_GROUP)
    rev = pw_re[CHUNK - 1 - jnp.arange(CHUNK)], pw_im[CHUNK - 1 - jnp.arange(CHUNK)]
    bst_re = rev[0][..., None] * bb_re[None] - rev[1][..., None] * bb_im[None]
    bst_im = rev[0][..., None] * bb_im[None] + rev[1][..., None] * bb_re[None]
    bst_re = bst_re.transpose(1, 0, 3, 2).reshape(N_SSM_GROUPS, CHUNK * SSM_GROUP, SSM_STATE)
    bst_im = bst_im.transpose(1, 0, 3, 2).reshape(N_SSM_GROUPS, CHUNK * SSM_GROUP, SSM_STATE)
    nx_re, nx_im = pw_re[1:], pw_im[1:]
    cs_re = c_re[None] * nx_re[:, :, None, :] - c_im[None] * nx_im[:, :, None, :]
    cs_im = c_re[None] * nx_im[:, :, None, :] + c_im[None] * nx_re[:, :, None, :]
    cst_re = cs_re.transpose(1, 3, 0, 2).reshape(N_SSM_GROUPS, SSM_STATE, CHUNK * SSM_GROUP)
    cst_im = (-cs_im).transpose(1, 3, 0, 2).reshape(N_SSM_GROUPS, SSM_STATE, CHUNK * SSM_GROUP)

    npair = N_SSM_GROUPS // GROUPS_PER_STEP
    w = CHUNK * SSM_GROUP

    def blockdiag_rows(m):
        m = m.reshape(npair, GROUPS_PER_STEP, w, SSM_STATE)
        z = jnp.zeros_like(m[:, 0])
        return jnp.concatenate([jnp.concatenate([m[:, 0], z], axis=2),
                                jnp.concatenate([z, m[:, 1]], axis=2)], axis=1)

    def blockdiag_cols(m):
        m = m.reshape(npair, GROUPS_PER_STEP, SSM_STATE, w)
        z = jnp.zeros_like(m[:, 0])
        return jnp.concatenate([jnp.concatenate([m[:, 0], z], axis=2),
                                jnp.concatenate([z, m[:, 1]], axis=2)], axis=1)

    toep2 = toep.reshape(npair, GROUPS_PER_STEP, w, w).astype(_BF16)
    a_chunk = jnp.stack([pw_re[CHUNK].reshape(npair, 1, GROUPS_PER_STEP * SSM_STATE),
                         pw_im[CHUNK].reshape(npair, 1, GROUPS_PER_STEP * SSM_STATE)], axis=1)
    a_chunk = a_chunk.reshape(npair, 2, GROUPS_PER_STEP * SSM_STATE)
    return (toep2, blockdiag_rows(bst_re).astype(_BF16), blockdiag_rows(bst_im).astype(_BF16),
            blockdiag_cols(cst_re).astype(_BF16), blockdiag_cols(cst_im).astype(_BF16), a_chunk)


def _ssm_kernel(n_seq, u_ref, toep_ref, bre_ref, bim_ref, cre_ref, cim_ref, a_ref, y_ref,
                sre, sim, xre, xim):
    w = CHUNK * SSM_GROUP
    u = u_ref[0]
    n_rows = u.shape[0]
    sre[...] = jnp.dot(u, bre_ref[0], preferred_element_type=_F32)
    sim[...] = jnp.dot(u, bim_ref[0], preferred_element_type=_F32)
    a = a_ref[0]
    ar = jnp.broadcast_to(a[0:1, :], (n_seq, LANES))
    ai = jnp.broadcast_to(a[1:2, :], (n_seq, LANES))

    def step(j, carry):
        xr, xi = carry
        r = pl.multiple_of(j * n_seq, n_seq)
        xre[pl.ds(r, n_seq), :] = xr
        xim[pl.ds(r, n_seq), :] = xi
        nr = ar * xr - ai * xi + sre[pl.ds(r, n_seq), :]
        ni = ar * xi + ai * xr + sim[pl.ds(r, n_seq), :]
        return nr, ni

    zero = jnp.zeros((n_seq, LANES), _F32)
    lax.fori_loop(0, n_rows // n_seq, step, (zero, zero), unroll=8)

    carry_y = (jnp.dot(xre[...].astype(_BF16), cre_ref[0], preferred_element_type=_F32)
               + jnp.dot(xim[...].astype(_BF16), cim_ref[0], preferred_element_type=_F32))
    for q in range(GROUPS_PER_STEP):
        y = carry_y[:, q * w:(q + 1) * w] + jnp.dot(
            u[:, q * w:(q + 1) * w], toep_ref[0, q], preferred_element_type=_F32)
        y_ref[0, :, q * w:(q + 1) * w] = jax.nn.gelu(y).astype(_BF16)


def _ssm(u2, tables, n_seq):
    toep, bre, bim, cre, cim, a_chunk = tables
    npair, n_rows, w2 = u2.shape
    w = CHUNK * SSM_GROUP
    return pl.pallas_call(
        functools.partial(_ssm_kernel, n_seq),
        out_shape=jax.ShapeDtypeStruct((npair, n_rows, w2), _BF16),
        grid=(npair,),
        in_specs=[pl.BlockSpec((1, n_rows, w2), lambda g: (g, 0, 0)),
                  pl.BlockSpec((1, GROUPS_PER_STEP, w, w), lambda g: (g, 0, 0, 0)),
                  pl.BlockSpec((1, w2, LANES), lambda g: (g, 0, 0)),
                  pl.BlockSpec((1, w2, LANES), lambda g: (g, 0, 0)),
                  pl.BlockSpec((1, LANES, w2), lambda g: (g, 0, 0)),
                  pl.BlockSpec((1, LANES, w2), lambda g: (g, 0, 0)),
                  pl.BlockSpec((1, 2, LANES), lambda g: (g, 0, 0))],
        out_specs=pl.BlockSpec((1, n_rows, w2), lambda g: (g, 0, 0)),
        scratch_shapes=[pltpu.VMEM((n_rows, LANES), _F32) for _ in range(4)],
        compiler_params=pltpu.CompilerParams(
            dimension_semantics=("arbitrary",), vmem_limit_bytes=VMEM_LIMIT_BYTES),
        name="ssm",
    )(u2, toep, bre, bim, cre, cim, a_chunk)


def _mix_route_kernel(x_ref, bz_ref, yg_ref, gm_ref, wg_ref, wco_ref, wglu_ref, wout_ref,
                      gf_ref, wr_ref, br_ref,
                      x1_ref, h_ref, ti_ref, tg_ref, rk_ref, cnt_ref, base):
    tm = x_ref.shape[0]
    i = pl.program_id(0)

    @pl.when(i == 0)
    def _():
        base[...] = jnp.zeros_like(base)

    x = x_ref[...]
    xn = _rmsnorm(x, gm_ref[...]).astype(_BF16)
    gates = jnp.dot(xn, wg_ref[...], preferred_element_type=_F32)
    y_a = jnp.dot(bz_ref[...], wco_ref[...], preferred_element_type=_F32)
    glu = jnp.dot(yg_ref[...], wglu_ref[...], preferred_element_type=_F32)
    y_b = glu[:, :D_MODEL] * _sigmoid(glu[:, D_MODEL:])
    merged = _sigmoid(gates[:, :D_MODEL]) * y_a + _sigmoid(gates[:, D_MODEL:]) * y_b
    x1 = x + jnp.dot(merged.astype(_BF16), wout_ref[...], preferred_element_type=_F32)
    x1_ref[...] = x1
    h = _rmsnorm(x1, gf_ref[...])
    h_ref[...] = h.astype(_BF16)

    logits = jnp.dot(h, wr_ref[...], preferred_element_type=_F32,
                     precision=lax.Precision.HIGHEST) + br_ref[...]
    lane = lax.broadcasted_iota(jnp.int32, (tm, LANES), 1).astype(_F32)
    neg_inf = jnp.float32(-jnp.inf)
    work = logits
    vals, idxs = [], []
    for _ in range(TOP_K):
        m = jnp.max(work, axis=-1, keepdims=True)
        idx = jnp.min(jnp.where(work == m, lane, float(LANES)), axis=-1, keepdims=True)
        vals.append(m)
        idxs.append(idx)
        work = jnp.where(lane == idx, neg_inf, work)
    exps = [jnp.exp(v - vals[0]) for v in vals]
    denom = exps[0] + exps[1] + exps[2] + exps[3]

    sel = jnp.zeros((tm, LANES), _F32)
    for idx in idxs:
        sel = sel + (lane == idx).astype(_F32)
    row = lax.broadcasted_iota(jnp.int32, (tm, tm), 0)
    col = lax.broadcasted_iota(jnp.int32, (tm, tm), 1)
    strict_lower = (col < row).astype(_BF16)
    before = jnp.dot(strict_lower, sel.astype(_BF16), preferred_element_type=_F32) + base[...]
    for k in range(TOP_K):
        ti_ref[:, k:k + 1] = idxs[k].astype(jnp.int32)
        tg_ref[:, k:k + 1] = exps[k] / denom
        rk = jnp.sum(jnp.where(lane == idxs[k], before, 0.0), axis=-1, keepdims=True)
        rk_ref[:, k:k + 1] = rk.astype(jnp.int32)
    base[...] = base[...] + jnp.sum(sel, axis=0, keepdims=True)
    cnt_ref[...] = base[...].astype(jnp.int32)


def _mix_route(x2, bz, yg, g_mix, w_gates, w_conv_out, w_glu, w_out, g_ffn, w_router, b_router):
    t = x2.shape[0]
    tm = TOKEN_TILE
    tok = lambda i: (i, 0)
    fixed = lambda i: (0, 0)
    return pl.pallas_call(
        _mix_route_kernel,
        out_shape=(jax.ShapeDtypeStruct((t, D_MODEL), _F32),
                   jax.ShapeDtypeStruct((t, D_MODEL), _BF16),
                   jax.ShapeDtypeStruct((t, TOP_K), jnp.int32),
                   jax.ShapeDtypeStruct((t, TOP_K), _F32),
                   jax.ShapeDtypeStruct((t, TOP_K), jnp.int32),
                   jax.ShapeDtypeStruct((1, LANES), jnp.int32)),
        grid=(t // tm,),
        in_specs=[pl.BlockSpec((tm, D_MODEL), tok),
                  pl.BlockSpec((tm, D_CONV), tok),
                  pl.BlockSpec((tm, D_SSM), tok),
                  pl.BlockSpec((1, D_MODEL), fixed),
                  pl.BlockSpec((D_MODEL, 2 * D_MODEL), fixed),
                  pl.BlockSpec((D_CONV, D_MODEL), fixed),
                  pl.BlockSpec((D_SSM, 2 * D_MODEL), fixed),
                  pl.BlockSpec((D_MODEL, D_MODEL), fixed),
                  pl.BlockSpec((1, D_MODEL), fixed),
                  pl.BlockSpec((D_MODEL, LANES), fixed),
                  pl.BlockSpec((1, LANES), fixed)],
        out_specs=(pl.BlockSpec((tm, D_MODEL), tok),
                   pl.BlockSpec((tm, D_MODEL), tok),
                   pl.BlockSpec((tm, TOP_K), tok),
                   pl.BlockSpec((tm, TOP_K), tok),
                   pl.BlockSpec((tm, TOP_K), tok),
                   pl.BlockSpec((1, LANES), fixed)),
        scratch_shapes=[pltpu.VMEM((1, LANES), _F32)],
        compiler_params=pltpu.CompilerParams(
            dimension_semantics=("arbitrary",), vmem_limit_bytes=VMEM_LIMIT_BYTES),
        name="mix_route",
    )(x2, bz, yg, g_mix, w_gates, w_conv_out, w_glu, w_out, g_ffn, w_router, b_router)


def _expert_ffn_kernel(be_ref, nv_ref, x_ref, wgu_ref, bgu_ref, wd_ref, bd_ref, y_ref):
    b = pl.program_id(0)

    @pl.when(b < nv_ref[0])
    def _():
        hgu = jnp.dot(x_ref[...], wgu_ref[0], preferred_element_type=_F32) + bgu_ref[0]
        g = jnp.minimum(hgu[:, :D_FF], SWIGLU_LIMIT)
        up = jnp.clip(hgu[:, D_FF:], -SWIGLU_LIMIT, SWIGLU_LIMIT)
        act = (up + 1.0) * (g * _sigmoid(SWIGLU_ALPHA * g))
        y_ref[...] = jnp.dot(act.astype(_BF16), wd_ref[0], preferred_element_type=_F32) + bd_ref[0]

    @pl.when(b >= nv_ref[0])
    def _():
        y_ref[...] = jnp.zeros_like(y_ref)


def _expert_ffn(block_e, n_valid, x_rows, w_gate_up, b_gate_up, w_down, b_down):
    n_rows = x_rows.shape[0]
    n_blocks = n_rows // MOE_BLOCK

    def xmap(b, be, nv):
        return (jnp.minimum(b, nv[0] - 1), 0)

    def wmap(b, be, nv):
        return (be[b], 0, 0)

    grid_spec = pltpu.PrefetchScalarGridSpec(
        num_scalar_prefetch=2,
        grid=(n_blocks,),
        in_specs=[pl.BlockSpec((MOE_BLOCK, D_MODEL), xmap),
                  pl.BlockSpec((1, D_MODEL, 2 * D_FF), wmap),
                  pl.BlockSpec((1, 1, 2 * D_FF), wmap),
                  pl.BlockSpec((1, D_FF, D_MODEL), wmap),
                  pl.BlockSpec((1, 1, D_MODEL), wmap)],
        out_specs=pl.BlockSpec((MOE_BLOCK, D_MODEL), lambda b, be, nv: (b, 0)),
    )
    return pl.pallas_call(
        _expert_ffn_kernel,
        out_shape=jax.ShapeDtypeStruct((n_rows, D_MODEL), _F32),
        grid_spec=grid_spec,
        compiler_params=pltpu.CompilerParams(
            dimension_semantics=("arbitrary",), vmem_limit_bytes=VMEM_LIMIT_BYTES),
        name="expert_ffn",
    )(block_e, n_valid, x_rows, w_gate_up, b_gate_up, w_down, b_down)


def _combine_kernel(x1_ref, ya_ref, tg_ref, g_ref, o_ref):
    acc = x1_ref[...]
    tg = tg_ref[...]
    for k in range(TOP_K):
        acc = acc + tg[:, k:k + 1] * ya_ref[:, k * D_MODEL:(k + 1) * D_MODEL]
    o_ref[...] = _rmsnorm(acc, g_ref[...])


def _combine(x1, y_assign, top_g, g_final):
    t = x1.shape[0]
    tm = TOKEN_TILE
    return pl.pallas_call(
        _combine_kernel,
        out_shape=jax.ShapeDtypeStruct((t, D_MODEL), _F32),
        grid=(t // tm,),
        in_specs=[pl.BlockSpec((tm, D_MODEL), lambda i: (i, 0)),
                  pl.BlockSpec((tm, TOP_K * D_MODEL), lambda i: (i, 0)),
                  pl.BlockSpec((tm, TOP_K), lambda i: (i, 0)),
                  pl.BlockSpec((1, D_MODEL), lambda i: (0, 0))],
        out_specs=pl.BlockSpec((tm, D_MODEL), lambda i: (i, 0)),
        compiler_params=pltpu.CompilerParams(
            dimension_semantics=("arbitrary",), vmem_limit_bytes=VMEM_LIMIT_BYTES),
        name="combine",
    )(x1, y_assign, top_g, g_final)


def kernel(x, norm_mix_g, w_in, conv_w, w_conv_out, ssm_lam_re, ssm_lam_im, ssm_log_dt, ssm_b_re, ssm_b_im, ssm_c_re, ssm_c_im, ssm_d, w_glu, w_out, norm_ffn_g, w_router, b_router, w_gate_up, b_gate_up, w_down, b_down, norm_f_g):
    bsz, seq, d = x.shape
    t = bsz * seq
    n_chunks = seq // CHUNK
    npair = N_SSM_GROUPS // GROUPS_PER_STEP
    x2 = x.reshape(t, d)
    assert seq % TOKEN_TILE == 0 and seq % CHUNK == 0 and w_in.shape[0] == 1

    w_in_b = w_in[0].astype(_BF16)
    n_bcvu = 3 * D_CONV + D_SSM
    g_mix = norm_mix_g[0].reshape(1, d)

    bz, u = _in_proj(x2, g_mix, w_in_b[:, :n_bcvu], conv_w[0], seq)

    u2 = u.reshape(bsz, n_chunks, CHUNK, npair, GROUPS_PER_STEP, SSM_GROUP)
    u2 = u2.transpose(3, 1, 0, 4, 2, 5).reshape(npair, n_chunks * bsz, GROUPS_PER_STEP * CHUNK * SSM_GROUP)
    tables = _ssm_tables(ssm_lam_re[0], ssm_lam_im[0], ssm_log_dt[0], ssm_b_re[0], ssm_b_im[0],
                         ssm_c_re[0], ssm_c_im[0], ssm_d[0])
    yg2 = _ssm(u2, tables, bsz)
    yg = yg2.reshape(npair, n_chunks, bsz, GROUPS_PER_STEP, CHUNK, SSM_GROUP)
    yg = yg.transpose(2, 1, 4, 0, 3, 5).reshape(t, D_SSM)

    w_router_p = jnp.zeros((d, LANES), _F32).at[:, :N_EXPERTS].set(w_router[0])
    b_router_p = jnp.full((1, LANES), -jnp.inf, _F32).at[0, :N_EXPERTS].set(b_router[0])
    x1, h, top_i, top_g, rank, counts = _mix_route(
        x2, bz, yg, g_mix, w_in_b[:, n_bcvu:], w_conv_out[0].astype(_BF16),
        w_glu[0].astype(_BF16), w_out[0].astype(_BF16), norm_ffn_g[0].reshape(1, d),
        w_router_p, b_router_p)

    counts = counts[0, :N_EXPERTS]
    padded = ((counts + MOE_BLOCK - 1) // MOE_BLOCK) * MOE_BLOCK
    pad_end = jnp.cumsum(padded)
    pad_start = pad_end - padded
    n_rows = t * TOP_K + N_EXPERTS * MOE_BLOCK
    n_blocks = n_rows // MOE_BLOCK
    dest = (pad_start[top_i] + rank).reshape(-1)
    tok = jnp.arange(t * TOP_K, dtype=jnp.int32) // TOP_K
    row_tok = jnp.zeros((n_rows,), jnp.int32).at[dest].set(tok, unique_indices=True)
    block_e = jnp.minimum(
        jnp.searchsorted(pad_end, jnp.arange(n_blocks) * MOE_BLOCK, side='right'),
        N_EXPERTS - 1).astype(jnp.int32)
    n_valid = (pad_end[-1:] // MOE_BLOCK).astype(jnp.int32)

    x_rows = jnp.take(h, row_tok, axis=0, mode='clip')
    y_rows = _expert_ffn(block_e, n_valid, x_rows, w_gate_up[0].astype(_BF16),
                         b_gate_up[0].reshape(N_EXPERTS, 1, 2 * D_FF), w_down[0].astype(_BF16),
                         b_down[0].reshape(N_EXPERTS, 1, D_MODEL))
    y_assign = jnp.take(y_rows, dest, axis=0, mode='clip').reshape(t, TOP_K * D_MODEL)
    out = _combine(x1, y_assign, top_g, norm_f_g.reshape(1, d))
    return out.reshape(bsz, seq, d)
```

```python
import functools

import jax
import jax.numpy as jnp
from jax import lax
from jax.experimental import pallas as pl
from jax.experimental.pallas import tpu as pltpu

D_MODEL = 1024
D_CONV = 512
CONV_WIDTH = 3
D_SSM = 512
SSM_GROUP = 16
N_SSM_GROUPS = 32
SSM_STATE = 64
N_EXPERTS = 32
TOP_K = 4
D_FF = 1024
SWIGLU_LIMIT = 7.0
SWIGLU_ALPHA = 1.702
MOE_BLOCK = 512
RMS_EPS = 1e-6

LANES = 128
MXU_DIM = 256
CHUNK = 16
SLAB_GROUPS = LANES // SSM_GROUP
N_SLABS = N_SSM_GROUPS // SLAB_GROUPS
SLAB_STATE = SLAB_GROUPS * SSM_STATE
FLAT = CHUNK * LANES
SSM_TIME_TILE = 256
TOKEN_TILE = 512
VMEM_LIMIT_BYTES = 56 * 1024 * 1024

_BF16 = jnp.bfloat16
_F32 = jnp.float32


def _rmsnorm(xf, g):
    return xf * lax.rsqrt(jnp.mean(xf * xf, axis=-1, keepdims=True) + RMS_EPS) * g


def _sigmoid(v):
    return 1.0 / (1.0 + jnp.exp(-v))


def _in_proj_kernel(tiles_per_seq, x_ref, g_ref, w_ref, cw_ref, bz_ref, u_ref, hbuf):
    tm = x_ref.shape[0]
    i = pl.program_id(0)

    @pl.when(i % tiles_per_seq == 0)
    def _():
        hbuf[0:8, :] = jnp.zeros((8, D_CONV), _F32)

    xn = _rmsnorm(x_ref[...], g_ref[...]).astype(_BF16)
    proj = jnp.dot(xn, w_ref[...], preferred_element_type=_F32)
    b_gate = proj[:, 0:D_CONV]
    c_gate = proj[:, D_CONV:2 * D_CONV]
    v = proj[:, 2 * D_CONV:3 * D_CONV]
    u_ref[...] = proj[:, 3 * D_CONV:]

    hbuf[8:8 + tm, :] = c_gate * v
    cw = cw_ref[...]
    z = (cw[0:1, :] * hbuf[6:6 + tm, :] + cw[1:2, :] * hbuf[7:7 + tm, :]
         + cw[2:3, :] * hbuf[8:8 + tm, :])
    bz_ref[...] = (b_gate * z).astype(_BF16)
    hbuf[0:8, :] = hbuf[tm:tm + 8, :]


def _in_proj(x2, g, w_bcvu, conv_w, seq):
    t = x2.shape[0]
    tm = TOKEN_TILE
    return pl.pallas_call(
        functools.partial(_in_proj_kernel, seq // tm),
        out_shape=(jax.ShapeDtypeStruct((t, D_CONV), _BF16),
                   jax.ShapeDtypeStruct((t, D_SSM), _F32)),
        grid=(t // tm,),
        in_specs=[pl.BlockSpec((tm, D_MODEL), lambda i: (i, 0)),
                  pl.BlockSpec((1, D_MODEL), lambda i: (0, 0)),
                  pl.BlockSpec((D_MODEL, 3 * D_CONV + D_SSM), lambda i: (0, 0)),
                  pl.BlockSpec((CONV_WIDTH, D_CONV), lambda i: (0, 0))],
        out_specs=(pl.BlockSpec((tm, D_CONV), lambda i: (i, 0)),
                   pl.BlockSpec((tm, D_SSM), lambda i: (i, 0))),
        scratch_shapes=[pltpu.VMEM((tm + 8, D_CONV), _F32)],
        compiler_params=pltpu.CompilerParams(
            dimension_semantics=("arbitrary",), vmem_limit_bytes=VMEM_LIMIT_BYTES),
        name="in_proj",
    )(x2, g, w_bcvu, conv_w)


def _ssm_tables(lam_re, lam_im, log_dt, b_re, b_im, c_re, c_im, d_skip):
    hi = lax.Precision.HIGHEST
    dt = jnp.exp(log_dt)[:, None]
    k = jnp.arange(CHUNK + 1, dtype=_F32)[:, None, None]
    mag = jnp.exp(k * (lam_re * dt))
    ang = k * (lam_im * dt)
    pw_re, pw_im = mag * jnp.cos(ang), mag * jnp.sin(ang)
    a_re, a_im = pw_re[1], pw_im[1]
    den = lam_re * lam_re + lam_im * lam_im
    q_re = ((a_re - 1.0) * lam_re + a_im * lam_im) / den
    q_im = (a_im * lam_re - (a_re - 1.0) * lam_im) / den
    bb_re = q_re[..., None] * b_re - q_im[..., None] * b_im
    bb_im = q_re[..., None] * b_im + q_im[..., None] * b_re
    pk_re, pk_im = pw_re[:CHUNK], pw_im[:CHUNK]
    ab_re = pk_re[..., None] * bb_re[None] - pk_im[..., None] * bb_im[None]
    ab_im = pk_re[..., None] * bb_im[None] + pk_im[..., None] * bb_re[None]
    kern = (jnp.einsum('gcp,lgpd->lgcd', c_re, ab_re, precision=hi)
            - jnp.einsum('gcp,lgpd->lgcd', c_im, ab_im, precision=hi))
    kern = kern.at[0].add(d_skip.reshape(N_SSM_GROUPS, SSM_GROUP)[:, :, None]
                          * jnp.eye(SSM_GROUP, dtype=_F32))
    eye = jnp.eye(SLAB_GROUPS, dtype=_F32)
    sg = (N_SLABS, SLAB_GROUPS)

    lag = jnp.arange(CHUNK)[None, :] - jnp.arange(CHUNK)[:, None]
    kt = jnp.where((lag >= 0)[:, :, None, None, None],
                   kern[jnp.clip(lag, 0, CHUNK - 1)], 0.0)
    kt = kt.reshape(CHUNK, CHUNK, *sg, SSM_GROUP, SSM_GROUP).transpose(2, 0, 3, 5, 1, 4)
    toep = kt[:, :, :, :, :, None, :] * eye[None, None, :, None, None, :, None]
    toep = toep.reshape(N_SLABS, FLAT, FLAT).astype(_BF16)

    rev = CHUNK - 1 - jnp.arange(CHUNK)
    bs_re = pw_re[rev][..., None] * bb_re[None] - pw_im[rev][..., None] * bb_im[None]
    bs_im = pw_re[rev][..., None] * bb_im[None] + pw_im[rev][..., None] * bb_re[None]
    bs = jnp.stack([bs_re, bs_im], axis=0).reshape(2, CHUNK, *sg, SSM_STATE, SSM_GROUP)
    bs = bs.transpose(2, 1, 3, 5, 0, 4)
    bst = bs[:, :, :, :, :, None, :] * eye[None, None, :, None, None, :, None]
    bst = bst.reshape(N_SLABS, FLAT, 2 * SLAB_STATE).astype(_BF16)

    nx_re, nx_im = pw_re[1:], pw_im[1:]
    cs_re = c_re[None] * nx_re[:, :, None, :] - c_im[None] * nx_im[:, :, None, :]
    cs_im = c_re[None] * nx_im[:, :, None, :] + c_im[None] * nx_re[:, :, None, :]
    cs = jnp.stack([cs_re, -cs_im], axis=0).reshape(2, CHUNK, *sg, SSM_GROUP, SSM_STATE)
    cs = cs.transpose(2, 0, 5, 1, 3, 4)
    cst = cs[:, :, None, :, :, :, :] * eye[None, None, :, None, None, :, None]
    cst = cst.reshape(N_SLABS, 2 * SLAB_STATE, FLAT).astype(_BF16)

    a_chunk = jnp.stack([pw_re[CHUNK].reshape(N_SLABS, SLAB_STATE),
                         pw_im[CHUNK].reshape(N_SLABS, SLAB_STATE)], axis=1)
    return toep, bst, cst, a_chunk


def _ssm_kernel(u_ref, toep_ref, bst_ref, cst_ref, a_ref, y_ref, uflat, s_scr, xc_scr, carry):
    nb, tt, _ = u_ref.shape
    nch = tt // CHUNK
    n = nb * nch

    @pl.when(pl.program_id(1) == 0)
    def _():
        carry[...] = jnp.zeros_like(carry)

    for s in range(CHUNK):
        part = u_ref[:, pl.ds(s, nch, stride=CHUNK), :]
        uflat[:, s * LANES:(s + 1) * LANES] = part.reshape(n, LANES).astype(_BF16)

    nblk = SLAB_STATE // LANES
    loc_all = jnp.dot(uflat[...], bst_ref[0], preferred_element_type=_F32)
    for kb in range(2 * nblk):
        s_scr[kb] = loc_all[:, kb * LANES:(kb + 1) * LANES]

    a = a_ref[0]
    are = [jnp.broadcast_to(a[0:1, kb * LANES:(kb + 1) * LANES], (nb, LANES)) for kb in range(nblk)]
    aim = [jnp.broadcast_to(a[1:2, kb * LANES:(kb + 1) * LANES], (nb, LANES)) for kb in range(nblk)]
    xr = [carry[kb] for kb in range(nblk)]
    xi = [carry[nblk + kb] for kb in range(nblk)]
    for j in range(nch):
        rows = pl.ds(j, nb, stride=nch)
        for kb in range(nblk):
            xc_scr[kb, rows, :] = xr[kb]
            xc_scr[nblk + kb, rows, :] = xi[kb]
            nr = are[kb] * xr[kb] - aim[kb] * xi[kb] + s_scr[kb, rows, :]
            ni = are[kb] * xi[kb] + aim[kb] * xr[kb] + s_scr[nblk + kb, rows, :]
            xr[kb], xi[kb] = nr, ni
    for kb in range(nblk):
        carry[kb] = xr[kb]
        carry[nblk + kb] = xi[kb]

    xc = jnp.concatenate([xc_scr[kb] for kb in range(2 * nblk)], axis=1).astype(_BF16)
    for cb in range(FLAT // MXU_DIM):
        kk = (cb + 1) * MXU_DIM
        cols = slice(cb * MXU_DIM, kk)
        y = (jnp.dot(uflat[:, :kk], toep_ref[0, :kk, cols], preferred_element_type=_F32)
             + jnp.dot(xc, cst_ref[0, :, cols], preferred_element_type=_F32))
        y = jax.nn.gelu(y)
        for h in range(MXU_DIM // LANES):
            s = cb * (MXU_DIM // LANES) + h
            y_ref[:, pl.ds(s, nch, stride=CHUNK), :] = (
                y[:, h * LANES:(h + 1) * LANES].reshape(nb, nch, LANES))


def _ssm(u3, tables):
    toep, bst, cst, a_chunk = tables
    nb, seq, _ = u3.shape
    tt = SSM_TIME_TILE
    n = nb * (tt // CHUNK)
    return pl.pallas_call(
        _ssm_kernel,
        out_shape=jax.ShapeDtypeStruct(u3.shape, _F32),
        grid=(N_SLABS, seq // tt),
        in_specs=[pl.BlockSpec((nb, tt, LANES), lambda sl, ti: (0, ti, sl)),
                  pl.BlockSpec((1, FLAT, FLAT), lambda sl, ti: (sl, 0, 0)),
                  pl.BlockSpec((1, FLAT, 2 * SLAB_STATE), lambda sl, ti: (sl, 0, 0)),
                  pl.BlockSpec((1, 2 * SLAB_STATE, FLAT), lambda sl, ti: (sl, 0, 0)),
                  pl.BlockSpec((1, 2, SLAB_STATE), lambda sl, ti: (sl, 0, 0))],
        out_specs=pl.BlockSpec((nb, tt, LANES), lambda sl, ti: (0, ti, sl)),
        scratch_shapes=[pltpu.VMEM((n, FLAT), _BF16),
                        pltpu.VMEM((2 * SLAB_STATE // LANES, n, LANES), _F32),
                        pltpu.VMEM((2 * SLAB_STATE // LANES, n, LANES), _F32),
                        pltpu.VMEM((2 * SLAB_STATE // LANES, nb, LANES), _F32)],
        compiler_params=pltpu.CompilerParams(
            dimension_semantics=("arbitrary", "arbitrary"), vmem_limit_bytes=VMEM_LIMIT_BYTES),
        name="ssm",
    )(u3, toep, bst, cst, a_chunk)


def _mix_route_kernel(x_ref, bz_ref, yg_ref, gm_ref, wg_ref, wco_ref, wglu_ref, wout_ref,
                      gf_ref, wr_ref, br_ref,
                      x1_ref, h_ref, ti_ref, tg_ref, rk_ref, cnt_ref, base):
    tm = x_ref.shape[0]
    i = pl.program_id(0)

    @pl.when(i == 0)
    def _():
        base[...] = jnp.zeros_like(base)

    x = x_ref[...]
    xn = _rmsnorm(x, gm_ref[...]).astype(_BF16)
    gates = jnp.dot(xn, wg_ref[...], preferred_element_type=_F32)
    y_a = jnp.dot(bz_ref[...], wco_ref[...], preferred_element_type=_F32)
    glu = jnp.dot(yg_ref[...].astype(_BF16), wglu_ref[...], preferred_element_type=_F32)
    y_b = glu[:, :D_MODEL] * _sigmoid(glu[:, D_MODEL:])
    merged = _sigmoid(gates[:, :D_MODEL]) * y_a + _sigmoid(gates[:, D_MODEL:]) * y_b
    x1 = x + jnp.dot(merged.astype(_BF16), wout_ref[...], preferred_element_type=_F32)
    x1_ref[...] = x1
    h = _rmsnorm(x1, gf_ref[...])
    h_ref[...] = h.astype(_BF16)

    logits = jnp.dot(h, wr_ref[...], preferred_element_type=_F32,
                     precision=lax.Precision.HIGHEST) + br_ref[...]
    lane = lax.broadcasted_iota(jnp.int32, (tm, LANES), 1).astype(_F32)
    neg_inf = jnp.float32(-jnp.inf)
    work = logits
    vals, idxs = [], []
    for _ in range(TOP_K):
        m = jnp.max(work, axis=-1, keepdims=True)
        idx = jnp.min(jnp.where(work == m, lane, float(LANES)), axis=-1, keepdims=True)
        vals.append(m)
        idxs.append(idx)
        work = jnp.where(lane == idx, neg_inf, work)
    exps = [jnp.exp(v - vals[0]) for v in vals]
    denom = exps[0] + exps[1] + exps[2] + exps[3]

    sel = jnp.zeros((tm, LANES), _F32)
    for idx in idxs:
        sel = sel + (lane == idx).astype(_F32)
    row = lax.broadcasted_iota(jnp.int32, (tm, tm), 0)
    col = lax.broadcasted_iota(jnp.int32, (tm, tm), 1)
    strict_lower = (col < row).astype(_BF16)
    before = jnp.dot(strict_lower, sel.astype(_BF16), preferred_element_type=_F32) + base[...]
    for k in range(TOP_K):
        ti_ref[:, k:k + 1] = idxs[k].astype(jnp.int32)
        tg_ref[:, k:k + 1] = exps[k] / denom
        rk = jnp.sum(jnp.where(lane == idxs[k], before, 0.0), axis=-1, keepdims=True)
        rk_ref[:, k:k + 1] = rk.astype(jnp.int32)
    base[...] = base[...] + jnp.sum(sel, axis=0, keepdims=True)
    cnt_ref[...] = base[...].astype(jnp.int32)


def _mix_route(x2, bz, yg, g_mix, w_gates, w_conv_out, w_glu, w_out, g_ffn, w_router, b_router):
    t = x2.shape[0]
    tm = TOKEN_TILE
    tok = lambda i: (i, 0)
    fixed = lambda i: (0, 0)
    return pl.pallas_call(
        _mix_route_kernel,
        out_shape=(jax.ShapeDtypeStruct((t, D_MODEL), _F32),
                   jax.ShapeDtypeStruct((t, D_MODEL), _BF16),
                   jax.ShapeDtypeStruct((t, TOP_K), jnp.int32),
                   jax.ShapeDtypeStruct((t, TOP_K), _F32),
                   jax.ShapeDtypeStruct((t, TOP_K), jnp.int32),
                   jax.ShapeDtypeStruct((1, LANES), jnp.int32)),
        grid=(t // tm,),
        in_specs=[pl.BlockSpec((tm, D_MODEL), tok),
                  pl.BlockSpec((tm, D_CONV), tok),
                  pl.BlockSpec((tm, D_SSM), tok),
                  pl.BlockSpec((1, D_MODEL), fixed),
                  pl.BlockSpec((D_MODEL, 2 * D_MODEL), fixed),
                  pl.BlockSpec((D_CONV, D_MODEL), fixed),
                  pl.BlockSpec((D_SSM, 2 * D_MODEL), fixed),
                  pl.BlockSpec((D_MODEL, D_MODEL), fixed),
                  pl.BlockSpec((1, D_MODEL), fixed),
                  pl.BlockSpec((D_MODEL, LANES), fixed),
                  pl.BlockSpec((1, LANES), fixed)],
        out_specs=(pl.BlockSpec((tm, D_MODEL), tok),
                   pl.BlockSpec((tm, D_MODEL), tok),
                   pl.BlockSpec((tm, TOP_K), tok),
                   pl.BlockSpec((tm, TOP_K), tok),
                   pl.BlockSpec((tm, TOP_K), tok),
                   pl.BlockSpec((1, LANES), fixed)),
        scratch_shapes=[pltpu.VMEM((1, LANES), _F32)],
        compiler_params=pltpu.CompilerParams(
            dimension_semantics=("arbitrary",), vmem_limit_bytes=VMEM_LIMIT_BYTES),
        name="mix_route",
    )(x2, bz, yg, g_mix, w_gates, w_conv_out, w_glu, w_out, g_ffn, w_router, b_router)


def _expert_ffn_kernel(be_ref, nv_ref, x_ref, wgu_ref, bgu_ref, wd_ref, bd_ref, y_ref):
    b = pl.program_id(0)

    @pl.when(b < nv_ref[0])
    def _():
        hgu = jnp.dot(x_ref[...], wgu_ref[0], preferred_element_type=_F32) + bgu_ref[0]
        g = jnp.minimum(hgu[:, :D_FF], SWIGLU_LIMIT)
        up = jnp.clip(hgu[:, D_FF:], -SWIGLU_LIMIT, SWIGLU_LIMIT)
        act = (up + 1.0) * (g * _sigmoid(SWIGLU_ALPHA * g))
        y_ref[...] = jnp.dot(act.astype(_BF16), wd_ref[0], preferred_element_type=_F32) + bd_ref[0]

    @pl.when(b >= nv_ref[0])
    def _():
        y_ref[...] = jnp.zeros_like(y_ref)


def _expert_ffn(block_e, n_valid, x_rows, w_gate_up, b_gate_up, w_down, b_down):
    n_rows = x_rows.shape[0]
    n_blocks = n_rows // MOE_BLOCK

    def xmap(b, be, nv):
        return (jnp.maximum(jnp.minimum(b, nv[0] - 1), 0), 0)

    def wmap(b, be, nv):
        return (be[b], 0, 0)

    grid_spec = pltpu.PrefetchScalarGridSpec(
        num_scalar_prefetch=2,
        grid=(n_blocks,),
        in_specs=[pl.BlockSpec((MOE_BLOCK, D_MODEL), xmap),
                  pl.BlockSpec((1, D_MODEL, 2 * D_FF), wmap),
                  pl.BlockSpec((1, 1, 2 * D_FF), wmap),
                  pl.BlockSpec((1, D_FF, D_MODEL), wmap),
                  pl.BlockSpec((1, 1, D_MODEL), wmap)],
        out_specs=pl.BlockSpec((MOE_BLOCK, D_MODEL), lambda b, be, nv: (b, 0)),
    )
    return pl.pallas_call(
        _expert_ffn_kernel,
        out_shape=jax.ShapeDtypeStruct((n_rows, D_MODEL), _F32),
        grid_spec=grid_spec,
        compiler_params=pltpu.CompilerParams(
            dimension_semantics=("arbitrary",), vmem_limit_bytes=VMEM_LIMIT_BYTES),
        name="expert_ffn",
    )(block_e, n_valid, x_rows, w_gate_up, b_gate_up, w_down, b_down)


def _combine_kernel(x1_ref, ya_ref, tg_ref, g_ref, o_ref):
    acc = x1_ref[...]
    tg = tg_ref[...]
    for k in range(TOP_K):
        acc = acc + tg[:, k:k + 1] * ya_ref[:, k * D_MODEL:(k + 1) * D_MODEL]
    o_ref[...] = _rmsnorm(acc, g_ref[...])


def _combine(x1, y_assign, top_g, g_final):
    t = x1.shape[0]
    tm = TOKEN_TILE
    return pl.pallas_call(
        _combine_kernel,
        out_shape=jax.ShapeDtypeStruct((t, D_MODEL), _F32),
        grid=(t // tm,),
        in_specs=[pl.BlockSpec((tm, D_MODEL), lambda i: (i, 0)),
                  pl.BlockSpec((tm, TOP_K * D_MODEL), lambda i: (i, 0)),
                  pl.BlockSpec((tm, TOP_K), lambda i: (i, 0)),
                  pl.BlockSpec((1, D_MODEL), lambda i: (0, 0))],
        out_specs=pl.BlockSpec((tm, D_MODEL), lambda i: (i, 0)),
        compiler_params=pltpu.CompilerParams(
            dimension_semantics=("arbitrary",), vmem_limit_bytes=VMEM_LIMIT_BYTES),
        name="combine",
    )(x1, y_assign, top_g, g_final)


def kernel(x, norm_mix_g, w_in, conv_w, w_conv_out, ssm_lam_re, ssm_lam_im, ssm_log_dt, ssm_b_re, ssm_b_im, ssm_c_re, ssm_c_im, ssm_d, w_glu, w_out, norm_ffn_g, w_router, b_router, w_gate_up, b_gate_up, w_down, b_down, norm_f_g):
    bsz, seq, d = x.shape
    t = bsz * seq
    x2 = x.reshape(t, d)
    assert seq % TOKEN_TILE == 0 and seq % SSM_TIME_TILE == 0 and w_in.shape[0] == 1

    w_in_b = w_in[0].astype(_BF16)
    n_bcvu = 3 * D_CONV + D_SSM
    g_mix = norm_mix_g[0].reshape(1, d)

    bz, u = _in_proj(x2, g_mix, w_in_b[:, :n_bcvu], conv_w[0], seq)

    tables = _ssm_tables(ssm_lam_re[0], ssm_lam_im[0], ssm_log_dt[0], ssm_b_re[0], ssm_b_im[0],
                         ssm_c_re[0], ssm_c_im[0], ssm_d[0])
    yg = _ssm(u.reshape(bsz, seq, D_SSM), tables).reshape(t, D_SSM)

    w_router_p = jnp.zeros((d, LANES), _F32).at[:, :N_EXPERTS].set(w_router[0])
    b_router_p = jnp.full((1, LANES), -jnp.inf, _F32).at[0, :N_EXPERTS].set(b_router[0])
    x1, h, top_i, top_g, rank, counts = _mix_route(
        x2, bz, yg, g_mix, w_in_b[:, n_bcvu:], w_conv_out[0].astype(_BF16),
        w_glu[0].astype(_BF16), w_out[0].astype(_BF16), norm_ffn_g[0].reshape(1, d),
        w_router_p, b_router_p)

    counts = counts[0, :N_EXPERTS]
    padded = ((counts + MOE_BLOCK - 1) // MOE_BLOCK) * MOE_BLOCK
    pad_end = jnp.cumsum(padded)
    pad_start = pad_end - padded
    n_rows = t * TOP_K + N_EXPERTS * MOE_BLOCK
    n_blocks = n_rows // MOE_BLOCK
    dest = (pad_start[top_i] + rank).reshape(-1)
    tok = jnp.arange(t * TOP_K, dtype=jnp.int32) // TOP_K
    row_tok = jnp.zeros((n_rows,), jnp.int32).at[dest].set(tok, unique_indices=True)
    block_e = jnp.minimum(
        jnp.searchsorted(pad_end, jnp.arange(n_blocks) * MOE_BLOCK, side='right'),
        N_EXPERTS - 1).astype(jnp.int32)
    n_valid = (pad_end[-1:] // MOE_BLOCK).astype(jnp.int32)

    x_rows = jnp.take(h, row_tok, axis=0, mode='clip')
    y_rows = _expert_ffn(block_e, n_valid, x_rows, w_gate_up[0].astype(_BF16),
                         b_gate_up[0].reshape(N_EXPERTS, 1, 2 * D_FF), w_down[0].astype(_BF16),
                         b_down[0].reshape(N_EXPERTS, 1, D_MODEL))
    y_assign = jnp.take(y_rows, dest, axis=0, mode='clip').reshape(t, TOP_K * D_MODEL)
    out = _combine(x1, y_assign, top_g, norm_f_g.reshape(1, d))
    return out.reshape(bsz, seq, d)
```

```python
import functools

import jax
import jax.numpy as jnp
from jax import lax
from jax.experimental import pallas as pl
from jax.experimental.pallas import tpu as pltpu

D_MODEL = 1024
D_CONV = 512
CONV_WIDTH = 3
D_SSM = 512
SSM_GROUP = 16
N_SSM_GROUPS = 32
SSM_STATE = 64
N_EXPERTS = 32
TOP_K = 4
D_FF = 1024
SWIGLU_LIMIT = 7.0
SWIGLU_ALPHA = 1.702
MOE_BLOCK = 512
RMS_EPS = 1e-6

LANES = 128
MXU_DIM = 256
CHUNK = 16
SLAB_GROUPS = LANES // SSM_GROUP
N_SLABS = N_SSM_GROUPS // SLAB_GROUPS
SLAB_STATE = SLAB_GROUPS * SSM_STATE
FLAT = CHUNK * LANES
SSM_TIME_TILE = 256
TOKEN_TILE = 512
VMEM_LIMIT_BYTES = 56 * 1024 * 1024

_BF16 = jnp.bfloat16
_F32 = jnp.float32


def _rmsnorm(xf, g):
    return xf * lax.rsqrt(jnp.mean(xf * xf, axis=-1, keepdims=True) + RMS_EPS) * g


def _sigmoid(v):
    return 1.0 / (1.0 + jnp.exp(-v))


def _in_proj_kernel(tiles_per_seq, x_ref, g_ref, w_ref, cw_ref, bz_ref, u_ref, hbuf):
    tm = x_ref.shape[0]
    i = pl.program_id(0)

    @pl.when(i % tiles_per_seq == 0)
    def _():
        hbuf[0:8, :] = jnp.zeros((8, D_CONV), _F32)

    xn = _rmsnorm(x_ref[...], g_ref[...]).astype(_BF16)
    proj = jnp.dot(xn, w_ref[...], preferred_element_type=_F32)
    b_gate = proj[:, 0:D_CONV]
    c_gate = proj[:, D_CONV:2 * D_CONV]
    v = proj[:, 2 * D_CONV:3 * D_CONV]
    u_ref[...] = proj[:, 3 * D_CONV:]

    hbuf[8:8 + tm, :] = c_gate * v
    cw = cw_ref[...]
    z = (cw[0:1, :] * hbuf[6:6 + tm, :] + cw[1:2, :] * hbuf[7:7 + tm, :]
         + cw[2:3, :] * hbuf[8:8 + tm, :])
    bz_ref[...] = (b_gate * z).astype(_BF16)
    hbuf[0:8, :] = hbuf[tm:tm + 8, :]


def _in_proj(x2, g, w_bcvu, conv_w, seq):
    t = x2.shape[0]
    tm = TOKEN_TILE
    return pl.pallas_call(
        functools.partial(_in_proj_kernel, seq // tm),
        out_shape=(jax.ShapeDtypeStruct((t, D_CONV), _BF16),
                   jax.ShapeDtypeStruct((t, D_SSM), _F32)),
        grid=(t // tm,),
        in_specs=[pl.BlockSpec((tm, D_MODEL), lambda i: (i, 0)),
                  pl.BlockSpec((1, D_MODEL), lambda i: (0, 0)),
                  pl.BlockSpec((D_MODEL, 3 * D_CONV + D_SSM), lambda i: (0, 0)),
                  pl.BlockSpec((CONV_WIDTH, D_CONV), lambda i: (0, 0))],
        out_specs=(pl.BlockSpec((tm, D_CONV), lambda i: (i, 0)),
                   pl.BlockSpec((tm, D_SSM), lambda i: (i, 0))),
        scratch_shapes=[pltpu.VMEM((tm + 8, D_CONV), _F32)],
        compiler_params=pltpu.CompilerParams(
            dimension_semantics=("arbitrary",), vmem_limit_bytes=VMEM_LIMIT_BYTES),
        name="in_proj",
    )(x2, g, w_bcvu, conv_w)


def _ssm_prep_kernel(lr_ref, lc_ref, bm_ref, cm_ref, d_ref, toep_ref, bst_ref, cst_ref, a_ref):
    def discretise(lre, lim, log_dt):
        dt = jnp.exp(log_dt)
        mag = jnp.exp(lre * dt)
        return mag * jnp.cos(lim * dt), mag * jnp.sin(lim * dt)

    def powers(are, aim):
        pre, pim = [jnp.ones_like(are)], [jnp.zeros_like(are)]
        for _ in range(CHUNK):
            pre, pim = (pre + [pre[-1] * are - pim[-1] * aim],
                        pim + [pre[-1] * aim + pim[-1] * are])
        return pre, pim

    lr = lr_ref[0]
    lre, lim = lr[0:1, :], lr[1:2, :]
    are, aim = discretise(lre, lim, lr[2:3, :])
    pre, pim = powers(are, aim)
    den = lre * lre + lim * lim
    q_re = ((are - 1.0) * lre + aim * lim) / den
    q_im = (aim * lre - (are - 1.0) * lim) / den
    bb_re = q_re * bm_ref[0, 0] - q_im * bm_ref[0, 1]
    bb_im = q_re * bm_ref[0, 1] + q_im * bm_ref[0, 0]
    cm_re, cm_im = cm_ref[0, 0], cm_ref[0, 1]
    hi = lax.Precision.HIGHEST
    kblk = []
    for k in range(CHUNK):
        ab_re = bb_re * pre[k] - bb_im * pim[k]
        ab_im = bb_re * pim[k] + bb_im * pre[k]
        rows = slice((CHUNK - 1 - k) * LANES, (CHUNK - k) * LANES)
        bst_ref[0, rows, :SLAB_STATE] = ab_re.astype(_BF16)
        bst_ref[0, rows, SLAB_STATE:] = ab_im.astype(_BF16)
        kblk.append(jnp.dot(ab_re, cm_re, preferred_element_type=_F32, precision=hi)
                    - jnp.dot(ab_im, cm_im, preferred_element_type=_F32, precision=hi))
    r = lax.broadcasted_iota(jnp.int32, (LANES, LANES), 0)
    c = lax.broadcasted_iota(jnp.int32, (LANES, LANES), 1)
    kblk[0] = kblk[0] + jnp.where(r == c, jnp.broadcast_to(d_ref[0], (LANES, LANES)), 0.0)
    kblk = [kb.astype(_BF16) for kb in kblk]
    zeros = jnp.zeros((LANES, LANES), _BF16)
    for sp in range(CHUNK):
        for s in range(CHUNK):
            toep_ref[0, sp * LANES:(sp + 1) * LANES, s * LANES:(s + 1) * LANES] = (
                kblk[s - sp] if s >= sp else zeros)

    lc = lc_ref[0]
    cre, cim = discretise(lc[:, 0:1], lc[:, 1:2], lc[:, 2:3])
    qre, qim = powers(cre, cim)
    for s in range(CHUNK):
        cols = slice(s * LANES, (s + 1) * LANES)
        cst_ref[0, :SLAB_STATE, cols] = (cm_re * qre[s + 1] - cm_im * qim[s + 1]).astype(_BF16)
        cst_ref[0, SLAB_STATE:, cols] = (-(cm_re * qim[s + 1] + cm_im * qre[s + 1])).astype(_BF16)
    a_ref[0, 0:1, :] = pre[CHUNK]
    a_ref[0, 1:2, :] = pim[CHUNK]


def _ssm_tables(lam_re, lam_im, log_dt, b_re, b_im, c_re, c_im, d_skip):
    sg = (N_SLABS, SLAB_GROUPS)
    eye = jnp.eye(SLAB_GROUPS, dtype=_F32)
    lam = jnp.stack([lam_re, lam_im, jnp.broadcast_to(log_dt[:, None], lam_re.shape)], axis=0)
    lam_row = lam.reshape(3, N_SLABS, SLAB_STATE).transpose(1, 0, 2)
    lam_col = lam_row.transpose(0, 2, 1)

    def b_blockdiag(b):
        bt = b.reshape(*sg, SSM_STATE, SSM_GROUP).transpose(0, 1, 3, 2)
        return (bt[:, :, :, None, :] * eye[None, :, None, :, None]).reshape(N_SLABS, LANES, SLAB_STATE)

    def c_blockdiag(c):
        ct = c.reshape(*sg, SSM_GROUP, SSM_STATE).transpose(0, 1, 3, 2)
        return (ct[:, :, :, None, :] * eye[None, :, None, :, None]).reshape(N_SLABS, SLAB_STATE, LANES)

    bm = jnp.stack([b_blockdiag(b_re), b_blockdiag(b_im)], axis=1)
    cm = jnp.stack([c_blockdiag(c_re), c_blockdiag(c_im)], axis=1)
    d = d_skip.reshape(N_SLABS, 1, LANES)
    slab3 = lambda sl: (sl, 0, 0)
    slab4 = lambda sl: (sl, 0, 0, 0)
    return pl.pallas_call(
        _ssm_prep_kernel,
        out_shape=(jax.ShapeDtypeStruct((N_SLABS, FLAT, FLAT), _BF16),
                   jax.ShapeDtypeStruct((N_SLABS, FLAT, 2 * SLAB_STATE), _BF16),
                   jax.ShapeDtypeStruct((N_SLABS, 2 * SLAB_STATE, FLAT), _BF16),
                   jax.ShapeDtypeStruct((N_SLABS, 2, SLAB_STATE), _F32)),
        grid=(N_SLABS,),
        in_specs=[pl.BlockSpec((1, 3, SLAB_STATE), slab3),
                  pl.BlockSpec((1, SLAB_STATE, 3), slab3),
                  pl.BlockSpec((1, 2, LANES, SLAB_STATE), slab4),
                  pl.BlockSpec((1, 2, SLAB_STATE, LANES), slab4),
                  pl.BlockSpec((1, 1, LANES), slab3)],
        out_specs=(pl.BlockSpec((1, FLAT, FLAT), slab3),
                   pl.BlockSpec((1, FLAT, 2 * SLAB_STATE), slab3),
                   pl.BlockSpec((1, 2 * SLAB_STATE, FLAT), slab3),
                   pl.BlockSpec((1, 2, SLAB_STATE), slab3)),
        compiler_params=pltpu.CompilerParams(
            dimension_semantics=("arbitrary",), vmem_limit_bytes=VMEM_LIMIT_BYTES),
        name="ssm_prep",
    )(lam_row, lam_col, bm, cm, d)


def _ssm_kernel(u_ref, toep_ref, bst_ref, cst_ref, a_ref, y_ref, uflat, s_scr, xc_scr, carry):
    nb, tt, _ = u_ref.shape
    nch = tt // CHUNK
    n = nb * nch

    @pl.when(pl.program_id(1) == 0)
    def _():
        carry[...] = jnp.zeros_like(carry)

    for s in range(CHUNK):
        part = u_ref[:, pl.ds(s, nch, stride=CHUNK), :]
        uflat[:, s * LANES:(s + 1) * LANES] = part.reshape(n, LANES).astype(_BF16)

    nblk = SLAB_STATE // LANES
    loc_all = jnp.dot(uflat[...], bst_ref[0], preferred_element_type=_F32)
    for kb in range(2 * nblk):
        s_scr[kb] = loc_all[:, kb * LANES:(kb + 1) * LANES]

    a = a_ref[0]
    are = [jnp.broadcast_to(a[0:1, kb * LANES:(kb + 1) * LANES], (nb, LANES)) for kb in range(nblk)]
    aim = [jnp.broadcast_to(a[1:2, kb * LANES:(kb + 1) * LANES], (nb, LANES)) for kb in range(nblk)]
    xr = [carry[kb] for kb in range(nblk)]
    xi = [carry[nblk + kb] for kb in range(nblk)]
    for j in range(nch):
        rows = pl.ds(j, nb, stride=nch)
        for kb in range(nblk):
            xc_scr[kb, rows, :] = xr[kb]
            xc_scr[nblk + kb, rows, :] = xi[kb]
            nr = are[kb] * xr[kb] - aim[kb] * xi[kb] + s_scr[kb, rows, :]
            ni = are[kb] * xi[kb] + aim[kb] * xr[kb] + s_scr[nblk + kb, rows, :]
            xr[kb], xi[kb] = nr, ni
    for kb in range(nblk):
        carry[kb] = xr[kb]
        carry[nblk + kb] = xi[kb]

    xc = jnp.concatenate([xc_scr[kb] for kb in range(2 * nblk)], axis=1).astype(_BF16)
    for cb in range(FLAT // MXU_DIM):
        kk = (cb + 1) * MXU_DIM
        cols = slice(cb * MXU_DIM, kk)
        y = (jnp.dot(uflat[:, :kk], toep_ref[0, :kk, cols], preferred_element_type=_F32)
             + jnp.dot(xc, cst_ref[0, :, cols], preferred_element_type=_F32))
        y = jax.nn.gelu(y)
        for h in range(MXU_DIM // LANES):
            s = cb * (MXU_DIM // LANES) + h
            y_ref[:, pl.ds(s, nch, stride=CHUNK), :] = (
                y[:, h * LANES:(h + 1) * LANES].reshape(nb, nch, LANES))


def _ssm(u3, tables):
    toep, bst, cst, a_chunk = tables
    nb, seq, _ = u3.shape
    tt = SSM_TIME_TILE
    n = nb * (tt // CHUNK)
    return pl.pallas_call(
        _ssm_kernel,
        out_shape=jax.ShapeDtypeStruct(u3.shape, _F32),
        grid=(N_SLABS, seq // tt),
        in_specs=[pl.BlockSpec((nb, tt, LANES), lambda sl, ti: (0, ti, sl)),
                  pl.BlockSpec((1, FLAT, FLAT), lambda sl, ti: (sl, 0, 0)),
                  pl.BlockSpec((1, FLAT, 2 * SLAB_STATE), lambda sl, ti: (sl, 0, 0)),
                  pl.BlockSpec((1, 2 * SLAB_STATE, FLAT), lambda sl, ti: (sl, 0, 0)),
                  pl.BlockSpec((1, 2, SLAB_STATE), lambda sl, ti: (sl, 0, 0))],
        out_specs=pl.BlockSpec((nb, tt, LANES), lambda sl, ti: (0, ti, sl)),
        scratch_shapes=[pltpu.VMEM((n, FLAT), _BF16),
                        pltpu.VMEM((2 * SLAB_STATE // LANES, n, LANES), _F32),
                        pltpu.VMEM((2 * SLAB_STATE // LANES, n, LANES), _F32),
                        pltpu.VMEM((2 * SLAB_STATE // LANES, nb, LANES), _F32)],
        compiler_params=pltpu.CompilerParams(
            dimension_semantics=("arbitrary", "arbitrary"), vmem_limit_bytes=VMEM_LIMIT_BYTES),
        name="ssm",
    )(u3, toep, bst, cst, a_chunk)


def _mix_route_kernel(x_ref, bz_ref, yg_ref, gm_ref, wg_ref, wco_ref, wglu_ref, wout_ref,
                      gf_ref, wr_ref, br_ref,
                      x1_ref, h_ref, ti_ref, tg_ref, rk_ref, cnt_ref, base):
    tm = x_ref.shape[0]
    i = pl.program_id(0)

    @pl.when(i == 0)
    def _():
        base[...] = jnp.zeros_like(base)

    x = x_ref[...]
    xn = _rmsnorm(x, gm_ref[...]).astype(_BF16)
    gates = jnp.dot(xn, wg_ref[...], preferred_element_type=_F32)
    y_a = jnp.dot(bz_ref[...], wco_ref[...], preferred_element_type=_F32)
    glu = jnp.dot(yg_ref[...].astype(_BF16), wglu_ref[...], preferred_element_type=_F32)
    y_b = glu[:, :D_MODEL] * _sigmoid(glu[:, D_MODEL:])
    merged = _sigmoid(gates[:, :D_MODEL]) * y_a + _sigmoid(gates[:, D_MODEL:]) * y_b
    x1 = x + jnp.dot(merged.astype(_BF16), wout_ref[...], preferred_element_type=_F32)
    x1_ref[...] = x1
    h = _rmsnorm(x1, gf_ref[...])
    h_ref[...] = h.astype(_BF16)

    logits = jnp.dot(h, wr_ref[...], preferred_element_type=_F32,
                     precision=lax.Precision.HIGHEST) + br_ref[...]
    lane = lax.broadcasted_iota(jnp.int32, (tm, LANES), 1).astype(_F32)
    neg_inf = jnp.float32(-jnp.inf)
    work = logits
    vals, idxs = [], []
    for _ in range(TOP_K):
        m = jnp.max(work, axis=-1, keepdims=True)
        idx = jnp.min(jnp.where(work == m, lane, float(LANES)), axis=-1, keepdims=True)
        vals.append(m)
        idxs.append(idx)
        work = jnp.where(lane == idx, neg_inf, work)
    exps = [jnp.exp(v - vals[0]) for v in vals]
    denom = exps[0] + exps[1] + exps[2] + exps[3]

    sel = jnp.zeros((tm, LANES), _F32)
    for idx in idxs:
        sel = sel + (lane == idx).astype(_F32)
    row = lax.broadcasted_iota(jnp.int32, (tm, tm), 0)
    col = lax.broadcasted_iota(jnp.int32, (tm, tm), 1)
    strict_lower = (col < row).astype(_BF16)
    before = jnp.dot(strict_lower, sel.astype(_BF16), preferred_element_type=_F32) + base[...]
    for k in range(TOP_K):
        ti_ref[:, k:k + 1] = idxs[k].astype(jnp.int32)
        tg_ref[:, k:k + 1] = exps[k] / denom
        rk = jnp.sum(jnp.where(lane == idxs[k], before, 0.0), axis=-1, keepdims=True)
        rk_ref[:, k:k + 1] = rk.astype(jnp.int32)
    base[...] = base[...] + jnp.sum(sel, axis=0, keepdims=True)
    cnt_ref[...] = base[...].astype(jnp.int32)


def _mix_route(x2, bz, yg, g_mix, w_gates, w_conv_out, w_glu, w_out, g_ffn, w_router, b_router):
    t = x2.shape[0]
    tm = TOKEN_TILE
    tok = lambda i: (i, 0)
    fixed = lambda i: (0, 0)
    return pl.pallas_call(
        _mix_route_kernel,
        out_shape=(jax.ShapeDtypeStruct((t, D_MODEL), _F32),
                   jax.ShapeDtypeStruct((t, D_MODEL), _BF16),
                   jax.ShapeDtypeStruct((t, TOP_K), jnp.int32),
                   jax.ShapeDtypeStruct((t, TOP_K), _F32),
                   jax.ShapeDtypeStruct((t, TOP_K), jnp.int32),
                   jax.ShapeDtypeStruct((1, LANES), jnp.int32)),
        grid=(t // tm,),
        in_specs=[pl.BlockSpec((tm, D_MODEL), tok),
                  pl.BlockSpec((tm, D_CONV), tok),
                  pl.BlockSpec((tm, D_SSM), tok),
                  pl.BlockSpec((1, D_MODEL), fixed),
                  pl.BlockSpec((D_MODEL, 2 * D_MODEL), fixed),
                  pl.BlockSpec((D_CONV, D_MODEL), fixed),
                  pl.BlockSpec((D_SSM, 2 * D_MODEL), fixed),
                  pl.BlockSpec((D_MODEL, D_MODEL), fixed),
                  pl.BlockSpec((1, D_MODEL), fixed),
                  pl.BlockSpec((D_MODEL, LANES), fixed),
                  pl.BlockSpec((1, LANES), fixed)],
        out_specs=(pl.BlockSpec((tm, D_MODEL), tok),
                   pl.BlockSpec((tm, D_MODEL), tok),
                   pl.BlockSpec((tm, TOP_K), tok),
                   pl.BlockSpec((tm, TOP_K), tok),
                   pl.BlockSpec((tm, TOP_K), tok),
                   pl.BlockSpec((1, LANES), fixed)),
        scratch_shapes=[pltpu.VMEM((1, LANES), _F32)],
        compiler_params=pltpu.CompilerParams(
            dimension_semantics=("arbitrary",), vmem_limit_bytes=VMEM_LIMIT_BYTES),
        name="mix_route",
    )(x2, bz, yg, g_mix, w_gates, w_conv_out, w_glu, w_out, g_ffn, w_router, b_router)


def _expert_ffn_kernel(be_ref, nv_ref, x_ref, wgu_ref, bgu_ref, wd_ref, bd_ref, y_ref, wgu_b, wd_b):
    b = pl.program_id(0)

    @pl.when((b == 0) | (be_ref[b] != be_ref[jnp.maximum(b - 1, 0)]))
    def _():
        wgu_b[...] = wgu_ref[0].astype(_BF16)
        wd_b[...] = wd_ref[0].astype(_BF16)

    @pl.when(b < nv_ref[0])
    def _():
        hgu = jnp.dot(x_ref[...], wgu_b[...], preferred_element_type=_F32) + bgu_ref[0]
        g = jnp.minimum(hgu[:, :D_FF], SWIGLU_LIMIT)
        up = jnp.clip(hgu[:, D_FF:], -SWIGLU_LIMIT, SWIGLU_LIMIT)
        act = (up + 1.0) * (g * _sigmoid(SWIGLU_ALPHA * g))
        y_ref[...] = jnp.dot(act.astype(_BF16), wd_b[...], preferred_element_type=_F32) + bd_ref[0]

    @pl.when(b >= nv_ref[0])
    def _():
        y_ref[...] = jnp.zeros_like(y_ref)


def _expert_ffn(block_e, n_valid, x_rows, w_gate_up, b_gate_up, w_down, b_down):
    n_rows = x_rows.shape[0]
    n_blocks = n_rows // MOE_BLOCK

    def xmap(b, be, nv):
        return (jnp.maximum(jnp.minimum(b, nv[0] - 1), 0), 0)

    def wmap(b, be, nv):
        return (be[b], 0, 0)

    grid_spec = pltpu.PrefetchScalarGridSpec(
        num_scalar_prefetch=2,
        grid=(n_blocks,),
        in_specs=[pl.BlockSpec((MOE_BLOCK, D_MODEL), xmap),
                  pl.BlockSpec((1, D_MODEL, 2 * D_FF), wmap),
                  pl.BlockSpec((1, 1, 2 * D_FF), wmap),
                  pl.BlockSpec((1, D_FF, D_MODEL), wmap),
                  pl.BlockSpec((1, 1, D_MODEL), wmap)],
        out_specs=pl.BlockSpec((MOE_BLOCK, D_MODEL), lambda b, be, nv: (b, 0)),
        scratch_shapes=[pltpu.VMEM((D_MODEL, 2 * D_FF), _BF16),
                        pltpu.VMEM((D_FF, D_MODEL), _BF16)],
    )
    return pl.pallas_call(
        _expert_ffn_kernel,
        out_shape=jax.ShapeDtypeStruct((n_rows, D_MODEL), _F32),
        grid_spec=grid_spec,
        compiler_params=pltpu.CompilerParams(
            dimension_semantics=("arbitrary",), vmem_limit_bytes=VMEM_LIMIT_BYTES),
        name="expert_ffn",
    )(block_e, n_valid, x_rows, w_gate_up, b_gate_up, w_down, b_down)


def _combine_kernel(x1_ref, ya_ref, tg_ref, g_ref, o_ref):
    acc = x1_ref[...]
    tg = tg_ref[...]
    for k in range(TOP_K):
        acc = acc + tg[:, k:k + 1] * ya_ref[:, k * D_MODEL:(k + 1) * D_MODEL]
    o_ref[...] = _rmsnorm(acc, g_ref[...])


def _combine(x1, y_assign, top_g, g_final):
    t = x1.shape[0]
    tm = TOKEN_TILE
    return pl.pallas_call(
        _combine_kernel,
        out_shape=jax.ShapeDtypeStruct((t, D_MODEL), _F32),
        grid=(t // tm,),
        in_specs=[pl.BlockSpec((tm, D_MODEL), lambda i: (i, 0)),
                  pl.BlockSpec((tm, TOP_K * D_MODEL), lambda i: (i, 0)),
                  pl.BlockSpec((tm, TOP_K), lambda i: (i, 0)),
                  pl.BlockSpec((1, D_MODEL), lambda i: (0, 0))],
        out_specs=pl.BlockSpec((tm, D_MODEL), lambda i: (i, 0)),
        compiler_params=pltpu.CompilerParams(
            dimension_semantics=("arbitrary",), vmem_limit_bytes=VMEM_LIMIT_BYTES),
        name="combine",
    )(x1, y_assign, top_g, g_final)


def kernel(x, norm_mix_g, w_in, conv_w, w_conv_out, ssm_lam_re, ssm_lam_im, ssm_log_dt, ssm_b_re, ssm_b_im, ssm_c_re, ssm_c_im, ssm_d, w_glu, w_out, norm_ffn_g, w_router, b_router, w_gate_up, b_gate_up, w_down, b_down, norm_f_g):
    bsz, seq, d = x.shape
    t = bsz * seq
    x2 = x.reshape(t, d)
    assert seq % TOKEN_TILE == 0 and seq % SSM_TIME_TILE == 0 and w_in.shape[0] == 1

    w_in_b = w_in[0].astype(_BF16)
    n_bcvu = 3 * D_CONV + D_SSM
    g_mix = norm_mix_g[0].reshape(1, d)

    bz, u = _in_proj(x2, g_mix, w_in_b[:, :n_bcvu], conv_w[0], seq)

    tables = _ssm_tables(ssm_lam_re[0], ssm_lam_im[0], ssm_log_dt[0], ssm_b_re[0], ssm_b_im[0],
                         ssm_c_re[0], ssm_c_im[0], ssm_d[0])
    yg = _ssm(u.reshape(bsz, seq, D_SSM), tables).reshape(t, D_SSM)

    w_router_p = jnp.zeros((d, LANES), _F32).at[:, :N_EXPERTS].set(w_router[0])
    b_router_p = jnp.full((1, LANES), -jnp.inf, _F32).at[0, :N_EXPERTS].set(b_router[0])
    x1, h, top_i, top_g, rank, counts = _mix_route(
        x2, bz, yg, g_mix, w_in_b[:, n_bcvu:], w_conv_out[0].astype(_BF16),
        w_glu[0].astype(_BF16), w_out[0].astype(_BF16), norm_ffn_g[0].reshape(1, d),
        w_router_p, b_router_p)

    counts = counts[0, :N_EXPERTS]
    padded = ((counts + MOE_BLOCK - 1) // MOE_BLOCK) * MOE_BLOCK
    pad_end = jnp.cumsum(padded)
    pad_start = pad_end - padded
    n_rows = t * TOP_K + N_EXPERTS * MOE_BLOCK
    n_blocks = n_rows // MOE_BLOCK
    dest = (pad_start[top_i] + rank).reshape(-1)
    tok = jnp.arange(t * TOP_K, dtype=jnp.int32) // TOP_K
    row_tok = jnp.zeros((n_rows,), jnp.int32).at[dest].set(tok, unique_indices=True)
    block_start = jnp.arange(n_blocks, dtype=jnp.int32) * MOE_BLOCK
    block_e = jnp.minimum(jnp.sum(pad_end[None, :] <= block_start[:, None], axis=1),
                          N_EXPERTS - 1).astype(jnp.int32)
    n_valid = (pad_end[-1:] // MOE_BLOCK).astype(jnp.int32)

    x_rows = jnp.take(h, row_tok, axis=0, mode='clip')
    y_rows = _expert_ffn(block_e, n_valid, x_rows, w_gate_up[0],
                         b_gate_up[0].reshape(N_EXPERTS, 1, 2 * D_FF), w_down[0],
                         b_down[0].reshape(N_EXPERTS, 1, D_MODEL))
    y_assign = jnp.take(y_rows, dest, axis=0, mode='clip').reshape(t, TOP_K * D_MODEL)
    out = _combine(x1, y_assign, top_g, norm_f_g.reshape(1, d))
    return out.reshape(bsz, seq, d)
```

```python
import functools

import jax
import jax.numpy as jnp
from jax import lax
from jax.experimental import pallas as pl
from jax.experimental.pallas import tpu as pltpu
from jax.experimental.pallas import tpu_sc as plsc

D_MODEL = 1024
D_CONV = 512
CONV_WIDTH = 3
D_SSM = 512
SSM_GROUP = 16
N_SSM_GROUPS = 32
SSM_STATE = 64
N_EXPERTS = 32
TOP_K = 4
D_FF = 1024
SWIGLU_LIMIT = 7.0
SWIGLU_ALPHA = 1.702
MOE_BLOCK = 512
RMS_EPS = 1e-6

LANES = 128
MXU_DIM = 256
CHUNK = 16
SLAB_GROUPS = LANES // SSM_GROUP
N_SLABS = N_SSM_GROUPS // SLAB_GROUPS
SLAB_STATE = SLAB_GROUPS * SSM_STATE
FLAT = CHUNK * LANES
SSM_TIME_TILE = 256
TOKEN_TILE = 512
SC_ROWS = 64
VMEM_LIMIT_BYTES = 56 * 1024 * 1024

_BF16 = jnp.bfloat16
_F32 = jnp.float32


def _rmsnorm(xf, g):
    return xf * lax.rsqrt(jnp.mean(xf * xf, axis=-1, keepdims=True) + RMS_EPS) * g


def _sigmoid(v):
    return 1.0 / (1.0 + jnp.exp(-v))


def _in_proj_kernel(tiles_per_seq, x_ref, g_ref, w_ref, cw_ref, bz_ref, u_ref, hbuf):
    tm = x_ref.shape[0]
    i = pl.program_id(0)

    @pl.when(i % tiles_per_seq == 0)
    def _():
        hbuf[0:8, :] = jnp.zeros((8, D_CONV), _F32)

    xn = _rmsnorm(x_ref[...], g_ref[...]).astype(_BF16)
    proj = jnp.dot(xn, w_ref[...], preferred_element_type=_F32)
    b_gate = proj[:, 0:D_CONV]
    c_gate = proj[:, D_CONV:2 * D_CONV]
    v = proj[:, 2 * D_CONV:3 * D_CONV]
    u_ref[...] = proj[:, 3 * D_CONV:]

    hbuf[8:8 + tm, :] = c_gate * v
    cw = cw_ref[...]
    z = (cw[0:1, :] * hbuf[6:6 + tm, :] + cw[1:2, :] * hbuf[7:7 + tm, :]
         + cw[2:3, :] * hbuf[8:8 + tm, :])
    bz_ref[...] = (b_gate * z).astype(_BF16)
    hbuf[0:8, :] = hbuf[tm:tm + 8, :]


def _in_proj(x2, g, w_bcvu, conv_w, seq):
    t = x2.shape[0]
    tm = TOKEN_TILE
    return pl.pallas_call(
        functools.partial(_in_proj_kernel, seq // tm),
        out_shape=(jax.ShapeDtypeStruct((t, D_CONV), _BF16),
                   jax.ShapeDtypeStruct((t, D_SSM), _F32)),
        grid=(t // tm,),
        in_specs=[pl.BlockSpec((tm, D_MODEL), lambda i: (i, 0)),
                  pl.BlockSpec((1, D_MODEL), lambda i: (0, 0)),
                  pl.BlockSpec((D_MODEL, 3 * D_CONV + D_SSM), lambda i: (0, 0)),
                  pl.BlockSpec((CONV_WIDTH, D_CONV), lambda i: (0, 0))],
        out_specs=(pl.BlockSpec((tm, D_CONV), lambda i: (i, 0)),
                   pl.BlockSpec((tm, D_SSM), lambda i: (i, 0))),
        scratch_shapes=[pltpu.VMEM((tm + 8, D_CONV), _F32)],
        compiler_params=pltpu.CompilerParams(
            dimension_semantics=("arbitrary",), vmem_limit_bytes=VMEM_LIMIT_BYTES),
        name="in_proj",
    )(x2, g, w_bcvu, conv_w)


def _ssm_prep_kernel(lr_ref, lc_ref, bm_ref, cm_ref, d_ref, toep_ref, bst_ref, cst_ref, a_ref):
    def discretise(lre, lim, log_dt):
        dt = jnp.exp(log_dt)
        mag = jnp.exp(lre * dt)
        return mag * jnp.cos(lim * dt), mag * jnp.sin(lim * dt)

    def powers(are, aim):
        pre, pim = [jnp.ones_like(are)], [jnp.zeros_like(are)]
        for _ in range(CHUNK):
            pre, pim = (pre + [pre[-1] * are - pim[-1] * aim],
                        pim + [pre[-1] * aim + pim[-1] * are])
        return pre, pim

    lr = lr_ref[0]
    lre, lim = lr[0:1, :], lr[1:2, :]
    are, aim = discretise(lre, lim, lr[2:3, :])
    pre, pim = powers(are, aim)
    den = lre * lre + lim * lim
    q_re = ((are - 1.0) * lre + aim * lim) / den
    q_im = (aim * lre - (are - 1.0) * lim) / den
    bb_re = q_re * bm_ref[0, 0] - q_im * bm_ref[0, 1]
    bb_im = q_re * bm_ref[0, 1] + q_im * bm_ref[0, 0]
    cm_re, cm_im = cm_ref[0, 0], cm_ref[0, 1]
    hi = lax.Precision.HIGHEST
    kblk = []
    for k in range(CHUNK):
        ab_re = bb_re * pre[k] - bb_im * pim[k]
        ab_im = bb_re * pim[k] + bb_im * pre[k]
        rows = slice((CHUNK - 1 - k) * LANES, (CHUNK - k) * LANES)
        bst_ref[0, rows, :SLAB_STATE] = ab_re.astype(_BF16)
        bst_ref[0, rows, SLAB_STATE:] = ab_im.astype(_BF16)
        kblk.append(jnp.dot(ab_re, cm_re, preferred_element_type=_F32, precision=hi)
                    - jnp.dot(ab_im, cm_im, preferred_element_type=_F32, precision=hi))
    r = lax.broadcasted_iota(jnp.int32, (LANES, LANES), 0)
    c = lax.broadcasted_iota(jnp.int32, (LANES, LANES), 1)
    kblk[0] = kblk[0] + jnp.where(r == c, jnp.broadcast_to(d_ref[0], (LANES, LANES)), 0.0)
    kblk = [kb.astype(_BF16) for kb in kblk]
    zeros = jnp.zeros((LANES, LANES), _BF16)
    for sp in range(CHUNK):
        for s in range(CHUNK):
            toep_ref[0, sp * LANES:(sp + 1) * LANES, s * LANES:(s + 1) * LANES] = (
                kblk[s - sp] if s >= sp else zeros)

    lc = lc_ref[0]
    cre, cim = discretise(lc[:, 0:1], lc[:, 1:2], lc[:, 2:3])
    qre, qim = powers(cre, cim)
    for s in range(CHUNK):
        cols = slice(s * LANES, (s + 1) * LANES)
        cst_ref[0, :SLAB_STATE, cols] = (cm_re * qre[s + 1] - cm_im * qim[s + 1]).astype(_BF16)
        cst_ref[0, SLAB_STATE:, cols] = (-(cm_re * qim[s + 1] + cm_im * qre[s + 1])).astype(_BF16)
    a_ref[0, 0:1, :] = pre[CHUNK]
    a_ref[0, 1:2, :] = pim[CHUNK]


def _ssm_tables(lam_re, lam_im, log_dt, b_re, b_im, c_re, c_im, d_skip):
    sg = (N_SLABS, SLAB_GROUPS)
    eye = jnp.eye(SLAB_GROUPS, dtype=_F32)
    lam = jnp.stack([lam_re, lam_im, jnp.broadcast_to(log_dt[:, None], lam_re.shape)], axis=0)
    lam_row = lam.reshape(3, N_SLABS, SLAB_STATE).transpose(1, 0, 2)
    lam_col = lam_row.transpose(0, 2, 1)

    def b_blockdiag(b):
        bt = b.reshape(*sg, SSM_STATE, SSM_GROUP).transpose(0, 1, 3, 2)
        return (bt[:, :, :, None, :] * eye[None, :, None, :, None]).reshape(N_SLABS, LANES, SLAB_STATE)

    def c_blockdiag(c):
        ct = c.reshape(*sg, SSM_GROUP, SSM_STATE).transpose(0, 1, 3, 2)
        return (ct[:, :, :, None, :] * eye[None, :, None, :, None]).reshape(N_SLABS, SLAB_STATE, LANES)

    bm = jnp.stack([b_blockdiag(b_re), b_blockdiag(b_im)], axis=1)
    cm = jnp.stack([c_blockdiag(c_re), c_blockdiag(c_im)], axis=1)
    d = d_skip.reshape(N_SLABS, 1, LANES)
    slab3 = lambda sl: (sl, 0, 0)
    slab4 = lambda sl: (sl, 0, 0, 0)
    return pl.pallas_call(
        _ssm_prep_kernel,
        out_shape=(jax.ShapeDtypeStruct((N_SLABS, FLAT, FLAT), _BF16),
                   jax.ShapeDtypeStruct((N_SLABS, FLAT, 2 * SLAB_STATE), _BF16),
                   jax.ShapeDtypeStruct((N_SLABS, 2 * SLAB_STATE, FLAT), _BF16),
                   jax.ShapeDtypeStruct((N_SLABS, 2, SLAB_STATE), _F32)),
        grid=(N_SLABS,),
        in_specs=[pl.BlockSpec((1, 3, SLAB_STATE), slab3),
                  pl.BlockSpec((1, SLAB_STATE, 3), slab3),
                  pl.BlockSpec((1, 2, LANES, SLAB_STATE), slab4),
                  pl.BlockSpec((1, 2, SLAB_STATE, LANES), slab4),
                  pl.BlockSpec((1, 1, LANES), slab3)],
        out_specs=(pl.BlockSpec((1, FLAT, FLAT), slab3),
                   pl.BlockSpec((1, FLAT, 2 * SLAB_STATE), slab3),
                   pl.BlockSpec((1, 2 * SLAB_STATE, FLAT), slab3),
                   pl.BlockSpec((1, 2, SLAB_STATE), slab3)),
        compiler_params=pltpu.CompilerParams(
            dimension_semantics=("arbitrary",), vmem_limit_bytes=VMEM_LIMIT_BYTES),
        name="ssm_prep",
    )(lam_row, lam_col, bm, cm, d)


def _ssm_kernel(u_ref, toep_ref, bst_ref, cst_ref, a_ref, y_ref, uflat, s_scr, xc_scr, carry):
    nb, tt, _ = u_ref.shape
    nch = tt // CHUNK
    n = nb * nch

    @pl.when(pl.program_id(1) == 0)
    def _():
        carry[...] = jnp.zeros_like(carry)

    for s in range(CHUNK):
        part = u_ref[:, pl.ds(s, nch, stride=CHUNK), :]
        uflat[:, s * LANES:(s + 1) * LANES] = part.reshape(n, LANES).astype(_BF16)

    nblk = SLAB_STATE // LANES
    loc_all = jnp.dot(uflat[...], bst_ref[0], preferred_element_type=_F32)
    for kb in range(2 * nblk):
        s_scr[kb] = loc_all[:, kb * LANES:(kb + 1) * LANES]

    a = a_ref[0]
    are = [jnp.broadcast_to(a[0:1, kb * LANES:(kb + 1) * LANES], (nb, LANES)) for kb in range(nblk)]
    aim = [jnp.broadcast_to(a[1:2, kb * LANES:(kb + 1) * LANES], (nb, LANES)) for kb in range(nblk)]
    xr = [carry[kb] for kb in range(nblk)]
    xi = [carry[nblk + kb] for kb in range(nblk)]
    for j in range(nch):
        rows = pl.ds(j, nb, stride=nch)
        for kb in range(nblk):
            xc_scr[kb, rows, :] = xr[kb]
            xc_scr[nblk + kb, rows, :] = xi[kb]
            nr = are[kb] * xr[kb] - aim[kb] * xi[kb] + s_scr[kb, rows, :]
            ni = are[kb] * xi[kb] + aim[kb] * xr[kb] + s_scr[nblk + kb, rows, :]
            xr[kb], xi[kb] = nr, ni
    for kb in range(nblk):
        carry[kb] = xr[kb]
        carry[nblk + kb] = xi[kb]

    xc = jnp.concatenate([xc_scr[kb] for kb in range(2 * nblk)], axis=1).astype(_BF16)
    for cb in range(FLAT // MXU_DIM):
        kk = (cb + 1) * MXU_DIM
        cols = slice(cb * MXU_DIM, kk)
        y = (jnp.dot(uflat[:, :kk], toep_ref[0, :kk, cols], preferred_element_type=_F32)
             + jnp.dot(xc, cst_ref[0, :, cols], preferred_element_type=_F32))
        y = jax.nn.gelu(y)
        for h in range(MXU_DIM // LANES):
            s = cb * (MXU_DIM // LANES) + h
            y_ref[:, pl.ds(s, nch, stride=CHUNK), :] = (
                y[:, h * LANES:(h + 1) * LANES].reshape(nb, nch, LANES))


def _ssm(u3, tables):
    toep, bst, cst, a_chunk = tables
    nb, seq, _ = u3.shape
    tt = SSM_TIME_TILE
    n = nb * (tt // CHUNK)
    return pl.pallas_call(
        _ssm_kernel,
        out_shape=jax.ShapeDtypeStruct(u3.shape, _F32),
        grid=(N_SLABS, seq // tt),
        in_specs=[pl.BlockSpec((nb, tt, LANES), lambda sl, ti: (0, ti, sl)),
                  pl.BlockSpec((1, FLAT, FLAT), lambda sl, ti: (sl, 0, 0)),
                  pl.BlockSpec((1, FLAT, 2 * SLAB_STATE), lambda sl, ti: (sl, 0, 0)),
                  pl.BlockSpec((1, 2 * SLAB_STATE, FLAT), lambda sl, ti: (sl, 0, 0)),
                  pl.BlockSpec((1, 2, SLAB_STATE), lambda sl, ti: (sl, 0, 0))],
        out_specs=pl.BlockSpec((nb, tt, LANES), lambda sl, ti: (0, ti, sl)),
        scratch_shapes=[pltpu.VMEM((n, FLAT), _BF16),
                        pltpu.VMEM((2 * SLAB_STATE // LANES, n, LANES), _F32),
                        pltpu.VMEM((2 * SLAB_STATE // LANES, n, LANES), _F32),
                        pltpu.VMEM((2 * SLAB_STATE // LANES, nb, LANES), _F32)],
        compiler_params=pltpu.CompilerParams(
            dimension_semantics=("arbitrary", "arbitrary"), vmem_limit_bytes=VMEM_LIMIT_BYTES),
        name="ssm",
    )(u3, toep, bst, cst, a_chunk)


def _mix_route_kernel(x_ref, bz_ref, yg_ref, gm_ref, wg_ref, wco_ref, wglu_ref, wout_ref,
                      gf_ref, wr_ref, br_ref,
                      x1_ref, h_ref, ti_ref, tg_ref, rk_ref, cnt_ref, base):
    tm = x_ref.shape[0]
    i = pl.program_id(0)

    @pl.when(i == 0)
    def _():
        base[...] = jnp.zeros_like(base)

    x = x_ref[...]
    xn = _rmsnorm(x, gm_ref[...]).astype(_BF16)
    gates = jnp.dot(xn, wg_ref[...], preferred_element_type=_F32)
    y_a = jnp.dot(bz_ref[...], wco_ref[...], preferred_element_type=_F32)
    glu = jnp.dot(yg_ref[...].astype(_BF16), wglu_ref[...], preferred_element_type=_F32)
    y_b = glu[:, :D_MODEL] * _sigmoid(glu[:, D_MODEL:])
    merged = _sigmoid(gates[:, :D_MODEL]) * y_a + _sigmoid(gates[:, D_MODEL:]) * y_b
    x1 = x + jnp.dot(merged.astype(_BF16), wout_ref[...], preferred_element_type=_F32)
    x1_ref[...] = x1
    h = _rmsnorm(x1, gf_ref[...])
    bits = pltpu.bitcast(h.astype(_BF16).astype(_F32), jnp.uint32)
    half = D_MODEL // 2
    h_ref[...] = (bits[:, :half] >> 16) | (bits[:, half:] & jnp.uint32(0xFFFF0000))

    logits = lax.dot_general(wr_ref[...], h, (((1,), (1,)), ((), ())),
                             preferred_element_type=_F32,
                             precision=lax.Precision.HIGHEST) + br_ref[...]
    erow = lax.broadcasted_iota(jnp.int32, (N_EXPERTS, tm), 0).astype(_F32)
    neg_inf = jnp.float32(-jnp.inf)
    work = logits
    vals, idxs = [], []
    for _ in range(TOP_K):
        m = jnp.max(work, axis=0, keepdims=True)
        idx = jnp.min(jnp.where(work == m, erow, float(N_EXPERTS)), axis=0, keepdims=True)
        vals.append(m)
        idxs.append(idx)
        work = jnp.where(erow == idx, neg_inf, work)
    exps = [jnp.exp(v - vals[0]) for v in vals]
    denom = exps[0] + exps[1] + exps[2] + exps[3]

    sel = jnp.zeros((N_EXPERTS, tm), _F32)
    for idx in idxs:
        sel = sel + (erow == idx).astype(_F32)
    row = lax.broadcasted_iota(jnp.int32, (tm, tm), 0)
    col = lax.broadcasted_iota(jnp.int32, (tm, tm), 1)
    earlier = (row < col).astype(_BF16)
    before = jnp.dot(sel.astype(_BF16), earlier, preferred_element_type=_F32) + base[...]
    tg_ref[...] = jnp.zeros_like(tg_ref)
    for k in range(TOP_K):
        ti_ref[k:k + 1, :] = idxs[k].astype(jnp.int32)
        tg_ref[k:k + 1, :] = exps[k] / denom
        rk = jnp.sum(jnp.where(erow == idxs[k], before, 0.0), axis=0, keepdims=True)
        rk_ref[k:k + 1, :] = rk.astype(jnp.int32)
    base[...] = base[...] + jnp.sum(sel, axis=1, keepdims=True)
    cnt_ref[...] = base[...].astype(jnp.int32)


def _mix_route(x2, bz, yg, g_mix, w_gates, w_conv_out, w_glu, w_out, g_ffn, w_router, b_router):
    t = x2.shape[0]
    tm = TOKEN_TILE
    tok = lambda i: (i, 0)
    tok_lanes = lambda i: (0, i)
    fixed = lambda i: (0, 0)
    return pl.pallas_call(
        _mix_route_kernel,
        out_shape=(jax.ShapeDtypeStruct((t, D_MODEL), _F32),
                   jax.ShapeDtypeStruct((t, D_MODEL // 2), jnp.uint32),
                   jax.ShapeDtypeStruct((TOP_K, t), jnp.int32),
                   jax.ShapeDtypeStruct((2 * TOP_K, t), _F32),
                   jax.ShapeDtypeStruct((TOP_K, t), jnp.int32),
                   jax.ShapeDtypeStruct((N_EXPERTS, 1), jnp.int32)),
        grid=(t // tm,),
        in_specs=[pl.BlockSpec((tm, D_MODEL), tok),
                  pl.BlockSpec((tm, D_CONV), tok),
                  pl.BlockSpec((tm, D_SSM), tok),
                  pl.BlockSpec((1, D_MODEL), fixed),
                  pl.BlockSpec((D_MODEL, 2 * D_MODEL), fixed),
                  pl.BlockSpec((D_CONV, D_MODEL), fixed),
                  pl.BlockSpec((D_SSM, 2 * D_MODEL), fixed),
                  pl.BlockSpec((D_MODEL, D_MODEL), fixed),
                  pl.BlockSpec((1, D_MODEL), fixed),
                  pl.BlockSpec((N_EXPERTS, D_MODEL), fixed),
                  pl.BlockSpec((N_EXPERTS, 1), fixed)],
        out_specs=(pl.BlockSpec((tm, D_MODEL), tok),
                   pl.BlockSpec((tm, D_MODEL // 2), tok),
                   pl.BlockSpec((TOP_K, tm), tok_lanes),
                   pl.BlockSpec((2 * TOP_K, tm), tok_lanes),
                   pl.BlockSpec((TOP_K, tm), tok_lanes),
                   pl.BlockSpec((N_EXPERTS, 1), fixed)),
        scratch_shapes=[pltpu.VMEM((N_EXPERTS, 1), _F32)],
        compiler_params=pltpu.CompilerParams(
            dimension_semantics=("arbitrary",), vmem_limit_bytes=VMEM_LIMIT_BYTES),
        name="mix_route",
    )(x2, bz, yg, g_mix, w_gates, w_conv_out, w_glu, w_out, g_ffn, w_router, b_router)


def _expert_ffn_kernel(be_ref, nr_ref, x_ref, wgu_ref, bgu_ref, wd_ref, bd_ref, y_ref, wgu_b, wd_b):
    b = pl.program_id(0)

    @pl.when((b == 0) | (be_ref[b] != be_ref[jnp.maximum(b - 1, 0)]))
    def _():
        wgu_b[...] = wgu_ref[0].astype(_BF16)
        wd_b[...] = wd_ref[0].astype(_BF16)

    @pl.when(nr_ref[b] > 0)
    def _():
        live = lax.broadcasted_iota(jnp.int32, x_ref.shape, 0) < nr_ref[b]
        w = jnp.where(live, x_ref[...], jnp.uint32(0))
        x = jnp.concatenate([pltpu.bitcast(w << 16, _F32),
                             pltpu.bitcast(w & jnp.uint32(0xFFFF0000), _F32)], axis=1).astype(_BF16)
        hgu = jnp.dot(x, wgu_b[...], preferred_element_type=_F32) + bgu_ref[0]
        g = jnp.minimum(hgu[:, :D_FF], SWIGLU_LIMIT)
        up = jnp.clip(hgu[:, D_FF:], -SWIGLU_LIMIT, SWIGLU_LIMIT)
        act = (up + 1.0) * (g * _sigmoid(SWIGLU_ALPHA * g))
        y_ref[...] = jnp.dot(act.astype(_BF16), wd_b[...], preferred_element_type=_F32) + bd_ref[0]

    @pl.when(nr_ref[b] == 0)
    def _():
        y_ref[...] = jnp.zeros_like(y_ref)


def _expert_ffn(block_e, block_rows, x_rows, w_gate_up, b_gate_up, w_down, b_down):
    n_rows = x_rows.shape[0]
    n_blocks = n_rows // MOE_BLOCK

    def wmap(b, be, nr):
        return (be[b], 0, 0)

    grid_spec = pltpu.PrefetchScalarGridSpec(
        num_scalar_prefetch=2,
        grid=(n_blocks,),
        in_specs=[pl.BlockSpec((MOE_BLOCK, D_MODEL // 2), lambda b, be, nr: (b, 0)),
                  pl.BlockSpec((1, D_MODEL, 2 * D_FF), wmap),
                  pl.BlockSpec((1, 1, 2 * D_FF), wmap),
                  pl.BlockSpec((1, D_FF, D_MODEL), wmap),
                  pl.BlockSpec((1, 1, D_MODEL), wmap)],
        out_specs=pl.BlockSpec((MOE_BLOCK, D_MODEL), lambda b, be, nr: (b, 0)),
        scratch_shapes=[pltpu.VMEM((D_MODEL, 2 * D_FF), _BF16),
                        pltpu.VMEM((D_FF, D_MODEL), _BF16)],
    )
    return pl.pallas_call(
        _expert_ffn_kernel,
        out_shape=jax.ShapeDtypeStruct((n_rows, D_MODEL), _F32),
        grid_spec=grid_spec,
        compiler_params=pltpu.CompilerParams(
            dimension_semantics=("arbitrary",), vmem_limit_bytes=VMEM_LIMIT_BYTES),
        name="expert_ffn",
    )(block_e, block_rows, x_rows, w_gate_up, b_gate_up, w_down, b_down)


def _sc_workers():
    info = plsc.get_sparse_core_info()
    return info.num_cores, info.num_cores * info.num_subcores


def _dispatch(h_packed, dest_flat, n_rows):
    t, width = h_packed.shape
    n_cores, n_workers = _sc_workers()
    per_w = t // n_workers
    n_chunks = per_w // SC_ROWS

    @functools.partial(
        pl.kernel, mesh=plsc.VectorSubcoreMesh(core_axis_name="c", subcore_axis_name="s"),
        out_type=jax.ShapeDtypeStruct((n_rows, width), h_packed.dtype),
        scratch_types=[pltpu.VMEM((TOP_K, SC_ROWS), jnp.int32),
                       pltpu.VMEM((SC_ROWS, width), h_packed.dtype),
                       pltpu.SemaphoreType.DMA])
    def scatter_rows(h_hbm, dest_hbm, out_hbm, idx_v, rows_v, sem):
        wid = lax.axis_index("s") * n_cores + lax.axis_index("c")

        @pl.loop(0, n_chunks)
        def _(ci):
            off = wid * per_w + ci * SC_ROWS
            pltpu.sync_copy(h_hbm.at[pl.ds(off, SC_ROWS)], rows_v)
            for k in range(TOP_K):
                pltpu.sync_copy(dest_hbm.at[pl.ds(k * t + off, SC_ROWS)], idx_v.at[k])
            copies = [pltpu.async_copy(rows_v, out_hbm.at[idx_v.at[k]], sem) for k in range(TOP_K)]
            for cp in copies:
                cp.wait()

    return scatter_rows(h_packed, dest_flat)


def _collect(y_rows, dest_flat):
    n_idx = dest_flat.shape[0]
    width = y_rows.shape[1]
    n_cores, n_workers = _sc_workers()
    per_w = n_idx // n_workers
    rows = SC_ROWS // 2
    n_chunks = per_w // rows

    @functools.partial(
        pl.kernel, mesh=plsc.VectorSubcoreMesh(core_axis_name="c", subcore_axis_name="s"),
        out_type=jax.ShapeDtypeStruct((n_idx, width), y_rows.dtype),
        scratch_types=[pltpu.VMEM((rows,), jnp.int32),
                       pltpu.VMEM((rows, width), y_rows.dtype),
                       pltpu.SemaphoreType.DMA])
    def gather_rows(y_hbm, dest_hbm, out_hbm, idx_v, rows_v, sem):
        wid = lax.axis_index("s") * n_cores + lax.axis_index("c")

        @pl.loop(0, n_chunks)
        def _(ci):
            off = wid * per_w + ci * rows
            pltpu.sync_copy(dest_hbm.at[pl.ds(off, rows)], idx_v)
            pltpu.async_copy(y_hbm.at[idx_v], rows_v, sem).wait()
            pltpu.sync_copy(rows_v, out_hbm.at[pl.ds(off, rows)])

    return gather_rows(y_rows, dest_flat)


def _combine_kernel(x1_ref, ya_ref, tg_ref, g_ref, o_ref):
    acc = x1_ref[...]
    tg = jnp.transpose(tg_ref[...])
    for k in range(TOP_K):
        acc = acc + tg[:, k:k + 1] * ya_ref[k]
    o_ref[...] = _rmsnorm(acc, g_ref[...])


def _combine(x1, y_assign, top_g, g_final):
    t = x1.shape[0]
    tm = TOKEN_TILE
    return pl.pallas_call(
        _combine_kernel,
        out_shape=jax.ShapeDtypeStruct((t, D_MODEL), _F32),
        grid=(t // tm,),
        in_specs=[pl.BlockSpec((tm, D_MODEL), lambda i: (i, 0)),
                  pl.BlockSpec((TOP_K, tm, D_MODEL), lambda i: (0, i, 0)),
                  pl.BlockSpec((2 * TOP_K, tm), lambda i: (0, i)),
                  pl.BlockSpec((1, D_MODEL), lambda i: (0, 0))],
        out_specs=pl.BlockSpec((tm, D_MODEL), lambda i: (i, 0)),
        compiler_params=pltpu.CompilerParams(
            dimension_semantics=("arbitrary",), vmem_limit_bytes=VMEM_LIMIT_BYTES),
        name="combine",
    )(x1, y_assign, top_g, g_final)


def kernel(x, norm_mix_g, w_in, conv_w, w_conv_out, ssm_lam_re, ssm_lam_im, ssm_log_dt, ssm_b_re, ssm_b_im, ssm_c_re, ssm_c_im, ssm_d, w_glu, w_out, norm_ffn_g, w_router, b_router, w_gate_up, b_gate_up, w_down, b_down, norm_f_g):
    bsz, seq, d = x.shape
    t = bsz * seq
    x2 = x.reshape(t, d)
    assert seq % TOKEN_TILE == 0 and seq % SSM_TIME_TILE == 0 and w_in.shape[0] == 1

    w_in_b = w_in[0].astype(_BF16)
    n_bcvu = 3 * D_CONV + D_SSM
    g_mix = norm_mix_g[0].reshape(1, d)

    bz, u = _in_proj(x2, g_mix, w_in_b[:, :n_bcvu], conv_w[0], seq)

    tables = _ssm_tables(ssm_lam_re[0], ssm_lam_im[0], ssm_log_dt[0], ssm_b_re[0], ssm_b_im[0],
                         ssm_c_re[0], ssm_c_im[0], ssm_d[0])
    yg = _ssm(u.reshape(bsz, seq, D_SSM), tables).reshape(t, D_SSM)

    x1, h_packed, top_i, top_g, rank, counts = _mix_route(
        x2, bz, yg, g_mix, w_in_b[:, n_bcvu:], w_conv_out[0].astype(_BF16),
        w_glu[0].astype(_BF16), w_out[0].astype(_BF16), norm_ffn_g[0].reshape(1, d),
        w_router[0].T, b_router[0].reshape(N_EXPERTS, 1))

    counts = counts[:, 0]
    padded = ((counts + MOE_BLOCK - 1) // MOE_BLOCK) * MOE_BLOCK
    pad_end = jnp.cumsum(padded)
    pad_start = pad_end - padded
    n_rows = t * TOP_K + N_EXPERTS * MOE_BLOCK
    n_blocks = n_rows // MOE_BLOCK
    dest = (jnp.take(pad_start, top_i, mode='clip') + rank).reshape(TOP_K * t)
    block_start = jnp.arange(n_blocks, dtype=jnp.int32) * MOE_BLOCK
    block_e = jnp.minimum(jnp.sum(pad_end[None, :] <= block_start[:, None], axis=1),
                          N_EXPERTS - 1).astype(jnp.int32)
    block_rows = jnp.clip(pad_start[block_e] + counts[block_e] - block_start, 0, MOE_BLOCK)
    block_rows = block_rows.astype(jnp.int32)

    x_rows = _dispatch(h_packed, dest, n_rows)
    y_rows = _expert_ffn(block_e, block_rows, x_rows, w_gate_up[0],
                         b_gate_up[0].reshape(N_EXPERTS, 1, 2 * D_FF), w_down[0],
                         b_down[0].reshape(N_EXPERTS, 1, D_MODEL))
    y_assign = _collect(y_rows, dest).reshape(TOP_K, t, D_MODEL)
    out = _combine(x1, y_assign, top_g, norm_f_g.reshape(1, d))
    return out.reshape(bsz, seq, d)
```

```python
import functools

import jax
import jax.numpy as jnp
from jax import lax
from jax.experimental import pallas as pl
from jax.experimental.pallas import tpu as pltpu
from jax.experimental.pallas import tpu_sc as plsc

D_MODEL = 1024
D_CONV = 512
CONV_WIDTH = 3
D_SSM = 512
SSM_GROUP = 16
N_SSM_GROUPS = 32
SSM_STATE = 64
N_EXPERTS = 32
TOP_K = 4
D_FF = 1024
SWIGLU_LIMIT = 7.0
SWIGLU_ALPHA = 1.702
MOE_BLOCK = 512
RMS_EPS = 1e-6

LANES = 128
MXU_DIM = 256
CHUNK = 16
SLAB_GROUPS = LANES // SSM_GROUP
N_SLABS = N_SSM_GROUPS // SLAB_GROUPS
SLAB_STATE = SLAB_GROUPS * SSM_STATE
FLAT = CHUNK * LANES
SSM_TIME_TILE = 256
TOKEN_TILE = 512
SC_ROWS = 64
VMEM_LIMIT_BYTES = 56 * 1024 * 1024

_BF16 = jnp.bfloat16
_F32 = jnp.float32


def _rmsnorm(xf, g):
    return xf * lax.rsqrt(jnp.mean(xf * xf, axis=-1, keepdims=True) + RMS_EPS) * g


def _sigmoid(v):
    return 1.0 / (1.0 + jnp.exp(-v))


def _pack_bf16_halves(v):
    n = v.shape[1] // 2
    bits = pltpu.bitcast(v.astype(_BF16).astype(_F32), jnp.uint32)
    return (bits[:, :n] >> 16) | (bits[:, n:] & jnp.uint32(0xFFFF0000))


def _unpack_bf16_halves(w):
    return jnp.concatenate([pltpu.bitcast(w << 16, _F32),
                            pltpu.bitcast(w & jnp.uint32(0xFFFF0000), _F32)], axis=1)


def _in_proj_kernel(tiles_per_seq, x_ref, g_ref, w_ref, cw_ref, bz_ref, u_ref, hbuf):
    tm = x_ref.shape[0]
    i = pl.program_id(0)

    @pl.when(i % tiles_per_seq == 0)
    def _():
        hbuf[0:8, :] = jnp.zeros((8, D_CONV), _F32)

    xn = _rmsnorm(x_ref[...], g_ref[...]).astype(_BF16)
    proj = jnp.dot(xn, w_ref[...], preferred_element_type=_F32)
    b_gate = proj[:, 0:D_CONV]
    c_gate = proj[:, D_CONV:2 * D_CONV]
    v = proj[:, 2 * D_CONV:3 * D_CONV]
    u_ref[...] = proj[:, 3 * D_CONV:]

    hbuf[8:8 + tm, :] = c_gate * v
    cw = cw_ref[...]
    z = (cw[0:1, :] * hbuf[6:6 + tm, :] + cw[1:2, :] * hbuf[7:7 + tm, :]
         + cw[2:3, :] * hbuf[8:8 + tm, :])
    bz_ref[...] = (b_gate * z).astype(_BF16)
    hbuf[0:8, :] = hbuf[tm:tm + 8, :]


def _in_proj(x2, g, w_bcvu, conv_w, seq):
    t = x2.shape[0]
    tm = TOKEN_TILE
    return pl.pallas_call(
        functools.partial(_in_proj_kernel, seq // tm),
        out_shape=(jax.ShapeDtypeStruct((t, D_CONV), _BF16),
                   jax.ShapeDtypeStruct((t, D_SSM), _F32)),
        grid=(t // tm,),
        in_specs=[pl.BlockSpec((tm, D_MODEL), lambda i: (i, 0)),
                  pl.BlockSpec((1, D_MODEL), lambda i: (0, 0)),
                  pl.BlockSpec((D_MODEL, 3 * D_CONV + D_SSM), lambda i: (0, 0)),
                  pl.BlockSpec((CONV_WIDTH, D_CONV), lambda i: (0, 0))],
        out_specs=(pl.BlockSpec((tm, D_CONV), lambda i: (i, 0)),
                   pl.BlockSpec((tm, D_SSM), lambda i: (i, 0))),
        scratch_shapes=[pltpu.VMEM((tm + 8, D_CONV), _F32)],
        compiler_params=pltpu.CompilerParams(
            dimension_semantics=("arbitrary",), vmem_limit_bytes=VMEM_LIMIT_BYTES),
        name="in_proj",
    )(x2, g, w_bcvu, conv_w)


def _ssm_prep_kernel(lr_ref, lc_ref, bm_ref, cm_ref, d_ref, toep_ref, bst_ref, cst_ref, a_ref):
    def discretise(lre, lim, log_dt):
        dt = jnp.exp(log_dt)
        mag = jnp.exp(lre * dt)
        return mag * jnp.cos(lim * dt), mag * jnp.sin(lim * dt)

    def powers(are, aim):
        pre, pim = [jnp.ones_like(are)], [jnp.zeros_like(are)]
        for _ in range(CHUNK):
            pre, pim = (pre + [pre[-1] * are - pim[-1] * aim],
                        pim + [pre[-1] * aim + pim[-1] * are])
        return pre, pim

    lr = lr_ref[0]
    lre, lim = lr[0:1, :], lr[1:2, :]
    are, aim = discretise(lre, lim, lr[2:3, :])
    pre, pim = powers(are, aim)
    den = lre * lre + lim * lim
    q_re = ((are - 1.0) * lre + aim * lim) / den
    q_im = (aim * lre - (are - 1.0) * lim) / den
    bb_re = q_re * bm_ref[0, 0] - q_im * bm_ref[0, 1]
    bb_im = q_re * bm_ref[0, 1] + q_im * bm_ref[0, 0]
    cm_re, cm_im = cm_ref[0, 0], cm_ref[0, 1]
    hi = lax.Precision.HIGHEST
    kblk = []
    for k in range(CHUNK):
        ab_re = bb_re * pre[k] - bb_im * pim[k]
        ab_im = bb_re * pim[k] + bb_im * pre[k]
        rows = slice((CHUNK - 1 - k) * LANES, (CHUNK - k) * LANES)
        bst_ref[0, rows, :SLAB_STATE] = ab_re.astype(_BF16)
        bst_ref[0, rows, SLAB_STATE:] = ab_im.astype(_BF16)
        kblk.append(jnp.dot(ab_re, cm_re, preferred_element_type=_F32, precision=hi)
                    - jnp.dot(ab_im, cm_im, preferred_element_type=_F32, precision=hi))
    r = lax.broadcasted_iota(jnp.int32, (LANES, LANES), 0)
    c = lax.broadcasted_iota(jnp.int32, (LANES, LANES), 1)
    kblk[0] = kblk[0] + jnp.where(r == c, jnp.broadcast_to(d_ref[0], (LANES, LANES)), 0.0)
    kblk = [kb.astype(_BF16) for kb in kblk]
    zeros = jnp.zeros((LANES, LANES), _BF16)
    for sp in range(CHUNK):
        for s in range(CHUNK):
            toep_ref[0, sp * LANES:(sp + 1) * LANES, s * LANES:(s + 1) * LANES] = (
                kblk[s - sp] if s >= sp else zeros)

    lc = lc_ref[0]
    cre, cim = discretise(lc[:, 0:1], lc[:, 1:2], lc[:, 2:3])
    qre, qim = powers(cre, cim)
    for s in range(CHUNK):
        cols = slice(s * LANES, (s + 1) * LANES)
        cst_ref[0, :SLAB_STATE, cols] = (cm_re * qre[s + 1] - cm_im * qim[s + 1]).astype(_BF16)
        cst_ref[0, SLAB_STATE:, cols] = (-(cm_re * qim[s + 1] + cm_im * qre[s + 1])).astype(_BF16)
    a_ref[0, 0:1, :] = pre[CHUNK]
    a_ref[0, 1:2, :] = pim[CHUNK]


def _ssm_tables(lam_re, lam_im, log_dt, b_re, b_im, c_re, c_im, d_skip):
    sg = (N_SLABS, SLAB_GROUPS)
    eye = jnp.eye(SLAB_GROUPS, dtype=_F32)
    lam = jnp.stack([lam_re, lam_im, jnp.broadcast_to(log_dt[:, None], lam_re.shape)], axis=0)
    lam_row = lam.reshape(3, N_SLABS, SLAB_STATE).transpose(1, 0, 2)
    lam_col = lam_row.transpose(0, 2, 1)

    def b_blockdiag(b):
        bt = b.reshape(*sg, SSM_STATE, SSM_GROUP).transpose(0, 1, 3, 2)
        return (bt[:, :, :, None, :] * eye[None, :, None, :, None]).reshape(N_SLABS, LANES, SLAB_STATE)

    def c_blockdiag(c):
        ct = c.reshape(*sg, SSM_GROUP, SSM_STATE).transpose(0, 1, 3, 2)
        return (ct[:, :, :, None, :] * eye[None, :, None, :, None]).reshape(N_SLABS, SLAB_STATE, LANES)

    bm = jnp.stack([b_blockdiag(b_re), b_blockdiag(b_im)], axis=1)
    cm = jnp.stack([c_blockdiag(c_re), c_blockdiag(c_im)], axis=1)
    d = d_skip.reshape(N_SLABS, 1, LANES)
    slab3 = lambda sl: (sl, 0, 0)
    slab4 = lambda sl: (sl, 0, 0, 0)
    return pl.pallas_call(
        _ssm_prep_kernel,
        out_shape=(jax.ShapeDtypeStruct((N_SLABS, FLAT, FLAT), _BF16),
                   jax.ShapeDtypeStruct((N_SLABS, FLAT, 2 * SLAB_STATE), _BF16),
                   jax.ShapeDtypeStruct((N_SLABS, 2 * SLAB_STATE, FLAT), _BF16),
                   jax.ShapeDtypeStruct((N_SLABS, 2, SLAB_STATE), _F32)),
        grid=(N_SLABS,),
        in_specs=[pl.BlockSpec((1, 3, SLAB_STATE), slab3),
                  pl.BlockSpec((1, SLAB_STATE, 3), slab3),
                  pl.BlockSpec((1, 2, LANES, SLAB_STATE), slab4),
                  pl.BlockSpec((1, 2, SLAB_STATE, LANES), slab4),
                  pl.BlockSpec((1, 1, LANES), slab3)],
        out_specs=(pl.BlockSpec((1, FLAT, FLAT), slab3),
                   pl.BlockSpec((1, FLAT, 2 * SLAB_STATE), slab3),
                   pl.BlockSpec((1, 2 * SLAB_STATE, FLAT), slab3),
                   pl.BlockSpec((1, 2, SLAB_STATE), slab3)),
        compiler_params=pltpu.CompilerParams(
            dimension_semantics=("arbitrary",), vmem_limit_bytes=VMEM_LIMIT_BYTES),
        name="ssm_prep",
    )(lam_row, lam_col, bm, cm, d)


def _ssm_kernel(u_ref, toep_ref, bst_ref, cst_ref, a_ref, y_ref, uflat, s_scr, xc_scr, carry):
    nb, tt, _ = u_ref.shape
    nch = tt // CHUNK
    n = nb * nch

    @pl.when(pl.program_id(1) == 0)
    def _():
        carry[...] = jnp.zeros_like(carry)

    for s in range(CHUNK):
        part = u_ref[:, pl.ds(s, nch, stride=CHUNK), :]
        uflat[:, s * LANES:(s + 1) * LANES] = part.reshape(n, LANES).astype(_BF16)

    nblk = SLAB_STATE // LANES
    loc_all = jnp.dot(uflat[...], bst_ref[0], preferred_element_type=_F32)
    for kb in range(2 * nblk):
        s_scr[kb] = loc_all[:, kb * LANES:(kb + 1) * LANES]

    a = a_ref[0]
    are = [jnp.broadcast_to(a[0:1, kb * LANES:(kb + 1) * LANES], (nb, LANES)) for kb in range(nblk)]
    aim = [jnp.broadcast_to(a[1:2, kb * LANES:(kb + 1) * LANES], (nb, LANES)) for kb in range(nblk)]
    xr = [carry[kb] for kb in range(nblk)]
    xi = [carry[nblk + kb] for kb in range(nblk)]
    for j in range(nch):
        rows = pl.ds(j, nb, stride=nch)
        for kb in range(nblk):
            xc_scr[kb, rows, :] = xr[kb]
            xc_scr[nblk + kb, rows, :] = xi[kb]
            nr = are[kb] * xr[kb] - aim[kb] * xi[kb] + s_scr[kb, rows, :]
            ni = are[kb] * xi[kb] + aim[kb] * xr[kb] + s_scr[nblk + kb, rows, :]
            xr[kb], xi[kb] = nr, ni
    for kb in range(nblk):
        carry[kb] = xr[kb]
        carry[nblk + kb] = xi[kb]

    xc = jnp.concatenate([xc_scr[kb] for kb in range(2 * nblk)], axis=1).astype(_BF16)
    for cb in range(FLAT // MXU_DIM):
        kk = (cb + 1) * MXU_DIM
        cols = slice(cb * MXU_DIM, kk)
        y = (jnp.dot(uflat[:, :kk], toep_ref[0, :kk, cols], preferred_element_type=_F32)
             + jnp.dot(xc, cst_ref[0, :, cols], preferred_element_type=_F32))
        y = jax.nn.gelu(y)
        for h in range(MXU_DIM // LANES):
            s = cb * (MXU_DIM // LANES) + h
            y_ref[:, pl.ds(s, nch, stride=CHUNK), :] = (
                y[:, h * LANES:(h + 1) * LANES].reshape(nb, nch, LANES))


def _ssm(u3, tables):
    toep, bst, cst, a_chunk = tables
    nb, seq, _ = u3.shape
    tt = SSM_TIME_TILE
    n = nb * (tt // CHUNK)
    return pl.pallas_call(
        _ssm_kernel,
        out_shape=jax.ShapeDtypeStruct(u3.shape, _F32),
        grid=(N_SLABS, seq // tt),
        in_specs=[pl.BlockSpec((nb, tt, LANES), lambda sl, ti: (0, ti, sl)),
                  pl.BlockSpec((1, FLAT, FLAT), lambda sl, ti: (sl, 0, 0)),
                  pl.BlockSpec((1, FLAT, 2 * SLAB_STATE), lambda sl, ti: (sl, 0, 0)),
                  pl.BlockSpec((1, 2 * SLAB_STATE, FLAT), lambda sl, ti: (sl, 0, 0)),
                  pl.BlockSpec((1, 2, SLAB_STATE), lambda sl, ti: (sl, 0, 0))],
        out_specs=pl.BlockSpec((nb, tt, LANES), lambda sl, ti: (0, ti, sl)),
        scratch_shapes=[pltpu.VMEM((n, FLAT), _BF16),
                        pltpu.VMEM((2 * SLAB_STATE // LANES, n, LANES), _F32),
                        pltpu.VMEM((2 * SLAB_STATE // LANES, n, LANES), _F32),
                        pltpu.VMEM((2 * SLAB_STATE // LANES, nb, LANES), _F32)],
        compiler_params=pltpu.CompilerParams(
            dimension_semantics=("arbitrary", "arbitrary"), vmem_limit_bytes=VMEM_LIMIT_BYTES),
        name="ssm",
    )(u3, toep, bst, cst, a_chunk)


def _mix_route_kernel(x_ref, bz_ref, yg_ref, gm_ref, wg_ref, wco_ref, wglu_ref, wout_ref,
                      gf_ref, wr_ref, br_ref,
                      x1_ref, h_ref, ti_ref, tg_ref, rk_ref, cnt_ref, base):
    tm = x_ref.shape[0]
    i = pl.program_id(0)

    @pl.when(i == 0)
    def _():
        base[...] = jnp.zeros_like(base)

    x = x_ref[...]
    xn = _rmsnorm(x, gm_ref[...]).astype(_BF16)
    gates = jnp.dot(xn, wg_ref[...], preferred_element_type=_F32)
    y_a = jnp.dot(bz_ref[...], wco_ref[...], preferred_element_type=_F32)
    glu = jnp.dot(yg_ref[...].astype(_BF16), wglu_ref[...], preferred_element_type=_F32)
    y_b = glu[:, :D_MODEL] * _sigmoid(glu[:, D_MODEL:])
    merged = _sigmoid(gates[:, :D_MODEL]) * y_a + _sigmoid(gates[:, D_MODEL:]) * y_b
    x1 = x + jnp.dot(merged.astype(_BF16), wout_ref[...], preferred_element_type=_F32)
    x1_ref[...] = x1
    h = _rmsnorm(x1, gf_ref[...])
    h_ref[...] = _pack_bf16_halves(h)

    logits = lax.dot_general(wr_ref[...], h, (((1,), (1,)), ((), ())),
                             preferred_element_type=_F32,
                             precision=lax.Precision.HIGHEST) + br_ref[...]
    erow = lax.broadcasted_iota(jnp.int32, (N_EXPERTS, tm), 0).astype(_F32)
    neg_inf = jnp.float32(-jnp.inf)
    work = logits
    vals, idxs = [], []
    for _ in range(TOP_K):
        m = jnp.max(work, axis=0, keepdims=True)
        idx = jnp.min(jnp.where(work == m, erow, float(N_EXPERTS)), axis=0, keepdims=True)
        vals.append(m)
        idxs.append(idx)
        work = jnp.where(erow == idx, neg_inf, work)
    exps = [jnp.exp(v - vals[0]) for v in vals]
    denom = exps[0] + exps[1] + exps[2] + exps[3]

    sel = jnp.zeros((N_EXPERTS, tm), _F32)
    for idx in idxs:
        sel = sel + (erow == idx).astype(_F32)
    row = lax.broadcasted_iota(jnp.int32, (tm, tm), 0)
    col = lax.broadcasted_iota(jnp.int32, (tm, tm), 1)
    earlier = (row < col).astype(_BF16)
    before = jnp.dot(sel.astype(_BF16), earlier, preferred_element_type=_F32) + base[...]
    tg_ref[...] = jnp.zeros_like(tg_ref)
    for k in range(TOP_K):
        ti_ref[k:k + 1, :] = idxs[k].astype(jnp.int32)
        tg_ref[k:k + 1, :] = exps[k] / denom
        rk = jnp.sum(jnp.where(erow == idxs[k], before, 0.0), axis=0, keepdims=True)
        rk_ref[k:k + 1, :] = rk.astype(jnp.int32)
    base[...] = base[...] + jnp.sum(sel, axis=1, keepdims=True)
    cnt_ref[...] = base[...].astype(jnp.int32)


def _mix_route(x2, bz, yg, g_mix, w_gates, w_conv_out, w_glu, w_out, g_ffn, w_router, b_router):
    t = x2.shape[0]
    tm = TOKEN_TILE
    tok = lambda i: (i, 0)
    tok_lanes = lambda i: (0, i)
    fixed = lambda i: (0, 0)
    return pl.pallas_call(
        _mix_route_kernel,
        out_shape=(jax.ShapeDtypeStruct((t, D_MODEL), _F32),
                   jax.ShapeDtypeStruct((t, D_MODEL // 2), jnp.uint32),
                   jax.ShapeDtypeStruct((TOP_K, t), jnp.int32),
                   jax.ShapeDtypeStruct((2 * TOP_K, t), _F32),
                   jax.ShapeDtypeStruct((TOP_K, t), jnp.int32),
                   jax.ShapeDtypeStruct((N_EXPERTS, 1), jnp.int32)),
        grid=(t // tm,),
        in_specs=[pl.BlockSpec((tm, D_MODEL), tok),
                  pl.BlockSpec((tm, D_CONV), tok),
                  pl.BlockSpec((tm, D_SSM), tok),
                  pl.BlockSpec((1, D_MODEL), fixed),
                  pl.BlockSpec((D_MODEL, 2 * D_MODEL), fixed),
                  pl.BlockSpec((D_CONV, D_MODEL), fixed),
                  pl.BlockSpec((D_SSM, 2 * D_MODEL), fixed),
                  pl.BlockSpec((D_MODEL, D_MODEL), fixed),
                  pl.BlockSpec((1, D_MODEL), fixed),
                  pl.BlockSpec((N_EXPERTS, D_MODEL), fixed),
                  pl.BlockSpec((N_EXPERTS, 1), fixed)],
        out_specs=(pl.BlockSpec((tm, D_MODEL), tok),
                   pl.BlockSpec((tm, D_MODEL // 2), tok),
                   pl.BlockSpec((TOP_K, tm), tok_lanes),
                   pl.BlockSpec((2 * TOP_K, tm), tok_lanes),
                   pl.BlockSpec((TOP_K, tm), tok_lanes),
                   pl.BlockSpec((N_EXPERTS, 1), fixed)),
        scratch_shapes=[pltpu.VMEM((N_EXPERTS, 1), _F32)],
        compiler_params=pltpu.CompilerParams(
            dimension_semantics=("arbitrary",), vmem_limit_bytes=VMEM_LIMIT_BYTES),
        name="mix_route",
    )(x2, bz, yg, g_mix, w_gates, w_conv_out, w_glu, w_out, g_ffn, w_router, b_router)


def _expert_ffn_kernel(be_ref, nr_ref, x_ref, wgu_ref, bgu_ref, wd_ref, bd_ref, y_ref, wgu_b, wd_b):
    b = pl.program_id(0)

    @pl.when((b == 0) | (be_ref[b] != be_ref[jnp.maximum(b - 1, 0)]))
    def _():
        wgu_b[...] = wgu_ref[0].astype(_BF16)
        wd_b[...] = wd_ref[0].astype(_BF16)

    @pl.when(nr_ref[b] > 0)
    def _():
        live = lax.broadcasted_iota(jnp.int32, x_ref.shape, 0) < nr_ref[b]
        x = _unpack_bf16_halves(jnp.where(live, x_ref[...], jnp.uint32(0))).astype(_BF16)
        hgu = jnp.dot(x, wgu_b[...], preferred_element_type=_F32) + bgu_ref[0]
        g = jnp.minimum(hgu[:, :D_FF], SWIGLU_LIMIT)
        up = jnp.clip(hgu[:, D_FF:], -SWIGLU_LIMIT, SWIGLU_LIMIT)
        act = (up + 1.0) * (g * _sigmoid(SWIGLU_ALPHA * g))
        y = jnp.dot(act.astype(_BF16), wd_b[...], preferred_element_type=_F32) + bd_ref[0]
        y_ref[...] = _pack_bf16_halves(y)

    @pl.when(nr_ref[b] == 0)
    def _():
        y_ref[...] = jnp.zeros_like(y_ref)


def _expert_ffn(block_e, block_rows, x_rows, w_gate_up, b_gate_up, w_down, b_down):
    n_rows = x_rows.shape[0]
    n_blocks = n_rows // MOE_BLOCK

    def wmap(b, be, nr):
        return (be[b], 0, 0)

    grid_spec = pltpu.PrefetchScalarGridSpec(
        num_scalar_prefetch=2,
        grid=(n_blocks,),
        in_specs=[pl.BlockSpec((MOE_BLOCK, D_MODEL // 2), lambda b, be, nr: (b, 0)),
                  pl.BlockSpec((1, D_MODEL, 2 * D_FF), wmap),
                  pl.BlockSpec((1, 1, 2 * D_FF), wmap),
                  pl.BlockSpec((1, D_FF, D_MODEL), wmap),
                  pl.BlockSpec((1, 1, D_MODEL), wmap)],
        out_specs=pl.BlockSpec((MOE_BLOCK, D_MODEL // 2), lambda b, be, nr: (b, 0)),
        scratch_shapes=[pltpu.VMEM((D_MODEL, 2 * D_FF), _BF16),
                        pltpu.VMEM((D_FF, D_MODEL), _BF16)],
    )
    return pl.pallas_call(
        _expert_ffn_kernel,
        out_shape=jax.ShapeDtypeStruct((n_rows, D_MODEL // 2), jnp.uint32),
        grid_spec=grid_spec,
        compiler_params=pltpu.CompilerParams(
            dimension_semantics=("arbitrary",), vmem_limit_bytes=VMEM_LIMIT_BYTES),
        name="expert_ffn",
    )(block_e, block_rows, x_rows, w_gate_up, b_gate_up, w_down, b_down)


def _sc_workers():
    info = plsc.get_sparse_core_info()
    return info.num_cores, info.num_cores * info.num_subcores


def _dispatch(h_packed, dest_flat, n_rows):
    t, width = h_packed.shape
    n_cores, n_workers = _sc_workers()
    per_w = t // n_workers
    n_chunks = per_w // SC_ROWS

    @functools.partial(
        pl.kernel, mesh=plsc.VectorSubcoreMesh(core_axis_name="c", subcore_axis_name="s"),
        out_type=jax.ShapeDtypeStruct((n_rows, width), h_packed.dtype),
        scratch_types=[pltpu.VMEM((TOP_K, SC_ROWS), jnp.int32),
                       pltpu.VMEM((SC_ROWS, width), h_packed.dtype),
                       pltpu.SemaphoreType.DMA])
    def scatter_rows(h_hbm, dest_hbm, out_hbm, idx_v, rows_v, sem):
        wid = lax.axis_index("s") * n_cores + lax.axis_index("c")

        @pl.loop(0, n_chunks)
        def _(ci):
            off = wid * per_w + ci * SC_ROWS
            pltpu.sync_copy(h_hbm.at[pl.ds(off, SC_ROWS)], rows_v)
            for k in range(TOP_K):
                pltpu.sync_copy(dest_hbm.at[pl.ds(k * t + off, SC_ROWS)], idx_v.at[k])
            copies = [pltpu.async_copy(rows_v, out_hbm.at[idx_v.at[k]], sem) for k in range(TOP_K)]
            for cp in copies:
                cp.wait()

    return scatter_rows(h_packed, dest_flat)


def _collect(y_rows, dest_flat):
    n_idx = dest_flat.shape[0]
    width = y_rows.shape[1]
    n_cores, n_workers = _sc_workers()
    n_chunks = n_idx // (n_workers * SC_ROWS)
    assert n_chunks % 2 == 0

    @functools.partial(
        pl.kernel, mesh=plsc.VectorSubcoreMesh(core_axis_name="c", subcore_axis_name="s"),
        out_type=jax.ShapeDtypeStruct((n_idx, width), y_rows.dtype),
        scratch_types=[pltpu.VMEM((n_chunks, SC_ROWS), jnp.int32),
                       pltpu.VMEM((2, SC_ROWS, width), y_rows.dtype),
                       pltpu.SemaphoreType.DMA((2,)),
                       pltpu.SemaphoreType.DMA((2,))])
    def gather_rows(y_hbm, dest_hbm, out_hbm, idx_v, buf, gsem, wsem):
        wid = lax.axis_index("s") * n_cores + lax.axis_index("c")
        c0 = wid * n_chunks
        pltpu.sync_copy(dest_hbm.at[pl.ds(c0, n_chunks)], idx_v)

        def gather(c, b):
            return pltpu.make_async_copy(y_hbm.at[idx_v.at[c]], buf.at[b], gsem.at[b])

        def write(c, b):
            return pltpu.make_async_copy(buf.at[b], out_hbm.at[pl.ds((c0 + c) * SC_ROWS, SC_ROWS)],
                                         wsem.at[b])

        gather(0, 0).start()

        @pl.loop(0, n_chunks, step=2)
        def _(ci):
            for b in range(2):
                c = ci + b

                @pl.when(c >= 1)
                def _():
                    write(c - 1, 1 - b).wait()

                @pl.when(c + 1 < n_chunks)
                def _():
                    gather(c + 1, 1 - b).start()

                gather(c, b).wait()
                write(c, b).start()

        write(n_chunks - 1, 1).wait()

    return gather_rows(y_rows, dest_flat.reshape(n_idx // SC_ROWS, SC_ROWS))


def _combine_kernel(x1_ref, ya_ref, tg_ref, g_ref, o_ref):
    acc = x1_ref[...]
    tg = jnp.transpose(tg_ref[...])
    for k in range(TOP_K):
        acc = acc + tg[:, k:k + 1] * _unpack_bf16_halves(ya_ref[k])
    o_ref[...] = _rmsnorm(acc, g_ref[...])


def _combine(x1, y_assign, top_g, g_final):
    t = x1.shape[0]
    tm = TOKEN_TILE
    return pl.pallas_call(
        _combine_kernel,
        out_shape=jax.ShapeDtypeStruct((t, D_MODEL), _F32),
        grid=(t // tm,),
        in_specs=[pl.BlockSpec((tm, D_MODEL), lambda i: (i, 0)),
                  pl.BlockSpec((TOP_K, tm, D_MODEL // 2), lambda i: (0, i, 0)),
                  pl.BlockSpec((2 * TOP_K, tm), lambda i: (0, i)),
                  pl.BlockSpec((1, D_MODEL), lambda i: (0, 0))],
        out_specs=pl.BlockSpec((tm, D_MODEL), lambda i: (i, 0)),
        compiler_params=pltpu.CompilerParams(
            dimension_semantics=("arbitrary",), vmem_limit_bytes=VMEM_LIMIT_BYTES),
        name="combine",
    )(x1, y_assign, top_g, g_final)


def kernel(x, norm_mix_g, w_in, conv_w, w_conv_out, ssm_lam_re, ssm_lam_im, ssm_log_dt, ssm_b_re, ssm_b_im, ssm_c_re, ssm_c_im, ssm_d, w_glu, w_out, norm_ffn_g, w_router, b_router, w_gate_up, b_gate_up, w_down, b_down, norm_f_g):
    bsz, seq, d = x.shape
    t = bsz * seq
    x2 = x.reshape(t, d)
    assert seq % TOKEN_TILE == 0 and seq % SSM_TIME_TILE == 0 and w_in.shape[0] == 1

    w_in_b = w_in[0].astype(_BF16)
    n_bcvu = 3 * D_CONV + D_SSM
    g_mix = norm_mix_g[0].reshape(1, d)

    bz, u = _in_proj(x2, g_mix, w_in_b[:, :n_bcvu], conv_w[0], seq)

    tables = _ssm_tables(ssm_lam_re[0], ssm_lam_im[0], ssm_log_dt[0], ssm_b_re[0], ssm_b_im[0],
                         ssm_c_re[0], ssm_c_im[0], ssm_d[0])
    yg = _ssm(u.reshape(bsz, seq, D_SSM), tables).reshape(t, D_SSM)

    x1, h_packed, top_i, top_g, rank, counts = _mix_route(
        x2, bz, yg, g_mix, w_in_b[:, n_bcvu:], w_conv_out[0].astype(_BF16),
        w_glu[0].astype(_BF16), w_out[0].astype(_BF16), norm_ffn_g[0].reshape(1, d),
        w_router[0].T, b_router[0].reshape(N_EXPERTS, 1))

    counts = counts[:, 0]
    padded = ((counts + MOE_BLOCK - 1) // MOE_BLOCK) * MOE_BLOCK
    pad_end = jnp.cumsum(padded)
    pad_start = pad_end - padded
    n_rows = t * TOP_K + N_EXPERTS * MOE_BLOCK
    n_blocks = n_rows // MOE_BLOCK
    expert_ids = jnp.arange(N_EXPERTS, dtype=jnp.int32)[:, None, None]
    row_start = jnp.sum(jnp.where(top_i[None] == expert_ids, pad_start[:, None, None], 0), axis=0)
    dest = (row_start + rank).reshape(TOP_K * t)
    block_start = jnp.arange(n_blocks, dtype=jnp.int32) * MOE_BLOCK
    block_e = jnp.minimum(jnp.sum(pad_end[None, :] <= block_start[:, None], axis=1),
                          N_EXPERTS - 1).astype(jnp.int32)
    block_rows = jnp.clip(pad_start[block_e] + counts[block_e] - block_start, 0, MOE_BLOCK)
    block_rows = block_rows.astype(jnp.int32)

    x_rows = _dispatch(h_packed, dest, n_rows)
    y_rows = _expert_ffn(block_e, block_rows, x_rows, w_gate_up[0],
                         b_gate_up[0].reshape(N_EXPERTS, 1, 2 * D_FF), w_down[0],
                         b_down[0].reshape(N_EXPERTS, 1, D_MODEL))
    y_assign = _collect(y_rows, dest).reshape(TOP_K, t, D_MODEL // 2)
    out = _combine(x1, y_assign, top_g, norm_f_g.reshape(1, d))
    return out.reshape(bsz, seq, d)
```

```python
import functools

import jax
import jax.numpy as jnp
from jax import lax
from jax.experimental import pallas as pl
from jax.experimental.pallas import tpu as pltpu
from jax.experimental.pallas import tpu_sc as plsc

D_MODEL = 1024
D_CONV = 512
CONV_WIDTH = 3
D_SSM = 512
SSM_GROUP = 16
N_SSM_GROUPS = 32
SSM_STATE = 64
N_EXPERTS = 32
TOP_K = 4
D_FF = 1024
SWIGLU_LIMIT = 7.0
SWIGLU_ALPHA = 1.702
MOE_BLOCK = 512
RMS_EPS = 1e-6

LANES = 128
MXU_DIM = 256
CHUNK = 16
SLAB_GROUPS = LANES // SSM_GROUP
N_SLABS = N_SSM_GROUPS // SLAB_GROUPS
SLAB_STATE = SLAB_GROUPS * SSM_STATE
FLAT = CHUNK * LANES
SSM_TIME_TILE = 256
TOKEN_TILE = 512
ROW_CHAIN = 256
SC_ROWS = 64
VMEM_LIMIT_BYTES = 56 * 1024 * 1024

_BF16 = jnp.bfloat16
_F32 = jnp.float32


def _rmsnorm(xf, g):
    return xf * lax.rsqrt(jnp.mean(xf * xf, axis=-1, keepdims=True) + RMS_EPS) * g


def _sigmoid(v):
    return 1.0 / (1.0 + jnp.exp(-v))


def _pack_bf16_halves(v):
    n = v.shape[1] // 2
    bits = pltpu.bitcast(v.astype(_BF16).astype(_F32), jnp.uint32)
    return (bits[:, :n] >> 16) | (bits[:, n:] & jnp.uint32(0xFFFF0000))


def _unpack_bf16_halves(w):
    return jnp.concatenate([pltpu.bitcast(w << 16, _F32),
                            pltpu.bitcast(w & jnp.uint32(0xFFFF0000), _F32)], axis=1)


def _in_proj_kernel(tiles_per_seq, x_ref, g_ref, w_ref, cw_ref, bz_ref, u_ref, hbuf):
    tm = x_ref.shape[0]
    i = pl.program_id(0)

    @pl.when(i % tiles_per_seq == 0)
    def _():
        hbuf[0:8, :] = jnp.zeros((8, D_CONV), _F32)

    xn = _rmsnorm(x_ref[...], g_ref[...]).astype(_BF16)
    cv = jnp.dot(xn, w_ref[:, D_CONV:3 * D_CONV], preferred_element_type=_F32)
    hbuf[8:8 + tm, :] = cv[:, :D_CONV] * cv[:, D_CONV:]
    u_ref[...] = jnp.dot(xn, w_ref[:, 3 * D_CONV:], preferred_element_type=_F32)
    cw = cw_ref[...]
    z = (cw[0:1, :] * hbuf[6:6 + tm, :] + cw[1:2, :] * hbuf[7:7 + tm, :]
         + cw[2:3, :] * hbuf[8:8 + tm, :])
    b_gate = jnp.dot(xn, w_ref[:, :D_CONV], preferred_element_type=_F32)
    bz_ref[...] = (b_gate * z).astype(_BF16)
    hbuf[0:8, :] = hbuf[tm:tm + 8, :]


def _in_proj(x2, g, w_bcvu, conv_w, seq):
    t = x2.shape[0]
    tm = TOKEN_TILE
    return pl.pallas_call(
        functools.partial(_in_proj_kernel, seq // tm),
        out_shape=(jax.ShapeDtypeStruct((t, D_CONV), _BF16),
                   jax.ShapeDtypeStruct((t, D_SSM), _F32)),
        grid=(t // tm,),
        in_specs=[pl.BlockSpec((tm, D_MODEL), lambda i: (i, 0)),
                  pl.BlockSpec((1, D_MODEL), lambda i: (0, 0)),
                  pl.BlockSpec((D_MODEL, 3 * D_CONV + D_SSM), lambda i: (0, 0)),
                  pl.BlockSpec((CONV_WIDTH, D_CONV), lambda i: (0, 0))],
        out_specs=(pl.BlockSpec((tm, D_CONV), lambda i: (i, 0)),
                   pl.BlockSpec((tm, D_SSM), lambda i: (i, 0))),
        scratch_shapes=[pltpu.VMEM((tm + 8, D_CONV), _F32)],
        compiler_params=pltpu.CompilerParams(
            dimension_semantics=("arbitrary",), vmem_limit_bytes=VMEM_LIMIT_BYTES),
        name="in_proj",
    )(x2, g, w_bcvu, conv_w)


def _ssm_prep_kernel(lr_ref, lc_ref, bm_ref, cm_ref, d_ref, toep_ref, bst_ref, cst_ref, a_ref):
    def discretise(lre, lim, log_dt):
        dt = jnp.exp(log_dt)
        mag = jnp.exp(lre * dt)
        return mag * jnp.cos(lim * dt), mag * jnp.sin(lim * dt)

    def powers(are, aim):
        pre, pim = [jnp.ones_like(are)], [jnp.zeros_like(are)]
        for _ in range(CHUNK):
            pre, pim = (pre + [pre[-1] * are - pim[-1] * aim],
                        pim + [pre[-1] * aim + pim[-1] * are])
        return pre, pim

    lr = lr_ref[0]
    lre, lim = lr[0:1, :], lr[1:2, :]
    are, aim = discretise(lre, lim, lr[2:3, :])
    pre, pim = powers(are, aim)
    den = lre * lre + lim * lim
    q_re = ((are - 1.0) * lre + aim * lim) / den
    q_im = (aim * lre - (are - 1.0) * lim) / den
    bb_re = q_re * bm_ref[0, 0] - q_im * bm_ref[0, 1]
    bb_im = q_re * bm_ref[0, 1] + q_im * bm_ref[0, 0]
    cm_re, cm_im = cm_ref[0, 0], cm_ref[0, 1]
    hi = lax.Precision.HIGHEST
    kblk = []
    for k in range(CHUNK):
        ab_re = bb_re * pre[k] - bb_im * pim[k]
        ab_im = bb_re * pim[k] + bb_im * pre[k]
        rows = slice((CHUNK - 1 - k) * LANES, (CHUNK - k) * LANES)
        bst_ref[0, rows, :SLAB_STATE] = ab_re.astype(_BF16)
        bst_ref[0, rows, SLAB_STATE:] = ab_im.astype(_BF16)
        kblk.append(jnp.dot(ab_re, cm_re, preferred_element_type=_F32, precision=hi)
                    - jnp.dot(ab_im, cm_im, preferred_element_type=_F32, precision=hi))
    r = lax.broadcasted_iota(jnp.int32, (LANES, LANES), 0)
    c = lax.broadcasted_iota(jnp.int32, (LANES, LANES), 1)
    kblk[0] = kblk[0] + jnp.where(r == c, jnp.broadcast_to(d_ref[0], (LANES, LANES)), 0.0)
    kblk = [kb.astype(_BF16) for kb in kblk]
    zeros = jnp.zeros((LANES, LANES), _BF16)
    for sp in range(CHUNK):
        for s in range(CHUNK):
            toep_ref[0, sp * LANES:(sp + 1) * LANES, s * LANES:(s + 1) * LANES] = (
                kblk[s - sp] if s >= sp else zeros)

    lc = lc_ref[0]
    cre, cim = discretise(lc[:, 0:1], lc[:, 1:2], lc[:, 2:3])
    qre, qim = powers(cre, cim)
    for s in range(CHUNK):
        cols = slice(s * LANES, (s + 1) * LANES)
        cst_ref[0, :SLAB_STATE, cols] = (cm_re * qre[s + 1] - cm_im * qim[s + 1]).astype(_BF16)
        cst_ref[0, SLAB_STATE:, cols] = (-(cm_re * qim[s + 1] + cm_im * qre[s + 1])).astype(_BF16)
    a_ref[0, 0:1, :] = pre[CHUNK]
    a_ref[0, 1:2, :] = pim[CHUNK]


def _ssm_tables(lam_re, lam_im, log_dt, b_re, b_im, c_re, c_im, d_skip):
    sg = (N_SLABS, SLAB_GROUPS)
    eye = jnp.eye(SLAB_GROUPS, dtype=_F32)
    lam = jnp.stack([lam_re, lam_im, jnp.broadcast_to(log_dt[:, None], lam_re.shape)], axis=0)
    lam_row = lam.reshape(3, N_SLABS, SLAB_STATE).transpose(1, 0, 2)
    lam_col = lam_row.transpose(0, 2, 1)

    def b_blockdiag(b):
        bt = b.reshape(*sg, SSM_STATE, SSM_GROUP).transpose(0, 1, 3, 2)
        return (bt[:, :, :, None, :] * eye[None, :, None, :, None]).reshape(N_SLABS, LANES, SLAB_STATE)

    def c_blockdiag(c):
        ct = c.reshape(*sg, SSM_GROUP, SSM_STATE).transpose(0, 1, 3, 2)
        return (ct[:, :, :, None, :] * eye[None, :, None, :, None]).reshape(N_SLABS, SLAB_STATE, LANES)

    bm = jnp.stack([b_blockdiag(b_re), b_blockdiag(b_im)], axis=1)
    cm = jnp.stack([c_blockdiag(c_re), c_blockdiag(c_im)], axis=1)
    d = d_skip.reshape(N_SLABS, 1, LANES)
    slab3 = lambda sl: (sl, 0, 0)
    slab4 = lambda sl: (sl, 0, 0, 0)
    return pl.pallas_call(
        _ssm_prep_kernel,
        out_shape=(jax.ShapeDtypeStruct((N_SLABS, FLAT, FLAT), _BF16),
                   jax.ShapeDtypeStruct((N_SLABS, FLAT, 2 * SLAB_STATE), _BF16),
                   jax.ShapeDtypeStruct((N_SLABS, 2 * SLAB_STATE, FLAT), _BF16),
                   jax.ShapeDtypeStruct((N_SLABS, 2, SLAB_STATE), _F32)),
        grid=(N_SLABS,),
        in_specs=[pl.BlockSpec((1, 3, SLAB_STATE), slab3),
                  pl.BlockSpec((1, SLAB_STATE, 3), slab3),
                  pl.BlockSpec((1, 2, LANES, SLAB_STATE), slab4),
                  pl.BlockSpec((1, 2, SLAB_STATE, LANES), slab4),
                  pl.BlockSpec((1, 1, LANES), slab3)],
        out_specs=(pl.BlockSpec((1, FLAT, FLAT), slab3),
                   pl.BlockSpec((1, FLAT, 2 * SLAB_STATE), slab3),
                   pl.BlockSpec((1, 2 * SLAB_STATE, FLAT), slab3),
                   pl.BlockSpec((1, 2, SLAB_STATE), slab3)),
        compiler_params=pltpu.CompilerParams(
            dimension_semantics=("arbitrary",), vmem_limit_bytes=VMEM_LIMIT_BYTES),
        name="ssm_prep",
    )(lam_row, lam_col, bm, cm, d)


def _ssm_kernel(u_ref, toep_ref, bst_ref, cst_ref, a_ref, y_ref, uflat, s_scr, xc_scr, carry, ytoep):
    nb, tt, _ = u_ref.shape
    nch = tt // CHUNK
    n = nb * nch

    @pl.when(pl.program_id(1) == 0)
    def _():
        carry[...] = jnp.zeros_like(carry)

    for s in range(CHUNK):
        part = u_ref[:, pl.ds(s, nch, stride=CHUNK), :]
        uflat[:, s * LANES:(s + 1) * LANES] = part.reshape(n, LANES).astype(_BF16)

    n_cb = FLAT // MXU_DIM

    def toeplitz(cb):
        kk = (cb + 1) * MXU_DIM
        cols = slice(cb * MXU_DIM, kk)
        ytoep[:, cols] = jnp.dot(uflat[:, :kk], toep_ref[0, :kk, cols], preferred_element_type=_F32)

    for cb in range(n_cb // 2):
        toeplitz(cb)

    nblk = SLAB_STATE // LANES
    loc_all = jnp.dot(uflat[...], bst_ref[0], preferred_element_type=_F32)
    for cb in range(n_cb // 2, n_cb):
        toeplitz(cb)
    for kb in range(2 * nblk):
        s_scr[kb] = loc_all[:, kb * LANES:(kb + 1) * LANES]

    a = a_ref[0]
    are = [jnp.broadcast_to(a[0:1, kb * LANES:(kb + 1) * LANES], (nb, LANES)) for kb in range(nblk)]
    aim = [jnp.broadcast_to(a[1:2, kb * LANES:(kb + 1) * LANES], (nb, LANES)) for kb in range(nblk)]
    xr = [carry[kb] for kb in range(nblk)]
    xi = [carry[nblk + kb] for kb in range(nblk)]
    for j in range(nch):
        rows = pl.ds(j, nb, stride=nch)
        for kb in range(nblk):
            xc_scr[kb, rows, :] = xr[kb]
            xc_scr[nblk + kb, rows, :] = xi[kb]
            nr = are[kb] * xr[kb] - aim[kb] * xi[kb] + s_scr[kb, rows, :]
            ni = are[kb] * xi[kb] + aim[kb] * xr[kb] + s_scr[nblk + kb, rows, :]
            xr[kb], xi[kb] = nr, ni
    for kb in range(nblk):
        carry[kb] = xr[kb]
        carry[nblk + kb] = xi[kb]

    xc = jnp.concatenate([xc_scr[kb] for kb in range(2 * nblk)], axis=1).astype(_BF16)
    for cb in range(n_cb):
        cols = slice(cb * MXU_DIM, (cb + 1) * MXU_DIM)
        y = ytoep[:, cols] + jnp.dot(xc, cst_ref[0, :, cols], preferred_element_type=_F32)
        y = jax.nn.gelu(y)
        for h in range(MXU_DIM // LANES):
            s = cb * (MXU_DIM // LANES) + h
            y_ref[:, pl.ds(s, nch, stride=CHUNK), :] = (
                y[:, h * LANES:(h + 1) * LANES].reshape(nb, nch, LANES))


def _ssm(u3, tables):
    toep, bst, cst, a_chunk = tables
    nb, seq, _ = u3.shape
    tt = SSM_TIME_TILE
    n = nb * (tt // CHUNK)
    return pl.pallas_call(
        _ssm_kernel,
        out_shape=jax.ShapeDtypeStruct(u3.shape, _F32),
        grid=(N_SLABS, seq // tt),
        in_specs=[pl.BlockSpec((nb, tt, LANES), lambda sl, ti: (0, ti, sl)),
                  pl.BlockSpec((1, FLAT, FLAT), lambda sl, ti: (sl, 0, 0)),
                  pl.BlockSpec((1, FLAT, 2 * SLAB_STATE), lambda sl, ti: (sl, 0, 0)),
                  pl.BlockSpec((1, 2 * SLAB_STATE, FLAT), lambda sl, ti: (sl, 0, 0)),
                  pl.BlockSpec((1, 2, SLAB_STATE), lambda sl, ti: (sl, 0, 0))],
        out_specs=pl.BlockSpec((nb, tt, LANES), lambda sl, ti: (0, ti, sl)),
        scratch_shapes=[pltpu.VMEM((n, FLAT), _BF16),
                        pltpu.VMEM((2 * SLAB_STATE // LANES, n, LANES), _F32),
                        pltpu.VMEM((2 * SLAB_STATE // LANES, n, LANES), _F32),
                        pltpu.VMEM((2 * SLAB_STATE // LANES, nb, LANES), _F32),
                        pltpu.VMEM((n, FLAT), _F32)],
        compiler_params=pltpu.CompilerParams(
            dimension_semantics=("arbitrary", "arbitrary"), vmem_limit_bytes=VMEM_LIMIT_BYTES),
        name="ssm",
    )(u3, toep, bst, cst, a_chunk)


def _mix_route_kernel(x_ref, bz_ref, yg_ref, gm_ref, wg_ref, wco_ref, wglu_ref, wout_ref,
                      gf_ref, wr_ref, br_ref,
                      x1_ref, h_ref, ti_ref, tg_ref, rk_ref, cnt_ref, base, merged):
    tm = x_ref.shape[0]

    @pl.when(pl.program_id(0) == 0)
    def _():
        base[...] = jnp.zeros_like(base)

    chains = [slice(r0, r0 + ROW_CHAIN) for r0 in range(0, tm, ROW_CHAIN)]
    hs = [_mix_rows(rows, x_ref, bz_ref, yg_ref, gm_ref, wg_ref, wco_ref, wglu_ref, wout_ref,
                    gf_ref, x1_ref, h_ref, merged) for rows in chains]
    for rows, h in zip(chains, hs):
        _route_rows(rows, h, wr_ref, br_ref, ti_ref, tg_ref, rk_ref, base)
    cnt_ref[...] = base[...].astype(jnp.int32)


def _mix_rows(rows, x_ref, bz_ref, yg_ref, gm_ref, wg_ref, wco_ref, wglu_ref, wout_ref,
              gf_ref, x1_ref, h_ref, merged):
    x = x_ref[rows, :]
    xn = _rmsnorm(x, gm_ref[...]).astype(_BF16)
    bz = bz_ref[rows, :]
    yg = yg_ref[rows, :].astype(_BF16)
    for c in range(D_MODEL // MXU_DIM):
        lo = slice(c * MXU_DIM, (c + 1) * MXU_DIM)
        hi = slice(D_MODEL + c * MXU_DIM, D_MODEL + (c + 1) * MXU_DIM)
        gate_a = jnp.dot(xn, wg_ref[:, lo], preferred_element_type=_F32)
        gate_b = jnp.dot(xn, wg_ref[:, hi], preferred_element_type=_F32)
        y_a = jnp.dot(bz, wco_ref[:, lo], preferred_element_type=_F32)
        val = jnp.dot(yg, wglu_ref[:, lo], preferred_element_type=_F32)
        glu_gate = jnp.dot(yg, wglu_ref[:, hi], preferred_element_type=_F32)
        y_b = val * _sigmoid(glu_gate)
        merged[rows, lo] = (_sigmoid(gate_a) * y_a + _sigmoid(gate_b) * y_b).astype(_BF16)
    x1 = x + jnp.dot(merged[rows, :], wout_ref[...], preferred_element_type=_F32)
    x1_ref[rows, :] = x1
    h = _rmsnorm(x1, gf_ref[...])
    h_ref[rows, :] = _pack_bf16_halves(h)
    return h.astype(_BF16)


def _route_rows(rows, h, wr_ref, br_ref, ti_ref, tg_ref, rk_ref, base):
    tm = rows.stop - rows.start
    logits_tok = jnp.dot(h, wr_ref[...], preferred_element_type=_F32)
    logits = jnp.transpose(logits_tok)[:N_EXPERTS, :] + br_ref[...]
    erow = lax.broadcasted_iota(jnp.int32, (N_EXPERTS, tm), 0).astype(_F32)
    neg_inf = jnp.float32(-jnp.inf)
    work = logits
    vals, idxs = [], []
    for _ in range(TOP_K):
        m = jnp.max(work, axis=0, keepdims=True)
        idx = jnp.min(jnp.where(work == m, erow, float(N_EXPERTS)), axis=0, keepdims=True)
        vals.append(m)
        idxs.append(idx)
        work = jnp.where(erow == idx, neg_inf, work)
    exps = [jnp.exp(v - vals[0]) for v in vals]
    denom = exps[0] + exps[1] + exps[2] + exps[3]

    sel = jnp.zeros((N_EXPERTS, tm), _F32)
    for idx in idxs:
        sel = sel + (erow == idx).astype(_F32)
    row = lax.broadcasted_iota(jnp.int32, (tm, tm), 0)
    col = lax.broadcasted_iota(jnp.int32, (tm, tm), 1)
    earlier = (row < col).astype(_BF16)
    before = jnp.dot(sel.astype(_BF16), earlier, preferred_element_type=_F32) + base[...]
    for k in range(TOP_K):
        ti_ref[k:k + 1, rows] = idxs[k].astype(jnp.int32)
        tg_ref[k:k + 1, rows] = exps[k] / denom
        tg_ref[TOP_K + k:TOP_K + k + 1, rows] = jnp.zeros((1, tm), _F32)
        rk = jnp.sum(jnp.where(erow == idxs[k], before, 0.0), axis=0, keepdims=True)
        rk_ref[k:k + 1, rows] = rk.astype(jnp.int32)
    base[...] = base[...] + jnp.sum(sel, axis=1, keepdims=True)


def _mix_route(x2, bz, yg, g_mix, w_gates, w_conv_out, w_glu, w_out, g_ffn, w_router, b_router):
    t = x2.shape[0]
    tm = TOKEN_TILE
    tok = lambda i: (i, 0)
    tok_lanes = lambda i: (0, i)
    fixed = lambda i: (0, 0)
    return pl.pallas_call(
        _mix_route_kernel,
        out_shape=(jax.ShapeDtypeStruct((t, D_MODEL), _F32),
                   jax.ShapeDtypeStruct((t, D_MODEL // 2), jnp.uint32),
                   jax.ShapeDtypeStruct((TOP_K, t), jnp.int32),
                   jax.ShapeDtypeStruct((2 * TOP_K, t), _F32),
                   jax.ShapeDtypeStruct((TOP_K, t), jnp.int32),
                   jax.ShapeDtypeStruct((N_EXPERTS, 1), jnp.int32)),
        grid=(t // tm,),
        in_specs=[pl.BlockSpec((tm, D_MODEL), tok),
                  pl.BlockSpec((tm, D_CONV), tok),
                  pl.BlockSpec((tm, D_SSM), tok),
                  pl.BlockSpec((1, D_MODEL), fixed),
                  pl.BlockSpec((D_MODEL, 2 * D_MODEL), fixed),
                  pl.BlockSpec((D_CONV, D_MODEL), fixed),
                  pl.BlockSpec((D_SSM, 2 * D_MODEL), fixed),
                  pl.BlockSpec((D_MODEL, D_MODEL), fixed),
                  pl.BlockSpec((1, D_MODEL), fixed),
                  pl.BlockSpec((D_MODEL, LANES), fixed),
                  pl.BlockSpec((N_EXPERTS, 1), fixed)],
        out_specs=(pl.BlockSpec((tm, D_MODEL), tok),
                   pl.BlockSpec((tm, D_MODEL // 2), tok),
                   pl.BlockSpec((TOP_K, tm), tok_lanes),
                   pl.BlockSpec((2 * TOP_K, tm), tok_lanes),
                   pl.BlockSpec((TOP_K, tm), tok_lanes),
                   pl.BlockSpec((N_EXPERTS, 1), fixed)),
        scratch_shapes=[pltpu.VMEM((N_EXPERTS, 1), _F32),
                        pltpu.VMEM((tm, D_MODEL), _BF16)],
        compiler_params=pltpu.CompilerParams(
            dimension_semantics=("arbitrary",), vmem_limit_bytes=VMEM_LIMIT_BYTES),
        name="mix_route",
    )(x2, bz, yg, g_mix, w_gates, w_conv_out, w_glu, w_out, g_ffn, w_router, b_router)


def _expert_ffn_kernel(be_ref, nr_ref, x_ref, wgu_ref, bgu_ref, wd_ref, bd_ref, y_ref, wgu_b, wd_b):
    b = pl.program_id(0)

    @pl.when((b == 0) | (be_ref[b] != be_ref[jnp.maximum(b - 1, 0)]))
    def _():
        wgu_b[...] = wgu_ref[0].astype(_BF16)
        wd_b[...] = wd_ref[0].astype(_BF16)

    @pl.when(nr_ref[b] > 0)
    def _():
        live = lax.broadcasted_iota(jnp.int32, x_ref.shape, 0) < nr_ref[b]
        x = _unpack_bf16_halves(jnp.where(live, x_ref[...], jnp.uint32(0))).astype(_BF16)
        hgu = jnp.dot(x, wgu_b[...], preferred_element_type=_F32) + bgu_ref[0]
        g = jnp.minimum(hgu[:, :D_FF], SWIGLU_LIMIT)
        up = jnp.clip(hgu[:, D_FF:], -SWIGLU_LIMIT, SWIGLU_LIMIT)
        act = (up + 1.0) * (g * _sigmoid(SWIGLU_ALPHA * g))
        y = jnp.dot(act.astype(_BF16), wd_b[...], preferred_element_type=_F32) + bd_ref[0]
        y_ref[...] = _pack_bf16_halves(y)

    @pl.when(nr_ref[b] == 0)
    def _():
        y_ref[...] = jnp.zeros_like(y_ref)


def _expert_ffn(block_e, block_rows, x_rows, w_gate_up, b_gate_up, w_down, b_down):
    n_rows = x_rows.shape[0]
    n_blocks = n_rows // MOE_BLOCK

    def wmap(b, be, nr):
        return (be[b], 0, 0)

    grid_spec = pltpu.PrefetchScalarGridSpec(
        num_scalar_prefetch=2,
        grid=(n_blocks,),
        in_specs=[pl.BlockSpec((MOE_BLOCK, D_MODEL // 2), lambda b, be, nr: (b, 0)),
                  pl.BlockSpec((1, D_MODEL, 2 * D_FF), wmap),
                  pl.BlockSpec((1, 1, 2 * D_FF), wmap),
                  pl.BlockSpec((1, D_FF, D_MODEL), wmap),
                  pl.BlockSpec((1, 1, D_MODEL), wmap)],
        out_specs=pl.BlockSpec((MOE_BLOCK, D_MODEL // 2), lambda b, be, nr: (b, 0)),
        scratch_shapes=[pltpu.VMEM((D_MODEL, 2 * D_FF), _BF16),
                        pltpu.VMEM((D_FF, D_MODEL), _BF16)],
    )
    return pl.pallas_call(
        _expert_ffn_kernel,
        out_shape=jax.ShapeDtypeStruct((n_rows, D_MODEL // 2), jnp.uint32),
        grid_spec=grid_spec,
        compiler_params=pltpu.CompilerParams(
            dimension_semantics=("arbitrary",), vmem_limit_bytes=VMEM_LIMIT_BYTES),
        name="expert_ffn",
    )(block_e, block_rows, x_rows, w_gate_up, b_gate_up, w_down, b_down)


def _sc_workers():
    info = plsc.get_sparse_core_info()
    return info.num_cores, info.num_cores * info.num_subcores


def _dispatch(h_packed, dest_flat, n_rows):
    t, width = h_packed.shape
    n_cores, n_workers = _sc_workers()
    per_w = t // n_workers
    n_chunks = per_w // SC_ROWS

    @functools.partial(
        pl.kernel, mesh=plsc.VectorSubcoreMesh(core_axis_name="c", subcore_axis_name="s"),
        out_type=jax.ShapeDtypeStruct((n_rows, width), h_packed.dtype),
        scratch_types=[pltpu.VMEM((TOP_K, SC_ROWS), jnp.int32),
                       pltpu.VMEM((SC_ROWS, width), h_packed.dtype),
                       pltpu.SemaphoreType.DMA])
    def scatter_rows(h_hbm, dest_hbm, out_hbm, idx_v, rows_v, sem):
        wid = lax.axis_index("s") * n_cores + lax.axis_index("c")

        @pl.loop(0, n_chunks)
        def _(ci):
            off = wid * per_w + ci * SC_ROWS
            pltpu.sync_copy(h_hbm.at[pl.ds(off, SC_ROWS)], rows_v)
            for k in range(TOP_K):
                pltpu.sync_copy(dest_hbm.at[pl.ds(k * t + off, SC_ROWS)], idx_v.at[k])
            copies = [pltpu.async_copy(rows_v, out_hbm.at[idx_v.at[k]], sem) for k in range(TOP_K)]
            for cp in copies:
                cp.wait()

    return scatter_rows(h_packed, dest_flat)


def _collect(y_rows, dest_flat):
    n_idx = dest_flat.shape[0]
    width = y_rows.shape[1]
    n_cores, n_workers = _sc_workers()
    n_chunks = n_idx // (n_workers * SC_ROWS)
    assert n_chunks % 2 == 0

    @functools.partial(
        pl.kernel, mesh=plsc.VectorSubcoreMesh(core_axis_name="c", subcore_axis_name="s"),
        out_type=jax.ShapeDtypeStruct((n_idx, width), y_rows.dtype),
        scratch_types=[pltpu.VMEM((n_chunks, SC_ROWS), jnp.int32),
                       pltpu.VMEM((2, SC_ROWS, width), y_rows.dtype),
                       pltpu.SemaphoreType.DMA((2,)),
                       pltpu.SemaphoreType.DMA((2,))])
    def gather_rows(y_hbm, dest_hbm, out_hbm, idx_v, buf, gsem, wsem):
        wid = lax.axis_index("s") * n_cores + lax.axis_index("c")
        c0 = wid * n_chunks
        pltpu.sync_copy(dest_hbm.at[pl.ds(c0, n_chunks)], idx_v)

        def gather(c, b):
            return pltpu.make_async_copy(y_hbm.at[idx_v.at[c]], buf.at[b], gsem.at[b])

        def write(c, b):
            return pltpu.make_async_copy(buf.at[b], out_hbm.at[pl.ds((c0 + c) * SC_ROWS, SC_ROWS)],
                                         wsem.at[b])

        gather(0, 0).start()

        @pl.loop(0, n_chunks, step=2)
        def _(ci):
            for b in range(2):
                c = ci + b

                @pl.when(c >= 1)
                def _():
                    write(c - 1, 1 - b).wait()

                @pl.when(c + 1 < n_chunks)
                def _():
                    gather(c + 1, 1 - b).start()

                gather(c, b).wait()
                write(c, b).start()

        write(n_chunks - 1, 1).wait()

    return gather_rows(y_rows, dest_flat.reshape(n_idx // SC_ROWS, SC_ROWS))


def _combine_kernel(x1_ref, ya_ref, tg_ref, g_ref, o_ref):
    acc = x1_ref[...]
    tg = jnp.transpose(tg_ref[...])
    for k in range(TOP_K):
        acc = acc + tg[:, k:k + 1] * _unpack_bf16_halves(ya_ref[k])
    o_ref[...] = _rmsnorm(acc, g_ref[...])


def _combine(x1, y_assign, top_g, g_final):
    t = x1.shape[0]
    tm = TOKEN_TILE
    return pl.pallas_call(
        _combine_kernel,
        out_shape=jax.ShapeDtypeStruct((t, D_MODEL), _F32),
        grid=(t // tm,),
        in_specs=[pl.BlockSpec((tm, D_MODEL), lambda i: (i, 0)),
                  pl.BlockSpec((TOP_K, tm, D_MODEL // 2), lambda i: (0, i, 0)),
                  pl.BlockSpec((2 * TOP_K, tm), lambda i: (0, i)),
                  pl.BlockSpec((1, D_MODEL), lambda i: (0, 0))],
        out_specs=pl.BlockSpec((tm, D_MODEL), lambda i: (i, 0)),
        compiler_params=pltpu.CompilerParams(
            dimension_semantics=("arbitrary",), vmem_limit_bytes=VMEM_LIMIT_BYTES),
        name="combine",
    )(x1, y_assign, top_g, g_final)


def kernel(x, norm_mix_g, w_in, conv_w, w_conv_out, ssm_lam_re, ssm_lam_im, ssm_log_dt, ssm_b_re, ssm_b_im, ssm_c_re, ssm_c_im, ssm_d, w_glu, w_out, norm_ffn_g, w_router, b_router, w_gate_up, b_gate_up, w_down, b_down, norm_f_g):
    bsz, seq, d = x.shape
    t = bsz * seq
    x2 = x.reshape(t, d)
    assert seq % TOKEN_TILE == 0 and seq % SSM_TIME_TILE == 0 and w_in.shape[0] == 1

    w_in_b = w_in[0].astype(_BF16)
    n_bcvu = 3 * D_CONV + D_SSM
    g_mix = norm_mix_g[0].reshape(1, d)

    bz, u = _in_proj(x2, g_mix, w_in_b[:, :n_bcvu], conv_w[0], seq)

    tables = _ssm_tables(ssm_lam_re[0], ssm_lam_im[0], ssm_log_dt[0], ssm_b_re[0], ssm_b_im[0],
                         ssm_c_re[0], ssm_c_im[0], ssm_d[0])
    yg = _ssm(u.reshape(bsz, seq, D_SSM), tables).reshape(t, D_SSM)

    x1, h_packed, top_i, top_g, rank, counts = _mix_route(
        x2, bz, yg, g_mix, w_in_b[:, n_bcvu:], w_conv_out[0].astype(_BF16),
        w_glu[0].astype(_BF16), w_out[0].astype(_BF16), norm_ffn_g[0].reshape(1, d),
        jnp.pad(w_router[0], ((0, 0), (0, LANES - N_EXPERTS))).astype(_BF16),
        b_router[0].reshape(N_EXPERTS, 1))

    counts = counts[:, 0]
    padded = ((counts + MOE_BLOCK - 1) // MOE_BLOCK) * MOE_BLOCK
    pad_end = jnp.cumsum(padded)
    pad_start = pad_end - padded
    n_rows = t * TOP_K + N_EXPERTS * MOE_BLOCK
    n_blocks = n_rows // MOE_BLOCK
    expert_ids = jnp.arange(N_EXPERTS, dtype=jnp.int32)[:, None, None]
    row_start = jnp.sum(jnp.where(top_i[None] == expert_ids, pad_start[:, None, None], 0), axis=0)
    dest = (row_start + rank).reshape(TOP_K * t)
    block_start = jnp.arange(n_blocks, dtype=jnp.int32) * MOE_BLOCK
    block_e = jnp.minimum(jnp.sum(pad_end[None, :] <= block_start[:, None], axis=1),
                          N_EXPERTS - 1).astype(jnp.int32)
    block_rows = jnp.clip(pad_start[block_e] + counts[block_e] - block_start, 0, MOE_BLOCK)
    block_rows = block_rows.astype(jnp.int32)

    x_rows = _dispatch(h_packed, dest, n_rows)
    y_rows = _expert_ffn(block_e, block_rows, x_rows, w_gate_up[0],
                         b_gate_up[0].reshape(N_EXPERTS, 1, 2 * D_FF), w_down[0],
                         b_down[0].reshape(N_EXPERTS, 1, D_MODEL))
    y_assign = _collect(y_rows, dest).reshape(TOP_K, t, D_MODEL // 2)
    out = _combine(x1, y_assign, top_g, norm_f_g.reshape(1, d))
    return out.reshape(bsz, seq, d)
```

```python
import functools

import jax
import jax.numpy as jnp
from jax import lax
from jax.experimental import pallas as pl
from jax.experimental.pallas import tpu as pltpu
from jax.experimental.pallas import tpu_sc as plsc

D_MODEL = 1024
D_CONV = 512
CONV_WIDTH = 3
D_SSM = 512
SSM_GROUP = 16
N_SSM_GROUPS = 32
SSM_STATE = 64
N_EXPERTS = 32
TOP_K = 4
D_FF = 1024
SWIGLU_LIMIT = 7.0
SWIGLU_ALPHA = 1.702
MOE_BLOCK = 512
RMS_EPS = 1e-6

LANES = 128
MXU_DIM = 256
CHUNK = 16
SLAB_GROUPS = LANES // SSM_GROUP
N_SLABS = N_SSM_GROUPS // SLAB_GROUPS
SLAB_STATE = SLAB_GROUPS * SSM_STATE
FLAT = CHUNK * LANES
SSM_TIME_TILE = 256
TOKEN_TILE = 512
COLLECT_GROUPS = 4
ROW_CHAIN = 256
SC_ROWS = 64
VMEM_LIMIT_BYTES = 56 * 1024 * 1024

_BF16 = jnp.bfloat16
_F32 = jnp.float32


def _rmsnorm(xf, g):
    return xf * lax.rsqrt(jnp.mean(xf * xf, axis=-1, keepdims=True) + RMS_EPS) * g


def _sigmoid(v):
    return 1.0 / (1.0 + jnp.exp(-v))


def _pack_bf16_halves(v):
    n = v.shape[1] // 2
    bits = pltpu.bitcast(v.astype(_BF16).astype(_F32), jnp.uint32)
    return (bits[:, :n] >> 16) | (bits[:, n:] & jnp.uint32(0xFFFF0000))


def _unpack_bf16_halves(w):
    return jnp.concatenate([pltpu.bitcast(w << 16, _F32),
                            pltpu.bitcast(w & jnp.uint32(0xFFFF0000), _F32)], axis=1)


def _in_proj_kernel(tiles_per_seq, x_ref, g_ref, w_ref, cw_ref, bz_ref, u_ref, hbuf):
    tm = x_ref.shape[0]
    i = pl.program_id(0)

    @pl.when(i % tiles_per_seq == 0)
    def _():
        hbuf[0:8, :] = jnp.zeros((8, D_CONV), _F32)

    xn = _rmsnorm(x_ref[...], g_ref[...]).astype(_BF16)
    cv = jnp.dot(xn, w_ref[:, D_CONV:3 * D_CONV], preferred_element_type=_F32)
    hbuf[8:8 + tm, :] = cv[:, :D_CONV] * cv[:, D_CONV:]
    u_ref[...] = jnp.dot(xn, w_ref[:, 3 * D_CONV:], preferred_element_type=_F32)
    cw = cw_ref[...]
    z = (cw[0:1, :] * hbuf[6:6 + tm, :] + cw[1:2, :] * hbuf[7:7 + tm, :]
         + cw[2:3, :] * hbuf[8:8 + tm, :])
    b_gate = jnp.dot(xn, w_ref[:, :D_CONV], preferred_element_type=_F32)
    bz_ref[...] = (b_gate * z).astype(_BF16)
    hbuf[0:8, :] = hbuf[tm:tm + 8, :]


def _in_proj(x2, g, w_bcvu, conv_w, seq):
    t = x2.shape[0]
    tm = TOKEN_TILE
    return pl.pallas_call(
        functools.partial(_in_proj_kernel, seq // tm),
        out_shape=(jax.ShapeDtypeStruct((t, D_CONV), _BF16),
                   jax.ShapeDtypeStruct((t, D_SSM), _F32)),
        grid=(t // tm,),
        in_specs=[pl.BlockSpec((tm, D_MODEL), lambda i: (i, 0)),
                  pl.BlockSpec((1, D_MODEL), lambda i: (0, 0)),
                  pl.BlockSpec((D_MODEL, 3 * D_CONV + D_SSM), lambda i: (0, 0)),
                  pl.BlockSpec((CONV_WIDTH, D_CONV), lambda i: (0, 0))],
        out_specs=(pl.BlockSpec((tm, D_CONV), lambda i: (i, 0)),
                   pl.BlockSpec((tm, D_SSM), lambda i: (i, 0))),
        scratch_shapes=[pltpu.VMEM((tm + 8, D_CONV), _F32)],
        compiler_params=pltpu.CompilerParams(
            dimension_semantics=("arbitrary",), vmem_limit_bytes=VMEM_LIMIT_BYTES),
        name="in_proj",
    )(x2, g, w_bcvu, conv_w)


def _ssm_prep_kernel(lr_ref, lc_ref, bm_ref, cm_ref, d_ref, toep_ref, bst_ref, cst_ref, a_ref):
    def discretise(lre, lim, log_dt):
        dt = jnp.exp(log_dt)
        mag = jnp.exp(lre * dt)
        return mag * jnp.cos(lim * dt), mag * jnp.sin(lim * dt)

    def powers(are, aim):
        pre, pim = [jnp.ones_like(are)], [jnp.zeros_like(are)]
        for _ in range(CHUNK):
            pre, pim = (pre + [pre[-1] * are - pim[-1] * aim],
                        pim + [pre[-1] * aim + pim[-1] * are])
        return pre, pim

    lr = lr_ref[0]
    lre, lim = lr[0:1, :], lr[1:2, :]
    are, aim = discretise(lre, lim, lr[2:3, :])
    pre, pim = powers(are, aim)
    den = lre * lre + lim * lim
    q_re = ((are - 1.0) * lre + aim * lim) / den
    q_im = (aim * lre - (are - 1.0) * lim) / den
    bb_re = q_re * bm_ref[0, 0] - q_im * bm_ref[0, 1]
    bb_im = q_re * bm_ref[0, 1] + q_im * bm_ref[0, 0]
    cm_re, cm_im = cm_ref[0, 0], cm_ref[0, 1]
    hi = lax.Precision.HIGHEST
    kblk = []
    for k in range(CHUNK):
        ab_re = bb_re * pre[k] - bb_im * pim[k]
        ab_im = bb_re * pim[k] + bb_im * pre[k]
        rows = slice((CHUNK - 1 - k) * LANES, (CHUNK - k) * LANES)
        bst_ref[0, rows, :SLAB_STATE] = ab_re.astype(_BF16)
        bst_ref[0, rows, SLAB_STATE:] = ab_im.astype(_BF16)
        kblk.append(jnp.dot(ab_re, cm_re, preferred_element_type=_F32, precision=hi)
                    - jnp.dot(ab_im, cm_im, preferred_element_type=_F32, precision=hi))
    r = lax.broadcasted_iota(jnp.int32, (LANES, LANES), 0)
    c = lax.broadcasted_iota(jnp.int32, (LANES, LANES), 1)
    kblk[0] = kblk[0] + jnp.where(r == c, jnp.broadcast_to(d_ref[0], (LANES, LANES)), 0.0)
    kblk = [kb.astype(_BF16) for kb in kblk]
    zeros = jnp.zeros((LANES, LANES), _BF16)
    for sp in range(CHUNK):
        for s in range(CHUNK):
            toep_ref[0, sp * LANES:(sp + 1) * LANES, s * LANES:(s + 1) * LANES] = (
                kblk[s - sp] if s >= sp else zeros)

    lc = lc_ref[0]
    cre, cim = discretise(lc[:, 0:1], lc[:, 1:2], lc[:, 2:3])
    qre, qim = powers(cre, cim)
    for s in range(CHUNK):
        cols = slice(s * LANES, (s + 1) * LANES)
        cst_ref[0, :SLAB_STATE, cols] = (cm_re * qre[s + 1] - cm_im * qim[s + 1]).astype(_BF16)
        cst_ref[0, SLAB_STATE:, cols] = (-(cm_re * qim[s + 1] + cm_im * qre[s + 1])).astype(_BF16)
    a_ref[0, 0:1, :] = pre[CHUNK]
    a_ref[0, 1:2, :] = pim[CHUNK]


def _ssm_tables(lam_re, lam_im, log_dt, b_re, b_im, c_re, c_im, d_skip):
    sg = (N_SLABS, SLAB_GROUPS)
    eye = jnp.eye(SLAB_GROUPS, dtype=_F32)
    lam = jnp.stack([lam_re, lam_im, jnp.broadcast_to(log_dt[:, None], lam_re.shape)], axis=0)
    lam_row = lam.reshape(3, N_SLABS, SLAB_STATE).transpose(1, 0, 2)
    lam_col = lam_row.transpose(0, 2, 1)

    def b_blockdiag(b):
        bt = b.reshape(*sg, SSM_STATE, SSM_GROUP).transpose(0, 1, 3, 2)
        return (bt[:, :, :, None, :] * eye[None, :, None, :, None]).reshape(N_SLABS, LANES, SLAB_STATE)

    def c_blockdiag(c):
        ct = c.reshape(*sg, SSM_GROUP, SSM_STATE).transpose(0, 1, 3, 2)
        return (ct[:, :, :, None, :] * eye[None, :, None, :, None]).reshape(N_SLABS, SLAB_STATE, LANES)

    bm = jnp.stack([b_blockdiag(b_re), b_blockdiag(b_im)], axis=1)
    cm = jnp.stack([c_blockdiag(c_re), c_blockdiag(c_im)], axis=1)
    d = d_skip.reshape(N_SLABS, 1, LANES)
    slab3 = lambda sl: (sl, 0, 0)
    slab4 = lambda sl: (sl, 0, 0, 0)
    return pl.pallas_call(
        _ssm_prep_kernel,
        out_shape=(jax.ShapeDtypeStruct((N_SLABS, FLAT, FLAT), _BF16),
                   jax.ShapeDtypeStruct((N_SLABS, FLAT, 2 * SLAB_STATE), _BF16),
                   jax.ShapeDtypeStruct((N_SLABS, 2 * SLAB_STATE, FLAT), _BF16),
                   jax.ShapeDtypeStruct((N_SLABS, 2, SLAB_STATE), _F32)),
        grid=(N_SLABS,),
        in_specs=[pl.BlockSpec((1, 3, SLAB_STATE), slab3),
                  pl.BlockSpec((1, SLAB_STATE, 3), slab3),
                  pl.BlockSpec((1, 2, LANES, SLAB_STATE), slab4),
                  pl.BlockSpec((1, 2, SLAB_STATE, LANES), slab4),
                  pl.BlockSpec((1, 1, LANES), slab3)],
        out_specs=(pl.BlockSpec((1, FLAT, FLAT), slab3),
                   pl.BlockSpec((1, FLAT, 2 * SLAB_STATE), slab3),
                   pl.BlockSpec((1, 2 * SLAB_STATE, FLAT), slab3),
                   pl.BlockSpec((1, 2, SLAB_STATE), slab3)),
        compiler_params=pltpu.CompilerParams(
            dimension_semantics=("arbitrary",), vmem_limit_bytes=VMEM_LIMIT_BYTES),
        name="ssm_prep",
    )(lam_row, lam_col, bm, cm, d)


def _ssm_kernel(u_ref, toep_ref, bst_ref, cst_ref, a_ref, y_ref, uflat, s_scr, xc_scr, carry, ytoep):
    nb, tt, _ = u_ref.shape
    nch = tt // CHUNK
    n = nb * nch

    @pl.when(pl.program_id(1) == 0)
    def _():
        carry[...] = jnp.zeros_like(carry)

    for s in range(CHUNK):
        part = u_ref[:, pl.ds(s, nch, stride=CHUNK), :]
        uflat[:, s * LANES:(s + 1) * LANES] = part.reshape(n, LANES).astype(_BF16)

    n_cb = FLAT // MXU_DIM

    def toeplitz(cb):
        kk = (cb + 1) * MXU_DIM
        cols = slice(cb * MXU_DIM, kk)
        ytoep[:, cols] = jnp.dot(uflat[:, :kk], toep_ref[0, :kk, cols], preferred_element_type=_F32)

    for cb in range(n_cb // 2):
        toeplitz(cb)

    nblk = SLAB_STATE // LANES
    loc_all = jnp.dot(uflat[...], bst_ref[0], preferred_element_type=_F32)
    for cb in range(n_cb // 2, n_cb):
        toeplitz(cb)
    for kb in range(2 * nblk):
        s_scr[kb] = loc_all[:, kb * LANES:(kb + 1) * LANES]

    a = a_ref[0]
    are = [jnp.broadcast_to(a[0:1, kb * LANES:(kb + 1) * LANES], (nb, LANES)) for kb in range(nblk)]
    aim = [jnp.broadcast_to(a[1:2, kb * LANES:(kb + 1) * LANES], (nb, LANES)) for kb in range(nblk)]
    xr = [carry[kb] for kb in range(nblk)]
    xi = [carry[nblk + kb] for kb in range(nblk)]
    for j in range(nch):
        rows = pl.ds(j, nb, stride=nch)
        for kb in range(nblk):
            xc_scr[kb, rows, :] = xr[kb]
            xc_scr[nblk + kb, rows, :] = xi[kb]
            nr = are[kb] * xr[kb] - aim[kb] * xi[kb] + s_scr[kb, rows, :]
            ni = are[kb] * xi[kb] + aim[kb] * xr[kb] + s_scr[nblk + kb, rows, :]
            xr[kb], xi[kb] = nr, ni
    for kb in range(nblk):
        carry[kb] = xr[kb]
        carry[nblk + kb] = xi[kb]

    xc = jnp.concatenate([xc_scr[kb] for kb in range(2 * nblk)], axis=1).astype(_BF16)
    for cb in range(n_cb):
        cols = slice(cb * MXU_DIM, (cb + 1) * MXU_DIM)
        y = ytoep[:, cols] + jnp.dot(xc, cst_ref[0, :, cols], preferred_element_type=_F32)
        y = jax.nn.gelu(y)
        for h in range(MXU_DIM // LANES):
            s = cb * (MXU_DIM // LANES) + h
            y_ref[:, pl.ds(s, nch, stride=CHUNK), :] = (
                y[:, h * LANES:(h + 1) * LANES].reshape(nb, nch, LANES))


def _ssm(u3, tables):
    toep, bst, cst, a_chunk = tables
    nb, seq, _ = u3.shape
    tt = SSM_TIME_TILE
    n = nb * (tt // CHUNK)
    return pl.pallas_call(
        _ssm_kernel,
        out_shape=jax.ShapeDtypeStruct(u3.shape, _F32),
        grid=(N_SLABS, seq // tt),
        in_specs=[pl.BlockSpec((nb, tt, LANES), lambda sl, ti: (0, ti, sl)),
                  pl.BlockSpec((1, FLAT, FLAT), lambda sl, ti: (sl, 0, 0)),
                  pl.BlockSpec((1, FLAT, 2 * SLAB_STATE), lambda sl, ti: (sl, 0, 0)),
                  pl.BlockSpec((1, 2 * SLAB_STATE, FLAT), lambda sl, ti: (sl, 0, 0)),
                  pl.BlockSpec((1, 2, SLAB_STATE), lambda sl, ti: (sl, 0, 0))],
        out_specs=pl.BlockSpec((nb, tt, LANES), lambda sl, ti: (0, ti, sl)),
        scratch_shapes=[pltpu.VMEM((n, FLAT), _BF16),
                        pltpu.VMEM((2 * SLAB_STATE // LANES, n, LANES), _F32),
                        pltpu.VMEM((2 * SLAB_STATE // LANES, n, LANES), _F32),
                        pltpu.VMEM((2 * SLAB_STATE // LANES, nb, LANES), _F32),
                        pltpu.VMEM((n, FLAT), _F32)],
        compiler_params=pltpu.CompilerParams(
            dimension_semantics=("arbitrary", "arbitrary"), vmem_limit_bytes=VMEM_LIMIT_BYTES),
        name="ssm",
    )(u3, toep, bst, cst, a_chunk)


def _mix_route_kernel(x_ref, bz_ref, yg_ref, gm_ref, wg_ref, wco_ref, wglu_ref, wout_ref,
                      gf_ref, wr_ref, br_ref,
                      x1_ref, h_ref, ti_ref, tg_ref, rk_ref, cnt_ref, base, merged):
    tm = x_ref.shape[0]

    @pl.when(pl.program_id(0) == 0)
    def _():
        base[...] = jnp.zeros_like(base)

    chains = [slice(r0, r0 + ROW_CHAIN) for r0 in range(0, tm, ROW_CHAIN)]
    hs = [_mix_rows(rows, x_ref, bz_ref, yg_ref, gm_ref, wg_ref, wco_ref, wglu_ref, wout_ref,
                    gf_ref, x1_ref, h_ref, merged) for rows in chains]
    for rows, h in zip(chains, hs):
        _route_rows(rows, h, wr_ref, br_ref, ti_ref, tg_ref, rk_ref, base)
    cnt_ref[...] = base[...].astype(jnp.int32)


def _mix_rows(rows, x_ref, bz_ref, yg_ref, gm_ref, wg_ref, wco_ref, wglu_ref, wout_ref,
              gf_ref, x1_ref, h_ref, merged):
    x = x_ref[rows, :]
    xn = _rmsnorm(x, gm_ref[...]).astype(_BF16)
    bz = bz_ref[rows, :]
    yg = yg_ref[rows, :].astype(_BF16)
    for c in range(D_MODEL // MXU_DIM):
        lo = slice(c * MXU_DIM, (c + 1) * MXU_DIM)
        hi = slice(D_MODEL + c * MXU_DIM, D_MODEL + (c + 1) * MXU_DIM)
        gate_a = jnp.dot(xn, wg_ref[:, lo], preferred_element_type=_F32)
        gate_b = jnp.dot(xn, wg_ref[:, hi], preferred_element_type=_F32)
        y_a = jnp.dot(bz, wco_ref[:, lo], preferred_element_type=_F32)
        val = jnp.dot(yg, wglu_ref[:, lo], preferred_element_type=_F32)
        glu_gate = jnp.dot(yg, wglu_ref[:, hi], preferred_element_type=_F32)
        y_b = val * _sigmoid(glu_gate)
        merged[rows, lo] = (_sigmoid(gate_a) * y_a + _sigmoid(gate_b) * y_b).astype(_BF16)
    x1 = x + jnp.dot(merged[rows, :], wout_ref[...], preferred_element_type=_F32)
    x1_ref[rows, :] = x1
    h = _rmsnorm(x1, gf_ref[...])
    h_ref[rows, :] = _pack_bf16_halves(h)
    return h.astype(_BF16)


def _route_rows(rows, h, wr_ref, br_ref, ti_ref, tg_ref, rk_ref, base):
    tm = rows.stop - rows.start
    logits_tok = jnp.dot(h, wr_ref[...], preferred_element_type=_F32)
    logits = jnp.transpose(logits_tok)[:N_EXPERTS, :] + br_ref[...]
    erow = lax.broadcasted_iota(jnp.int32, (N_EXPERTS, tm), 0).astype(_F32)
    neg_inf = jnp.float32(-jnp.inf)
    work = logits
    vals, idxs = [], []
    for _ in range(TOP_K):
        m = jnp.max(work, axis=0, keepdims=True)
        idx = jnp.min(jnp.where(work == m, erow, float(N_EXPERTS)), axis=0, keepdims=True)
        vals.append(m)
        idxs.append(idx)
        work = jnp.where(erow == idx, neg_inf, work)
    exps = [jnp.exp(v - vals[0]) for v in vals]
    denom = exps[0] + exps[1] + exps[2] + exps[3]

    sel = jnp.zeros((N_EXPERTS, tm), _F32)
    for idx in idxs:
        sel = sel + (erow == idx).astype(_F32)
    row = lax.broadcasted_iota(jnp.int32, (tm, tm), 0)
    col = lax.broadcasted_iota(jnp.int32, (tm, tm), 1)
    earlier = (row < col).astype(_BF16)
    before = jnp.dot(sel.astype(_BF16), earlier, preferred_element_type=_F32) + base[...]
    for k in range(TOP_K):
        ti_ref[k:k + 1, rows] = idxs[k].astype(jnp.int32)
        tg_ref[k:k + 1, rows] = exps[k] / denom
        tg_ref[TOP_K + k:TOP_K + k + 1, rows] = jnp.zeros((1, tm), _F32)
        rk = jnp.sum(jnp.where(erow == idxs[k], before, 0.0), axis=0, keepdims=True)
        rk_ref[k:k + 1, rows] = rk.astype(jnp.int32)
    base[...] = base[...] + jnp.sum(sel, axis=1, keepdims=True)


def _mix_route(x2, bz, yg, g_mix, w_gates, w_conv_out, w_glu, w_out, g_ffn, w_router, b_router):
    t = x2.shape[0]
    tm = TOKEN_TILE
    tok = lambda i: (i, 0)
    tok_lanes = lambda i: (0, i)
    fixed = lambda i: (0, 0)
    return pl.pallas_call(
        _mix_route_kernel,
        out_shape=(jax.ShapeDtypeStruct((t, D_MODEL), _F32),
                   jax.ShapeDtypeStruct((t, D_MODEL // 2), jnp.uint32),
                   jax.ShapeDtypeStruct((TOP_K, t), jnp.int32),
                   jax.ShapeDtypeStruct((2 * TOP_K, t), _F32),
                   jax.ShapeDtypeStruct((TOP_K, t), jnp.int32),
                   jax.ShapeDtypeStruct((N_EXPERTS, 1), jnp.int32)),
        grid=(t // tm,),
        in_specs=[pl.BlockSpec((tm, D_MODEL), tok),
                  pl.BlockSpec((tm, D_CONV), tok),
                  pl.BlockSpec((tm, D_SSM), tok),
                  pl.BlockSpec((1, D_MODEL), fixed),
                  pl.BlockSpec((D_MODEL, 2 * D_MODEL), fixed),
                  pl.BlockSpec((D_CONV, D_MODEL), fixed),
                  pl.BlockSpec((D_SSM, 2 * D_MODEL), fixed),
                  pl.BlockSpec((D_MODEL, D_MODEL), fixed),
                  pl.BlockSpec((1, D_MODEL), fixed),
                  pl.BlockSpec((D_MODEL, LANES), fixed),
                  pl.BlockSpec((N_EXPERTS, 1), fixed)],
        out_specs=(pl.BlockSpec((tm, D_MODEL), tok),
                   pl.BlockSpec((tm, D_MODEL // 2), tok),
                   pl.BlockSpec((TOP_K, tm), tok_lanes),
                   pl.BlockSpec((2 * TOP_K, tm), tok_lanes),
                   pl.BlockSpec((TOP_K, tm), tok_lanes),
                   pl.BlockSpec((N_EXPERTS, 1), fixed)),
        scratch_shapes=[pltpu.VMEM((N_EXPERTS, 1), _F32),
                        pltpu.VMEM((tm, D_MODEL), _BF16)],
        compiler_params=pltpu.CompilerParams(
            dimension_semantics=("arbitrary",), vmem_limit_bytes=VMEM_LIMIT_BYTES),
        name="mix_route",
    )(x2, bz, yg, g_mix, w_gates, w_conv_out, w_glu, w_out, g_ffn, w_router, b_router)


def _expert_ffn_kernel(be_ref, nr_ref, x_ref, wgu_ref, bgu_ref, wd_ref, bd_ref, y_ref, wgu_b, wd_b):
    b = pl.program_id(0)

    @pl.when((b == 0) | (be_ref[b] != be_ref[jnp.maximum(b - 1, 0)]))
    def _():
        wgu_b[...] = wgu_ref[0].astype(_BF16)
        wd_b[...] = wd_ref[0].astype(_BF16)

    @pl.when(nr_ref[b] > 0)
    def _():
        live = lax.broadcasted_iota(jnp.int32, x_ref.shape, 0) < nr_ref[b]
        x = _unpack_bf16_halves(jnp.where(live, x_ref[...], jnp.uint32(0))).astype(_BF16)
        hgu = jnp.dot(x, wgu_b[...], preferred_element_type=_F32) + bgu_ref[0]
        g = jnp.minimum(hgu[:, :D_FF], SWIGLU_LIMIT)
        up = jnp.clip(hgu[:, D_FF:], -SWIGLU_LIMIT, SWIGLU_LIMIT)
        act = (up + 1.0) * (g * _sigmoid(SWIGLU_ALPHA * g))
        y = jnp.dot(act.astype(_BF16), wd_b[...], preferred_element_type=_F32) + bd_ref[0]
        y_ref[...] = _pack_bf16_halves(y)

    @pl.when(nr_ref[b] == 0)
    def _():
        y_ref[...] = jnp.zeros_like(y_ref)


def _expert_ffn(block_e, block_rows, x_rows, w_gate_up, b_gate_up, w_down, b_down):
    n_rows = x_rows.shape[0]
    n_blocks = n_rows // MOE_BLOCK

    def wmap(b, be, nr):
        return (be[b], 0, 0)

    grid_spec = pltpu.PrefetchScalarGridSpec(
        num_scalar_prefetch=2,
        grid=(n_blocks,),
        in_specs=[pl.BlockSpec((MOE_BLOCK, D_MODEL // 2), lambda b, be, nr: (b, 0)),
                  pl.BlockSpec((1, D_MODEL, 2 * D_FF), wmap),
                  pl.BlockSpec((1, 1, 2 * D_FF), wmap),
                  pl.BlockSpec((1, D_FF, D_MODEL), wmap),
                  pl.BlockSpec((1, 1, D_MODEL), wmap)],
        out_specs=pl.BlockSpec((MOE_BLOCK, D_MODEL // 2), lambda b, be, nr: (b, 0)),
        scratch_shapes=[pltpu.VMEM((D_MODEL, 2 * D_FF), _BF16),
                        pltpu.VMEM((D_FF, D_MODEL), _BF16)],
    )
    return pl.pallas_call(
        _expert_ffn_kernel,
        out_shape=jax.ShapeDtypeStruct((n_rows, D_MODEL // 2), jnp.uint32),
        grid_spec=grid_spec,
        compiler_params=pltpu.CompilerParams(
            dimension_semantics=("arbitrary",), vmem_limit_bytes=VMEM_LIMIT_BYTES),
        name="expert_ffn",
    )(block_e, block_rows, x_rows, w_gate_up, b_gate_up, w_down, b_down)


def _sc_workers():
    info = plsc.get_sparse_core_info()
    return info.num_cores, info.num_cores * info.num_subcores


def _dispatch(h_packed, dest_flat, n_rows):
    t, width = h_packed.shape
    n_cores, n_workers = _sc_workers()
    per_w = t // n_workers
    n_chunks = per_w // SC_ROWS

    @functools.partial(
        pl.kernel, mesh=plsc.VectorSubcoreMesh(core_axis_name="c", subcore_axis_name="s"),
        out_type=jax.ShapeDtypeStruct((n_rows, width), h_packed.dtype),
        scratch_types=[pltpu.VMEM((TOP_K, SC_ROWS), jnp.int32),
                       pltpu.VMEM((SC_ROWS, width), h_packed.dtype),
                       pltpu.SemaphoreType.DMA])
    def scatter_rows(h_hbm, dest_hbm, out_hbm, idx_v, rows_v, sem):
        wid = lax.axis_index("s") * n_cores + lax.axis_index("c")

        @pl.loop(0, n_chunks)
        def _(ci):
            off = wid * per_w + ci * SC_ROWS
            pltpu.sync_copy(h_hbm.at[pl.ds(off, SC_ROWS)], rows_v)
            for k in range(TOP_K):
                pltpu.sync_copy(dest_hbm.at[pl.ds(k * t + off, SC_ROWS)], idx_v.at[k])
            copies = [pltpu.async_copy(rows_v, out_hbm.at[idx_v.at[k]], sem) for k in range(TOP_K)]
            for cp in copies:
                cp.wait()

    return scatter_rows(h_packed, dest_flat)


def _collect(y_rows, dest_flat):
    n_idx = dest_flat.shape[0]
    width = y_rows.shape[1]
    n_cores, n_workers = _sc_workers()
    n_chunks = n_idx // (n_workers * SC_ROWS)
    assert n_chunks % 2 == 0

    @functools.partial(
        pl.kernel, mesh=plsc.VectorSubcoreMesh(core_axis_name="c", subcore_axis_name="s"),
        out_type=jax.ShapeDtypeStruct((n_idx, width), y_rows.dtype),
        scratch_types=[pltpu.VMEM((n_chunks, SC_ROWS), jnp.int32),
                       pltpu.VMEM((2, SC_ROWS, width), y_rows.dtype),
                       pltpu.SemaphoreType.DMA((2,)),
                       pltpu.SemaphoreType.DMA((2,))])
    def gather_rows(y_hbm, dest_hbm, out_hbm, idx_v, buf, gsem, wsem):
        wid = lax.axis_index("s") * n_cores + lax.axis_index("c")
        c0 = wid * n_chunks
        pltpu.sync_copy(dest_hbm.at[pl.ds(c0, n_chunks)], idx_v)

        def gather(c, b):
            return pltpu.make_async_copy(y_hbm.at[idx_v.at[c]], buf.at[b], gsem.at[b])

        def write(c, b):
            return pltpu.make_async_copy(buf.at[b], out_hbm.at[pl.ds((c0 + c) * SC_ROWS, SC_ROWS)],
                                         wsem.at[b])

        gather(0, 0).start()

        @pl.loop(0, n_chunks, step=2)
        def _(ci):
            for b in range(2):
                c = ci + b

                @pl.when(c >= 1)
                def _():
                    write(c - 1, 1 - b).wait()

                @pl.when(c + 1 < n_chunks)
                def _():
                    gather(c + 1, 1 - b).start()

                gather(c, b).wait()
                write(c, b).start()

        write(n_chunks - 1, 1).wait()

    return gather_rows(y_rows, dest_flat.reshape(n_idx // SC_ROWS, SC_ROWS))


def _combine_kernel(x1_ref, ya_ref, tg_ref, g_ref, *rest):
    o_ref = rest[-1]
    acc = x1_ref[...]
    tg = jnp.transpose(tg_ref[...])
    for k in range(TOP_K):
        acc = acc + tg[:, k:k + 1] * _unpack_bf16_halves(ya_ref[k])
    o_ref[...] = _rmsnorm(acc, g_ref[...])


def _combine(x1, y_group, top_g, g_final, group, out_prev):
    t = x1.shape[0]
    tm = TOKEN_TILE
    steps = y_group.shape[1] // tm
    first = group * steps
    in_specs = [pl.BlockSpec((tm, D_MODEL), lambda i: (first + i, 0)),
                pl.BlockSpec((TOP_K, tm, D_MODEL // 2), lambda i: (0, i, 0)),
                pl.BlockSpec((2 * TOP_K, tm), lambda i: (0, first + i)),
                pl.BlockSpec((1, D_MODEL), lambda i: (0, 0))]
    args = [x1, y_group, top_g, g_final]
    aliases = {}
    if out_prev is not None:
        in_specs.append(pl.BlockSpec(memory_space=pl.ANY))
        args.append(out_prev)
        aliases = {len(args) - 1: 0}
    return pl.pallas_call(
        _combine_kernel,
        out_shape=jax.ShapeDtypeStruct((t, D_MODEL), _F32),
        grid=(steps,),
        in_specs=in_specs,
        out_specs=pl.BlockSpec((tm, D_MODEL), lambda i: (first + i, 0)),
        input_output_aliases=aliases,
        compiler_params=pltpu.CompilerParams(
            dimension_semantics=("arbitrary",), vmem_limit_bytes=VMEM_LIMIT_BYTES),
        name=f"combine{group}",
    )(*args)


def kernel(x, norm_mix_g, w_in, conv_w, w_conv_out, ssm_lam_re, ssm_lam_im, ssm_log_dt, ssm_b_re, ssm_b_im, ssm_c_re, ssm_c_im, ssm_d, w_glu, w_out, norm_ffn_g, w_router, b_router, w_gate_up, b_gate_up, w_down, b_down, norm_f_g):
    bsz, seq, d = x.shape
    t = bsz * seq
    x2 = x.reshape(t, d)
    assert seq % TOKEN_TILE == 0 and seq % SSM_TIME_TILE == 0 and w_in.shape[0] == 1

    w_in_b = w_in[0].astype(_BF16)
    n_bcvu = 3 * D_CONV + D_SSM
    g_mix = norm_mix_g[0].reshape(1, d)

    bz, u = _in_proj(x2, g_mix, w_in_b[:, :n_bcvu], conv_w[0], seq)

    tables = _ssm_tables(ssm_lam_re[0], ssm_lam_im[0], ssm_log_dt[0], ssm_b_re[0], ssm_b_im[0],
                         ssm_c_re[0], ssm_c_im[0], ssm_d[0])
    yg = _ssm(u.reshape(bsz, seq, D_SSM), tables).reshape(t, D_SSM)

    x1, h_packed, top_i, top_g, rank, counts = _mix_route(
        x2, bz, yg, g_mix, w_in_b[:, n_bcvu:], w_conv_out[0].astype(_BF16),
        w_glu[0].astype(_BF16), w_out[0].astype(_BF16), norm_ffn_g[0].reshape(1, d),
        jnp.pad(w_router[0], ((0, 0), (0, LANES - N_EXPERTS))).astype(_BF16),
        b_router[0].reshape(N_EXPERTS, 1))

    counts = counts[:, 0]
    padded = ((counts + MOE_BLOCK - 1) // MOE_BLOCK) * MOE_BLOCK
    pad_end = jnp.cumsum(padded)
    pad_start = pad_end - padded
    n_rows = t * TOP_K + N_EXPERTS * MOE_BLOCK
    n_blocks = n_rows // MOE_BLOCK
    expert_ids = jnp.arange(N_EXPERTS, dtype=jnp.int32)[:, None, None]
    row_start = jnp.sum(jnp.where(top_i[None] == expert_ids, pad_start[:, None, None], 0), axis=0)
    dest = (row_start + rank).reshape(TOP_K * t)
    block_start = jnp.arange(n_blocks, dtype=jnp.int32) * MOE_BLOCK
    block_e = jnp.minimum(jnp.sum(pad_end[None, :] <= block_start[:, None], axis=1),
                          N_EXPERTS - 1).astype(jnp.int32)
    block_rows = jnp.clip(pad_start[block_e] + counts[block_e] - block_start, 0, MOE_BLOCK)
    block_rows = block_rows.astype(jnp.int32)

    x_rows = _dispatch(h_packed, dest, n_rows)
    y_rows = _expert_ffn(block_e, block_rows, x_rows, w_gate_up[0],
                         b_gate_up[0].reshape(N_EXPERTS, 1, 2 * D_FF), w_down[0],
                         b_down[0].reshape(N_EXPERTS, 1, D_MODEL))
    tq = t // COLLECT_GROUPS
    dest_kt = dest.reshape(TOP_K, t)
    out = None
    for q in range(COLLECT_GROUPS):
        dest_q = dest_kt[:, q * tq:(q + 1) * tq].reshape(TOP_K * tq)
        y_q = _collect(y_rows, dest_q).reshape(TOP_K, tq, D_MODEL // 2)
        out = _combine(x1, y_q, top_g, norm_f_g.reshape(1, d), q, out)
    return out.reshape(bsz, seq, d)
```

```python
import functools

import jax
import jax.numpy as jnp
from jax import lax
from jax.experimental import pallas as pl
from jax.experimental.pallas import tpu as pltpu
from jax.experimental.pallas import tpu_sc as plsc

D_MODEL = 1024
D_CONV = 512
CONV_WIDTH = 3
D_SSM = 512
SSM_GROUP = 16
N_SSM_GROUPS = 32
SSM_STATE = 64
N_EXPERTS = 32
TOP_K = 4
D_FF = 1024
SWIGLU_LIMIT = 7.0
SWIGLU_ALPHA = 1.702
MOE_BLOCK = 512
RMS_EPS = 1e-6

LANES = 128
MXU_DIM = 256
CHUNK = 16
SLAB_GROUPS = LANES // SSM_GROUP
N_SLABS = N_SSM_GROUPS // SLAB_GROUPS
SLAB_STATE = SLAB_GROUPS * SSM_STATE
FLAT = CHUNK * LANES
SSM_TIME_TILE = 256
TOKEN_TILE = 512
ROW_CHAIN = 256
SC_ROWS = 64
VMEM_LIMIT_BYTES = 56 * 1024 * 1024

_BF16 = jnp.bfloat16
_F32 = jnp.float32


def _rmsnorm(xf, g):
    return xf * lax.rsqrt(jnp.mean(xf * xf, axis=-1, keepdims=True) + RMS_EPS) * g


def _sigmoid(v):
    return 1.0 / (1.0 + jnp.exp(-v))


def _pack_bf16_halves(v):
    n = v.shape[1] // 2
    bits = pltpu.bitcast(v.astype(_BF16).astype(_F32), jnp.uint32)
    return (bits[:, :n] >> 16) | (bits[:, n:] & jnp.uint32(0xFFFF0000))


def _unpack_bf16_halves(w):
    return jnp.concatenate([pltpu.bitcast(w << 16, _F32),
                            pltpu.bitcast(w & jnp.uint32(0xFFFF0000), _F32)], axis=1)


def _in_proj_kernel(tiles_per_seq, x_ref, g_ref, w_ref, cw_ref, bz_ref, u_ref, hbuf):
    tm = x_ref.shape[0]
    i = pl.program_id(0)

    @pl.when(i % tiles_per_seq == 0)
    def _():
        hbuf[0:8, :] = jnp.zeros((8, D_CONV), _F32)

    xn = _rmsnorm(x_ref[...], g_ref[...]).astype(_BF16)
    cv = jnp.dot(xn, w_ref[:, D_CONV:3 * D_CONV], preferred_element_type=_F32)
    hbuf[8:8 + tm, :] = cv[:, :D_CONV] * cv[:, D_CONV:]
    u_ref[...] = jnp.dot(xn, w_ref[:, 3 * D_CONV:], preferred_element_type=_F32)
    cw = cw_ref[...]
    z = (cw[0:1, :] * hbuf[6:6 + tm, :] + cw[1:2, :] * hbuf[7:7 + tm, :]
         + cw[2:3, :] * hbuf[8:8 + tm, :])
    b_gate = jnp.dot(xn, w_ref[:, :D_CONV], preferred_element_type=_F32)
    bz_ref[...] = (b_gate * z).astype(_BF16)
    hbuf[0:8, :] = hbuf[tm:tm + 8, :]


def _in_proj(x2, g, w_bcvu, conv_w, seq):
    t = x2.shape[0]
    tm = TOKEN_TILE
    return pl.pallas_call(
        functools.partial(_in_proj_kernel, seq // tm),
        out_shape=(jax.ShapeDtypeStruct((t, D_CONV), _BF16),
                   jax.ShapeDtypeStruct((t, D_SSM), _F32)),
        grid=(t // tm,),
        in_specs=[pl.BlockSpec((tm, D_MODEL), lambda i: (i, 0)),
                  pl.BlockSpec((1, D_MODEL), lambda i: (0, 0)),
                  pl.BlockSpec((D_MODEL, 3 * D_CONV + D_SSM), lambda i: (0, 0)),
                  pl.BlockSpec((CONV_WIDTH, D_CONV), lambda i: (0, 0))],
        out_specs=(pl.BlockSpec((tm, D_CONV), lambda i: (i, 0)),
                   pl.BlockSpec((tm, D_SSM), lambda i: (i, 0))),
        scratch_shapes=[pltpu.VMEM((tm + 8, D_CONV), _F32)],
        compiler_params=pltpu.CompilerParams(
            dimension_semantics=("arbitrary",), vmem_limit_bytes=VMEM_LIMIT_BYTES),
        name="in_proj",
    )(x2, g, w_bcvu, conv_w)


def _ssm_prep_kernel(lr_ref, lc_ref, bm_ref, cm_ref, d_ref, toep_ref, bst_ref, cst_ref, a_ref):
    def discretise(lre, lim, log_dt):
        dt = jnp.exp(log_dt)
        mag = jnp.exp(lre * dt)
        return mag * jnp.cos(lim * dt), mag * jnp.sin(lim * dt)

    def powers(are, aim):
        pre, pim = [jnp.ones_like(are)], [jnp.zeros_like(are)]
        for _ in range(CHUNK):
            pre, pim = (pre + [pre[-1] * are - pim[-1] * aim],
                        pim + [pre[-1] * aim + pim[-1] * are])
        return pre, pim

    lr = lr_ref[0]
    lre, lim = lr[0:1, :], lr[1:2, :]
    are, aim = discretise(lre, lim, lr[2:3, :])
    pre, pim = powers(are, aim)
    den = lre * lre + lim * lim
    q_re = ((are - 1.0) * lre + aim * lim) / den
    q_im = (aim * lre - (are - 1.0) * lim) / den
    bb_re = q_re * bm_ref[0, 0] - q_im * bm_ref[0, 1]
    bb_im = q_re * bm_ref[0, 1] + q_im * bm_ref[0, 0]
    cm_re, cm_im = cm_ref[0, 0], cm_ref[0, 1]
    hi = lax.Precision.HIGHEST
    kblk = []
    for k in range(CHUNK):
        ab_re = bb_re * pre[k] - bb_im * pim[k]
        ab_im = bb_re * pim[k] + bb_im * pre[k]
        rows = slice((CHUNK - 1 - k) * LANES, (CHUNK - k) * LANES)
        bst_ref[0, rows, :SLAB_STATE] = ab_re.astype(_BF16)
        bst_ref[0, rows, SLAB_STATE:] = ab_im.astype(_BF16)
        kblk.append(jnp.dot(ab_re, cm_re, preferred_element_type=_F32, precision=hi)
                    - jnp.dot(ab_im, cm_im, preferred_element_type=_F32, precision=hi))
    r = lax.broadcasted_iota(jnp.int32, (LANES, LANES), 0)
    c = lax.broadcasted_iota(jnp.int32, (LANES, LANES), 1)
    kblk[0] = kblk[0] + jnp.where(r == c, jnp.broadcast_to(d_ref[0], (LANES, LANES)), 0.0)
    kblk = [kb.astype(_BF16) for kb in kblk]
    zeros = jnp.zeros((LANES, LANES), _BF16)
    for sp in range(CHUNK):
        for s in range(CHUNK):
            toep_ref[0, sp * LANES:(sp + 1) * LANES, s * LANES:(s + 1) * LANES] = (
                kblk[s - sp] if s >= sp else zeros)

    lc = lc_ref[0]
    cre, cim = discretise(lc[:, 0:1], lc[:, 1:2], lc[:, 2:3])
    qre, qim = powers(cre, cim)
    for s in range(CHUNK):
        cols = slice(s * LANES, (s + 1) * LANES)
        cst_ref[0, :SLAB_STATE, cols] = (cm_re * qre[s + 1] - cm_im * qim[s + 1]).astype(_BF16)
        cst_ref[0, SLAB_STATE:, cols] = (-(cm_re * qim[s + 1] + cm_im * qre[s + 1])).astype(_BF16)
    a_ref[0, 0:1, :] = pre[CHUNK]
    a_ref[0, 1:2, :] = pim[CHUNK]


def _ssm_tables(lam_re, lam_im, log_dt, b_re, b_im, c_re, c_im, d_skip):
    sg = (N_SLABS, SLAB_GROUPS)
    eye = jnp.eye(SLAB_GROUPS, dtype=_F32)
    lam = jnp.stack([lam_re, lam_im, jnp.broadcast_to(log_dt[:, None], lam_re.shape)], axis=0)
    lam_row = lam.reshape(3, N_SLABS, SLAB_STATE).transpose(1, 0, 2)
    lam_col = lam_row.transpose(0, 2, 1)

    def b_blockdiag(b):
        bt = b.reshape(*sg, SSM_STATE, SSM_GROUP).transpose(0, 1, 3, 2)
        return (bt[:, :, :, None, :] * eye[None, :, None, :, None]).reshape(N_SLABS, LANES, SLAB_STATE)

    def c_blockdiag(c):
        ct = c.reshape(*sg, SSM_GROUP, SSM_STATE).transpose(0, 1, 3, 2)
        return (ct[:, :, :, None, :] * eye[None, :, None, :, None]).reshape(N_SLABS, SLAB_STATE, LANES)

    bm = jnp.stack([b_blockdiag(b_re), b_blockdiag(b_im)], axis=1)
    cm = jnp.stack([c_blockdiag(c_re), c_blockdiag(c_im)], axis=1)
    d = d_skip.reshape(N_SLABS, 1, LANES)
    slab3 = lambda sl: (sl, 0, 0)
    slab4 = lambda sl: (sl, 0, 0, 0)
    return pl.pallas_call(
        _ssm_prep_kernel,
        out_shape=(jax.ShapeDtypeStruct((N_SLABS, FLAT, FLAT), _BF16),
                   jax.ShapeDtypeStruct((N_SLABS, FLAT, 2 * SLAB_STATE), _BF16),
                   jax.ShapeDtypeStruct((N_SLABS, 2 * SLAB_STATE, FLAT), _BF16),
                   jax.ShapeDtypeStruct((N_SLABS, 2, SLAB_STATE), _F32)),
        grid=(N_SLABS,),
        in_specs=[pl.BlockSpec((1, 3, SLAB_STATE), slab3),
                  pl.BlockSpec((1, SLAB_STATE, 3), slab3),
                  pl.BlockSpec((1, 2, LANES, SLAB_STATE), slab4),
                  pl.BlockSpec((1, 2, SLAB_STATE, LANES), slab4),
                  pl.BlockSpec((1, 1, LANES), slab3)],
        out_specs=(pl.BlockSpec((1, FLAT, FLAT), slab3),
                   pl.BlockSpec((1, FLAT, 2 * SLAB_STATE), slab3),
                   pl.BlockSpec((1, 2 * SLAB_STATE, FLAT), slab3),
                   pl.BlockSpec((1, 2, SLAB_STATE), slab3)),
        compiler_params=pltpu.CompilerParams(
            dimension_semantics=("arbitrary",), vmem_limit_bytes=VMEM_LIMIT_BYTES),
        name="ssm_prep",
    )(lam_row, lam_col, bm, cm, d)


def _ssm_kernel(u_ref, toep_ref, bst_ref, cst_ref, a_ref, y_ref, uflat, s_scr, xc_scr, carry, ytoep):
    nb, tt, _ = u_ref.shape
    nch = tt // CHUNK
    n = nb * nch

    @pl.when(pl.program_id(1) == 0)
    def _():
        carry[...] = jnp.zeros_like(carry)

    for s in range(CHUNK):
        part = u_ref[:, pl.ds(s, nch, stride=CHUNK), :]
        uflat[:, s * LANES:(s + 1) * LANES] = part.reshape(n, LANES).astype(_BF16)

    n_cb = FLAT // MXU_DIM

    def toeplitz(cb):
        kk = (cb + 1) * MXU_DIM
        cols = slice(cb * MXU_DIM, kk)
        ytoep[:, cols] = jnp.dot(uflat[:, :kk], toep_ref[0, :kk, cols], preferred_element_type=_F32)

    for cb in range(n_cb // 2):
        toeplitz(cb)

    nblk = SLAB_STATE // LANES
    loc_all = jnp.dot(uflat[...], bst_ref[0], preferred_element_type=_F32)
    for cb in range(n_cb // 2, n_cb):
        toeplitz(cb)
    for kb in range(2 * nblk):
        s_scr[kb] = loc_all[:, kb * LANES:(kb + 1) * LANES]

    a = a_ref[0]
    are = [jnp.broadcast_to(a[0:1, kb * LANES:(kb + 1) * LANES], (nb, LANES)) for kb in range(nblk)]
    aim = [jnp.broadcast_to(a[1:2, kb * LANES:(kb + 1) * LANES], (nb, LANES)) for kb in range(nblk)]
    xr = [carry[kb] for kb in range(nblk)]
    xi = [carry[nblk + kb] for kb in range(nblk)]
    for j in range(nch):
        rows = pl.ds(j, nb, stride=nch)
        for kb in range(nblk):
            xc_scr[kb, rows, :] = xr[kb]
            xc_scr[nblk + kb, rows, :] = xi[kb]
            nr = are[kb] * xr[kb] - aim[kb] * xi[kb] + s_scr[kb, rows, :]
            ni = are[kb] * xi[kb] + aim[kb] * xr[kb] + s_scr[nblk + kb, rows, :]
            xr[kb], xi[kb] = nr, ni
    for kb in range(nblk):
        carry[kb] = xr[kb]
        carry[nblk + kb] = xi[kb]

    xc = jnp.concatenate([xc_scr[kb] for kb in range(2 * nblk)], axis=1).astype(_BF16)
    for cb in range(n_cb):
        cols = slice(cb * MXU_DIM, (cb + 1) * MXU_DIM)
        y = ytoep[:, cols] + jnp.dot(xc, cst_ref[0, :, cols], preferred_element_type=_F32)
        y = jax.nn.gelu(y)
        for h in range(MXU_DIM // LANES):
            s = cb * (MXU_DIM // LANES) + h
            y_ref[:, pl.ds(s, nch, stride=CHUNK), :] = (
                y[:, h * LANES:(h + 1) * LANES].reshape(nb, nch, LANES))


def _ssm(u3, tables):
    toep, bst, cst, a_chunk = tables
    nb, seq, _ = u3.shape
    tt = SSM_TIME_TILE
    n = nb * (tt // CHUNK)
    return pl.pallas_call(
        _ssm_kernel,
        out_shape=jax.ShapeDtypeStruct(u3.shape, _F32),
        grid=(N_SLABS, seq // tt),
        in_specs=[pl.BlockSpec((nb, tt, LANES), lambda sl, ti: (0, ti, sl)),
                  pl.BlockSpec((1, FLAT, FLAT), lambda sl, ti: (sl, 0, 0)),
                  pl.BlockSpec((1, FLAT, 2 * SLAB_STATE), lambda sl, ti: (sl, 0, 0)),
                  pl.BlockSpec((1, 2 * SLAB_STATE, FLAT), lambda sl, ti: (sl, 0, 0)),
                  pl.BlockSpec((1, 2, SLAB_STATE), lambda sl, ti: (sl, 0, 0))],
        out_specs=pl.BlockSpec((nb, tt, LANES), lambda sl, ti: (0, ti, sl)),
        scratch_shapes=[pltpu.VMEM((n, FLAT), _BF16),
                        pltpu.VMEM((2 * SLAB_STATE // LANES, n, LANES), _F32),
                        pltpu.VMEM((2 * SLAB_STATE // LANES, n, LANES), _F32),
                        pltpu.VMEM((2 * SLAB_STATE // LANES, nb, LANES), _F32),
                        pltpu.VMEM((n, FLAT), _F32)],
        compiler_params=pltpu.CompilerParams(
            dimension_semantics=("arbitrary", "arbitrary"), vmem_limit_bytes=VMEM_LIMIT_BYTES),
        name="ssm",
    )(u3, toep, bst, cst, a_chunk)


def _mix_route_kernel(x_ref, bz_ref, yg_ref, gm_ref, wg_ref, wco_ref, wglu_ref, wout_ref,
                      gf_ref, wr_ref, br_ref,
                      x1_ref, h_ref, ti_ref, tg_ref, rk_ref, cnt_ref, base, merged):
    tm = x_ref.shape[0]

    @pl.when(pl.program_id(0) == 0)
    def _():
        base[...] = jnp.zeros_like(base)

    chains = [slice(r0, r0 + ROW_CHAIN) for r0 in range(0, tm, ROW_CHAIN)]
    hs = [_mix_rows(rows, x_ref, bz_ref, yg_ref, gm_ref, wg_ref, wco_ref, wglu_ref, wout_ref,
                    gf_ref, x1_ref, h_ref, merged) for rows in chains]
    for rows, h in zip(chains, hs):
        _route_rows(rows, h, wr_ref, br_ref, ti_ref, tg_ref, rk_ref, base)
    cnt_ref[...] = base[...].astype(jnp.int32)


def _mix_rows(rows, x_ref, bz_ref, yg_ref, gm_ref, wg_ref, wco_ref, wglu_ref, wout_ref,
              gf_ref, x1_ref, h_ref, merged):
    x = x_ref[rows, :]
    xn = _rmsnorm(x, gm_ref[...]).astype(_BF16)
    bz = bz_ref[rows, :]
    yg = yg_ref[rows, :].astype(_BF16)
    for c in range(D_MODEL // MXU_DIM):
        lo = slice(c * MXU_DIM, (c + 1) * MXU_DIM)
        hi = slice(D_MODEL + c * MXU_DIM, D_MODEL + (c + 1) * MXU_DIM)
        gate_a = jnp.dot(xn, wg_ref[:, lo], preferred_element_type=_F32)
        gate_b = jnp.dot(xn, wg_ref[:, hi], preferred_element_type=_F32)
        y_a = jnp.dot(bz, wco_ref[:, lo], preferred_element_type=_F32)
        val = jnp.dot(yg, wglu_ref[:, lo], preferred_element_type=_F32)
        glu_gate = jnp.dot(yg, wglu_ref[:, hi], preferred_element_type=_F32)
        y_b = val * _sigmoid(glu_gate)
        merged[rows, lo] = (_sigmoid(gate_a) * y_a + _sigmoid(gate_b) * y_b).astype(_BF16)
    x1 = x + jnp.dot(merged[rows, :], wout_ref[...], preferred_element_type=_F32)
    x1_ref[rows, :] = x1
    h = _rmsnorm(x1, gf_ref[...])
    h_ref[rows, :] = _pack_bf16_halves(h)
    return h.astype(_BF16)


def _route_rows(rows, h, wr_ref, br_ref, ti_ref, tg_ref, rk_ref, base):
    tm = rows.stop - rows.start
    logits_tok = jnp.dot(h, wr_ref[...], preferred_element_type=_F32)
    logits = jnp.transpose(logits_tok)[:N_EXPERTS, :] + br_ref[...]
    erow = lax.broadcasted_iota(jnp.int32, (N_EXPERTS, tm), 0).astype(_F32)
    neg_inf = jnp.float32(-jnp.inf)
    work = logits
    vals, idxs = [], []
    for _ in range(TOP_K):
        m = jnp.max(work, axis=0, keepdims=True)
        idx = jnp.min(jnp.where(work == m, erow, float(N_EXPERTS)), axis=0, keepdims=True)
        vals.append(m)
        idxs.append(idx)
        work = jnp.where(erow == idx, neg_inf, work)
    exps = [jnp.exp(v - vals[0]) for v in vals]
    denom = exps[0] + exps[1] + exps[2] + exps[3]

    sel = jnp.zeros((N_EXPERTS, tm), _F32)
    for idx in idxs:
        sel = sel + (erow == idx).astype(_F32)
    row = lax.broadcasted_iota(jnp.int32, (tm, tm), 0)
    col = lax.broadcasted_iota(jnp.int32, (tm, tm), 1)
    earlier = (row < col).astype(_BF16)
    before = jnp.dot(sel.astype(_BF16), earlier, preferred_element_type=_F32) + base[...]
    for k in range(TOP_K):
        ti_ref[k:k + 1, rows] = idxs[k].astype(jnp.int32)
        tg_ref[k:k + 1, rows] = exps[k] / denom
        tg_ref[TOP_K + k:TOP_K + k + 1, rows] = jnp.zeros((1, tm), _F32)
        rk = jnp.sum(jnp.where(erow == idxs[k], before, 0.0), axis=0, keepdims=True)
        rk_ref[k:k + 1, rows] = rk.astype(jnp.int32)
    base[...] = base[...] + jnp.sum(sel, axis=1, keepdims=True)


def _mix_route(x2, bz, yg, g_mix, w_gates, w_conv_out, w_glu, w_out, g_ffn, w_router, b_router):
    t = x2.shape[0]
    tm = TOKEN_TILE
    tok = lambda i: (i, 0)
    tok_lanes = lambda i: (0, i)
    fixed = lambda i: (0, 0)
    return pl.pallas_call(
        _mix_route_kernel,
        out_shape=(jax.ShapeDtypeStruct((t, D_MODEL), _F32),
                   jax.ShapeDtypeStruct((t, D_MODEL // 2), jnp.uint32),
                   jax.ShapeDtypeStruct((TOP_K, t), jnp.int32),
                   jax.ShapeDtypeStruct((2 * TOP_K, t), _F32),
                   jax.ShapeDtypeStruct((TOP_K, t), jnp.int32),
                   jax.ShapeDtypeStruct((N_EXPERTS, 1), jnp.int32)),
        grid=(t // tm,),
        in_specs=[pl.BlockSpec((tm, D_MODEL), tok),
                  pl.BlockSpec((tm, D_CONV), tok),
                  pl.BlockSpec((tm, D_SSM), tok),
                  pl.BlockSpec((1, D_MODEL), fixed),
                  pl.BlockSpec((D_MODEL, 2 * D_MODEL), fixed),
                  pl.BlockSpec((D_CONV, D_MODEL), fixed),
                  pl.BlockSpec((D_SSM, 2 * D_MODEL), fixed),
                  pl.BlockSpec((D_MODEL, D_MODEL), fixed),
                  pl.BlockSpec((1, D_MODEL), fixed),
                  pl.BlockSpec((D_MODEL, LANES), fixed),
                  pl.BlockSpec((N_EXPERTS, 1), fixed)],
        out_specs=(pl.BlockSpec((tm, D_MODEL), tok),
                   pl.BlockSpec((tm, D_MODEL // 2), tok),
                   pl.BlockSpec((TOP_K, tm), tok_lanes),
                   pl.BlockSpec((2 * TOP_K, tm), tok_lanes),
                   pl.BlockSpec((TOP_K, tm), tok_lanes),
                   pl.BlockSpec((N_EXPERTS, 1), fixed)),
        scratch_shapes=[pltpu.VMEM((N_EXPERTS, 1), _F32),
                        pltpu.VMEM((tm, D_MODEL), _BF16)],
        compiler_params=pltpu.CompilerParams(
            dimension_semantics=("arbitrary",), vmem_limit_bytes=VMEM_LIMIT_BYTES),
        name="mix_route",
    )(x2, bz, yg, g_mix, w_gates, w_conv_out, w_glu, w_out, g_ffn, w_router, b_router)


def _expert_ffn_kernel(be_ref, nr_ref, slot_ref, next_ref, x_ref, wgu_hbm, bgu_ref, wd_hbm, bd_ref,
                       y_ref, stage_gu, stage_d, wgu_b, wd_b, sem_gu, sem_d):
    b = pl.program_id(0)
    expert = be_ref[b]
    live = nr_ref[b] > 0

    def weight_copies(e, slot):
        return (pltpu.make_async_copy(wgu_hbm.at[e], stage_gu.at[slot], sem_gu.at[slot]),
                pltpu.make_async_copy(wd_hbm.at[e], stage_d.at[slot], sem_d.at[slot]))

    @pl.when(live & ((b == 0) | (be_ref[jnp.maximum(b - 1, 0)] != expert)))
    def _():
        slot = slot_ref[b]

        @pl.when(b == 0)
        def _():
            for cp in weight_copies(expert, slot):
                cp.start()

        for cp in weight_copies(expert, slot):
            cp.wait()
        wgu_b[...] = stage_gu[slot].astype(_BF16)
        wd_b[...] = stage_d[slot].astype(_BF16)

        @pl.when(next_ref[b] < N_EXPERTS)
        def _():
            for cp in weight_copies(next_ref[b], 1 - slot):
                cp.start()

    @pl.when(live)
    def _():
        valid = lax.broadcasted_iota(jnp.int32, x_ref.shape, 0) < nr_ref[b]
        x = _unpack_bf16_halves(jnp.where(valid, x_ref[...], jnp.uint32(0))).astype(_BF16)
        hgu = jnp.dot(x, wgu_b[...], preferred_element_type=_F32) + bgu_ref[0]
        g = jnp.minimum(hgu[:, :D_FF], SWIGLU_LIMIT)
        up = jnp.clip(hgu[:, D_FF:], -SWIGLU_LIMIT, SWIGLU_LIMIT)
        act = (up + 1.0) * (g * _sigmoid(SWIGLU_ALPHA * g))
        y = jnp.dot(act.astype(_BF16), wd_b[...], preferred_element_type=_F32) + bd_ref[0]
        y_ref[...] = _pack_bf16_halves(y)

    @pl.when(jnp.logical_not(live))
    def _():
        y_ref[...] = jnp.zeros_like(y_ref)


def _expert_ffn(block_e, block_rows, block_slot, block_next, x_rows, w_gate_up, b_gate_up, w_down,
                b_down):
    n_rows = x_rows.shape[0]
    n_blocks = n_rows // MOE_BLOCK

    def bias_map(b, be, nr, sl, nx):
        return (be[b], 0, 0)

    def row_map(b, be, nr, sl, nx):
        return (b, 0)

    grid_spec = pltpu.PrefetchScalarGridSpec(
        num_scalar_prefetch=4,
        grid=(n_blocks,),
        in_specs=[pl.BlockSpec((MOE_BLOCK, D_MODEL // 2), row_map),
                  pl.BlockSpec(memory_space=pl.ANY),
                  pl.BlockSpec((1, 1, 2 * D_FF), bias_map),
                  pl.BlockSpec(memory_space=pl.ANY),
                  pl.BlockSpec((1, 1, D_MODEL), bias_map)],
        out_specs=pl.BlockSpec((MOE_BLOCK, D_MODEL // 2), row_map),
        scratch_shapes=[pltpu.VMEM((2, D_MODEL, 2 * D_FF), _F32),
                        pltpu.VMEM((2, D_FF, D_MODEL), _F32),
                        pltpu.VMEM((D_MODEL, 2 * D_FF), _BF16),
                        pltpu.VMEM((D_FF, D_MODEL), _BF16),
                        pltpu.SemaphoreType.DMA((2,)),
                        pltpu.SemaphoreType.DMA((2,))],
    )
    return pl.pallas_call(
        _expert_ffn_kernel,
        out_shape=jax.ShapeDtypeStruct((n_rows, D_MODEL // 2), jnp.uint32),
        grid_spec=grid_spec,
        compiler_params=pltpu.CompilerParams(
            dimension_semantics=("arbitrary",), vmem_limit_bytes=VMEM_LIMIT_BYTES),
        name="expert_ffn",
    )(block_e, block_rows, block_slot, block_next, x_rows, w_gate_up, b_gate_up, w_down, b_down)


def _sc_workers():
    info = plsc.get_sparse_core_info()
    return info.num_cores, info.num_cores * info.num_subcores


def _dispatch(h_packed, dest_flat, n_rows):
    t, width = h_packed.shape
    n_cores, n_workers = _sc_workers()
    per_w = t // n_workers
    n_chunks = per_w // SC_ROWS

    @functools.partial(
        pl.kernel, mesh=plsc.VectorSubcoreMesh(core_axis_name="c", subcore_axis_name="s"),
        out_type=jax.ShapeDtypeStruct((n_rows, width), h_packed.dtype),
        scratch_types=[pltpu.VMEM((TOP_K, SC_ROWS), jnp.int32),
                       pltpu.VMEM((SC_ROWS, width), h_packed.dtype),
                       pltpu.SemaphoreType.DMA])
    def scatter_rows(h_hbm, dest_hbm, out_hbm, idx_v, rows_v, sem):
        wid = lax.axis_index("s") * n_cores + lax.axis_index("c")

        @pl.loop(0, n_chunks)
        def _(ci):
            off = wid * per_w + ci * SC_ROWS
            pltpu.sync_copy(h_hbm.at[pl.ds(off, SC_ROWS)], rows_v)
            for k in range(TOP_K):
                pltpu.sync_copy(dest_hbm.at[pl.ds(k * t + off, SC_ROWS)], idx_v.at[k])
            copies = [pltpu.async_copy(rows_v, out_hbm.at[idx_v.at[k]], sem) for k in range(TOP_K)]
            for cp in copies:
                cp.wait()

    return scatter_rows(h_packed, dest_flat)


def _collect(y_rows, dest_flat):
    n_idx = dest_flat.shape[0]
    width = y_rows.shape[1]
    n_cores, n_workers = _sc_workers()
    n_chunks = n_idx // (n_workers * SC_ROWS)
    assert n_chunks % 2 == 0

    @functools.partial(
        pl.kernel, mesh=plsc.VectorSubcoreMesh(core_axis_name="c", subcore_axis_name="s"),
        out_type=jax.ShapeDtypeStruct((n_idx, width), y_rows.dtype),
        scratch_types=[pltpu.VMEM((n_chunks, SC_ROWS), jnp.int32),
                       pltpu.VMEM((2, SC_ROWS, width), y_rows.dtype),
                       pltpu.SemaphoreType.DMA((2,)),
                       pltpu.SemaphoreType.DMA((2,))])
    def gather_rows(y_hbm, dest_hbm, out_hbm, idx_v, buf, gsem, wsem):
        wid = lax.axis_index("s") * n_cores + lax.axis_index("c")
        c0 = wid * n_chunks
        pltpu.sync_copy(dest_hbm.at[pl.ds(c0, n_chunks)], idx_v)

        def gather(c, b):
            return pltpu.make_async_copy(y_hbm.at[idx_v.at[c]], buf.at[b], gsem.at[b])

        def write(c, b):
            return pltpu.make_async_copy(buf.at[b], out_hbm.at[pl.ds((c0 + c) * SC_ROWS, SC_ROWS)],
                                         wsem.at[b])

        gather(0, 0).start()

        @pl.loop(0, n_chunks, step=2)
        def _(ci):
            for b in range(2):
                c = ci + b

                @pl.when(c >= 1)
                def _():
                    write(c - 1, 1 - b).wait()

                @pl.when(c + 1 < n_chunks)
                def _():
                    gather(c + 1, 1 - b).start()

                gather(c, b).wait()
                write(c, b).start()

        write(n_chunks - 1, 1).wait()

    return gather_rows(y_rows, dest_flat.reshape(n_idx // SC_ROWS, SC_ROWS))


def _combine_kernel(x1_ref, ya_ref, tg_ref, g_ref, o_ref):
    acc = x1_ref[...]
    tg = jnp.transpose(tg_ref[...])
    for k in range(TOP_K):
        acc = acc + tg[:, k:k + 1] * _unpack_bf16_halves(ya_ref[k])
    o_ref[...] = _rmsnorm(acc, g_ref[...])


def _combine(x1, y_assign, top_g, g_final):
    t = x1.shape[0]
    tm = TOKEN_TILE
    return pl.pallas_call(
        _combine_kernel,
        out_shape=jax.ShapeDtypeStruct((t, D_MODEL), _F32),
        grid=(t // tm,),
        in_specs=[pl.BlockSpec((tm, D_MODEL), lambda i: (i, 0)),
                  pl.BlockSpec((TOP_K, tm, D_MODEL // 2), lambda i: (0, i, 0)),
                  pl.BlockSpec((2 * TOP_K, tm), lambda i: (0, i)),
                  pl.BlockSpec((1, D_MODEL), lambda i: (0, 0))],
        out_specs=pl.BlockSpec((tm, D_MODEL), lambda i: (i, 0)),
        compiler_params=pltpu.CompilerParams(
            dimension_semantics=("arbitrary",), vmem_limit_bytes=VMEM_LIMIT_BYTES),
        name="combine",
    )(x1, y_assign, top_g, g_final)


def _block_plan(counts, n_blocks):
    padded = ((counts + MOE_BLOCK - 1) // MOE_BLOCK) * MOE_BLOCK
    pad_end = jnp.cumsum(padded)
    pad_start = pad_end - padded
    block_start = (jnp.arange(n_blocks, dtype=jnp.int32) * MOE_BLOCK)[:, None]
    eidx = jnp.arange(N_EXPERTS, dtype=jnp.int32)
    owns = (pad_start[None, :] <= block_start) & (block_start < pad_end[None, :])
    has_blocks = (padded > 0).astype(jnp.int32)
    ordinal = jnp.cumsum(has_blocks) - has_blocks
    later = (eidx[None, :] > eidx[:, None]) & (padded[None, :] > 0)
    next_expert = jnp.min(jnp.where(later, eidx[None, :], N_EXPERTS), axis=1)

    def per_block(per_expert):
        return jnp.sum(jnp.where(owns, per_expert, 0), axis=1).astype(jnp.int32)

    block_e = per_block(eidx[None, :])
    block_rows = per_block(jnp.clip((pad_start + counts)[None, :] - block_start, 0, MOE_BLOCK))
    block_slot = per_block((ordinal % 2)[None, :])
    block_next = per_block(next_expert[None, :])
    return pad_start, (block_e, block_rows, block_slot, block_next)


def kernel(x, norm_mix_g, w_in, conv_w, w_conv_out, ssm_lam_re, ssm_lam_im, ssm_log_dt, ssm_b_re, ssm_b_im, ssm_c_re, ssm_c_im, ssm_d, w_glu, w_out, norm_ffn_g, w_router, b_router, w_gate_up, b_gate_up, w_down, b_down, norm_f_g):
    bsz, seq, d = x.shape
    t = bsz * seq
    x2 = x.reshape(t, d)
    assert seq % TOKEN_TILE == 0 and seq % SSM_TIME_TILE == 0 and w_in.shape[0] == 1

    w_in_b = w_in[0].astype(_BF16)
    n_bcvu = 3 * D_CONV + D_SSM
    g_mix = norm_mix_g[0].reshape(1, d)

    bz, u = _in_proj(x2, g_mix, w_in_b[:, :n_bcvu], conv_w[0], seq)

    tables = _ssm_tables(ssm_lam_re[0], ssm_lam_im[0], ssm_log_dt[0], ssm_b_re[0], ssm_b_im[0],
                         ssm_c_re[0], ssm_c_im[0], ssm_d[0])
    yg = _ssm(u.reshape(bsz, seq, D_SSM), tables).reshape(t, D_SSM)

    x1, h_packed, top_i, top_g, rank, counts = _mix_route(
        x2, bz, yg, g_mix, w_in_b[:, n_bcvu:], w_conv_out[0].astype(_BF16),
        w_glu[0].astype(_BF16), w_out[0].astype(_BF16), norm_ffn_g[0].reshape(1, d),
        jnp.pad(w_router[0], ((0, 0), (0, LANES - N_EXPERTS))).astype(_BF16),
        b_router[0].reshape(N_EXPERTS, 1))

    n_rows = t * TOP_K + N_EXPERTS * MOE_BLOCK
    pad_start, block_plan = _block_plan(counts[:, 0], n_rows // MOE_BLOCK)
    expert_ids = jnp.arange(N_EXPERTS, dtype=jnp.int32)[:, None, None]
    row_start = jnp.sum(jnp.where(top_i[None] == expert_ids, pad_start[:, None, None], 0), axis=0)
    dest = (row_start + rank).reshape(TOP_K * t)

    x_rows = _dispatch(h_packed, dest, n_rows)
    y_rows = _expert_ffn(*block_plan, x_rows, w_gate_up[0],
                         b_gate_up[0].reshape(N_EXPERTS, 1, 2 * D_FF), w_down[0],
                         b_down[0].reshape(N_EXPERTS, 1, D_MODEL))
    y_assign = _collect(y_rows, dest).reshape(TOP_K, t, D_MODEL // 2)
    out = _combine(x1, y_assign, top_g, norm_f_g.reshape(1, d))
    return out.reshape(bsz, seq, d)
```

```python
import functools

import jax
import jax.numpy as jnp
from jax import lax
from jax.experimental import pallas as pl
from jax.experimental.pallas import tpu as pltpu
from jax.experimental.pallas import tpu_sc as plsc

D_MODEL = 1024
D_CONV = 512
CONV_WIDTH = 3
D_SSM = 512
SSM_GROUP = 16
N_SSM_GROUPS = 32
SSM_STATE = 64
N_EXPERTS = 32
TOP_K = 4
D_FF = 1024
SWIGLU_LIMIT = 7.0
SWIGLU_ALPHA = 1.702
MOE_BLOCK = 512
RMS_EPS = 1e-6

LANES = 128
MXU_DIM = 256
CHUNK = 16
SLAB_GROUPS = LANES // SSM_GROUP
N_SLABS = N_SSM_GROUPS // SLAB_GROUPS
SLAB_STATE = SLAB_GROUPS * SSM_STATE
FLAT = CHUNK * LANES
SSM_TIME_TILE = 256
TOKEN_TILE = 512
ROW_CHAIN = 256
SC_ROWS = 64
VMEM_LIMIT_BYTES = 56 * 1024 * 1024

_BF16 = jnp.bfloat16
_F32 = jnp.float32


def _rmsnorm(xf, g):
    return xf * lax.rsqrt(jnp.mean(xf * xf, axis=-1, keepdims=True) + RMS_EPS) * g


def _sigmoid(v):
    return 1.0 / (1.0 + jnp.exp(-v))


def _pack_bf16_halves(v):
    n = v.shape[1] // 2
    bits = pltpu.bitcast(v.astype(_BF16).astype(_F32), jnp.uint32)
    return (bits[:, :n] >> 16) | (bits[:, n:] & jnp.uint32(0xFFFF0000))


def _unpack_bf16_halves(w):
    return jnp.concatenate([pltpu.bitcast(w << 16, _F32),
                            pltpu.bitcast(w & jnp.uint32(0xFFFF0000), _F32)], axis=1)


def _in_proj_kernel(tiles_per_seq, x_ref, g_ref, w_ref, cw_ref, bz_ref, u_ref, hbuf):
    tm = x_ref.shape[0]
    i = pl.program_id(0)

    @pl.when(i % tiles_per_seq == 0)
    def _():
        hbuf[0:8, :] = jnp.zeros((8, D_CONV), _F32)

    xn = _rmsnorm(x_ref[...], g_ref[...]).astype(_BF16)
    cv = jnp.dot(xn, w_ref[:, D_CONV:3 * D_CONV], preferred_element_type=_F32)
    hbuf[8:8 + tm, :] = cv[:, :D_CONV] * cv[:, D_CONV:]
    u_ref[...] = jnp.dot(xn, w_ref[:, 3 * D_CONV:], preferred_element_type=_F32)
    cw = cw_ref[...]
    z = (cw[0:1, :] * hbuf[6:6 + tm, :] + cw[1:2, :] * hbuf[7:7 + tm, :]
         + cw[2:3, :] * hbuf[8:8 + tm, :])
    b_gate = jnp.dot(xn, w_ref[:, :D_CONV], preferred_element_type=_F32)
    bz_ref[...] = (b_gate * z).astype(_BF16)
    hbuf[0:8, :] = hbuf[tm:tm + 8, :]


def _in_proj(x2, g, w_bcvu, conv_w, seq):
    t = x2.shape[0]
    tm = TOKEN_TILE
    return pl.pallas_call(
        functools.partial(_in_proj_kernel, seq // tm),
        out_shape=(jax.ShapeDtypeStruct((t, D_CONV), _BF16),
                   jax.ShapeDtypeStruct((t, D_SSM), _F32)),
        grid=(t // tm,),
        in_specs=[pl.BlockSpec((tm, D_MODEL), lambda i: (i, 0)),
                  pl.BlockSpec((1, D_MODEL), lambda i: (0, 0)),
                  pl.BlockSpec((D_MODEL, 3 * D_CONV + D_SSM), lambda i: (0, 0)),
                  pl.BlockSpec((CONV_WIDTH, D_CONV), lambda i: (0, 0))],
        out_specs=(pl.BlockSpec((tm, D_CONV), lambda i: (i, 0)),
                   pl.BlockSpec((tm, D_SSM), lambda i: (i, 0))),
        scratch_shapes=[pltpu.VMEM((tm + 8, D_CONV), _F32)],
        compiler_params=pltpu.CompilerParams(
            dimension_semantics=("arbitrary",), vmem_limit_bytes=VMEM_LIMIT_BYTES),
        name="in_proj",
    )(x2, g, w_bcvu, conv_w)


def _ssm_prep_kernel(lr_ref, lc_ref, bm_ref, cm_ref, d_ref, toep_ref, bst_ref, cst_ref, a_ref):
    def discretise(lre, lim, log_dt):
        dt = jnp.exp(log_dt)
        mag = jnp.exp(lre * dt)
        return mag * jnp.cos(lim * dt), mag * jnp.sin(lim * dt)

    def powers(are, aim):
        pre, pim = [jnp.ones_like(are)], [jnp.zeros_like(are)]
        for _ in range(CHUNK):
            pre, pim = (pre + [pre[-1] * are - pim[-1] * aim],
                        pim + [pre[-1] * aim + pim[-1] * are])
        return pre, pim

    lr = lr_ref[0]
    lre, lim = lr[0:1, :], lr[1:2, :]
    are, aim = discretise(lre, lim, lr[2:3, :])
    pre, pim = powers(are, aim)
    den = lre * lre + lim * lim
    q_re = ((are - 1.0) * lre + aim * lim) / den
    q_im = (aim * lre - (are - 1.0) * lim) / den
    bb_re = q_re * bm_ref[0, 0] - q_im * bm_ref[0, 1]
    bb_im = q_re * bm_ref[0, 1] + q_im * bm_ref[0, 0]
    cm_re, cm_im = cm_ref[0, 0], cm_ref[0, 1]
    hi = lax.Precision.HIGHEST
    kblk = []
    for k in range(CHUNK):
        ab_re = bb_re * pre[k] - bb_im * pim[k]
        ab_im = bb_re * pim[k] + bb_im * pre[k]
        rows = slice((CHUNK - 1 - k) * LANES, (CHUNK - k) * LANES)
        bst_ref[0, rows, :SLAB_STATE] = ab_re.astype(_BF16)
        bst_ref[0, rows, SLAB_STATE:] = ab_im.astype(_BF16)
        kblk.append(jnp.dot(ab_re, cm_re, preferred_element_type=_F32, precision=hi)
                    - jnp.dot(ab_im, cm_im, preferred_element_type=_F32, precision=hi))
    r = lax.broadcasted_iota(jnp.int32, (LANES, LANES), 0)
    c = lax.broadcasted_iota(jnp.int32, (LANES, LANES), 1)
    kblk[0] = kblk[0] + jnp.where(r == c, jnp.broadcast_to(d_ref[0], (LANES, LANES)), 0.0)
    kblk = [kb.astype(_BF16) for kb in kblk]
    zeros = jnp.zeros((LANES, LANES), _BF16)
    for sp in range(CHUNK):
        for s in range(CHUNK):
            toep_ref[0, sp * LANES:(sp + 1) * LANES, s * LANES:(s + 1) * LANES] = (
                kblk[s - sp] if s >= sp else zeros)

    lc = lc_ref[0]
    cre, cim = discretise(lc[:, 0:1], lc[:, 1:2], lc[:, 2:3])
    qre, qim = powers(cre, cim)
    for s in range(CHUNK):
        cols = slice(s * LANES, (s + 1) * LANES)
        cst_ref[0, :SLAB_STATE, cols] = (cm_re * qre[s + 1] - cm_im * qim[s + 1]).astype(_BF16)
        cst_ref[0, SLAB_STATE:, cols] = (-(cm_re * qim[s + 1] + cm_im * qre[s + 1])).astype(_BF16)
    a_ref[0, 0:1, :] = pre[CHUNK]
    a_ref[0, 1:2, :] = pim[CHUNK]


def _ssm_tables(lam_re, lam_im, log_dt, b_re, b_im, c_re, c_im, d_skip):
    sg = (N_SLABS, SLAB_GROUPS)
    eye = jnp.eye(SLAB_GROUPS, dtype=_F32)
    lam = jnp.stack([lam_re, lam_im, jnp.broadcast_to(log_dt[:, None], lam_re.shape)], axis=0)
    lam_row = lam.reshape(3, N_SLABS, SLAB_STATE).transpose(1, 0, 2)
    lam_col = lam_row.transpose(0, 2, 1)

    def b_blockdiag(b):
        bt = b.reshape(*sg, SSM_STATE, SSM_GROUP).transpose(0, 1, 3, 2)
        return (bt[:, :, :, None, :] * eye[None, :, None, :, None]).reshape(N_SLABS, LANES, SLAB_STATE)

    def c_blockdiag(c):
        ct = c.reshape(*sg, SSM_GROUP, SSM_STATE).transpose(0, 1, 3, 2)
        return (ct[:, :, :, None, :] * eye[None, :, None, :, None]).reshape(N_SLABS, SLAB_STATE, LANES)

    bm = jnp.stack([b_blockdiag(b_re), b_blockdiag(b_im)], axis=1)
    cm = jnp.stack([c_blockdiag(c_re), c_blockdiag(c_im)], axis=1)
    d = d_skip.reshape(N_SLABS, 1, LANES)
    slab3 = lambda sl: (sl, 0, 0)
    slab4 = lambda sl: (sl, 0, 0, 0)
    return pl.pallas_call(
        _ssm_prep_kernel,
        out_shape=(jax.ShapeDtypeStruct((N_SLABS, FLAT, FLAT), _BF16),
                   jax.ShapeDtypeStruct((N_SLABS, FLAT, 2 * SLAB_STATE), _BF16),
                   jax.ShapeDtypeStruct((N_SLABS, 2 * SLAB_STATE, FLAT), _BF16),
                   jax.ShapeDtypeStruct((N_SLABS, 2, SLAB_STATE), _F32)),
        grid=(N_SLABS,),
        in_specs=[pl.BlockSpec((1, 3, SLAB_STATE), slab3),
                  pl.BlockSpec((1, SLAB_STATE, 3), slab3),
                  pl.BlockSpec((1, 2, LANES, SLAB_STATE), slab4),
                  pl.BlockSpec((1, 2, SLAB_STATE, LANES), slab4),
                  pl.BlockSpec((1, 1, LANES), slab3)],
        out_specs=(pl.BlockSpec((1, FLAT, FLAT), slab3),
                   pl.BlockSpec((1, FLAT, 2 * SLAB_STATE), slab3),
                   pl.BlockSpec((1, 2 * SLAB_STATE, FLAT), slab3),
                   pl.BlockSpec((1, 2, SLAB_STATE), slab3)),
        compiler_params=pltpu.CompilerParams(
            dimension_semantics=("arbitrary",), vmem_limit_bytes=VMEM_LIMIT_BYTES),
        name="ssm_prep",
    )(lam_row, lam_col, bm, cm, d)


def _ssm_kernel(u_ref, toep_ref, bst_ref, cst_ref, a_ref, y_ref, uflat, s_scr, xc_scr, carry, ytoep):
    nb, tt, _ = u_ref.shape
    nch = tt // CHUNK
    n = nb * nch

    @pl.when(pl.program_id(1) == 0)
    def _():
        carry[...] = jnp.zeros_like(carry)

    for s in range(CHUNK):
        part = u_ref[:, pl.ds(s, nch, stride=CHUNK), :]
        uflat[:, s * LANES:(s + 1) * LANES] = part.reshape(n, LANES).astype(_BF16)

    n_cb = FLAT // MXU_DIM

    def toeplitz(cb):
        kk = (cb + 1) * MXU_DIM
        cols = slice(cb * MXU_DIM, kk)
        ytoep[:, cols] = jnp.dot(uflat[:, :kk], toep_ref[0, :kk, cols], preferred_element_type=_F32)

    for cb in range(n_cb // 2):
        toeplitz(cb)

    nblk = SLAB_STATE // LANES
    loc_all = jnp.dot(uflat[...], bst_ref[0], preferred_element_type=_F32)
    for cb in range(n_cb // 2, n_cb):
        toeplitz(cb)
    for kb in range(2 * nblk):
        s_scr[kb] = loc_all[:, kb * LANES:(kb + 1) * LANES]

    a = a_ref[0]
    are = [jnp.broadcast_to(a[0:1, kb * LANES:(kb + 1) * LANES], (nb, LANES)) for kb in range(nblk)]
    aim = [jnp.broadcast_to(a[1:2, kb * LANES:(kb + 1) * LANES], (nb, LANES)) for kb in range(nblk)]
    xr = [carry[kb] for kb in range(nblk)]
    xi = [carry[nblk + kb] for kb in range(nblk)]
    for j in range(nch):
        rows = pl.ds(j, nb, stride=nch)
        for kb in range(nblk):
            xc_scr[kb, rows, :] = xr[kb]
            xc_scr[nblk + kb, rows, :] = xi[kb]
            nr = are[kb] * xr[kb] - aim[kb] * xi[kb] + s_scr[kb, rows, :]
            ni = are[kb] * xi[kb] + aim[kb] * xr[kb] + s_scr[nblk + kb, rows, :]
            xr[kb], xi[kb] = nr, ni
    for kb in range(nblk):
        carry[kb] = xr[kb]
        carry[nblk + kb] = xi[kb]

    xc = jnp.concatenate([xc_scr[kb] for kb in range(2 * nblk)], axis=1).astype(_BF16)
    for cb in range(n_cb):
        cols = slice(cb * MXU_DIM, (cb + 1) * MXU_DIM)
        y = ytoep[:, cols] + jnp.dot(xc, cst_ref[0, :, cols], preferred_element_type=_F32)
        y = jax.nn.gelu(y)
        for h in range(MXU_DIM // LANES):
            s = cb * (MXU_DIM // LANES) + h
            y_ref[:, pl.ds(s, nch, stride=CHUNK), :] = (
                y[:, h * LANES:(h + 1) * LANES].reshape(nb, nch, LANES))


def _ssm(u3, tables):
    toep, bst, cst, a_chunk = tables
    nb, seq, _ = u3.shape
    tt = SSM_TIME_TILE
    n = nb * (tt // CHUNK)
    return pl.pallas_call(
        _ssm_kernel,
        out_shape=jax.ShapeDtypeStruct(u3.shape, _F32),
        grid=(N_SLABS, seq // tt),
        in_specs=[pl.BlockSpec((nb, tt, LANES), lambda sl, ti: (0, ti, sl)),
                  pl.BlockSpec((1, FLAT, FLAT), lambda sl, ti: (sl, 0, 0)),
                  pl.BlockSpec((1, FLAT, 2 * SLAB_STATE), lambda sl, ti: (sl, 0, 0)),
                  pl.BlockSpec((1, 2 * SLAB_STATE, FLAT), lambda sl, ti: (sl, 0, 0)),
                  pl.BlockSpec((1, 2, SLAB_STATE), lambda sl, ti: (sl, 0, 0))],
        out_specs=pl.BlockSpec((nb, tt, LANES), lambda sl, ti: (0, ti, sl)),
        scratch_shapes=[pltpu.VMEM((n, FLAT), _BF16),
                        pltpu.VMEM((2 * SLAB_STATE // LANES, n, LANES), _F32),
                        pltpu.VMEM((2 * SLAB_STATE // LANES, n, LANES), _F32),
                        pltpu.VMEM((2 * SLAB_STATE // LANES, nb, LANES), _F32),
                        pltpu.VMEM((n, FLAT), _F32)],
        compiler_params=pltpu.CompilerParams(
            dimension_semantics=("arbitrary", "arbitrary"), vmem_limit_bytes=VMEM_LIMIT_BYTES),
        name="ssm",
    )(u3, toep, bst, cst, a_chunk)


def _mix_route_kernel(x_ref, bz_ref, yg_ref, gm_ref, wg_ref, wco_ref, wglu_ref, wout_ref,
                      gf_ref, wr_ref, br_ref,
                      x1_ref, h_ref, ti_ref, tg_ref, rk_ref, cnt_ref, base, merged):
    tm = x_ref.shape[0]

    @pl.when(pl.program_id(0) == 0)
    def _():
        base[...] = jnp.zeros_like(base)

    chains = [slice(r0, r0 + ROW_CHAIN) for r0 in range(0, tm, ROW_CHAIN)]
    hs = [_mix_rows(rows, x_ref, bz_ref, yg_ref, gm_ref, wg_ref, wco_ref, wglu_ref, wout_ref,
                    gf_ref, x1_ref, h_ref, merged) for rows in chains]
    for rows, h in zip(chains, hs):
        _route_rows(rows, h, wr_ref, br_ref, ti_ref, tg_ref, rk_ref, base)
    cnt_ref[...] = base[...].astype(jnp.int32)


def _mix_rows(rows, x_ref, bz_ref, yg_ref, gm_ref, wg_ref, wco_ref, wglu_ref, wout_ref,
              gf_ref, x1_ref, h_ref, merged):
    x = x_ref[rows, :]
    xn = _rmsnorm(x, gm_ref[...]).astype(_BF16)
    bz = bz_ref[rows, :]
    yg = yg_ref[rows, :].astype(_BF16)
    for c in range(D_MODEL // MXU_DIM):
        lo = slice(c * MXU_DIM, (c + 1) * MXU_DIM)
        hi = slice(D_MODEL + c * MXU_DIM, D_MODEL + (c + 1) * MXU_DIM)
        gate_a = jnp.dot(xn, wg_ref[:, lo], preferred_element_type=_F32)
        gate_b = jnp.dot(xn, wg_ref[:, hi], preferred_element_type=_F32)
        y_a = jnp.dot(bz, wco_ref[:, lo], preferred_element_type=_F32)
        val = jnp.dot(yg, wglu_ref[:, lo], preferred_element_type=_F32)
        glu_gate = jnp.dot(yg, wglu_ref[:, hi], preferred_element_type=_F32)
        y_b = val * _sigmoid(glu_gate)
        merged[rows, lo] = (_sigmoid(gate_a) * y_a + _sigmoid(gate_b) * y_b).astype(_BF16)
    x1 = x + jnp.dot(merged[rows, :], wout_ref[...], preferred_element_type=_F32)
    x1_ref[rows, :] = x1
    h = _rmsnorm(x1, gf_ref[...])
    h_ref[rows, :] = _pack_bf16_halves(h)
    return h.astype(_BF16)


def _route_rows(rows, h, wr_ref, br_ref, ti_ref, tg_ref, rk_ref, base):
    tm = rows.stop - rows.start
    logits_tok = jnp.dot(h, wr_ref[...], preferred_element_type=_F32)
    logits = jnp.transpose(logits_tok)[:N_EXPERTS, :] + br_ref[...]
    erow = lax.broadcasted_iota(jnp.int32, (N_EXPERTS, tm), 0).astype(_F32)
    neg_inf = jnp.float32(-jnp.inf)
    work = logits
    vals, idxs = [], []
    for _ in range(TOP_K):
        m = jnp.max(work, axis=0, keepdims=True)
        idx = jnp.min(jnp.where(work == m, erow, float(N_EXPERTS)), axis=0, keepdims=True)
        vals.append(m)
        idxs.append(idx)
        work = jnp.where(erow == idx, neg_inf, work)
    exps = [jnp.exp(v - vals[0]) for v in vals]
    denom = exps[0] + exps[1] + exps[2] + exps[3]

    sel = jnp.zeros((N_EXPERTS, tm), _F32)
    for idx in idxs:
        sel = sel + (erow == idx).astype(_F32)
    row = lax.broadcasted_iota(jnp.int32, (tm, tm), 0)
    col = lax.broadcasted_iota(jnp.int32, (tm, tm), 1)
    earlier = (row < col).astype(_BF16)
    before = jnp.dot(sel.astype(_BF16), earlier, preferred_element_type=_F32) + base[...]
    for k in range(TOP_K):
        ti_ref[k:k + 1, rows] = idxs[k].astype(jnp.int32)
        tg_ref[k:k + 1, rows] = exps[k] / denom
        tg_ref[TOP_K + k:TOP_K + k + 1, rows] = jnp.zeros((1, tm), _F32)
        rk = jnp.sum(jnp.where(erow == idxs[k], before, 0.0), axis=0, keepdims=True)
        rk_ref[k:k + 1, rows] = rk.astype(jnp.int32)
    base[...] = base[...] + jnp.sum(sel, axis=1, keepdims=True)


def _mix_route(x2, bz, yg, g_mix, w_gates, w_conv_out, w_glu, w_out, g_ffn, w_router, b_router):
    t = x2.shape[0]
    tm = TOKEN_TILE
    tok = lambda i: (i, 0)
    tok_lanes = lambda i: (0, i)
    fixed = lambda i: (0, 0)
    return pl.pallas_call(
        _mix_route_kernel,
        out_shape=(jax.ShapeDtypeStruct((t, D_MODEL), _F32),
                   jax.ShapeDtypeStruct((t, D_MODEL // 2), jnp.uint32),
                   jax.ShapeDtypeStruct((TOP_K, t), jnp.int32),
                   jax.ShapeDtypeStruct((2 * TOP_K, t), _F32),
                   jax.ShapeDtypeStruct((TOP_K, t), jnp.int32),
                   jax.ShapeDtypeStruct((N_EXPERTS, 1), jnp.int32)),
        grid=(t // tm,),
        in_specs=[pl.BlockSpec((tm, D_MODEL), tok),
                  pl.BlockSpec((tm, D_CONV), tok),
                  pl.BlockSpec((tm, D_SSM), tok),
                  pl.BlockSpec((1, D_MODEL), fixed),
                  pl.BlockSpec((D_MODEL, 2 * D_MODEL), fixed),
                  pl.BlockSpec((D_CONV, D_MODEL), fixed),
                  pl.BlockSpec((D_SSM, 2 * D_MODEL), fixed),
                  pl.BlockSpec((D_MODEL, D_MODEL), fixed),
                  pl.BlockSpec((1, D_MODEL), fixed),
                  pl.BlockSpec((D_MODEL, LANES), fixed),
                  pl.BlockSpec((N_EXPERTS, 1), fixed)],
        out_specs=(pl.BlockSpec((tm, D_MODEL), tok),
                   pl.BlockSpec((tm, D_MODEL // 2), tok),
                   pl.BlockSpec((TOP_K, tm), tok_lanes),
                   pl.BlockSpec((2 * TOP_K, tm), tok_lanes),
                   pl.BlockSpec((TOP_K, tm), tok_lanes),
                   pl.BlockSpec((N_EXPERTS, 1), fixed)),
        scratch_shapes=[pltpu.VMEM((N_EXPERTS, 1), _F32),
                        pltpu.VMEM((tm, D_MODEL), _BF16)],
        compiler_params=pltpu.CompilerParams(
            dimension_semantics=("arbitrary",), vmem_limit_bytes=VMEM_LIMIT_BYTES),
        name="mix_route",
    )(x2, bz, yg, g_mix, w_gates, w_conv_out, w_glu, w_out, g_ffn, w_router, b_router)


def _expert_ffn_kernel(be_ref, nr_ref, slot_ref, next_ref, x_ref, wgu_hbm, bgu_ref, wd_hbm, bd_ref,
                       y_ref, stage_gu, stage_d, wgu_b, wd_b, sem_gu, sem_d):
    b = pl.program_id(0)
    expert = be_ref[b]
    live = nr_ref[b] > 0

    def weight_copies(e, slot):
        return (pltpu.make_async_copy(wgu_hbm.at[e], stage_gu.at[slot], sem_gu.at[slot]),
                pltpu.make_async_copy(wd_hbm.at[e], stage_d.at[slot], sem_d.at[slot]))

    @pl.when(live & ((b == 0) | (be_ref[jnp.maximum(b - 1, 0)] != expert)))
    def _():
        slot = slot_ref[b]

        @pl.when(b == 0)
        def _():
            for cp in weight_copies(expert, slot):
                cp.start()

        for cp in weight_copies(expert, slot):
            cp.wait()
        wgu_b[...] = stage_gu[slot].astype(_BF16)
        wd_b[...] = stage_d[slot].astype(_BF16)

        @pl.when(next_ref[b] < N_EXPERTS)
        def _():
            for cp in weight_copies(next_ref[b], 1 - slot):
                cp.start()

    @pl.when(live)
    def _():
        valid = lax.broadcasted_iota(jnp.int32, x_ref.shape, 0) < nr_ref[b]
        x = _unpack_bf16_halves(jnp.where(valid, x_ref[...], jnp.uint32(0))).astype(_BF16)
        hgu = jnp.dot(x, wgu_b[...], preferred_element_type=_F32) + bgu_ref[0]
        g = jnp.minimum(hgu[:, :D_FF], SWIGLU_LIMIT)
        up = jnp.clip(hgu[:, D_FF:], -SWIGLU_LIMIT, SWIGLU_LIMIT)
        act = (up + 1.0) * (g * _sigmoid(SWIGLU_ALPHA * g))
        y = jnp.dot(act.astype(_BF16), wd_b[...], preferred_element_type=_F32) + bd_ref[0]
        y_ref[...] = _pack_bf16_halves(y)

    @pl.when(jnp.logical_not(live))
    def _():
        y_ref[...] = jnp.zeros_like(y_ref)


def _expert_ffn(block_e, block_rows, block_slot, block_next, x_rows, w_gate_up, b_gate_up, w_down,
                b_down):
    n_rows = x_rows.shape[0]
    n_blocks = n_rows // MOE_BLOCK

    def bias_map(b, be, nr, sl, nx):
        return (be[b], 0, 0)

    def row_map(b, be, nr, sl, nx):
        return (b, 0)

    grid_spec = pltpu.PrefetchScalarGridSpec(
        num_scalar_prefetch=4,
        grid=(n_blocks,),
        in_specs=[pl.BlockSpec((MOE_BLOCK, D_MODEL // 2), row_map),
                  pl.BlockSpec(memory_space=pl.ANY),
                  pl.BlockSpec((1, 1, 2 * D_FF), bias_map),
                  pl.BlockSpec(memory_space=pl.ANY),
                  pl.BlockSpec((1, 1, D_MODEL), bias_map)],
        out_specs=pl.BlockSpec((MOE_BLOCK, D_MODEL // 2), row_map),
        scratch_shapes=[pltpu.VMEM((2, D_MODEL, 2 * D_FF), _F32),
                        pltpu.VMEM((2, D_FF, D_MODEL), _F32),
                        pltpu.VMEM((D_MODEL, 2 * D_FF), _BF16),
                        pltpu.VMEM((D_FF, D_MODEL), _BF16),
                        pltpu.SemaphoreType.DMA((2,)),
                        pltpu.SemaphoreType.DMA((2,))],
    )
    return pl.pallas_call(
        _expert_ffn_kernel,
        out_shape=jax.ShapeDtypeStruct((n_rows, D_MODEL // 2), jnp.uint32),
        grid_spec=grid_spec,
        compiler_params=pltpu.CompilerParams(
            dimension_semantics=("arbitrary",), vmem_limit_bytes=VMEM_LIMIT_BYTES),
        name="expert_ffn",
    )(block_e, block_rows, block_slot, block_next, x_rows, w_gate_up, b_gate_up, w_down, b_down)


def _sc_workers():
    info = plsc.get_sparse_core_info()
    return info.num_cores, info.num_cores * info.num_subcores


def _dispatch(h_packed, dest_flat, n_rows):
    t, width = h_packed.shape
    n_cores, n_workers = _sc_workers()
    n_chunks = t // (n_workers * SC_ROWS)
    chunks_per_k = t // SC_ROWS
    assert n_chunks % 2 == 0

    @functools.partial(
        pl.kernel, mesh=plsc.VectorSubcoreMesh(core_axis_name="c", subcore_axis_name="s"),
        out_type=jax.ShapeDtypeStruct((n_rows, width), h_packed.dtype),
        scratch_types=[pltpu.VMEM((TOP_K, n_chunks, SC_ROWS), jnp.int32),
                       pltpu.VMEM((2, SC_ROWS, width), h_packed.dtype),
                       pltpu.SemaphoreType.DMA((2,)),
                       pltpu.SemaphoreType.DMA((2,))])
    def scatter_rows(h_hbm, dest_hbm, out_hbm, idx_v, buf, lsem, ssem):
        wid = lax.axis_index("s") * n_cores + lax.axis_index("c")
        c0 = wid * n_chunks
        for k in range(TOP_K):
            pltpu.sync_copy(dest_hbm.at[pl.ds(k * chunks_per_k + c0, n_chunks)], idx_v.at[k])

        def load(c, b):
            return pltpu.make_async_copy(h_hbm.at[pl.ds((c0 + c) * SC_ROWS, SC_ROWS)], buf.at[b],
                                         lsem.at[b])

        def scatters(c, b):
            return [pltpu.make_async_copy(buf.at[b], out_hbm.at[idx_v.at[k, c]], ssem.at[b])
                    for k in range(TOP_K)]

        load(0, 0).start()

        @pl.loop(0, n_chunks, step=2)
        def _(ci):
            for b in range(2):
                c = ci + b

                @pl.when(c >= 1)
                def _():
                    for cp in scatters(c - 1, 1 - b):
                        cp.wait()

                @pl.when(c + 1 < n_chunks)
                def _():
                    load(c + 1, 1 - b).start()

                load(c, b).wait()
                for cp in scatters(c, b):
                    cp.start()

        for cp in scatters(n_chunks - 1, 1):
            cp.wait()

    return scatter_rows(h_packed, dest_flat.reshape(TOP_K * chunks_per_k, SC_ROWS))


def _collect(y_rows, dest_flat):
    n_idx = dest_flat.shape[0]
    width = y_rows.shape[1]
    n_cores, n_workers = _sc_workers()
    n_chunks = n_idx // (n_workers * SC_ROWS)
    assert n_chunks % 2 == 0

    @functools.partial(
        pl.kernel, mesh=plsc.VectorSubcoreMesh(core_axis_name="c", subcore_axis_name="s"),
        out_type=jax.ShapeDtypeStruct((n_idx, width), y_rows.dtype),
        scratch_types=[pltpu.VMEM((n_chunks, SC_ROWS), jnp.int32),
                       pltpu.VMEM((2, SC_ROWS, width), y_rows.dtype),
                       pltpu.SemaphoreType.DMA((2,)),
                       pltpu.SemaphoreType.DMA((2,))])
    def gather_rows(y_hbm, dest_hbm, out_hbm, idx_v, buf, gsem, wsem):
        wid = lax.axis_index("s") * n_cores + lax.axis_index("c")
        c0 = wid * n_chunks
        pltpu.sync_copy(dest_hbm.at[pl.ds(c0, n_chunks)], idx_v)

        def gather(c, b):
            return pltpu.make_async_copy(y_hbm.at[idx_v.at[c]], buf.at[b], gsem.at[b])

        def write(c, b):
            return pltpu.make_async_copy(buf.at[b], out_hbm.at[pl.ds((c0 + c) * SC_ROWS, SC_ROWS)],
                                         wsem.at[b])

        gather(0, 0).start()

        @pl.loop(0, n_chunks, step=2)
        def _(ci):
            for b in range(2):
                c = ci + b

                @pl.when(c >= 1)
                def _():
                    write(c - 1, 1 - b).wait()

                @pl.when(c + 1 < n_chunks)
                def _():
                    gather(c + 1, 1 - b).start()

                gather(c, b).wait()
                write(c, b).start()

        write(n_chunks - 1, 1).wait()

    return gather_rows(y_rows, dest_flat.reshape(n_idx // SC_ROWS, SC_ROWS))


def _combine_kernel(x1_ref, ya_ref, tg_ref, g_ref, o_ref):
    acc = x1_ref[...]
    tg = jnp.transpose(tg_ref[...])
    for k in range(TOP_K):
        acc = acc + tg[:, k:k + 1] * _unpack_bf16_halves(ya_ref[k])
    o_ref[...] = _rmsnorm(acc, g_ref[...])


def _combine(x1, y_assign, top_g, g_final):
    t = x1.shape[0]
    tm = TOKEN_TILE
    return pl.pallas_call(
        _combine_kernel,
        out_shape=jax.ShapeDtypeStruct((t, D_MODEL), _F32),
        grid=(t // tm,),
        in_specs=[pl.BlockSpec((tm, D_MODEL), lambda i: (i, 0)),
                  pl.BlockSpec((TOP_K, tm, D_MODEL // 2), lambda i: (0, i, 0)),
                  pl.BlockSpec((2 * TOP_K, tm), lambda i: (0, i)),
                  pl.BlockSpec((1, D_MODEL), lambda i: (0, 0))],
        out_specs=pl.BlockSpec((tm, D_MODEL), lambda i: (i, 0)),
        compiler_params=pltpu.CompilerParams(
            dimension_semantics=("arbitrary",), vmem_limit_bytes=VMEM_LIMIT_BYTES),
        name="combine",
    )(x1, y_assign, top_g, g_final)


def _block_plan(counts, n_blocks):
    padded = ((counts + MOE_BLOCK - 1) // MOE_BLOCK) * MOE_BLOCK
    pad_end = jnp.cumsum(padded)
    pad_start = pad_end - padded
    block_start = (jnp.arange(n_blocks, dtype=jnp.int32) * MOE_BLOCK)[:, None]
    eidx = jnp.arange(N_EXPERTS, dtype=jnp.int32)
    owns = (pad_start[None, :] <= block_start) & (block_start < pad_end[None, :])
    has_blocks = (padded > 0).astype(jnp.int32)
    ordinal = jnp.cumsum(has_blocks) - has_blocks
    later = (eidx[None, :] > eidx[:, None]) & (padded[None, :] > 0)
    next_expert = jnp.min(jnp.where(later, eidx[None, :], N_EXPERTS), axis=1)

    def per_block(per_expert):
        return jnp.sum(jnp.where(owns, per_expert, 0), axis=1).astype(jnp.int32)

    block_e = per_block(eidx[None, :])
    block_rows = per_block(jnp.clip((pad_start + counts)[None, :] - block_start, 0, MOE_BLOCK))
    block_slot = per_block((ordinal % 2)[None, :])
    block_next = per_block(next_expert[None, :])
    return pad_start, (block_e, block_rows, block_slot, block_next)


def kernel(x, norm_mix_g, w_in, conv_w, w_conv_out, ssm_lam_re, ssm_lam_im, ssm_log_dt, ssm_b_re, ssm_b_im, ssm_c_re, ssm_c_im, ssm_d, w_glu, w_out, norm_ffn_g, w_router, b_router, w_gate_up, b_gate_up, w_down, b_down, norm_f_g):
    bsz, seq, d = x.shape
    t = bsz * seq
    x2 = x.reshape(t, d)
    assert seq % TOKEN_TILE == 0 and seq % SSM_TIME_TILE == 0 and w_in.shape[0] == 1

    w_in_b = w_in[0].astype(_BF16)
    n_bcvu = 3 * D_CONV + D_SSM
    g_mix = norm_mix_g[0].reshape(1, d)

    bz, u = _in_proj(x2, g_mix, w_in_b[:, :n_bcvu], conv_w[0], seq)

    tables = _ssm_tables(ssm_lam_re[0], ssm_lam_im[0], ssm_log_dt[0], ssm_b_re[0], ssm_b_im[0],
                         ssm_c_re[0], ssm_c_im[0], ssm_d[0])
    yg = _ssm(u.reshape(bsz, seq, D_SSM), tables).reshape(t, D_SSM)

    x1, h_packed, top_i, top_g, rank, counts = _mix_route(
        x2, bz, yg, g_mix, w_in_b[:, n_bcvu:], w_conv_out[0].astype(_BF16),
        w_glu[0].astype(_BF16), w_out[0].astype(_BF16), norm_ffn_g[0].reshape(1, d),
        jnp.pad(w_router[0], ((0, 0), (0, LANES - N_EXPERTS))).astype(_BF16),
        b_router[0].reshape(N_EXPERTS, 1))

    n_rows = t * TOP_K + N_EXPERTS * MOE_BLOCK
    pad_start, block_plan = _block_plan(counts[:, 0], n_rows // MOE_BLOCK)
    expert_ids = jnp.arange(N_EXPERTS, dtype=jnp.int32)[:, None, None]
    row_start = jnp.sum(jnp.where(top_i[None] == expert_ids, pad_start[:, None, None], 0), axis=0)
    dest = (row_start + rank).reshape(TOP_K * t)

    x_rows = _dispatch(h_packed, dest, n_rows)
    y_rows = _expert_ffn(*block_plan, x_rows, w_gate_up[0],
                         b_gate_up[0].reshape(N_EXPERTS, 1, 2 * D_FF), w_down[0],
                         b_down[0].reshape(N_EXPERTS, 1, D_MODEL))
    y_assign = _collect(y_rows, dest).reshape(TOP_K, t, D_MODEL // 2)
    out = _combine(x1, y_assign, top_g, norm_f_g.reshape(1, d))
    return out.reshape(bsz, seq, d)
```

```python
import functools

import jax
import jax.numpy as jnp
from jax import lax
from jax.experimental import pallas as pl
from jax.experimental.pallas import tpu as pltpu
from jax.experimental.pallas import tpu_sc as plsc

D_MODEL = 1024
D_CONV = 512
CONV_WIDTH = 3
D_SSM = 512
SSM_GROUP = 16
N_SSM_GROUPS = 32
SSM_STATE = 64
N_EXPERTS = 32
TOP_K = 4
D_FF = 1024
SWIGLU_LIMIT = 7.0
SWIGLU_ALPHA = 1.702
MOE_BLOCK = 512
RMS_EPS = 1e-6

LANES = 128
MXU_DIM = 256
CHUNK = 16
SLAB_GROUPS = LANES // SSM_GROUP
N_SLABS = N_SSM_GROUPS // SLAB_GROUPS
SLAB_STATE = SLAB_GROUPS * SSM_STATE
FLAT = CHUNK * LANES
SSM_TIME_TILE = 256
TOKEN_TILE = 512
ROW_CHAIN = 256
SC_ROWS = 64
VMEM_LIMIT_BYTES = 56 * 1024 * 1024

_BF16 = jnp.bfloat16
_F32 = jnp.float32


def _rmsnorm(xf, g):
    return xf * lax.rsqrt(jnp.mean(xf * xf, axis=-1, keepdims=True) + RMS_EPS) * g


def _sigmoid(v):
    return 1.0 / (1.0 + jnp.exp(-v))


def _pack_bf16_halves(v):
    n = v.shape[1] // 2
    bits = pltpu.bitcast(v.astype(_BF16).astype(_F32), jnp.uint32)
    return (bits[:, :n] >> 16) | (bits[:, n:] & jnp.uint32(0xFFFF0000))


def _unpack_bf16_halves(w):
    return jnp.concatenate([pltpu.bitcast(w << 16, _F32),
                            pltpu.bitcast(w & jnp.uint32(0xFFFF0000), _F32)], axis=1)


def _in_proj_kernel(tiles_per_seq, x_ref, g_ref, w_ref, cw_ref, bz_ref, u_ref, hbuf):
    tm = x_ref.shape[0]
    i = pl.program_id(0)

    @pl.when(i % tiles_per_seq == 0)
    def _():
        hbuf[0:8, :] = jnp.zeros((8, D_CONV), _F32)

    xn = _rmsnorm(x_ref[...], g_ref[...]).astype(_BF16)
    cv = jnp.dot(xn, w_ref[:, D_CONV:3 * D_CONV], preferred_element_type=_F32)
    hbuf[8:8 + tm, :] = cv[:, :D_CONV] * cv[:, D_CONV:]
    u_ref[...] = jnp.dot(xn, w_ref[:, 3 * D_CONV:], preferred_element_type=_F32)
    cw = cw_ref[...]
    z = (cw[0:1, :] * hbuf[6:6 + tm, :] + cw[1:2, :] * hbuf[7:7 + tm, :]
         + cw[2:3, :] * hbuf[8:8 + tm, :])
    b_gate = jnp.dot(xn, w_ref[:, :D_CONV], preferred_element_type=_F32)
    bz_ref[...] = (b_gate * z).astype(_BF16)
    hbuf[0:8, :] = hbuf[tm:tm + 8, :]


def _in_proj(x2, g, w_bcvu, conv_w, seq):
    t = x2.shape[0]
    tm = TOKEN_TILE
    return pl.pallas_call(
        functools.partial(_in_proj_kernel, seq // tm),
        out_shape=(jax.ShapeDtypeStruct((t, D_CONV), _BF16),
                   jax.ShapeDtypeStruct((t, D_SSM), _F32)),
        grid=(t // tm,),
        in_specs=[pl.BlockSpec((tm, D_MODEL), lambda i: (i, 0)),
                  pl.BlockSpec((1, D_MODEL), lambda i: (0, 0)),
                  pl.BlockSpec((D_MODEL, 3 * D_CONV + D_SSM), lambda i: (0, 0)),
                  pl.BlockSpec((CONV_WIDTH, D_CONV), lambda i: (0, 0))],
        out_specs=(pl.BlockSpec((tm, D_CONV), lambda i: (i, 0)),
                   pl.BlockSpec((tm, D_SSM), lambda i: (i, 0))),
        scratch_shapes=[pltpu.VMEM((tm + 8, D_CONV), _F32)],
        compiler_params=pltpu.CompilerParams(
            dimension_semantics=("arbitrary",), vmem_limit_bytes=VMEM_LIMIT_BYTES),
        name="in_proj",
    )(x2, g, w_bcvu, conv_w)


def _ssm_prep_kernel(lr_ref, lc_ref, bm_ref, cm_ref, d_ref, toep_ref, bst_ref, cst_ref, a_ref):
    def discretise(lre, lim, log_dt):
        dt = jnp.exp(log_dt)
        mag = jnp.exp(lre * dt)
        return mag * jnp.cos(lim * dt), mag * jnp.sin(lim * dt)

    def powers(are, aim):
        pre, pim = [jnp.ones_like(are)], [jnp.zeros_like(are)]
        for _ in range(CHUNK):
            pre, pim = (pre + [pre[-1] * are - pim[-1] * aim],
                        pim + [pre[-1] * aim + pim[-1] * are])
        return pre, pim

    lr = lr_ref[0]
    lre, lim = lr[0:1, :], lr[1:2, :]
    are, aim = discretise(lre, lim, lr[2:3, :])
    pre, pim = powers(are, aim)
    den = lre * lre + lim * lim
    q_re = ((are - 1.0) * lre + aim * lim) / den
    q_im = (aim * lre - (are - 1.0) * lim) / den
    bb_re = q_re * bm_ref[0, 0] - q_im * bm_ref[0, 1]
    bb_im = q_re * bm_ref[0, 1] + q_im * bm_ref[0, 0]
    cm_re, cm_im = cm_ref[0, 0], cm_ref[0, 1]

    def split_bf16(v):
        v_hi = v.astype(_BF16)
        return v_hi, (v - v_hi.astype(_F32)).astype(_BF16)

    c_hi, c_lo = split_bf16(jnp.concatenate([cm_re, -cm_im], axis=0))
    kblk = []
    for k in range(CHUNK):
        ab_re = bb_re * pre[k] - bb_im * pim[k]
        ab_im = bb_re * pim[k] + bb_im * pre[k]
        rows = slice((CHUNK - 1 - k) * LANES, (CHUNK - k) * LANES)
        bst_ref[0, rows, :SLAB_STATE] = ab_re.astype(_BF16)
        bst_ref[0, rows, SLAB_STATE:] = ab_im.astype(_BF16)
        ab_hi, ab_lo = split_bf16(jnp.concatenate([ab_re, ab_im], axis=1))
        kblk.append(jnp.dot(ab_hi, c_hi, preferred_element_type=_F32)
                    + (jnp.dot(ab_lo, c_hi, preferred_element_type=_F32)
                       + jnp.dot(ab_hi, c_lo, preferred_element_type=_F32)))
    r = lax.broadcasted_iota(jnp.int32, (LANES, LANES), 0)
    c = lax.broadcasted_iota(jnp.int32, (LANES, LANES), 1)
    kblk[0] = kblk[0] + jnp.where(r == c, jnp.broadcast_to(d_ref[0], (LANES, LANES)), 0.0)
    kblk = [kb.astype(_BF16) for kb in kblk]
    zeros = jnp.zeros((LANES, LANES), _BF16)
    for sp in range(CHUNK):
        for s in range(CHUNK):
            toep_ref[0, sp * LANES:(sp + 1) * LANES, s * LANES:(s + 1) * LANES] = (
                kblk[s - sp] if s >= sp else zeros)

    lc = lc_ref[0]
    cre, cim = discretise(lc[:, 0:1], lc[:, 1:2], lc[:, 2:3])
    qre, qim = powers(cre, cim)
    for s in range(CHUNK):
        cols = slice(s * LANES, (s + 1) * LANES)
        cst_ref[0, :SLAB_STATE, cols] = (cm_re * qre[s + 1] - cm_im * qim[s + 1]).astype(_BF16)
        cst_ref[0, SLAB_STATE:, cols] = (-(cm_re * qim[s + 1] + cm_im * qre[s + 1])).astype(_BF16)
    a_ref[0, 0:1, :] = pre[CHUNK]
    a_ref[0, 1:2, :] = pim[CHUNK]


def _ssm_tables(lam_re, lam_im, log_dt, b_re, b_im, c_re, c_im, d_skip):
    sg = (N_SLABS, SLAB_GROUPS)
    eye = jnp.eye(SLAB_GROUPS, dtype=_F32)
    lam = jnp.stack([lam_re, lam_im, jnp.broadcast_to(log_dt[:, None], lam_re.shape)], axis=0)
    lam_row = lam.reshape(3, N_SLABS, SLAB_STATE).transpose(1, 0, 2)
    lam_col = lam_row.transpose(0, 2, 1)

    def b_blockdiag(b):
        bt = b.reshape(*sg, SSM_STATE, SSM_GROUP).transpose(0, 1, 3, 2)
        return (bt[:, :, :, None, :] * eye[None, :, None, :, None]).reshape(N_SLABS, LANES, SLAB_STATE)

    def c_blockdiag(c):
        ct = c.reshape(*sg, SSM_GROUP, SSM_STATE).transpose(0, 1, 3, 2)
        return (ct[:, :, :, None, :] * eye[None, :, None, :, None]).reshape(N_SLABS, SLAB_STATE, LANES)

    bm = jnp.stack([b_blockdiag(b_re), b_blockdiag(b_im)], axis=1)
    cm = jnp.stack([c_blockdiag(c_re), c_blockdiag(c_im)], axis=1)
    d = d_skip.reshape(N_SLABS, 1, LANES)
    slab3 = lambda sl: (sl, 0, 0)
    slab4 = lambda sl: (sl, 0, 0, 0)
    return pl.pallas_call(
        _ssm_prep_kernel,
        out_shape=(jax.ShapeDtypeStruct((N_SLABS, FLAT, FLAT), _BF16),
                   jax.ShapeDtypeStruct((N_SLABS, FLAT, 2 * SLAB_STATE), _BF16),
                   jax.ShapeDtypeStruct((N_SLABS, 2 * SLAB_STATE, FLAT), _BF16),
                   jax.ShapeDtypeStruct((N_SLABS, 2, SLAB_STATE), _F32)),
        grid=(N_SLABS,),
        in_specs=[pl.BlockSpec((1, 3, SLAB_STATE), slab3),
                  pl.BlockSpec((1, SLAB_STATE, 3), slab3),
                  pl.BlockSpec((1, 2, LANES, SLAB_STATE), slab4),
                  pl.BlockSpec((1, 2, SLAB_STATE, LANES), slab4),
                  pl.BlockSpec((1, 1, LANES), slab3)],
        out_specs=(pl.BlockSpec((1, FLAT, FLAT), slab3),
                   pl.BlockSpec((1, FLAT, 2 * SLAB_STATE), slab3),
                   pl.BlockSpec((1, 2 * SLAB_STATE, FLAT), slab3),
                   pl.BlockSpec((1, 2, SLAB_STATE), slab3)),
        compiler_params=pltpu.CompilerParams(
            dimension_semantics=("arbitrary",), vmem_limit_bytes=VMEM_LIMIT_BYTES),
        name="ssm_prep",
    )(lam_row, lam_col, bm, cm, d)


def _ssm_kernel(u_ref, toep_ref, bst_ref, cst_ref, a_ref, y_ref, uflat, s_scr, xc_scr, carry, ytoep):
    nb, tt, _ = u_ref.shape
    nch = tt // CHUNK
    n = nb * nch

    @pl.when(pl.program_id(1) == 0)
    def _():
        carry[...] = jnp.zeros_like(carry)

    for s in range(CHUNK):
        part = u_ref[:, pl.ds(s, nch, stride=CHUNK), :]
        uflat[:, s * LANES:(s + 1) * LANES] = part.reshape(n, LANES).astype(_BF16)

    n_cb = FLAT // MXU_DIM

    def toeplitz(cb):
        kk = (cb + 1) * MXU_DIM
        cols = slice(cb * MXU_DIM, kk)
        ytoep[:, cols] = jnp.dot(uflat[:, :kk], toep_ref[0, :kk, cols], preferred_element_type=_F32)

    for cb in range(n_cb // 2):
        toeplitz(cb)

    nblk = SLAB_STATE // LANES
    loc_all = jnp.dot(uflat[...], bst_ref[0], preferred_element_type=_F32)
    for cb in range(n_cb // 2, n_cb):
        toeplitz(cb)
    for kb in range(2 * nblk):
        s_scr[kb] = loc_all[:, kb * LANES:(kb + 1) * LANES]

    a = a_ref[0]
    are = [jnp.broadcast_to(a[0:1, kb * LANES:(kb + 1) * LANES], (nb, LANES)) for kb in range(nblk)]
    aim = [jnp.broadcast_to(a[1:2, kb * LANES:(kb + 1) * LANES], (nb, LANES)) for kb in range(nblk)]
    xr = [carry[kb] for kb in range(nblk)]
    xi = [carry[nblk + kb] for kb in range(nblk)]
    for j in range(nch):
        rows = pl.ds(j, nb, stride=nch)
        for kb in range(nblk):
            xc_scr[kb, rows, :] = xr[kb]
            xc_scr[nblk + kb, rows, :] = xi[kb]
            nr = are[kb] * xr[kb] - aim[kb] * xi[kb] + s_scr[kb, rows, :]
            ni = are[kb] * xi[kb] + aim[kb] * xr[kb] + s_scr[nblk + kb, rows, :]
            xr[kb], xi[kb] = nr, ni
    for kb in range(nblk):
        carry[kb] = xr[kb]
        carry[nblk + kb] = xi[kb]

    xc = jnp.concatenate([xc_scr[kb] for kb in range(2 * nblk)], axis=1).astype(_BF16)
    for cb in range(n_cb):
        cols = slice(cb * MXU_DIM, (cb + 1) * MXU_DIM)
        y = ytoep[:, cols] + jnp.dot(xc, cst_ref[0, :, cols], preferred_element_type=_F32)
        y = jax.nn.gelu(y)
        for h in range(MXU_DIM // LANES):
            s = cb * (MXU_DIM // LANES) + h
            y_ref[:, pl.ds(s, nch, stride=CHUNK), :] = (
                y[:, h * LANES:(h + 1) * LANES].reshape(nb, nch, LANES))


def _ssm(u3, tables):
    toep, bst, cst, a_chunk = tables
    nb, seq, _ = u3.shape
    tt = SSM_TIME_TILE
    n = nb * (tt // CHUNK)
    return pl.pallas_call(
        _ssm_kernel,
        out_shape=jax.ShapeDtypeStruct(u3.shape, _F32),
        grid=(N_SLABS, seq // tt),
        in_specs=[pl.BlockSpec((nb, tt, LANES), lambda sl, ti: (0, ti, sl)),
                  pl.BlockSpec((1, FLAT, FLAT), lambda sl, ti: (sl, 0, 0)),
                  pl.BlockSpec((1, FLAT, 2 * SLAB_STATE), lambda sl, ti: (sl, 0, 0)),
                  pl.BlockSpec((1, 2 * SLAB_STATE, FLAT), lambda sl, ti: (sl, 0, 0)),
                  pl.BlockSpec((1, 2, SLAB_STATE), lambda sl, ti: (sl, 0, 0))],
        out_specs=pl.BlockSpec((nb, tt, LANES), lambda sl, ti: (0, ti, sl)),
        scratch_shapes=[pltpu.VMEM((n, FLAT), _BF16),
                        pltpu.VMEM((2 * SLAB_STATE // LANES, n, LANES), _F32),
                        pltpu.VMEM((2 * SLAB_STATE // LANES, n, LANES), _F32),
                        pltpu.VMEM((2 * SLAB_STATE // LANES, nb, LANES), _F32),
                        pltpu.VMEM((n, FLAT), _F32)],
        compiler_params=pltpu.CompilerParams(
            dimension_semantics=("arbitrary", "arbitrary"), vmem_limit_bytes=VMEM_LIMIT_BYTES),
        name="ssm",
    )(u3, toep, bst, cst, a_chunk)


def _mix_route_kernel(x_ref, bz_ref, yg_ref, gm_ref, wg_ref, wco_ref, wglu_ref, wout_ref,
                      gf_ref, wr_ref, br_ref,
                      x1_ref, h_ref, ti_ref, tg_ref, rk_ref, cnt_ref, base, merged):
    tm = x_ref.shape[0]

    @pl.when(pl.program_id(0) == 0)
    def _():
        base[...] = jnp.zeros_like(base)

    chains = [slice(r0, r0 + ROW_CHAIN) for r0 in range(0, tm, ROW_CHAIN)]
    hs = [_mix_rows(rows, x_ref, bz_ref, yg_ref, gm_ref, wg_ref, wco_ref, wglu_ref, wout_ref,
                    gf_ref, x1_ref, h_ref, merged) for rows in chains]
    for rows, h in zip(chains, hs):
        _route_rows(rows, h, wr_ref, br_ref, ti_ref, tg_ref, rk_ref, base)
    cnt_ref[...] = base[...].astype(jnp.int32)


def _mix_rows(rows, x_ref, bz_ref, yg_ref, gm_ref, wg_ref, wco_ref, wglu_ref, wout_ref,
              gf_ref, x1_ref, h_ref, merged):
    x = x_ref[rows, :]
    xn = _rmsnorm(x, gm_ref[...]).astype(_BF16)
    bz = bz_ref[rows, :]
    yg = yg_ref[rows, :].astype(_BF16)
    for c in range(D_MODEL // MXU_DIM):
        lo = slice(c * MXU_DIM, (c + 1) * MXU_DIM)
        hi = slice(D_MODEL + c * MXU_DIM, D_MODEL + (c + 1) * MXU_DIM)
        gate_a = jnp.dot(xn, wg_ref[:, lo], preferred_element_type=_F32)
        gate_b = jnp.dot(xn, wg_ref[:, hi], preferred_element_type=_F32)
        y_a = jnp.dot(bz, wco_ref[:, lo], preferred_element_type=_F32)
        val = jnp.dot(yg, wglu_ref[:, lo], preferred_element_type=_F32)
        glu_gate = jnp.dot(yg, wglu_ref[:, hi], preferred_element_type=_F32)
        y_b = val * _sigmoid(glu_gate)
        merged[rows, lo] = (_sigmoid(gate_a) * y_a + _sigmoid(gate_b) * y_b).astype(_BF16)
    x1 = x + jnp.dot(merged[rows, :], wout_ref[...], preferred_element_type=_F32)
    x1_ref[rows, :] = x1
    h = _rmsnorm(x1, gf_ref[...])
    h_ref[rows, :] = _pack_bf16_halves(h)
    return h.astype(_BF16)


def _route_rows(rows, h, wr_ref, br_ref, ti_ref, tg_ref, rk_ref, base):
    tm = rows.stop - rows.start
    logits_tok = jnp.dot(h, wr_ref[...], preferred_element_type=_F32)
    logits = jnp.transpose(logits_tok)[:N_EXPERTS, :] + br_ref[...]
    erow = lax.broadcasted_iota(jnp.int32, (N_EXPERTS, tm), 0).astype(_F32)
    neg_inf = jnp.float32(-jnp.inf)
    work = logits
    vals, idxs = [], []
    for _ in range(TOP_K):
        m = jnp.max(work, axis=0, keepdims=True)
        idx = jnp.min(jnp.where(work == m, erow, float(N_EXPERTS)), axis=0, keepdims=True)
        vals.append(m)
        idxs.append(idx)
        work = jnp.where(erow == idx, neg_inf, work)
    exps = [jnp.exp(v - vals[0]) for v in vals]
    denom = exps[0] + exps[1] + exps[2] + exps[3]

    sel = jnp.zeros((N_EXPERTS, tm), _F32)
    for idx in idxs:
        sel = sel + (erow == idx).astype(_F32)
    row = lax.broadcasted_iota(jnp.int32, (tm, tm), 0)
    col = lax.broadcasted_iota(jnp.int32, (tm, tm), 1)
    earlier = (row < col).astype(_BF16)
    before = jnp.dot(sel.astype(_BF16), earlier, preferred_element_type=_F32) + base[...]
    for k in range(TOP_K):
        ti_ref[k:k + 1, rows] = idxs[k].astype(jnp.int32)
        tg_ref[k:k + 1, rows] = exps[k] / denom
        tg_ref[TOP_K + k:TOP_K + k + 1, rows] = jnp.zeros((1, tm), _F32)
        rk = jnp.sum(jnp.where(erow == idxs[k], before, 0.0), axis=0, keepdims=True)
        rk_ref[k:k + 1, rows] = rk.astype(jnp.int32)
    base[...] = base[...] + jnp.sum(sel, axis=1, keepdims=True)


def _mix_route(x2, bz, yg, g_mix, w_gates, w_conv_out, w_glu, w_out, g_ffn, w_router, b_router):
    t = x2.shape[0]
    tm = TOKEN_TILE
    tok = lambda i: (i, 0)
    tok_lanes = lambda i: (0, i)
    fixed = lambda i: (0, 0)
    return pl.pallas_call(
        _mix_route_kernel,
        out_shape=(jax.ShapeDtypeStruct((t, D_MODEL), _F32),
                   jax.ShapeDtypeStruct((t, D_MODEL // 2), jnp.uint32),
                   jax.ShapeDtypeStruct((TOP_K, t), jnp.int32),
                   jax.ShapeDtypeStruct((2 * TOP_K, t), _F32),
                   jax.ShapeDtypeStruct((TOP_K, t), jnp.int32),
                   jax.ShapeDtypeStruct((N_EXPERTS, 1), jnp.int32)),
        grid=(t // tm,),
        in_specs=[pl.BlockSpec((tm, D_MODEL), tok),
                  pl.BlockSpec((tm, D_CONV), tok),
                  pl.BlockSpec((tm, D_SSM), tok),
                  pl.BlockSpec((1, D_MODEL), fixed),
                  pl.BlockSpec((D_MODEL, 2 * D_MODEL), fixed),
                  pl.BlockSpec((D_CONV, D_MODEL), fixed),
                  pl.BlockSpec((D_SSM, 2 * D_MODEL), fixed),
                  pl.BlockSpec((D_MODEL, D_MODEL), fixed),
                  pl.BlockSpec((1, D_MODEL), fixed),
                  pl.BlockSpec((D_MODEL, LANES), fixed),
                  pl.BlockSpec((N_EXPERTS, 1), fixed)],
        out_specs=(pl.BlockSpec((tm, D_MODEL), tok),
                   pl.BlockSpec((tm, D_MODEL // 2), tok),
                   pl.BlockSpec((TOP_K, tm), tok_lanes),
                   pl.BlockSpec((2 * TOP_K, tm), tok_lanes),
                   pl.BlockSpec((TOP_K, tm), tok_lanes),
                   pl.BlockSpec((N_EXPERTS, 1), fixed)),
        scratch_shapes=[pltpu.VMEM((N_EXPERTS, 1), _F32),
                        pltpu.VMEM((tm, D_MODEL), _BF16)],
        compiler_params=pltpu.CompilerParams(
            dimension_semantics=("arbitrary",), vmem_limit_bytes=VMEM_LIMIT_BYTES),
        name="mix_route",
    )(x2, bz, yg, g_mix, w_gates, w_conv_out, w_glu, w_out, g_ffn, w_router, b_router)


def _expert_ffn_kernel(be_ref, nr_ref, slot_ref, next_ref, x_ref, wgu_hbm, bgu_ref, wd_hbm, bd_ref,
                       y_ref, stage_gu, stage_d, wgu_b, wd_b, sem_gu, sem_d):
    b = pl.program_id(0)
    expert = be_ref[b]
    live = nr_ref[b] > 0

    def weight_copies(e, slot):
        return (pltpu.make_async_copy(wgu_hbm.at[e], stage_gu.at[slot], sem_gu.at[slot]),
                pltpu.make_async_copy(wd_hbm.at[e], stage_d.at[slot], sem_d.at[slot]))

    @pl.when(live & ((b == 0) | (be_ref[jnp.maximum(b - 1, 0)] != expert)))
    def _():
        slot = slot_ref[b]

        @pl.when(b == 0)
        def _():
            for cp in weight_copies(expert, slot):
                cp.start()

        for cp in weight_copies(expert, slot):
            cp.wait()
        wgu_b[...] = stage_gu[slot].astype(_BF16)
        wd_b[...] = stage_d[slot].astype(_BF16)

        @pl.when(next_ref[b] < N_EXPERTS)
        def _():
            for cp in weight_copies(next_ref[b], 1 - slot):
                cp.start()

    def ffn_rows(n_rows):
        xw = x_ref[:n_rows, :]
        valid = lax.broadcasted_iota(jnp.int32, xw.shape, 0) < nr_ref[b]
        x = _unpack_bf16_halves(jnp.where(valid, xw, jnp.uint32(0))).astype(_BF16)
        hgu = jnp.dot(x, wgu_b[...], preferred_element_type=_F32) + bgu_ref[0]
        g = jnp.minimum(hgu[:, :D_FF], SWIGLU_LIMIT)
        up = jnp.clip(hgu[:, D_FF:], -SWIGLU_LIMIT, SWIGLU_LIMIT)
        act = (up + 1.0) * (g * _sigmoid(SWIGLU_ALPHA * g))
        y = jnp.dot(act.astype(_BF16), wd_b[...], preferred_element_type=_F32) + bd_ref[0]
        y_ref[:n_rows, :] = _pack_bf16_halves(y)
        if n_rows < MOE_BLOCK:
            y_ref[n_rows:, :] = jnp.zeros((MOE_BLOCK - n_rows, y_ref.shape[1]), y_ref.dtype)

    half = MOE_BLOCK // 2

    @pl.when(nr_ref[b] > half)
    def _():
        ffn_rows(MOE_BLOCK)

    @pl.when(live & (nr_ref[b] <= half))
    def _():
        ffn_rows(half)

    @pl.when(jnp.logical_not(live))
    def _():
        y_ref[...] = jnp.zeros_like(y_ref)


def _expert_ffn(block_e, block_rows, block_slot, block_next, x_rows, w_gate_up, b_gate_up, w_down,
                b_down):
    n_rows = x_rows.shape[0]
    n_blocks = n_rows // MOE_BLOCK

    def bias_map(b, be, nr, sl, nx):
        return (be[b], 0, 0)

    def row_map(b, be, nr, sl, nx):
        return (b, 0)

    grid_spec = pltpu.PrefetchScalarGridSpec(
        num_scalar_prefetch=4,
        grid=(n_blocks,),
        in_specs=[pl.BlockSpec((MOE_BLOCK, D_MODEL // 2), row_map),
                  pl.BlockSpec(memory_space=pl.ANY),
                  pl.BlockSpec((1, 1, 2 * D_FF), bias_map),
                  pl.BlockSpec(memory_space=pl.ANY),
                  pl.BlockSpec((1, 1, D_MODEL), bias_map)],
        out_specs=pl.BlockSpec((MOE_BLOCK, D_MODEL // 2), row_map),
        scratch_shapes=[pltpu.VMEM((2, D_MODEL, 2 * D_FF), _F32),
                        pltpu.VMEM((2, D_FF, D_MODEL), _F32),
                        pltpu.VMEM((D_MODEL, 2 * D_FF), _BF16),
                        pltpu.VMEM((D_FF, D_MODEL), _BF16),
                        pltpu.SemaphoreType.DMA((2,)),
                        pltpu.SemaphoreType.DMA((2,))],
    )
    return pl.pallas_call(
        _expert_ffn_kernel,
        out_shape=jax.ShapeDtypeStruct((n_rows, D_MODEL // 2), jnp.uint32),
        grid_spec=grid_spec,
        compiler_params=pltpu.CompilerParams(
            dimension_semantics=("arbitrary",), vmem_limit_bytes=VMEM_LIMIT_BYTES),
        name="expert_ffn",
    )(block_e, block_rows, block_slot, block_next, x_rows, w_gate_up, b_gate_up, w_down, b_down)


def _sc_workers():
    info = plsc.get_sparse_core_info()
    return info.num_cores, info.num_cores * info.num_subcores


def _dispatch(h_packed, dest_flat, n_rows):
    t, width = h_packed.shape
    n_cores, n_workers = _sc_workers()
    n_chunks = t // (n_workers * SC_ROWS)
    chunks_per_k = t // SC_ROWS
    assert n_chunks % 2 == 0

    @functools.partial(
        pl.kernel, mesh=plsc.VectorSubcoreMesh(core_axis_name="c", subcore_axis_name="s"),
        out_type=jax.ShapeDtypeStruct((n_rows, width), h_packed.dtype),
        scratch_types=[pltpu.VMEM((TOP_K, n_chunks, SC_ROWS), jnp.int32),
                       pltpu.VMEM((2, SC_ROWS, width), h_packed.dtype),
                       pltpu.SemaphoreType.DMA((2,)),
                       pltpu.SemaphoreType.DMA((2,))])
    def scatter_rows(h_hbm, dest_hbm, out_hbm, idx_v, buf, lsem, ssem):
        wid = lax.axis_index("s") * n_cores + lax.axis_index("c")
        c0 = wid * n_chunks
        for k in range(TOP_K):
            pltpu.sync_copy(dest_hbm.at[pl.ds(k * chunks_per_k + c0, n_chunks)], idx_v.at[k])

        def load(c, b):
            return pltpu.make_async_copy(h_hbm.at[pl.ds((c0 + c) * SC_ROWS, SC_ROWS)], buf.at[b],
                                         lsem.at[b])

        def scatters(c, b):
            return [pltpu.make_async_copy(buf.at[b], out_hbm.at[idx_v.at[k, c]], ssem.at[b])
                    for k in range(TOP_K)]

        load(0, 0).start()

        @pl.loop(0, n_chunks, step=2)
        def _(ci):
            for b in range(2):
                c = ci + b

                @pl.when(c >= 1)
                def _():
                    for cp in scatters(c - 1, 1 - b):
                        cp.wait()

                @pl.when(c + 1 < n_chunks)
                def _():
                    load(c + 1, 1 - b).start()

                load(c, b).wait()
                for cp in scatters(c, b):
                    cp.start()

        for cp in scatters(n_chunks - 1, 1):
            cp.wait()

    return scatter_rows(h_packed, dest_flat.reshape(TOP_K * chunks_per_k, SC_ROWS))


def _collect(y_rows, dest_flat):
    n_idx = dest_flat.shape[0]
    width = y_rows.shape[1]
    n_cores, n_workers = _sc_workers()
    n_chunks = n_idx // (n_workers * SC_ROWS)
    assert n_chunks % 2 == 0

    @functools.partial(
        pl.kernel, mesh=plsc.VectorSubcoreMesh(core_axis_name="c", subcore_axis_name="s"),
        out_type=jax.ShapeDtypeStruct((n_idx, width), y_rows.dtype),
        scratch_types=[pltpu.VMEM((n_chunks, SC_ROWS), jnp.int32),
                       pltpu.VMEM((2, SC_ROWS, width), y_rows.dtype),
                       pltpu.SemaphoreType.DMA((2,)),
                       pltpu.SemaphoreType.DMA((2,))])
    def gather_rows(y_hbm, dest_hbm, out_hbm, idx_v, buf, gsem, wsem):
        wid = lax.axis_index("s") * n_cores + lax.axis_index("c")
        c0 = wid * n_chunks
        pltpu.sync_copy(dest_hbm.at[pl.ds(c0, n_chunks)], idx_v)

        def gather(c, b):
            return pltpu.make_async_copy(y_hbm.at[idx_v.at[c]], buf.at[b], gsem.at[b])

        def write(c, b):
            return pltpu.make_async_copy(buf.at[b], out_hbm.at[pl.ds((c0 + c) * SC_ROWS, SC_ROWS)],
                                         wsem.at[b])

        gather(0, 0).start()

        @pl.loop(0, n_chunks, step=2)
        def _(ci):
            for b in range(2):
                c = ci + b

                @pl.when(c >= 1)
                def _():
                    write(c - 1, 1 - b).wait()

                @pl.when(c + 1 < n_chunks)
                def _():
                    gather(c + 1, 1 - b).start()

                gather(c, b).wait()
                write(c, b).start()

        write(n_chunks - 1, 1).wait()

    return gather_rows(y_rows, dest_flat.reshape(n_idx // SC_ROWS, SC_ROWS))


def _combine_kernel(x1_ref, ya_ref, tg_ref, g_ref, o_ref):
    acc = x1_ref[...]
    tg = jnp.transpose(tg_ref[...])
    for k in range(TOP_K):
        acc = acc + tg[:, k:k + 1] * _unpack_bf16_halves(ya_ref[k])
    o_ref[...] = _rmsnorm(acc, g_ref[...])


def _combine(x1, y_assign, top_g, g_final):
    t = x1.shape[0]
    tm = TOKEN_TILE
    return pl.pallas_call(
        _combine_kernel,
        out_shape=jax.ShapeDtypeStruct((t, D_MODEL), _F32),
        grid=(t // tm,),
        in_specs=[pl.BlockSpec((tm, D_MODEL), lambda i: (i, 0)),
                  pl.BlockSpec((TOP_K, tm, D_MODEL // 2), lambda i: (0, i, 0)),
                  pl.BlockSpec((2 * TOP_K, tm), lambda i: (0, i)),
                  pl.BlockSpec((1, D_MODEL), lambda i: (0, 0))],
        out_specs=pl.BlockSpec((tm, D_MODEL), lambda i: (i, 0)),
        compiler_params=pltpu.CompilerParams(
            dimension_semantics=("arbitrary",), vmem_limit_bytes=VMEM_LIMIT_BYTES),
        name="combine",
    )(x1, y_assign, top_g, g_final)


def _block_plan(counts, n_blocks):
    padded = ((counts + MOE_BLOCK - 1) // MOE_BLOCK) * MOE_BLOCK
    pad_end = jnp.cumsum(padded)
    pad_start = pad_end - padded
    block_start = (jnp.arange(n_blocks, dtype=jnp.int32) * MOE_BLOCK)[:, None]
    eidx = jnp.arange(N_EXPERTS, dtype=jnp.int32)
    owns = (pad_start[None, :] <= block_start) & (block_start < pad_end[None, :])
    has_blocks = (padded > 0).astype(jnp.int32)
    ordinal = jnp.cumsum(has_blocks) - has_blocks
    later = (eidx[None, :] > eidx[:, None]) & (padded[None, :] > 0)
    next_expert = jnp.min(jnp.where(later, eidx[None, :], N_EXPERTS), axis=1)

    def per_block(per_expert):
        return jnp.sum(jnp.where(owns, per_expert, 0), axis=1).astype(jnp.int32)

    block_e = per_block(eidx[None, :])
    block_rows = per_block(jnp.clip((pad_start + counts)[None, :] - block_start, 0, MOE_BLOCK))
    block_slot = per_block((ordinal % 2)[None, :])
    block_next = per_block(next_expert[None, :])
    return pad_start, (block_e, block_rows, block_slot, block_next)


def kernel(x, norm_mix_g, w_in, conv_w, w_conv_out, ssm_lam_re, ssm_lam_im, ssm_log_dt, ssm_b_re, ssm_b_im, ssm_c_re, ssm_c_im, ssm_d, w_glu, w_out, norm_ffn_g, w_router, b_router, w_gate_up, b_gate_up, w_down, b_down, norm_f_g):
    bsz, seq, d = x.shape
    t = bsz * seq
    x2 = x.reshape(t, d)
    assert seq % TOKEN_TILE == 0 and seq % SSM_TIME_TILE == 0 and w_in.shape[0] == 1

    w_in_b = w_in[0].astype(_BF16)
    n_bcvu = 3 * D_CONV + D_SSM
    g_mix = norm_mix_g[0].reshape(1, d)

    bz, u = _in_proj(x2, g_mix, w_in_b[:, :n_bcvu], conv_w[0], seq)

    tables = _ssm_tables(ssm_lam_re[0], ssm_lam_im[0], ssm_log_dt[0], ssm_b_re[0], ssm_b_im[0],
                         ssm_c_re[0], ssm_c_im[0], ssm_d[0])
    yg = _ssm(u.reshape(bsz, seq, D_SSM), tables).reshape(t, D_SSM)

    x1, h_packed, top_i, top_g, rank, counts = _mix_route(
        x2, bz, yg, g_mix, w_in_b[:, n_bcvu:], w_conv_out[0].astype(_BF16),
        w_glu[0].astype(_BF16), w_out[0].astype(_BF16), norm_ffn_g[0].reshape(1, d),
        jnp.pad(w_router[0], ((0, 0), (0, LANES - N_EXPERTS))).astype(_BF16),
        b_router[0].reshape(N_EXPERTS, 1))

    n_rows = t * TOP_K + N_EXPERTS * MOE_BLOCK
    pad_start, block_plan = _block_plan(counts[:, 0], n_rows // MOE_BLOCK)
    expert_ids = jnp.arange(N_EXPERTS, dtype=jnp.int32)[:, None, None]
    row_start = jnp.sum(jnp.where(top_i[None] == expert_ids, pad_start[:, None, None], 0), axis=0)
    dest = (row_start + rank).reshape(TOP_K * t)

    x_rows = _dispatch(h_packed, dest, n_rows)
    y_rows = _expert_ffn(*block_plan, x_rows, w_gate_up[0],
                         b_gate_up[0].reshape(N_EXPERTS, 1, 2 * D_FF), w_down[0],
                         b_down[0].reshape(N_EXPERTS, 1, D_MODEL))
    y_assign = _collect(y_rows, dest).reshape(TOP_K, t, D_MODEL // 2)
    out = _combine(x1, y_assign, top_g, norm_f_g.reshape(1, d))
    return out.reshape(bsz, seq, d)
```

```python
import functools

import jax
import jax.numpy as jnp
from jax import lax
from jax.experimental import pallas as pl
from jax.experimental.pallas import tpu as pltpu
from jax.experimental.pallas import tpu_sc as plsc

D_MODEL = 1024
D_CONV = 512
CONV_WIDTH = 3
D_SSM = 512
SSM_GROUP = 16
N_SSM_GROUPS = 32
SSM_STATE = 64
N_EXPERTS = 32
TOP_K = 4
D_FF = 1024
SWIGLU_LIMIT = 7.0
SWIGLU_ALPHA = 1.702
MOE_BLOCK = 512
RMS_EPS = 1e-6

LANES = 128
SUBLANES = 8
MXU_DIM = 256
CHUNK = 16
SLAB_GROUPS = LANES // SSM_GROUP
N_SLABS = N_SSM_GROUPS // SLAB_GROUPS
SLAB_STATE = SLAB_GROUPS * SSM_STATE
FLAT = CHUNK * LANES
SSM_TIME_TILE = 256
TOKEN_TILE = 512
IN_TILE = 1024
IN_CHAIN = 512
MIX_TILE = 1024
ROW_CHAIN = 256
SC_ROWS = 64
VMEM_LIMIT_BYTES = 56 * 1024 * 1024

_BF16 = jnp.bfloat16
_F32 = jnp.float32


def _rmsnorm(xf, g):
    return xf * lax.rsqrt(jnp.mean(xf * xf, axis=-1, keepdims=True) + RMS_EPS) * g


def _sigmoid(v):
    return 1.0 / (1.0 + jnp.exp(-v))


def _pack_bf16_halves(v):
    n = v.shape[1] // 2
    bits = pltpu.bitcast(v.astype(_BF16).astype(_F32), jnp.uint32)
    return (bits[:, :n] >> 16) | (bits[:, n:] & jnp.uint32(0xFFFF0000))


def _unpack_bf16_halves(w):
    return jnp.concatenate([pltpu.bitcast(w << 16, _F32),
                            pltpu.bitcast(w & jnp.uint32(0xFFFF0000), _F32)], axis=1)


def _in_proj_kernel(tiles_per_seq, x_ref, g_ref, w_ref, cw_ref, bz_ref, u_ref, hbuf):
    tm = x_ref.shape[0]
    halo = SUBLANES

    @pl.when(pl.program_id(0) % tiles_per_seq == 0)
    def _():
        hbuf[0:halo, :] = jnp.zeros((halo, D_CONV), _F32)

    cw = cw_ref[...]
    for r0 in range(0, tm, IN_CHAIN):
        rows = slice(r0, r0 + IN_CHAIN)
        xn = _rmsnorm(x_ref[rows, :], g_ref[...]).astype(_BF16)
        cv = jnp.dot(xn, w_ref[:, D_CONV:3 * D_CONV], preferred_element_type=_F32)
        hbuf[halo + r0:halo + r0 + IN_CHAIN, :] = cv[:, :D_CONV] * cv[:, D_CONV:]
        u_ref[rows, :] = jnp.dot(xn, w_ref[:, 3 * D_CONV:], preferred_element_type=_F32)
        z = cw[CONV_WIDTH - 1:CONV_WIDTH, :] * hbuf[halo + r0:halo + r0 + IN_CHAIN, :]
        for lag in range(1, CONV_WIDTH):
            z = z + (cw[CONV_WIDTH - 1 - lag:CONV_WIDTH - lag, :]
                     * hbuf[halo + r0 - lag:halo + r0 - lag + IN_CHAIN, :])
        b_gate = jnp.dot(xn, w_ref[:, :D_CONV], preferred_element_type=_F32)
        bz_ref[rows, :] = (b_gate * z).astype(_BF16)
    hbuf[0:halo, :] = hbuf[tm:tm + halo, :]


def _in_proj(x2, g, w_bcvu, conv_w, seq):
    t = x2.shape[0]
    tm = IN_TILE
    assert SUBLANES >= CONV_WIDTH - 1 and seq % tm == 0
    return pl.pallas_call(
        functools.partial(_in_proj_kernel, seq // tm),
        out_shape=(jax.ShapeDtypeStruct((t, D_CONV), _BF16),
                   jax.ShapeDtypeStruct((t, D_SSM), _F32)),
        grid=(t // tm,),
        in_specs=[pl.BlockSpec((tm, D_MODEL), lambda i: (i, 0)),
                  pl.BlockSpec((1, D_MODEL), lambda i: (0, 0)),
                  pl.BlockSpec((D_MODEL, 3 * D_CONV + D_SSM), lambda i: (0, 0)),
                  pl.BlockSpec((CONV_WIDTH, D_CONV), lambda i: (0, 0))],
        out_specs=(pl.BlockSpec((tm, D_CONV), lambda i: (i, 0)),
                   pl.BlockSpec((tm, D_SSM), lambda i: (i, 0))),
        scratch_shapes=[pltpu.VMEM((tm + SUBLANES, D_CONV), _F32)],
        compiler_params=pltpu.CompilerParams(
            dimension_semantics=("arbitrary",), vmem_limit_bytes=VMEM_LIMIT_BYTES),
        name="in_proj",
    )(x2, g, w_bcvu, conv_w)


def _ssm_prep_kernel(lr_ref, lc_ref, bm_ref, cm_ref, d_ref, toep_ref, bst_ref, cst_ref, a_ref):
    def discretise(lre, lim, log_dt):
        dt = jnp.exp(log_dt)
        mag = jnp.exp(lre * dt)
        return mag * jnp.cos(lim * dt), mag * jnp.sin(lim * dt)

    def powers(are, aim):
        pre, pim = [jnp.ones_like(are)], [jnp.zeros_like(are)]
        for _ in range(CHUNK):
            pre, pim = (pre + [pre[-1] * are - pim[-1] * aim],
                        pim + [pre[-1] * aim + pim[-1] * are])
        return pre, pim

    lr = lr_ref[0]
    lre, lim = lr[0:1, :], lr[1:2, :]
    are, aim = discretise(lre, lim, lr[2:3, :])
    pre, pim = powers(are, aim)
    den = lre * lre + lim * lim
    q_re = ((are - 1.0) * lre + aim * lim) / den
    q_im = (aim * lre - (are - 1.0) * lim) / den
    bb_re = q_re * bm_ref[0, 0] - q_im * bm_ref[0, 1]
    bb_im = q_re * bm_ref[0, 1] + q_im * bm_ref[0, 0]
    cm_re, cm_im = cm_ref[0, 0], cm_ref[0, 1]

    def split_bf16(v):
        v_hi = v.astype(_BF16)
        return v_hi, (v - v_hi.astype(_F32)).astype(_BF16)

    c_hi, c_lo = split_bf16(jnp.concatenate([cm_re, -cm_im], axis=0))
    kblk = []
    for k in range(CHUNK):
        ab_re = bb_re * pre[k] - bb_im * pim[k]
        ab_im = bb_re * pim[k] + bb_im * pre[k]
        rows = slice((CHUNK - 1 - k) * LANES, (CHUNK - k) * LANES)
        bst_ref[0, rows, :SLAB_STATE] = ab_re.astype(_BF16)
        bst_ref[0, rows, SLAB_STATE:] = ab_im.astype(_BF16)
        ab_hi, ab_lo = split_bf16(jnp.concatenate([ab_re, ab_im], axis=1))
        kblk.append(jnp.dot(ab_hi, c_hi, preferred_element_type=_F32)
                    + (jnp.dot(ab_lo, c_hi, preferred_element_type=_F32)
                       + jnp.dot(ab_hi, c_lo, preferred_element_type=_F32)))
    r = lax.broadcasted_iota(jnp.int32, (LANES, LANES), 0)
    c = lax.broadcasted_iota(jnp.int32, (LANES, LANES), 1)
    kblk[0] = kblk[0] + jnp.where(r == c, jnp.broadcast_to(d_ref[0], (LANES, LANES)), 0.0)
    kblk = [kb.astype(_BF16) for kb in kblk]
    zeros = jnp.zeros((LANES, LANES), _BF16)
    for sp in range(CHUNK):
        for s in range(CHUNK):
            toep_ref[0, sp * LANES:(sp + 1) * LANES, s * LANES:(s + 1) * LANES] = (
                kblk[s - sp] if s >= sp else zeros)

    lc = lc_ref[0]
    cre, cim = discretise(lc[:, 0:1], lc[:, 1:2], lc[:, 2:3])
    qre, qim = powers(cre, cim)
    for s in range(CHUNK):
        cols = slice(s * LANES, (s + 1) * LANES)
        cst_ref[0, :SLAB_STATE, cols] = (cm_re * qre[s + 1] - cm_im * qim[s + 1]).astype(_BF16)
        cst_ref[0, SLAB_STATE:, cols] = (-(cm_re * qim[s + 1] + cm_im * qre[s + 1])).astype(_BF16)
    a_ref[0, 0:1, :] = pre[CHUNK]
    a_ref[0, 1:2, :] = pim[CHUNK]


def _ssm_tables(lam_re, lam_im, log_dt, b_re, b_im, c_re, c_im, d_skip):
    sg = (N_SLABS, SLAB_GROUPS)
    eye = jnp.eye(SLAB_GROUPS, dtype=_F32)
    lam = jnp.stack([lam_re, lam_im, jnp.broadcast_to(log_dt[:, None], lam_re.shape)], axis=0)
    lam_row = lam.reshape(3, N_SLABS, SLAB_STATE).transpose(1, 0, 2)
    lam_col = lam_row.transpose(0, 2, 1)

    def b_blockdiag(b):
        bt = b.reshape(*sg, SSM_STATE, SSM_GROUP).transpose(0, 1, 3, 2)
        return (bt[:, :, :, None, :] * eye[None, :, None, :, None]).reshape(N_SLABS, LANES, SLAB_STATE)

    def c_blockdiag(c):
        ct = c.reshape(*sg, SSM_GROUP, SSM_STATE).transpose(0, 1, 3, 2)
        return (ct[:, :, :, None, :] * eye[None, :, None, :, None]).reshape(N_SLABS, SLAB_STATE, LANES)

    bm = jnp.stack([b_blockdiag(b_re), b_blockdiag(b_im)], axis=1)
    cm = jnp.stack([c_blockdiag(c_re), c_blockdiag(c_im)], axis=1)
    d = d_skip.reshape(N_SLABS, 1, LANES)
    slab3 = lambda sl: (sl, 0, 0)
    slab4 = lambda sl: (sl, 0, 0, 0)
    return pl.pallas_call(
        _ssm_prep_kernel,
        out_shape=(jax.ShapeDtypeStruct((N_SLABS, FLAT, FLAT), _BF16),
                   jax.ShapeDtypeStruct((N_SLABS, FLAT, 2 * SLAB_STATE), _BF16),
                   jax.ShapeDtypeStruct((N_SLABS, 2 * SLAB_STATE, FLAT), _BF16),
                   jax.ShapeDtypeStruct((N_SLABS, 2, SLAB_STATE), _F32)),
        grid=(N_SLABS,),
        in_specs=[pl.BlockSpec((1, 3, SLAB_STATE), slab3),
                  pl.BlockSpec((1, SLAB_STATE, 3), slab3),
                  pl.BlockSpec((1, 2, LANES, SLAB_STATE), slab4),
                  pl.BlockSpec((1, 2, SLAB_STATE, LANES), slab4),
                  pl.BlockSpec((1, 1, LANES), slab3)],
        out_specs=(pl.BlockSpec((1, FLAT, FLAT), slab3),
                   pl.BlockSpec((1, FLAT, 2 * SLAB_STATE), slab3),
                   pl.BlockSpec((1, 2 * SLAB_STATE, FLAT), slab3),
                   pl.BlockSpec((1, 2, SLAB_STATE), slab3)),
        compiler_params=pltpu.CompilerParams(
            dimension_semantics=("arbitrary",), vmem_limit_bytes=VMEM_LIMIT_BYTES),
        name="ssm_prep",
    )(lam_row, lam_col, bm, cm, d)


def _ssm_kernel(u_ref, toep_ref, bst_ref, cst_ref, a_ref, y_ref, uflat, s_scr, xc_scr, carry, ytoep):
    nb, tt, _ = u_ref.shape
    nch = tt // CHUNK
    n = nb * nch

    @pl.when(pl.program_id(1) == 0)
    def _():
        carry[...] = jnp.zeros_like(carry)

    for s in range(CHUNK):
        part = u_ref[:, pl.ds(s, nch, stride=CHUNK), :]
        uflat[:, s * LANES:(s + 1) * LANES] = part.reshape(n, LANES).astype(_BF16)

    n_cb = FLAT // MXU_DIM

    def toeplitz(cb):
        kk = (cb + 1) * MXU_DIM
        cols = slice(cb * MXU_DIM, kk)
        ytoep[:, cols] = jnp.dot(uflat[:, :kk], toep_ref[0, :kk, cols], preferred_element_type=_F32)

    for cb in range(n_cb // 2):
        toeplitz(cb)

    nblk = SLAB_STATE // LANES
    loc_all = jnp.dot(uflat[...], bst_ref[0], preferred_element_type=_F32)
    for cb in range(n_cb // 2, n_cb):
        toeplitz(cb)
    for kb in range(2 * nblk):
        s_scr[kb] = loc_all[:, kb * LANES:(kb + 1) * LANES]

    a = a_ref[0]
    are = [jnp.broadcast_to(a[0:1, kb * LANES:(kb + 1) * LANES], (nb, LANES)) for kb in range(nblk)]
    aim = [jnp.broadcast_to(a[1:2, kb * LANES:(kb + 1) * LANES], (nb, LANES)) for kb in range(nblk)]
    xr = [carry[kb] for kb in range(nblk)]
    xi = [carry[nblk + kb] for kb in range(nblk)]
    for j in range(nch):
        rows = pl.ds(j, nb, stride=nch)
        for kb in range(nblk):
            xc_scr[kb, rows, :] = xr[kb]
            xc_scr[nblk + kb, rows, :] = xi[kb]
            nr = are[kb] * xr[kb] - aim[kb] * xi[kb] + s_scr[kb, rows, :]
            ni = are[kb] * xi[kb] + aim[kb] * xr[kb] + s_scr[nblk + kb, rows, :]
            xr[kb], xi[kb] = nr, ni
    for kb in range(nblk):
        carry[kb] = xr[kb]
        carry[nblk + kb] = xi[kb]

    xc = jnp.concatenate([xc_scr[kb] for kb in range(2 * nblk)], axis=1).astype(_BF16)
    for cb in range(n_cb):
        cols = slice(cb * MXU_DIM, (cb + 1) * MXU_DIM)
        y = ytoep[:, cols] + jnp.dot(xc, cst_ref[0, :, cols], preferred_element_type=_F32)
        y = jax.nn.gelu(y)
        for h in range(MXU_DIM // LANES):
            s = cb * (MXU_DIM // LANES) + h
            y_ref[:, pl.ds(s, nch, stride=CHUNK), :] = (
                y[:, h * LANES:(h + 1) * LANES].reshape(nb, nch, LANES))


def _ssm(u3, tables):
    toep, bst, cst, a_chunk = tables
    nb, seq, _ = u3.shape
    tt = SSM_TIME_TILE
    n = nb * (tt // CHUNK)
    return pl.pallas_call(
        _ssm_kernel,
        out_shape=jax.ShapeDtypeStruct(u3.shape, _F32),
        grid=(N_SLABS, seq // tt),
        in_specs=[pl.BlockSpec((nb, tt, LANES), lambda sl, ti: (0, ti, sl)),
                  pl.BlockSpec((1, FLAT, FLAT), lambda sl, ti: (sl, 0, 0)),
                  pl.BlockSpec((1, FLAT, 2 * SLAB_STATE), lambda sl, ti: (sl, 0, 0)),
                  pl.BlockSpec((1, 2 * SLAB_STATE, FLAT), lambda sl, ti: (sl, 0, 0)),
                  pl.BlockSpec((1, 2, SLAB_STATE), lambda sl, ti: (sl, 0, 0))],
        out_specs=pl.BlockSpec((nb, tt, LANES), lambda sl, ti: (0, ti, sl)),
        scratch_shapes=[pltpu.VMEM((n, FLAT), _BF16),
                        pltpu.VMEM((2 * SLAB_STATE // LANES, n, LANES), _F32),
                        pltpu.VMEM((2 * SLAB_STATE // LANES, n, LANES), _F32),
                        pltpu.VMEM((2 * SLAB_STATE // LANES, nb, LANES), _F32),
                        pltpu.VMEM((n, FLAT), _F32)],
        compiler_params=pltpu.CompilerParams(
            dimension_semantics=("arbitrary", "arbitrary"), vmem_limit_bytes=VMEM_LIMIT_BYTES),
        name="ssm",
    )(u3, toep, bst, cst, a_chunk)


def _mix_route_kernel(x_ref, bz_ref, yg_ref, gm_ref, wg_ref, wco_ref, wglu_ref, wout_ref,
                      gf_ref, wr_ref, br_ref,
                      x1_ref, h_ref, ti_ref, tg_ref, rk_ref, cnt_ref, base, merged):
    tm = x_ref.shape[0]

    @pl.when(pl.program_id(0) == 0)
    def _():
        base[...] = jnp.zeros_like(base)

    chains = [slice(r0, r0 + ROW_CHAIN) for r0 in range(0, tm, ROW_CHAIN)]
    hs, picks = [], []

    def route(j):
        logits_tok = jnp.dot(hs[j], wr_ref[...], preferred_element_type=_F32)
        picks.append(_top_k_rows(chains[j], logits_tok, br_ref, ti_ref, tg_ref))

    def rank(j):
        _rank_rows(chains[j], *picks[j], rk_ref, base)

    for j, rows in enumerate(chains):
        hs.append(_mix_rows(rows, x_ref, bz_ref, yg_ref, gm_ref, wg_ref, wco_ref, wglu_ref,
                            wout_ref, gf_ref, x1_ref, h_ref, merged))
        if j >= 1:
            route(j - 1)
        if j >= 2:
            rank(j - 2)
    last = len(chains) - 1
    route(last)
    for j in range(max(last - 1, 0), last + 1):
        rank(j)
    cnt_ref[...] = base[...].astype(jnp.int32)


def _mix_rows(rows, x_ref, bz_ref, yg_ref, gm_ref, wg_ref, wco_ref, wglu_ref, wout_ref,
              gf_ref, x1_ref, h_ref, merged):
    x = x_ref[rows, :]
    xn = _rmsnorm(x, gm_ref[...]).astype(_BF16)
    bz = bz_ref[rows, :]
    yg = yg_ref[rows, :].astype(_BF16)
    for c in range(D_MODEL // MXU_DIM):
        lo = slice(c * MXU_DIM, (c + 1) * MXU_DIM)
        hi = slice(D_MODEL + c * MXU_DIM, D_MODEL + (c + 1) * MXU_DIM)
        gate_a = jnp.dot(xn, wg_ref[:, lo], preferred_element_type=_F32)
        gate_b = jnp.dot(xn, wg_ref[:, hi], preferred_element_type=_F32)
        y_a = jnp.dot(bz, wco_ref[:, lo], preferred_element_type=_F32)
        val = jnp.dot(yg, wglu_ref[:, lo], preferred_element_type=_F32)
        glu_gate = jnp.dot(yg, wglu_ref[:, hi], preferred_element_type=_F32)
        y_b = val * _sigmoid(glu_gate)
        merged[rows, lo] = (_sigmoid(gate_a) * y_a + _sigmoid(gate_b) * y_b).astype(_BF16)
    x1 = x + jnp.dot(merged[rows, :], wout_ref[...], preferred_element_type=_F32)
    x1_ref[rows, :] = x1
    h = _rmsnorm(x1, gf_ref[...])
    h_ref[rows, :] = _pack_bf16_halves(h)
    return h.astype(_BF16)


def _top_k_rows(rows, logits_tok, br_ref, ti_ref, tg_ref):
    tm = rows.stop - rows.start
    logits = jnp.transpose(logits_tok)[:N_EXPERTS, :] + br_ref[...]
    erow = lax.broadcasted_iota(jnp.int32, (N_EXPERTS, tm), 0).astype(_F32)
    neg_inf = jnp.float32(-jnp.inf)
    work = logits
    vals, idxs = [], []
    for _ in range(TOP_K):
        m = jnp.max(work, axis=0, keepdims=True)
        idx = jnp.min(jnp.where(work == m, erow, float(N_EXPERTS)), axis=0, keepdims=True)
        vals.append(m)
        idxs.append(idx)
        work = jnp.where(erow == idx, neg_inf, work)
    exps = [jnp.exp(v - vals[0]) for v in vals]
    denom = exps[0] + exps[1] + exps[2] + exps[3]
    sel = jnp.zeros((N_EXPERTS, tm), _F32)
    for k in range(TOP_K):
        ti_ref[k:k + 1, rows] = idxs[k].astype(jnp.int32)
        tg_ref[k:k + 1, rows] = exps[k] / denom
        tg_ref[TOP_K + k:TOP_K + k + 1, rows] = jnp.zeros((1, tm), _F32)
        sel = sel + (erow == idxs[k]).astype(_F32)
    return sel, idxs


def _rank_rows(rows, sel, idxs, rk_ref, base):
    tm = rows.stop - rows.start
    erow = lax.broadcasted_iota(jnp.int32, (N_EXPERTS, tm), 0).astype(_F32)
    row = lax.broadcasted_iota(jnp.int32, (tm, tm), 0)
    col = lax.broadcasted_iota(jnp.int32, (tm, tm), 1)
    earlier = (row < col).astype(_BF16)
    before = jnp.dot(sel.astype(_BF16), earlier, preferred_element_type=_F32) + base[...]
    for k in range(TOP_K):
        rk = jnp.sum(jnp.where(erow == idxs[k], before, 0.0), axis=0, keepdims=True)
        rk_ref[k:k + 1, rows] = rk.astype(jnp.int32)
    base[...] = base[...] + jnp.sum(sel, axis=1, keepdims=True)


def _mix_route(x2, bz, yg, g_mix, w_gates, w_conv_out, w_glu, w_out, g_ffn, w_router, b_router):
    t = x2.shape[0]
    tm = MIX_TILE
    tok = lambda i: (i, 0)
    tok_lanes = lambda i: (0, i)
    fixed = lambda i: (0, 0)
    weight = lambda shape: pl.BlockSpec(shape, fixed, pipeline_mode=pl.Buffered(1))
    return pl.pallas_call(
        _mix_route_kernel,
        out_shape=(jax.ShapeDtypeStruct((t, D_MODEL), _F32),
                   jax.ShapeDtypeStruct((t, D_MODEL // 2), jnp.uint32),
                   jax.ShapeDtypeStruct((TOP_K, t), jnp.int32),
                   jax.ShapeDtypeStruct((2 * TOP_K, t), _F32),
                   jax.ShapeDtypeStruct((TOP_K, t), jnp.int32),
                   jax.ShapeDtypeStruct((N_EXPERTS, 1), jnp.int32)),
        grid=(t // tm,),
        in_specs=[pl.BlockSpec((tm, D_MODEL), tok),
                  pl.BlockSpec((tm, D_CONV), tok),
                  pl.BlockSpec((tm, D_SSM), tok),
                  pl.BlockSpec((1, D_MODEL), fixed),
                  weight((D_MODEL, 2 * D_MODEL)),
                  weight((D_CONV, D_MODEL)),
                  weight((D_SSM, 2 * D_MODEL)),
                  weight((D_MODEL, D_MODEL)),
                  pl.BlockSpec((1, D_MODEL), fixed),
                  weight((D_MODEL, LANES)),
                  pl.BlockSpec((N_EXPERTS, 1), fixed)],
        out_specs=(pl.BlockSpec((tm, D_MODEL), tok),
                   pl.BlockSpec((tm, D_MODEL // 2), tok),
                   pl.BlockSpec((TOP_K, tm), tok_lanes),
                   pl.BlockSpec((2 * TOP_K, tm), tok_lanes),
                   pl.BlockSpec((TOP_K, tm), tok_lanes),
                   pl.BlockSpec((N_EXPERTS, 1), fixed)),
        scratch_shapes=[pltpu.VMEM((N_EXPERTS, 1), _F32),
                        pltpu.VMEM((tm, D_MODEL), _BF16)],
        compiler_params=pltpu.CompilerParams(
            dimension_semantics=("arbitrary",), vmem_limit_bytes=VMEM_LIMIT_BYTES),
        name="mix_route",
    )(x2, bz, yg, g_mix, w_gates, w_conv_out, w_glu, w_out, g_ffn, w_router, b_router)


def _expert_ffn_kernel(be_ref, nr_ref, slot_ref, next_ref, x_ref, wgu_hbm, bgu_ref, wd_hbm, bd_ref,
                       y_ref, stage_gu, stage_d, wgu_b, wd_b, sem_gu, sem_d):
    b = pl.program_id(0)
    expert = be_ref[b]
    live = nr_ref[b] > 0

    def weight_copies(e, slot):
        return (pltpu.make_async_copy(wgu_hbm.at[e], stage_gu.at[slot], sem_gu.at[slot]),
                pltpu.make_async_copy(wd_hbm.at[e], stage_d.at[slot], sem_d.at[slot]))

    @pl.when(live & ((b == 0) | (be_ref[jnp.maximum(b - 1, 0)] != expert)))
    def _():
        slot = slot_ref[b]

        @pl.when(b == 0)
        def _():
            for cp in weight_copies(expert, slot):
                cp.start()

        for cp in weight_copies(expert, slot):
            cp.wait()
        wgu_b[...] = stage_gu[slot].astype(_BF16)
        wd_b[...] = stage_d[slot].astype(_BF16)

        @pl.when(next_ref[b] < N_EXPERTS)
        def _():
            for cp in weight_copies(next_ref[b], 1 - slot):
                cp.start()

    def ffn_rows(n_rows):
        xw = x_ref[:n_rows, :]
        valid = lax.broadcasted_iota(jnp.int32, xw.shape, 0) < nr_ref[b]
        x = _unpack_bf16_halves(jnp.where(valid, xw, jnp.uint32(0))).astype(_BF16)
        hgu = jnp.dot(x, wgu_b[...], preferred_element_type=_F32) + bgu_ref[0]
        g = jnp.minimum(hgu[:, :D_FF], SWIGLU_LIMIT)
        up = jnp.clip(hgu[:, D_FF:], -SWIGLU_LIMIT, SWIGLU_LIMIT)
        act = (up + 1.0) * (g * _sigmoid(SWIGLU_ALPHA * g))
        y = jnp.dot(act.astype(_BF16), wd_b[...], preferred_element_type=_F32) + bd_ref[0]
        y_ref[:n_rows, :] = _pack_bf16_halves(y)
        if n_rows < MOE_BLOCK:
            y_ref[n_rows:, :] = jnp.zeros((MOE_BLOCK - n_rows, y_ref.shape[1]), y_ref.dtype)

    half = MOE_BLOCK // 2

    @pl.when(nr_ref[b] > half)
    def _():
        ffn_rows(MOE_BLOCK)

    @pl.when(live & (nr_ref[b] <= half))
    def _():
        ffn_rows(half)

    @pl.when(jnp.logical_not(live))
    def _():
        y_ref[...] = jnp.zeros_like(y_ref)


def _expert_ffn(block_e, block_rows, block_slot, block_next, x_rows, w_gate_up, b_gate_up, w_down,
                b_down):
    n_rows = x_rows.shape[0]
    n_blocks = n_rows // MOE_BLOCK

    def bias_map(b, be, nr, sl, nx):
        return (be[b], 0, 0)

    def row_map(b, be, nr, sl, nx):
        return (b, 0)

    grid_spec = pltpu.PrefetchScalarGridSpec(
        num_scalar_prefetch=4,
        grid=(n_blocks,),
        in_specs=[pl.BlockSpec((MOE_BLOCK, D_MODEL // 2), row_map),
                  pl.BlockSpec(memory_space=pl.ANY),
                  pl.BlockSpec((1, 1, 2 * D_FF), bias_map),
                  pl.BlockSpec(memory_space=pl.ANY),
                  pl.BlockSpec((1, 1, D_MODEL), bias_map)],
        out_specs=pl.BlockSpec((MOE_BLOCK, D_MODEL // 2), row_map),
        scratch_shapes=[pltpu.VMEM((2, D_MODEL, 2 * D_FF), _F32),
                        pltpu.VMEM((2, D_FF, D_MODEL), _F32),
                        pltpu.VMEM((D_MODEL, 2 * D_FF), _BF16),
                        pltpu.VMEM((D_FF, D_MODEL), _BF16),
                        pltpu.SemaphoreType.DMA((2,)),
                        pltpu.SemaphoreType.DMA((2,))],
    )
    return pl.pallas_call(
        _expert_ffn_kernel,
        out_shape=jax.ShapeDtypeStruct((n_rows, D_MODEL // 2), jnp.uint32),
        grid_spec=grid_spec,
        compiler_params=pltpu.CompilerParams(
            dimension_semantics=("arbitrary",), vmem_limit_bytes=VMEM_LIMIT_BYTES),
        name="expert_ffn",
    )(block_e, block_rows, block_slot, block_next, x_rows, w_gate_up, b_gate_up, w_down, b_down)


def _sc_workers():
    info = plsc.get_sparse_core_info()
    return info.num_cores, info.num_cores * info.num_subcores


def _dispatch(h_packed, dest_flat, n_rows):
    t, width = h_packed.shape
    n_cores, n_workers = _sc_workers()
    n_chunks = t // (n_workers * SC_ROWS)
    chunks_per_k = t // SC_ROWS
    assert n_chunks % 2 == 0

    @functools.partial(
        pl.kernel, mesh=plsc.VectorSubcoreMesh(core_axis_name="c", subcore_axis_name="s"),
        out_type=jax.ShapeDtypeStruct((n_rows, width), h_packed.dtype),
        scratch_types=[pltpu.VMEM((TOP_K, n_chunks, SC_ROWS), jnp.int32),
                       pltpu.VMEM((2, SC_ROWS, width), h_packed.dtype),
                       pltpu.SemaphoreType.DMA((2,)),
                       pltpu.SemaphoreType.DMA((2,))])
    def scatter_rows(h_hbm, dest_hbm, out_hbm, idx_v, buf, lsem, ssem):
        wid = lax.axis_index("s") * n_cores + lax.axis_index("c")
        c0 = wid * n_chunks
        for k in range(TOP_K):
            pltpu.sync_copy(dest_hbm.at[pl.ds(k * chunks_per_k + c0, n_chunks)], idx_v.at[k])

        def load(c, b):
            return pltpu.make_async_copy(h_hbm.at[pl.ds((c0 + c) * SC_ROWS, SC_ROWS)], buf.at[b],
                                         lsem.at[b])

        def scatters(c, b):
            return [pltpu.make_async_copy(buf.at[b], out_hbm.at[idx_v.at[k, c]], ssem.at[b])
                    for k in range(TOP_K)]

        load(0, 0).start()

        @pl.loop(0, n_chunks, step=2)
        def _(ci):
            for b in range(2):
                c = ci + b

                @pl.when(c >= 1)
                def _():
                    for cp in scatters(c - 1, 1 - b):
                        cp.wait()

                @pl.when(c + 1 < n_chunks)
                def _():
                    load(c + 1, 1 - b).start()

                load(c, b).wait()
                for cp in scatters(c, b):
                    cp.start()

        for cp in scatters(n_chunks - 1, 1):
            cp.wait()

    return scatter_rows(h_packed, dest_flat.reshape(TOP_K * chunks_per_k, SC_ROWS))


def _collect(y_rows, dest_flat):
    n_idx = dest_flat.shape[0]
    width = y_rows.shape[1]
    n_cores, n_workers = _sc_workers()
    n_chunks = n_idx // (n_workers * SC_ROWS)
    assert n_chunks % 2 == 0

    @functools.partial(
        pl.kernel, mesh=plsc.VectorSubcoreMesh(core_axis_name="c", subcore_axis_name="s"),
        out_type=jax.ShapeDtypeStruct((n_idx, width), y_rows.dtype),
        scratch_types=[pltpu.VMEM((n_chunks, SC_ROWS), jnp.int32),
                       pltpu.VMEM((2, SC_ROWS, width), y_rows.dtype),
                       pltpu.SemaphoreType.DMA((2,)),
                       pltpu.SemaphoreType.DMA((2,))])
    def gather_rows(y_hbm, dest_hbm, out_hbm, idx_v, buf, gsem, wsem):
        wid = lax.axis_index("s") * n_cores + lax.axis_index("c")
        c0 = wid * n_chunks
        pltpu.sync_copy(dest_hbm.at[pl.ds(c0, n_chunks)], idx_v)

        def gather(c, b):
            return pltpu.make_async_copy(y_hbm.at[idx_v.at[c]], buf.at[b], gsem.at[b])

        def write(c, b):
            return pltpu.make_async_copy(buf.at[b], out_hbm.at[pl.ds((c0 + c) * SC_ROWS, SC_ROWS)],
                                         wsem.at[b])

        gather(0, 0).start()

        @pl.loop(0, n_chunks, step=2)
        def _(ci):
            for b in range(2):
                c = ci + b

                @pl.when(c >= 1)
                def _():
                    write(c - 1, 1 - b).wait()

                @pl.when(c + 1 < n_chunks)
                def _():
                    gather(c + 1, 1 - b).start()

                gather(c, b).wait()
                write(c, b).start()

        write(n_chunks - 1, 1).wait()

    return gather_rows(y_rows, dest_flat.reshape(n_idx // SC_ROWS, SC_ROWS))


def _combine_kernel(x1_ref, ya_ref, tg_ref, g_ref, o_ref):
    acc = x1_ref[...]
    tg = jnp.transpose(tg_ref[...])
    for k in range(TOP_K):
        acc = acc + tg[:, k:k + 1] * _unpack_bf16_halves(ya_ref[k])
    o_ref[...] = _rmsnorm(acc, g_ref[...])


def _combine(x1, y_assign, top_g, g_final):
    t = x1.shape[0]
    tm = TOKEN_TILE
    return pl.pallas_call(
        _combine_kernel,
        out_shape=jax.ShapeDtypeStruct((t, D_MODEL), _F32),
        grid=(t // tm,),
        in_specs=[pl.BlockSpec((tm, D_MODEL), lambda i: (i, 0)),
                  pl.BlockSpec((TOP_K, tm, D_MODEL // 2), lambda i: (0, i, 0)),
                  pl.BlockSpec((2 * TOP_K, tm), lambda i: (0, i)),
                  pl.BlockSpec((1, D_MODEL), lambda i: (0, 0))],
        out_specs=pl.BlockSpec((tm, D_MODEL), lambda i: (i, 0)),
        compiler_params=pltpu.CompilerParams(
            dimension_semantics=("arbitrary",), vmem_limit_bytes=VMEM_LIMIT_BYTES),
        name="combine",
    )(x1, y_assign, top_g, g_final)


def _block_plan(counts, n_blocks):
    padded = ((counts + MOE_BLOCK - 1) // MOE_BLOCK) * MOE_BLOCK
    pad_end = jnp.cumsum(padded)
    pad_start = pad_end - padded
    block_start = (jnp.arange(n_blocks, dtype=jnp.int32) * MOE_BLOCK)[:, None]
    eidx = jnp.arange(N_EXPERTS, dtype=jnp.int32)
    owns = (pad_start[None, :] <= block_start) & (block_start < pad_end[None, :])
    has_blocks = (padded > 0).astype(jnp.int32)
    ordinal = jnp.cumsum(has_blocks) - has_blocks
    later = (eidx[None, :] > eidx[:, None]) & (padded[None, :] > 0)
    next_expert = jnp.min(jnp.where(later, eidx[None, :], N_EXPERTS), axis=1)

    def per_block(per_expert):
        return jnp.sum(jnp.where(owns, per_expert, 0), axis=1).astype(jnp.int32)

    block_e = per_block(eidx[None, :])
    block_rows = per_block(jnp.clip((pad_start + counts)[None, :] - block_start, 0, MOE_BLOCK))
    block_slot = per_block((ordinal % 2)[None, :])
    block_next = per_block(next_expert[None, :])
    return pad_start, (block_e, block_rows, block_slot, block_next)


def kernel(x, norm_mix_g, w_in, conv_w, w_conv_out, ssm_lam_re, ssm_lam_im, ssm_log_dt, ssm_b_re, ssm_b_im, ssm_c_re, ssm_c_im, ssm_d, w_glu, w_out, norm_ffn_g, w_router, b_router, w_gate_up, b_gate_up, w_down, b_down, norm_f_g):
    bsz, seq, d = x.shape
    t = bsz * seq
    x2 = x.reshape(t, d)
    assert seq % TOKEN_TILE == 0 and seq % SSM_TIME_TILE == 0 and w_in.shape[0] == 1

    w_in_b = w_in[0].astype(_BF16)
    n_bcvu = 3 * D_CONV + D_SSM
    g_mix = norm_mix_g[0].reshape(1, d)

    bz, u = _in_proj(x2, g_mix, w_in_b[:, :n_bcvu], conv_w[0], seq)

    tables = _ssm_tables(ssm_lam_re[0], ssm_lam_im[0], ssm_log_dt[0], ssm_b_re[0], ssm_b_im[0],
                         ssm_c_re[0], ssm_c_im[0], ssm_d[0])
    yg = _ssm(u.reshape(bsz, seq, D_SSM), tables).reshape(t, D_SSM)

    x1, h_packed, top_i, top_g, rank, counts = _mix_route(
        x2, bz, yg, g_mix, w_in_b[:, n_bcvu:], w_conv_out[0].astype(_BF16),
        w_glu[0].astype(_BF16), w_out[0].astype(_BF16), norm_ffn_g[0].reshape(1, d),
        jnp.pad(w_router[0], ((0, 0), (0, LANES - N_EXPERTS))).astype(_BF16),
        b_router[0].reshape(N_EXPERTS, 1))

    n_rows = t * TOP_K + N_EXPERTS * MOE_BLOCK
    pad_start, block_plan = _block_plan(counts[:, 0], n_rows // MOE_BLOCK)
    expert_ids = jnp.arange(N_EXPERTS, dtype=jnp.int32)[:, None, None]
    row_start = jnp.sum(jnp.where(top_i[None] == expert_ids, pad_start[:, None, None], 0), axis=0)
    dest = (row_start + rank).reshape(TOP_K * t)

    x_rows = _dispatch(h_packed, dest, n_rows)
    y_rows = _expert_ffn(*block_plan, x_rows, w_gate_up[0],
                         b_gate_up[0].reshape(N_EXPERTS, 1, 2 * D_FF), w_down[0],
                         b_down[0].reshape(N_EXPERTS, 1, D_MODEL))
    y_assign = _collect(y_rows, dest).reshape(TOP_K, t, D_MODEL // 2)
    out = _combine(x1, y_assign, top_g, norm_f_g.reshape(1, d))
    return out.reshape(bsz, seq, d)
```

```python
import functools

import jax
import jax.numpy as jnp
from jax import lax
from jax.experimental import pallas as pl
from jax.experimental.pallas import tpu as pltpu
from jax.experimental.pallas import tpu_sc as plsc

D_MODEL = 1024
D_CONV = 512
CONV_WIDTH = 3
D_SSM = 512
SSM_GROUP = 16
N_SSM_GROUPS = 32
SSM_STATE = 64
N_EXPERTS = 32
TOP_K = 4
D_FF = 1024
SWIGLU_LIMIT = 7.0
SWIGLU_ALPHA = 1.702
RMS_EPS = 1e-6

LANES = 128
SUBLANES = 8
MXU_DIM = 256
CHUNK = 16
SLAB_GROUPS = LANES // SSM_GROUP
N_SLABS = N_SSM_GROUPS // SLAB_GROUPS
SLAB_STATE = SLAB_GROUPS * SSM_STATE
FLAT = CHUNK * LANES
SSM_TIME_TILE = 256
TOKEN_TILE = 512
IN_TILE = 1024
IN_CHAIN = 512
MIX_TILE = 1024
ROW_CHAIN = 256
MOE_BLOCK = 1024
FFN_ROW_STEP = 256
SC_ROWS = 64
VMEM_LIMIT_BYTES = 56 * 1024 * 1024

_BF16 = jnp.bfloat16
_F32 = jnp.float32


def _rmsnorm(xf, g):
    return xf * lax.rsqrt(jnp.mean(xf * xf, axis=-1, keepdims=True) + RMS_EPS) * g


def _sigmoid(v):
    return 1.0 / (1.0 + jnp.exp(-v))


def _pack_bf16_halves(v):
    n = v.shape[1] // 2
    bits = pltpu.bitcast(v.astype(_BF16).astype(_F32), jnp.uint32)
    return (bits[:, :n] >> 16) | (bits[:, n:] & jnp.uint32(0xFFFF0000))


def _unpack_bf16_halves(w):
    return jnp.concatenate([pltpu.bitcast(w << 16, _F32),
                            pltpu.bitcast(w & jnp.uint32(0xFFFF0000), _F32)], axis=1)


def _in_proj_kernel(tiles_per_seq, x_ref, g_ref, w_ref, cw_ref, bz_ref, u_ref, hbuf):
    tm = x_ref.shape[0]
    halo = SUBLANES

    @pl.when(pl.program_id(0) % tiles_per_seq == 0)
    def _():
        hbuf[0:halo, :] = jnp.zeros((halo, D_CONV), _F32)

    cw = cw_ref[...]
    for r0 in range(0, tm, IN_CHAIN):
        rows = slice(r0, r0 + IN_CHAIN)
        xn = _rmsnorm(x_ref[rows, :], g_ref[...]).astype(_BF16)
        cv = jnp.dot(xn, w_ref[:, D_CONV:3 * D_CONV], preferred_element_type=_F32)
        hbuf[halo + r0:halo + r0 + IN_CHAIN, :] = cv[:, :D_CONV] * cv[:, D_CONV:]
        u_ref[rows, :] = jnp.dot(xn, w_ref[:, 3 * D_CONV:], preferred_element_type=_F32)
        z = cw[CONV_WIDTH - 1:CONV_WIDTH, :] * hbuf[halo + r0:halo + r0 + IN_CHAIN, :]
        for lag in range(1, CONV_WIDTH):
            z = z + (cw[CONV_WIDTH - 1 - lag:CONV_WIDTH - lag, :]
                     * hbuf[halo + r0 - lag:halo + r0 - lag + IN_CHAIN, :])
        b_gate = jnp.dot(xn, w_ref[:, :D_CONV], preferred_element_type=_F32)
        bz_ref[rows, :] = (b_gate * z).astype(_BF16)
    hbuf[0:halo, :] = hbuf[tm:tm + halo, :]


def _in_proj(x2, g, w_bcvu, conv_w, seq):
    t = x2.shape[0]
    tm = IN_TILE
    assert SUBLANES >= CONV_WIDTH - 1 and seq % tm == 0
    return pl.pallas_call(
        functools.partial(_in_proj_kernel, seq // tm),
        out_shape=(jax.ShapeDtypeStruct((t, D_CONV), _BF16),
                   jax.ShapeDtypeStruct((t, D_SSM), _F32)),
        grid=(t // tm,),
        in_specs=[pl.BlockSpec((tm, D_MODEL), lambda i: (i, 0)),
                  pl.BlockSpec((1, D_MODEL), lambda i: (0, 0)),
                  pl.BlockSpec((D_MODEL, 3 * D_CONV + D_SSM), lambda i: (0, 0)),
                  pl.BlockSpec((CONV_WIDTH, D_CONV), lambda i: (0, 0))],
        out_specs=(pl.BlockSpec((tm, D_CONV), lambda i: (i, 0)),
                   pl.BlockSpec((tm, D_SSM), lambda i: (i, 0))),
        scratch_shapes=[pltpu.VMEM((tm + SUBLANES, D_CONV), _F32)],
        compiler_params=pltpu.CompilerParams(
            dimension_semantics=("arbitrary",), vmem_limit_bytes=VMEM_LIMIT_BYTES),
        name="in_proj",
    )(x2, g, w_bcvu, conv_w)


def _ssm_prep_kernel(lr_ref, lc_ref, bm_ref, cm_ref, d_ref, toep_ref, bst_ref, cst_ref, a_ref):
    def discretise(lre, lim, log_dt):
        dt = jnp.exp(log_dt)
        mag = jnp.exp(lre * dt)
        return mag * jnp.cos(lim * dt), mag * jnp.sin(lim * dt)

    def powers(are, aim):
        pre, pim = [jnp.ones_like(are)], [jnp.zeros_like(are)]
        for _ in range(CHUNK):
            pre, pim = (pre + [pre[-1] * are - pim[-1] * aim],
                        pim + [pre[-1] * aim + pim[-1] * are])
        return pre, pim

    lr = lr_ref[0]
    lre, lim = lr[0:1, :], lr[1:2, :]
    are, aim = discretise(lre, lim, lr[2:3, :])
    pre, pim = powers(are, aim)
    den = lre * lre + lim * lim
    q_re = ((are - 1.0) * lre + aim * lim) / den
    q_im = (aim * lre - (are - 1.0) * lim) / den
    bb_re = q_re * bm_ref[0, 0] - q_im * bm_ref[0, 1]
    bb_im = q_re * bm_ref[0, 1] + q_im * bm_ref[0, 0]
    cm_re, cm_im = cm_ref[0, 0], cm_ref[0, 1]

    def split_bf16(v):
        v_hi = v.astype(_BF16)
        return v_hi, (v - v_hi.astype(_F32)).astype(_BF16)

    c_hi, c_lo = split_bf16(jnp.concatenate([cm_re, -cm_im], axis=0))
    kblk = []
    for k in range(CHUNK):
        ab_re = bb_re * pre[k] - bb_im * pim[k]
        ab_im = bb_re * pim[k] + bb_im * pre[k]
        rows = slice((CHUNK - 1 - k) * LANES, (CHUNK - k) * LANES)
        bst_ref[0, rows, :SLAB_STATE] = ab_re.astype(_BF16)
        bst_ref[0, rows, SLAB_STATE:] = ab_im.astype(_BF16)
        ab_hi, ab_lo = split_bf16(jnp.concatenate([ab_re, ab_im], axis=1))
        kblk.append(jnp.dot(ab_hi, c_hi, preferred_element_type=_F32)
                    + (jnp.dot(ab_lo, c_hi, preferred_element_type=_F32)
                       + jnp.dot(ab_hi, c_lo, preferred_element_type=_F32)))
    r = lax.broadcasted_iota(jnp.int32, (LANES, LANES), 0)
    c = lax.broadcasted_iota(jnp.int32, (LANES, LANES), 1)
    kblk[0] = kblk[0] + jnp.where(r == c, jnp.broadcast_to(d_ref[0], (LANES, LANES)), 0.0)
    kblk = [kb.astype(_BF16) for kb in kblk]
    zeros = jnp.zeros((LANES, LANES), _BF16)
    for sp in range(CHUNK):
        for s in range(CHUNK):
            toep_ref[0, sp * LANES:(sp + 1) * LANES, s * LANES:(s + 1) * LANES] = (
                kblk[s - sp] if s >= sp else zeros)

    lc = lc_ref[0]
    cre, cim = discretise(lc[:, 0:1], lc[:, 1:2], lc[:, 2:3])
    qre, qim = powers(cre, cim)
    for s in range(CHUNK):
        cols = slice(s * LANES, (s + 1) * LANES)
        cst_ref[0, :SLAB_STATE, cols] = (cm_re * qre[s + 1] - cm_im * qim[s + 1]).astype(_BF16)
        cst_ref[0, SLAB_STATE:, cols] = (-(cm_re * qim[s + 1] + cm_im * qre[s + 1])).astype(_BF16)
    a_ref[0, 0:1, :] = pre[CHUNK]
    a_ref[0, 1:2, :] = pim[CHUNK]


def _ssm_tables(lam_re, lam_im, log_dt, b_re, b_im, c_re, c_im, d_skip):
    sg = (N_SLABS, SLAB_GROUPS)
    eye = jnp.eye(SLAB_GROUPS, dtype=_F32)
    lam = jnp.stack([lam_re, lam_im, jnp.broadcast_to(log_dt[:, None], lam_re.shape)], axis=0)
    lam_row = lam.reshape(3, N_SLABS, SLAB_STATE).transpose(1, 0, 2)
    lam_col = lam_row.transpose(0, 2, 1)

    def b_blockdiag(b):
        bt = b.reshape(*sg, SSM_STATE, SSM_GROUP).transpose(0, 1, 3, 2)
        return (bt[:, :, :, None, :] * eye[None, :, None, :, None]).reshape(N_SLABS, LANES, SLAB_STATE)

    def c_blockdiag(c):
        ct = c.reshape(*sg, SSM_GROUP, SSM_STATE).transpose(0, 1, 3, 2)
        return (ct[:, :, :, None, :] * eye[None, :, None, :, None]).reshape(N_SLABS, SLAB_STATE, LANES)

    bm = jnp.stack([b_blockdiag(b_re), b_blockdiag(b_im)], axis=1)
    cm = jnp.stack([c_blockdiag(c_re), c_blockdiag(c_im)], axis=1)
    d = d_skip.reshape(N_SLABS, 1, LANES)
    slab3 = lambda sl: (sl, 0, 0)
    slab4 = lambda sl: (sl, 0, 0, 0)
    return pl.pallas_call(
        _ssm_prep_kernel,
        out_shape=(jax.ShapeDtypeStruct((N_SLABS, FLAT, FLAT), _BF16),
                   jax.ShapeDtypeStruct((N_SLABS, FLAT, 2 * SLAB_STATE), _BF16),
                   jax.ShapeDtypeStruct((N_SLABS, 2 * SLAB_STATE, FLAT), _BF16),
                   jax.ShapeDtypeStruct((N_SLABS, 2, SLAB_STATE), _F32)),
        grid=(N_SLABS,),
        in_specs=[pl.BlockSpec((1, 3, SLAB_STATE), slab3),
                  pl.BlockSpec((1, SLAB_STATE, 3), slab3),
                  pl.BlockSpec((1, 2, LANES, SLAB_STATE), slab4),
                  pl.BlockSpec((1, 2, SLAB_STATE, LANES), slab4),
                  pl.BlockSpec((1, 1, LANES), slab3)],
        out_specs=(pl.BlockSpec((1, FLAT, FLAT), slab3),
                   pl.BlockSpec((1, FLAT, 2 * SLAB_STATE), slab3),
                   pl.BlockSpec((1, 2 * SLAB_STATE, FLAT), slab3),
                   pl.BlockSpec((1, 2, SLAB_STATE), slab3)),
        compiler_params=pltpu.CompilerParams(
            dimension_semantics=("arbitrary",), vmem_limit_bytes=VMEM_LIMIT_BYTES),
        name="ssm_prep",
    )(lam_row, lam_col, bm, cm, d)


def _ssm_kernel(u_ref, toep_ref, bst_ref, cst_ref, a_ref, y_ref, uflat, s_scr, xc_scr, carry, ytoep):
    nb, tt, _ = u_ref.shape
    nch = tt // CHUNK
    n = nb * nch

    @pl.when(pl.program_id(1) == 0)
    def _():
        carry[...] = jnp.zeros_like(carry)

    for s in range(CHUNK):
        part = u_ref[:, pl.ds(s, nch, stride=CHUNK), :]
        uflat[:, s * LANES:(s + 1) * LANES] = part.reshape(n, LANES).astype(_BF16)

    n_cb = FLAT // MXU_DIM

    def toeplitz(cb):
        kk = (cb + 1) * MXU_DIM
        cols = slice(cb * MXU_DIM, kk)
        ytoep[:, cols] = jnp.dot(uflat[:, :kk], toep_ref[0, :kk, cols], preferred_element_type=_F32)

    for cb in range(n_cb // 2):
        toeplitz(cb)

    nblk = SLAB_STATE // LANES
    loc_all = jnp.dot(uflat[...], bst_ref[0], preferred_element_type=_F32)
    for cb in range(n_cb // 2, n_cb):
        toeplitz(cb)
    for kb in range(2 * nblk):
        s_scr[kb] = loc_all[:, kb * LANES:(kb + 1) * LANES]

    a = a_ref[0]
    are = [jnp.broadcast_to(a[0:1, kb * LANES:(kb + 1) * LANES], (nb, LANES)) for kb in range(nblk)]
    aim = [jnp.broadcast_to(a[1:2, kb * LANES:(kb + 1) * LANES], (nb, LANES)) for kb in range(nblk)]
    xr = [carry[kb] for kb in range(nblk)]
    xi = [carry[nblk + kb] for kb in range(nblk)]
    for j in range(nch):
        rows = pl.ds(j, nb, stride=nch)
        for kb in range(nblk):
            xc_scr[kb, rows, :] = xr[kb]
            xc_scr[nblk + kb, rows, :] = xi[kb]
            nr = are[kb] * xr[kb] - aim[kb] * xi[kb] + s_scr[kb, rows, :]
            ni = are[kb] * xi[kb] + aim[kb] * xr[kb] + s_scr[nblk + kb, rows, :]
            xr[kb], xi[kb] = nr, ni
    for kb in range(nblk):
        carry[kb] = xr[kb]
        carry[nblk + kb] = xi[kb]

    xc = jnp.concatenate([xc_scr[kb] for kb in range(2 * nblk)], axis=1).astype(_BF16)
    for cb in range(n_cb):
        cols = slice(cb * MXU_DIM, (cb + 1) * MXU_DIM)
        y = ytoep[:, cols] + jnp.dot(xc, cst_ref[0, :, cols], preferred_element_type=_F32)
        y = jax.nn.gelu(y)
        for h in range(MXU_DIM // LANES):
            s = cb * (MXU_DIM // LANES) + h
            y_ref[:, pl.ds(s, nch, stride=CHUNK), :] = (
                y[:, h * LANES:(h + 1) * LANES].reshape(nb, nch, LANES))


def _ssm(u3, tables):
    toep, bst, cst, a_chunk = tables
    nb, seq, _ = u3.shape
    tt = SSM_TIME_TILE
    n = nb * (tt // CHUNK)
    return pl.pallas_call(
        _ssm_kernel,
        out_shape=jax.ShapeDtypeStruct(u3.shape, _F32),
        grid=(N_SLABS, seq // tt),
        in_specs=[pl.BlockSpec((nb, tt, LANES), lambda sl, ti: (0, ti, sl)),
                  pl.BlockSpec((1, FLAT, FLAT), lambda sl, ti: (sl, 0, 0)),
                  pl.BlockSpec((1, FLAT, 2 * SLAB_STATE), lambda sl, ti: (sl, 0, 0)),
                  pl.BlockSpec((1, 2 * SLAB_STATE, FLAT), lambda sl, ti: (sl, 0, 0)),
                  pl.BlockSpec((1, 2, SLAB_STATE), lambda sl, ti: (sl, 0, 0))],
        out_specs=pl.BlockSpec((nb, tt, LANES), lambda sl, ti: (0, ti, sl)),
        scratch_shapes=[pltpu.VMEM((n, FLAT), _BF16),
                        pltpu.VMEM((2 * SLAB_STATE // LANES, n, LANES), _F32),
                        pltpu.VMEM((2 * SLAB_STATE // LANES, n, LANES), _F32),
                        pltpu.VMEM((2 * SLAB_STATE // LANES, nb, LANES), _F32),
                        pltpu.VMEM((n, FLAT), _F32)],
        compiler_params=pltpu.CompilerParams(
            dimension_semantics=("arbitrary", "arbitrary"), vmem_limit_bytes=VMEM_LIMIT_BYTES),
        name="ssm",
    )(u3, toep, bst, cst, a_chunk)


def _mix_route_kernel(x_ref, bz_ref, yg_ref, gm_ref, wg_ref, wco_ref, wglu_ref, wout_ref,
                      gf_ref, wr_ref, br_ref,
                      x1_ref, h_ref, ti_ref, tg_ref, rk_ref, cnt_ref, base, merged):
    tm = x_ref.shape[0]

    @pl.when(pl.program_id(0) == 0)
    def _():
        base[...] = jnp.zeros_like(base)

    chains = [slice(r0, r0 + ROW_CHAIN) for r0 in range(0, tm, ROW_CHAIN)]
    hs, picks = [], []

    def route(j):
        logits_tok = jnp.dot(hs[j], wr_ref[...], preferred_element_type=_F32)
        picks.append(_top_k_rows(chains[j], logits_tok, br_ref, ti_ref, tg_ref))

    def rank(j):
        _rank_rows(chains[j], *picks[j], rk_ref, base)

    for j, rows in enumerate(chains):
        hs.append(_mix_rows(rows, x_ref, bz_ref, yg_ref, gm_ref, wg_ref, wco_ref, wglu_ref,
                            wout_ref, gf_ref, x1_ref, h_ref, merged))
        if j >= 1:
            route(j - 1)
        if j >= 2:
            rank(j - 2)
    last = len(chains) - 1
    route(last)
    for j in range(max(last - 1, 0), last + 1):
        rank(j)
    cnt_ref[...] = base[...].astype(jnp.int32)


def _mix_rows(rows, x_ref, bz_ref, yg_ref, gm_ref, wg_ref, wco_ref, wglu_ref, wout_ref,
              gf_ref, x1_ref, h_ref, merged):
    x = x_ref[rows, :]
    xn = _rmsnorm(x, gm_ref[...]).astype(_BF16)
    bz = bz_ref[rows, :]
    yg = yg_ref[rows, :].astype(_BF16)
    for c in range(D_MODEL // MXU_DIM):
        lo = slice(c * MXU_DIM, (c + 1) * MXU_DIM)
        hi = slice(D_MODEL + c * MXU_DIM, D_MODEL + (c + 1) * MXU_DIM)
        gate_a = jnp.dot(xn, wg_ref[:, lo], preferred_element_type=_F32)
        gate_b = jnp.dot(xn, wg_ref[:, hi], preferred_element_type=_F32)
        y_a = jnp.dot(bz, wco_ref[:, lo], preferred_element_type=_F32)
        val = jnp.dot(yg, wglu_ref[:, lo], preferred_element_type=_F32)
        glu_gate = jnp.dot(yg, wglu_ref[:, hi], preferred_element_type=_F32)
        y_b = val * _sigmoid(glu_gate)
        merged[rows, lo] = (_sigmoid(gate_a) * y_a + _sigmoid(gate_b) * y_b).astype(_BF16)
    x1 = x + jnp.dot(merged[rows, :], wout_ref[...], preferred_element_type=_F32)
    x1_ref[rows, :] = x1
    h = _rmsnorm(x1, gf_ref[...])
    h_ref[rows, :] = _pack_bf16_halves(h)
    return h.astype(_BF16)


def _top_k_rows(rows, logits_tok, br_ref, ti_ref, tg_ref):
    tm = rows.stop - rows.start
    logits = jnp.transpose(logits_tok)[:N_EXPERTS, :] + br_ref[...]
    erow = lax.broadcasted_iota(jnp.int32, (N_EXPERTS, tm), 0).astype(_F32)
    neg_inf = jnp.float32(-jnp.inf)
    work = logits
    vals, idxs = [], []
    for _ in range(TOP_K):
        m = jnp.max(work, axis=0, keepdims=True)
        idx = jnp.min(jnp.where(work == m, erow, float(N_EXPERTS)), axis=0, keepdims=True)
        vals.append(m)
        idxs.append(idx)
        work = jnp.where(erow == idx, neg_inf, work)
    exps = [jnp.exp(v - vals[0]) for v in vals]
    denom = exps[0] + exps[1] + exps[2] + exps[3]
    sel = jnp.zeros((N_EXPERTS, tm), _F32)
    for k in range(TOP_K):
        ti_ref[k:k + 1, rows] = idxs[k].astype(jnp.int32)
        tg_ref[k:k + 1, rows] = exps[k] / denom
        tg_ref[TOP_K + k:TOP_K + k + 1, rows] = jnp.zeros((1, tm), _F32)
        sel = sel + (erow == idxs[k]).astype(_F32)
    return sel, idxs


def _rank_rows(rows, sel, idxs, rk_ref, base):
    tm = rows.stop - rows.start
    erow = lax.broadcasted_iota(jnp.int32, (N_EXPERTS, tm), 0).astype(_F32)
    row = lax.broadcasted_iota(jnp.int32, (tm, tm), 0)
    col = lax.broadcasted_iota(jnp.int32, (tm, tm), 1)
    earlier = (row < col).astype(_BF16)
    before = jnp.dot(sel.astype(_BF16), earlier, preferred_element_type=_F32) + base[...]
    for k in range(TOP_K):
        rk = jnp.sum(jnp.where(erow == idxs[k], before, 0.0), axis=0, keepdims=True)
        rk_ref[k:k + 1, rows] = rk.astype(jnp.int32)
    base[...] = base[...] + jnp.sum(sel, axis=1, keepdims=True)


def _mix_route(x2, bz, yg, g_mix, w_gates, w_conv_out, w_glu, w_out, g_ffn, w_router, b_router):
    t = x2.shape[0]
    tm = MIX_TILE
    tok = lambda i: (i, 0)
    tok_lanes = lambda i: (0, i)
    fixed = lambda i: (0, 0)
    weight = lambda shape: pl.BlockSpec(shape, fixed, pipeline_mode=pl.Buffered(1))
    return pl.pallas_call(
        _mix_route_kernel,
        out_shape=(jax.ShapeDtypeStruct((t, D_MODEL), _F32),
                   jax.ShapeDtypeStruct((t, D_MODEL // 2), jnp.uint32),
                   jax.ShapeDtypeStruct((TOP_K, t), jnp.int32),
                   jax.ShapeDtypeStruct((2 * TOP_K, t), _F32),
                   jax.ShapeDtypeStruct((TOP_K, t), jnp.int32),
                   jax.ShapeDtypeStruct((N_EXPERTS, 1), jnp.int32)),
        grid=(t // tm,),
        in_specs=[pl.BlockSpec((tm, D_MODEL), tok),
                  pl.BlockSpec((tm, D_CONV), tok),
                  pl.BlockSpec((tm, D_SSM), tok),
                  pl.BlockSpec((1, D_MODEL), fixed),
                  weight((D_MODEL, 2 * D_MODEL)),
                  weight((D_CONV, D_MODEL)),
                  weight((D_SSM, 2 * D_MODEL)),
                  weight((D_MODEL, D_MODEL)),
                  pl.BlockSpec((1, D_MODEL), fixed),
                  weight((D_MODEL, LANES)),
                  pl.BlockSpec((N_EXPERTS, 1), fixed)],
        out_specs=(pl.BlockSpec((tm, D_MODEL), tok),
                   pl.BlockSpec((tm, D_MODEL // 2), tok),
                   pl.BlockSpec((TOP_K, tm), tok_lanes),
                   pl.BlockSpec((2 * TOP_K, tm), tok_lanes),
                   pl.BlockSpec((TOP_K, tm), tok_lanes),
                   pl.BlockSpec((N_EXPERTS, 1), fixed)),
        scratch_shapes=[pltpu.VMEM((N_EXPERTS, 1), _F32),
                        pltpu.VMEM((tm, D_MODEL), _BF16)],
        compiler_params=pltpu.CompilerParams(
            dimension_semantics=("arbitrary",), vmem_limit_bytes=VMEM_LIMIT_BYTES),
        name="mix_route",
    )(x2, bz, yg, g_mix, w_gates, w_conv_out, w_glu, w_out, g_ffn, w_router, b_router)


def _expert_ffn_kernel(be_ref, nr_ref, slot_ref, next_ref, x_ref, wgu_hbm, bgu_ref, wd_hbm, bd_ref,
                       y_ref, stage_gu, stage_d, wgu_b, wd_b, sem_gu, sem_d):
    b = pl.program_id(0)
    expert = be_ref[b]
    live = nr_ref[b] > 0

    def weight_copies(e, slot):
        return (pltpu.make_async_copy(wgu_hbm.at[e], stage_gu.at[slot], sem_gu.at[slot]),
                pltpu.make_async_copy(wd_hbm.at[e], stage_d.at[slot], sem_d.at[slot]))

    @pl.when(live & ((b == 0) | (be_ref[jnp.maximum(b - 1, 0)] != expert)))
    def _():
        slot = slot_ref[b]

        @pl.when(b == 0)
        def _():
            for cp in weight_copies(expert, slot):
                cp.start()

        for cp in weight_copies(expert, slot):
            cp.wait()
        wgu_b[...] = stage_gu[slot].astype(_BF16)
        wd_b[...] = stage_d[slot].astype(_BF16)

        @pl.when(next_ref[b] < N_EXPERTS)
        def _():
            for cp in weight_copies(next_ref[b], 1 - slot):
                cp.start()

    def ffn_rows(n_rows):
        xw = x_ref[:n_rows, :]
        valid = lax.broadcasted_iota(jnp.int32, xw.shape, 0) < nr_ref[b]
        x = _unpack_bf16_halves(jnp.where(valid, xw, jnp.uint32(0))).astype(_BF16)
        hgu = jnp.dot(x, wgu_b[...], preferred_element_type=_F32) + bgu_ref[0]
        g = jnp.minimum(hgu[:, :D_FF], SWIGLU_LIMIT)
        up = jnp.clip(hgu[:, D_FF:], -SWIGLU_LIMIT, SWIGLU_LIMIT)
        act = (up + 1.0) * (g * _sigmoid(SWIGLU_ALPHA * g))
        y = jnp.dot(act.astype(_BF16), wd_b[...], preferred_element_type=_F32) + bd_ref[0]
        y_ref[:n_rows, :] = _pack_bf16_halves(y)
        if n_rows < MOE_BLOCK:
            y_ref[n_rows:, :] = jnp.zeros((MOE_BLOCK - n_rows, y_ref.shape[1]), y_ref.dtype)

    for height in range(FFN_ROW_STEP, MOE_BLOCK + 1, FFN_ROW_STEP):
        @pl.when((nr_ref[b] > height - FFN_ROW_STEP) & (nr_ref[b] <= height))
        def _():
            ffn_rows(height)

    @pl.when(jnp.logical_not(live))
    def _():
        y_ref[...] = jnp.zeros_like(y_ref)


def _expert_ffn(block_e, block_rows, block_slot, block_next, x_rows, w_gate_up, b_gate_up, w_down,
                b_down):
    n_rows = x_rows.shape[0]
    n_blocks = n_rows // MOE_BLOCK

    def bias_map(b, be, nr, sl, nx):
        return (be[b], 0, 0)

    def row_map(b, be, nr, sl, nx):
        return (b, 0)

    grid_spec = pltpu.PrefetchScalarGridSpec(
        num_scalar_prefetch=4,
        grid=(n_blocks,),
        in_specs=[pl.BlockSpec((MOE_BLOCK, D_MODEL // 2), row_map),
                  pl.BlockSpec(memory_space=pl.ANY),
                  pl.BlockSpec((1, 1, 2 * D_FF), bias_map),
                  pl.BlockSpec(memory_space=pl.ANY),
                  pl.BlockSpec((1, 1, D_MODEL), bias_map)],
        out_specs=pl.BlockSpec((MOE_BLOCK, D_MODEL // 2), row_map),
        scratch_shapes=[pltpu.VMEM((2, D_MODEL, 2 * D_FF), _F32),
                        pltpu.VMEM((2, D_FF, D_MODEL), _F32),
                        pltpu.VMEM((D_MODEL, 2 * D_FF), _BF16),
                        pltpu.VMEM((D_FF, D_MODEL), _BF16),
                        pltpu.SemaphoreType.DMA((2,)),
                        pltpu.SemaphoreType.DMA((2,))],
    )
    return pl.pallas_call(
        _expert_ffn_kernel,
        out_shape=jax.ShapeDtypeStruct((n_rows, D_MODEL // 2), jnp.uint32),
        grid_spec=grid_spec,
        compiler_params=pltpu.CompilerParams(
            dimension_semantics=("arbitrary",), vmem_limit_bytes=VMEM_LIMIT_BYTES),
        name="expert_ffn",
    )(block_e, block_rows, block_slot, block_next, x_rows, w_gate_up, b_gate_up, w_down, b_down)


def _sc_workers():
    info = plsc.get_sparse_core_info()
    return info.num_cores, info.num_cores * info.num_subcores


def _dispatch(h_packed, dest_flat, n_rows):
    t, width = h_packed.shape
    n_cores, n_workers = _sc_workers()
    n_chunks = t // (n_workers * SC_ROWS)
    chunks_per_k = t // SC_ROWS
    assert n_chunks % 2 == 0

    @functools.partial(
        pl.kernel, mesh=plsc.VectorSubcoreMesh(core_axis_name="c", subcore_axis_name="s"),
        out_type=jax.ShapeDtypeStruct((n_rows, width), h_packed.dtype),
        scratch_types=[pltpu.VMEM((TOP_K, n_chunks, SC_ROWS), jnp.int32),
                       pltpu.VMEM((2, SC_ROWS, width), h_packed.dtype),
                       pltpu.SemaphoreType.DMA((2,)),
                       pltpu.SemaphoreType.DMA((2,))])
    def scatter_rows(h_hbm, dest_hbm, out_hbm, idx_v, buf, lsem, ssem):
        wid = lax.axis_index("s") * n_cores + lax.axis_index("c")
        c0 = wid * n_chunks
        for k in range(TOP_K):
            pltpu.sync_copy(dest_hbm.at[pl.ds(k * chunks_per_k + c0, n_chunks)], idx_v.at[k])

        def load(c, b):
            return pltpu.make_async_copy(h_hbm.at[pl.ds((c0 + c) * SC_ROWS, SC_ROWS)], buf.at[b],
                                         lsem.at[b])

        def scatters(c, b):
            return [pltpu.make_async_copy(buf.at[b], out_hbm.at[idx_v.at[k, c]], ssem.at[b])
                    for k in range(TOP_K)]

        load(0, 0).start()

        @pl.loop(0, n_chunks, step=2)
        def _(ci):
            for b in range(2):
                c = ci + b

                @pl.when(c >= 1)
                def _():
                    for cp in scatters(c - 1, 1 - b):
                        cp.wait()

                @pl.when(c + 1 < n_chunks)
                def _():
                    load(c + 1, 1 - b).start()

                load(c, b).wait()
                for cp in scatters(c, b):
                    cp.start()

        for cp in scatters(n_chunks - 1, 1):
            cp.wait()

    return scatter_rows(h_packed, dest_flat.reshape(TOP_K * chunks_per_k, SC_ROWS))


def _collect(y_rows, dest_flat):
    n_idx = dest_flat.shape[0]
    width = y_rows.shape[1]
    n_cores, n_workers = _sc_workers()
    n_chunks = n_idx // (n_workers * SC_ROWS)
    assert n_chunks % 2 == 0

    @functools.partial(
        pl.kernel, mesh=plsc.VectorSubcoreMesh(core_axis_name="c", subcore_axis_name="s"),
        out_type=jax.ShapeDtypeStruct((n_idx, width), y_rows.dtype),
        scratch_types=[pltpu.VMEM((n_chunks, SC_ROWS), jnp.int32),
                       pltpu.VMEM((2, SC_ROWS, width), y_rows.dtype),
                       pltpu.SemaphoreType.DMA((2,)),
                       pltpu.SemaphoreType.DMA((2,))])
    def gather_rows(y_hbm, dest_hbm, out_hbm, idx_v, buf, gsem, wsem):
        wid = lax.axis_index("s") * n_cores + lax.axis_index("c")
        c0 = wid * n_chunks
        pltpu.sync_copy(dest_hbm.at[pl.ds(c0, n_chunks)], idx_v)

        def gather(c, b):
            return pltpu.make_async_copy(y_hbm.at[idx_v.at[c]], buf.at[b], gsem.at[b])

        def write(c, b):
            return pltpu.make_async_copy(buf.at[b], out_hbm.at[pl.ds((c0 + c) * SC_ROWS, SC_ROWS)],
                                         wsem.at[b])

        gather(0, 0).start()

        @pl.loop(0, n_chunks, step=2)
        def _(ci):
            for b in range(2):
                c = ci + b

                @pl.when(c >= 1)
                def _():
                    write(c - 1, 1 - b).wait()

                @pl.when(c + 1 < n_chunks)
                def _():
                    gather(c + 1, 1 - b).start()

                gather(c, b).wait()
                write(c, b).start()

        write(n_chunks - 1, 1).wait()

    return gather_rows(y_rows, dest_flat.reshape(n_idx // SC_ROWS, SC_ROWS))


def _combine_kernel(x1_ref, ya_ref, tg_ref, g_ref, o_ref):
    acc = x1_ref[...]
    tg = jnp.transpose(tg_ref[...])
    for k in range(TOP_K):
        acc = acc + tg[:, k:k + 1] * _unpack_bf16_halves(ya_ref[k])
    o_ref[...] = _rmsnorm(acc, g_ref[...])


def _combine(x1, y_assign, top_g, g_final):
    t = x1.shape[0]
    tm = TOKEN_TILE
    return pl.pallas_call(
        _combine_kernel,
        out_shape=jax.ShapeDtypeStruct((t, D_MODEL), _F32),
        grid=(t // tm,),
        in_specs=[pl.BlockSpec((tm, D_MODEL), lambda i: (i, 0)),
                  pl.BlockSpec((TOP_K, tm, D_MODEL // 2), lambda i: (0, i, 0)),
                  pl.BlockSpec((2 * TOP_K, tm), lambda i: (0, i)),
                  pl.BlockSpec((1, D_MODEL), lambda i: (0, 0))],
        out_specs=pl.BlockSpec((tm, D_MODEL), lambda i: (i, 0)),
        compiler_params=pltpu.CompilerParams(
            dimension_semantics=("arbitrary",), vmem_limit_bytes=VMEM_LIMIT_BYTES),
        name="combine",
    )(x1, y_assign, top_g, g_final)


def _block_plan(counts, n_blocks):
    padded = ((counts + MOE_BLOCK - 1) // MOE_BLOCK) * MOE_BLOCK
    pad_end = jnp.cumsum(padded)
    pad_start = pad_end - padded
    block_start = (jnp.arange(n_blocks, dtype=jnp.int32) * MOE_BLOCK)[:, None]
    eidx = jnp.arange(N_EXPERTS, dtype=jnp.int32)
    owns = (pad_start[None, :] <= block_start) & (block_start < pad_end[None, :])
    has_blocks = (padded > 0).astype(jnp.int32)
    ordinal = jnp.cumsum(has_blocks) - has_blocks
    later = (eidx[None, :] > eidx[:, None]) & (padded[None, :] > 0)
    next_expert = jnp.min(jnp.where(later, eidx[None, :], N_EXPERTS), axis=1)

    def per_block(per_expert):
        return jnp.sum(jnp.where(owns, per_expert, 0), axis=1).astype(jnp.int32)

    block_e = per_block(eidx[None, :])
    block_rows = per_block(jnp.clip((pad_start + counts)[None, :] - block_start, 0, MOE_BLOCK))
    block_slot = per_block((ordinal % 2)[None, :])
    block_next = per_block(next_expert[None, :])
    return pad_start, (block_e, block_rows, block_slot, block_next)


def kernel(x, norm_mix_g, w_in, conv_w, w_conv_out, ssm_lam_re, ssm_lam_im, ssm_log_dt, ssm_b_re, ssm_b_im, ssm_c_re, ssm_c_im, ssm_d, w_glu, w_out, norm_ffn_g, w_router, b_router, w_gate_up, b_gate_up, w_down, b_down, norm_f_g):
    bsz, seq, d = x.shape
    t = bsz * seq
    x2 = x.reshape(t, d)
    assert seq % TOKEN_TILE == 0 and seq % SSM_TIME_TILE == 0 and w_in.shape[0] == 1

    w_in_b = w_in[0].astype(_BF16)
    n_bcvu = 3 * D_CONV + D_SSM
    g_mix = norm_mix_g[0].reshape(1, d)

    bz, u = _in_proj(x2, g_mix, w_in_b[:, :n_bcvu], conv_w[0], seq)

    tables = _ssm_tables(ssm_lam_re[0], ssm_lam_im[0], ssm_log_dt[0], ssm_b_re[0], ssm_b_im[0],
                         ssm_c_re[0], ssm_c_im[0], ssm_d[0])
    yg = _ssm(u.reshape(bsz, seq, D_SSM), tables).reshape(t, D_SSM)

    x1, h_packed, top_i, top_g, rank, counts = _mix_route(
        x2, bz, yg, g_mix, w_in_b[:, n_bcvu:], w_conv_out[0].astype(_BF16),
        w_glu[0].astype(_BF16), w_out[0].astype(_BF16), norm_ffn_g[0].reshape(1, d),
        jnp.pad(w_router[0], ((0, 0), (0, LANES - N_EXPERTS))).astype(_BF16),
        b_router[0].reshape(N_EXPERTS, 1))

    n_rows = t * TOP_K + N_EXPERTS * MOE_BLOCK
    pad_start, block_plan = _block_plan(counts[:, 0], n_rows // MOE_BLOCK)
    expert_ids = jnp.arange(N_EXPERTS, dtype=jnp.int32)[:, None, None]
    row_start = jnp.sum(jnp.where(top_i[None] == expert_ids, pad_start[:, None, None], 0), axis=0)
    dest = (row_start + rank).reshape(TOP_K * t)

    x_rows = _dispatch(h_packed, dest, n_rows)
    y_rows = _expert_ffn(*block_plan, x_rows, w_gate_up[0],
                         b_gate_up[0].reshape(N_EXPERTS, 1, 2 * D_FF), w_down[0],
                         b_down[0].reshape(N_EXPERTS, 1, D_MODEL))
    y_assign = _collect(y_rows, dest).reshape(TOP_K, t, D_MODEL // 2)
    out = _combine(x1, y_assign, top_g, norm_f_g.reshape(1, d))
    return out.reshape(bsz, seq, d)
```

```python
import functools

import jax
import jax.numpy as jnp
from jax import lax
from jax.experimental import pallas as pl
from jax.experimental.pallas import tpu as pltpu
from jax.experimental.pallas import tpu_sc as plsc

D_MODEL = 1024
D_CONV = 512
CONV_WIDTH = 3
D_SSM = 512
SSM_GROUP = 16
N_SSM_GROUPS = 32
SSM_STATE = 64
N_EXPERTS = 32
TOP_K = 4
D_FF = 1024
SWIGLU_LIMIT = 7.0
SWIGLU_ALPHA = 1.702
RMS_EPS = 1e-6

LANES = 128
SUBLANES = 8
MXU_DIM = 256
CHUNK = 16
SLAB_GROUPS = LANES // SSM_GROUP
N_SLABS = N_SSM_GROUPS // SLAB_GROUPS
SLAB_STATE = SLAB_GROUPS * SSM_STATE
FLAT = CHUNK * LANES
SSM_TIME_TILE = 256
TOKEN_TILE = 512
IN_TILE = 1024
IN_CHAIN = 512
MIX_TILE = 1024
ROW_CHAIN = 256
MOE_BLOCK = 1024
FFN_ROW_STEP = 256
SC_ROWS = 64
COLLECT_ROWS = 32
COLLECT_RING = 4
VMEM_LIMIT_BYTES = 56 * 1024 * 1024

_BF16 = jnp.bfloat16
_F32 = jnp.float32


def _rmsnorm(xf, g):
    return xf * lax.rsqrt(jnp.mean(xf * xf, axis=-1, keepdims=True) + RMS_EPS) * g


def _sigmoid(v):
    return 1.0 / (1.0 + jnp.exp(-v))


def _pack_bf16_halves(v):
    n = v.shape[1] // 2
    bits = pltpu.bitcast(v.astype(_BF16).astype(_F32), jnp.uint32)
    return (bits[:, :n] >> 16) | (bits[:, n:] & jnp.uint32(0xFFFF0000))


def _unpack_bf16_halves(w):
    return jnp.concatenate([pltpu.bitcast(w << 16, _F32),
                            pltpu.bitcast(w & jnp.uint32(0xFFFF0000), _F32)], axis=1)


def _in_proj_kernel(tiles_per_seq, x_ref, g_ref, w_ref, cw_ref, bz_ref, u_ref, hbuf):
    tm = x_ref.shape[0]
    halo = SUBLANES

    @pl.when(pl.program_id(0) % tiles_per_seq == 0)
    def _():
        hbuf[0:halo, :] = jnp.zeros((halo, D_CONV), _F32)

    cw = cw_ref[...]
    for r0 in range(0, tm, IN_CHAIN):
        rows = slice(r0, r0 + IN_CHAIN)
        xn = _rmsnorm(x_ref[rows, :], g_ref[...]).astype(_BF16)
        cv = jnp.dot(xn, w_ref[:, D_CONV:3 * D_CONV], preferred_element_type=_F32)
        hbuf[halo + r0:halo + r0 + IN_CHAIN, :] = cv[:, :D_CONV] * cv[:, D_CONV:]
        u_ref[rows, :] = jnp.dot(xn, w_ref[:, 3 * D_CONV:], preferred_element_type=_F32)
        z = cw[CONV_WIDTH - 1:CONV_WIDTH, :] * hbuf[halo + r0:halo + r0 + IN_CHAIN, :]
        for lag in range(1, CONV_WIDTH):
            z = z + (cw[CONV_WIDTH - 1 - lag:CONV_WIDTH - lag, :]
                     * hbuf[halo + r0 - lag:halo + r0 - lag + IN_CHAIN, :])
        b_gate = jnp.dot(xn, w_ref[:, :D_CONV], preferred_element_type=_F32)
        bz_ref[rows, :] = (b_gate * z).astype(_BF16)
    hbuf[0:halo, :] = hbuf[tm:tm + halo, :]


def _in_proj(x2, g, w_bcvu, conv_w, seq):
    t = x2.shape[0]
    tm = IN_TILE
    assert SUBLANES >= CONV_WIDTH - 1 and seq % tm == 0
    return pl.pallas_call(
        functools.partial(_in_proj_kernel, seq // tm),
        out_shape=(jax.ShapeDtypeStruct((t, D_CONV), _BF16),
                   jax.ShapeDtypeStruct((t, D_SSM), _F32)),
        grid=(t // tm,),
        in_specs=[pl.BlockSpec((tm, D_MODEL), lambda i: (i, 0)),
                  pl.BlockSpec((1, D_MODEL), lambda i: (0, 0)),
                  pl.BlockSpec((D_MODEL, 3 * D_CONV + D_SSM), lambda i: (0, 0)),
                  pl.BlockSpec((CONV_WIDTH, D_CONV), lambda i: (0, 0))],
        out_specs=(pl.BlockSpec((tm, D_CONV), lambda i: (i, 0)),
                   pl.BlockSpec((tm, D_SSM), lambda i: (i, 0))),
        scratch_shapes=[pltpu.VMEM((tm + SUBLANES, D_CONV), _F32)],
        compiler_params=pltpu.CompilerParams(
            dimension_semantics=("arbitrary",), vmem_limit_bytes=VMEM_LIMIT_BYTES),
        name="in_proj",
    )(x2, g, w_bcvu, conv_w)


def _ssm_prep_kernel(lr_ref, lc_ref, bm_ref, cm_ref, d_ref, toep_ref, bst_ref, cst_ref, a_ref):
    def discretise(lre, lim, log_dt):
        dt = jnp.exp(log_dt)
        mag = jnp.exp(lre * dt)
        return mag * jnp.cos(lim * dt), mag * jnp.sin(lim * dt)

    def powers(are, aim):
        pre, pim = [jnp.ones_like(are)], [jnp.zeros_like(are)]
        for _ in range(CHUNK):
            pre, pim = (pre + [pre[-1] * are - pim[-1] * aim],
                        pim + [pre[-1] * aim + pim[-1] * are])
        return pre, pim

    lr = lr_ref[0]
    lre, lim = lr[0:1, :], lr[1:2, :]
    are, aim = discretise(lre, lim, lr[2:3, :])
    pre, pim = powers(are, aim)
    den = lre * lre + lim * lim
    q_re = ((are - 1.0) * lre + aim * lim) / den
    q_im = (aim * lre - (are - 1.0) * lim) / den
    bb_re = q_re * bm_ref[0, 0] - q_im * bm_ref[0, 1]
    bb_im = q_re * bm_ref[0, 1] + q_im * bm_ref[0, 0]
    cm_re, cm_im = cm_ref[0, 0], cm_ref[0, 1]

    def split_bf16(v):
        v_hi = v.astype(_BF16)
        return v_hi, (v - v_hi.astype(_F32)).astype(_BF16)

    c_hi, c_lo = split_bf16(jnp.concatenate([cm_re, -cm_im], axis=0))
    kblk = []
    for k in range(CHUNK):
        ab_re = bb_re * pre[k] - bb_im * pim[k]
        ab_im = bb_re * pim[k] + bb_im * pre[k]
        rows = slice((CHUNK - 1 - k) * LANES, (CHUNK - k) * LANES)
        bst_ref[0, rows, :SLAB_STATE] = ab_re.astype(_BF16)
        bst_ref[0, rows, SLAB_STATE:] = ab_im.astype(_BF16)
        ab_hi, ab_lo = split_bf16(jnp.concatenate([ab_re, ab_im], axis=1))
        kblk.append(jnp.dot(ab_hi, c_hi, preferred_element_type=_F32)
                    + (jnp.dot(ab_lo, c_hi, preferred_element_type=_F32)
                       + jnp.dot(ab_hi, c_lo, preferred_element_type=_F32)))
    r = lax.broadcasted_iota(jnp.int32, (LANES, LANES), 0)
    c = lax.broadcasted_iota(jnp.int32, (LANES, LANES), 1)
    kblk[0] = kblk[0] + jnp.where(r == c, jnp.broadcast_to(d_ref[0], (LANES, LANES)), 0.0)
    kblk = [kb.astype(_BF16) for kb in kblk]
    zeros = jnp.zeros((LANES, LANES), _BF16)
    for sp in range(CHUNK):
        for s in range(CHUNK):
            toep_ref[0, sp * LANES:(sp + 1) * LANES, s * LANES:(s + 1) * LANES] = (
                kblk[s - sp] if s >= sp else zeros)

    lc = lc_ref[0]
    cre, cim = discretise(lc[:, 0:1], lc[:, 1:2], lc[:, 2:3])
    qre, qim = powers(cre, cim)
    for s in range(CHUNK):
        cols = slice(s * LANES, (s + 1) * LANES)
        cst_ref[0, :SLAB_STATE, cols] = (cm_re * qre[s + 1] - cm_im * qim[s + 1]).astype(_BF16)
        cst_ref[0, SLAB_STATE:, cols] = (-(cm_re * qim[s + 1] + cm_im * qre[s + 1])).astype(_BF16)
    a_ref[0, 0:1, :] = pre[CHUNK]
    a_ref[0, 1:2, :] = pim[CHUNK]


def _ssm_tables(lam_re, lam_im, log_dt, b_re, b_im, c_re, c_im, d_skip):
    sg = (N_SLABS, SLAB_GROUPS)
    eye = jnp.eye(SLAB_GROUPS, dtype=_F32)
    lam = jnp.stack([lam_re, lam_im, jnp.broadcast_to(log_dt[:, None], lam_re.shape)], axis=0)
    lam_row = lam.reshape(3, N_SLABS, SLAB_STATE).transpose(1, 0, 2)
    lam_col = lam_row.transpose(0, 2, 1)

    def b_blockdiag(b):
        bt = b.reshape(*sg, SSM_STATE, SSM_GROUP).transpose(0, 1, 3, 2)
        return (bt[:, :, :, None, :] * eye[None, :, None, :, None]).reshape(N_SLABS, LANES, SLAB_STATE)

    def c_blockdiag(c):
        ct = c.reshape(*sg, SSM_GROUP, SSM_STATE).transpose(0, 1, 3, 2)
        return (ct[:, :, :, None, :] * eye[None, :, None, :, None]).reshape(N_SLABS, SLAB_STATE, LANES)

    bm = jnp.stack([b_blockdiag(b_re), b_blockdiag(b_im)], axis=1)
    cm = jnp.stack([c_blockdiag(c_re), c_blockdiag(c_im)], axis=1)
    d = d_skip.reshape(N_SLABS, 1, LANES)
    slab3 = lambda sl: (sl, 0, 0)
    slab4 = lambda sl: (sl, 0, 0, 0)
    return pl.pallas_call(
        _ssm_prep_kernel,
        out_shape=(jax.ShapeDtypeStruct((N_SLABS, FLAT, FLAT), _BF16),
                   jax.ShapeDtypeStruct((N_SLABS, FLAT, 2 * SLAB_STATE), _BF16),
                   jax.ShapeDtypeStruct((N_SLABS, 2 * SLAB_STATE, FLAT), _BF16),
                   jax.ShapeDtypeStruct((N_SLABS, 2, SLAB_STATE), _F32)),
        grid=(N_SLABS,),
        in_specs=[pl.BlockSpec((1, 3, SLAB_STATE), slab3),
                  pl.BlockSpec((1, SLAB_STATE, 3), slab3),
                  pl.BlockSpec((1, 2, LANES, SLAB_STATE), slab4),
                  pl.BlockSpec((1, 2, SLAB_STATE, LANES), slab4),
                  pl.BlockSpec((1, 1, LANES), slab3)],
        out_specs=(pl.BlockSpec((1, FLAT, FLAT), slab3),
                   pl.BlockSpec((1, FLAT, 2 * SLAB_STATE), slab3),
                   pl.BlockSpec((1, 2 * SLAB_STATE, FLAT), slab3),
                   pl.BlockSpec((1, 2, SLAB_STATE), slab3)),
        compiler_params=pltpu.CompilerParams(
            dimension_semantics=("arbitrary",), vmem_limit_bytes=VMEM_LIMIT_BYTES),
        name="ssm_prep",
    )(lam_row, lam_col, bm, cm, d)


def _ssm_kernel(u_ref, toep_ref, bst_ref, cst_ref, a_ref, y_ref, uflat, s_scr, xc_scr, carry, ytoep):
    nb, tt, _ = u_ref.shape
    nch = tt // CHUNK
    n = nb * nch

    @pl.when(pl.program_id(1) == 0)
    def _():
        carry[...] = jnp.zeros_like(carry)

    for s in range(CHUNK):
        part = u_ref[:, pl.ds(s, nch, stride=CHUNK), :]
        uflat[:, s * LANES:(s + 1) * LANES] = part.reshape(n, LANES).astype(_BF16)

    n_cb = FLAT // MXU_DIM

    def toeplitz(cb):
        kk = (cb + 1) * MXU_DIM
        cols = slice(cb * MXU_DIM, kk)
        ytoep[:, cols] = jnp.dot(uflat[:, :kk], toep_ref[0, :kk, cols], preferred_element_type=_F32)

    for cb in range(n_cb // 2):
        toeplitz(cb)

    nblk = SLAB_STATE // LANES
    loc_all = jnp.dot(uflat[...], bst_ref[0], preferred_element_type=_F32)
    for cb in range(n_cb // 2, n_cb):
        toeplitz(cb)
    for kb in range(2 * nblk):
        s_scr[kb] = loc_all[:, kb * LANES:(kb + 1) * LANES]

    a = a_ref[0]
    are = [jnp.broadcast_to(a[0:1, kb * LANES:(kb + 1) * LANES], (nb, LANES)) for kb in range(nblk)]
    aim = [jnp.broadcast_to(a[1:2, kb * LANES:(kb + 1) * LANES], (nb, LANES)) for kb in range(nblk)]
    xr = [carry[kb] for kb in range(nblk)]
    xi = [carry[nblk + kb] for kb in range(nblk)]
    for j in range(nch):
        rows = pl.ds(j, nb, stride=nch)
        for kb in range(nblk):
            xc_scr[kb, rows, :] = xr[kb]
            xc_scr[nblk + kb, rows, :] = xi[kb]
            nr = are[kb] * xr[kb] - aim[kb] * xi[kb] + s_scr[kb, rows, :]
            ni = are[kb] * xi[kb] + aim[kb] * xr[kb] + s_scr[nblk + kb, rows, :]
            xr[kb], xi[kb] = nr, ni
    for kb in range(nblk):
        carry[kb] = xr[kb]
        carry[nblk + kb] = xi[kb]

    xc = jnp.concatenate([xc_scr[kb] for kb in range(2 * nblk)], axis=1).astype(_BF16)
    for cb in range(n_cb):
        cols = slice(cb * MXU_DIM, (cb + 1) * MXU_DIM)
        y = ytoep[:, cols] + jnp.dot(xc, cst_ref[0, :, cols], preferred_element_type=_F32)
        y = jax.nn.gelu(y)
        for h in range(MXU_DIM // LANES):
            s = cb * (MXU_DIM // LANES) + h
            y_ref[:, pl.ds(s, nch, stride=CHUNK), :] = (
                y[:, h * LANES:(h + 1) * LANES].reshape(nb, nch, LANES))


def _ssm(u3, tables):
    toep, bst, cst, a_chunk = tables
    nb, seq, _ = u3.shape
    tt = SSM_TIME_TILE
    n = nb * (tt // CHUNK)
    return pl.pallas_call(
        _ssm_kernel,
        out_shape=jax.ShapeDtypeStruct(u3.shape, _F32),
        grid=(N_SLABS, seq // tt),
        in_specs=[pl.BlockSpec((nb, tt, LANES), lambda sl, ti: (0, ti, sl)),
                  pl.BlockSpec((1, FLAT, FLAT), lambda sl, ti: (sl, 0, 0)),
                  pl.BlockSpec((1, FLAT, 2 * SLAB_STATE), lambda sl, ti: (sl, 0, 0)),
                  pl.BlockSpec((1, 2 * SLAB_STATE, FLAT), lambda sl, ti: (sl, 0, 0)),
                  pl.BlockSpec((1, 2, SLAB_STATE), lambda sl, ti: (sl, 0, 0))],
        out_specs=pl.BlockSpec((nb, tt, LANES), lambda sl, ti: (0, ti, sl)),
        scratch_shapes=[pltpu.VMEM((n, FLAT), _BF16),
                        pltpu.VMEM((2 * SLAB_STATE // LANES, n, LANES), _F32),
                        pltpu.VMEM((2 * SLAB_STATE // LANES, n, LANES), _F32),
                        pltpu.VMEM((2 * SLAB_STATE // LANES, nb, LANES), _F32),
                        pltpu.VMEM((n, FLAT), _F32)],
        compiler_params=pltpu.CompilerParams(
            dimension_semantics=("arbitrary", "arbitrary"), vmem_limit_bytes=VMEM_LIMIT_BYTES),
        name="ssm",
    )(u3, toep, bst, cst, a_chunk)


def _mix_route_kernel(x_ref, bz_ref, yg_ref, gm_ref, wg_ref, wco_ref, wglu_ref, wout_ref,
                      gf_ref, wr_ref, br_ref,
                      x1_ref, h_ref, ti_ref, tg_ref, rk_ref, cnt_ref, base, merged):
    tm = x_ref.shape[0]

    @pl.when(pl.program_id(0) == 0)
    def _():
        base[...] = jnp.zeros_like(base)

    chains = [slice(r0, r0 + ROW_CHAIN) for r0 in range(0, tm, ROW_CHAIN)]
    hs, picks = [], []

    def route(j):
        logits_tok = jnp.dot(hs[j], wr_ref[...], preferred_element_type=_F32)
        picks.append(_top_k_rows(chains[j], logits_tok, br_ref, ti_ref, tg_ref))

    def rank(j):
        _rank_rows(chains[j], *picks[j], rk_ref, base)

    for j, rows in enumerate(chains):
        hs.append(_mix_rows(rows, x_ref, bz_ref, yg_ref, gm_ref, wg_ref, wco_ref, wglu_ref,
                            wout_ref, gf_ref, x1_ref, h_ref, merged))
        if j >= 1:
            route(j - 1)
        if j >= 2:
            rank(j - 2)
    last = len(chains) - 1
    route(last)
    for j in range(max(last - 1, 0), last + 1):
        rank(j)
    cnt_ref[...] = base[...].astype(jnp.int32)


def _mix_rows(rows, x_ref, bz_ref, yg_ref, gm_ref, wg_ref, wco_ref, wglu_ref, wout_ref,
              gf_ref, x1_ref, h_ref, merged):
    x = x_ref[rows, :]
    xn = _rmsnorm(x, gm_ref[...]).astype(_BF16)
    bz = bz_ref[rows, :]
    yg = yg_ref[rows, :].astype(_BF16)
    for c in range(D_MODEL // MXU_DIM):
        lo = slice(c * MXU_DIM, (c + 1) * MXU_DIM)
        hi = slice(D_MODEL + c * MXU_DIM, D_MODEL + (c + 1) * MXU_DIM)
        gate_a = jnp.dot(xn, wg_ref[:, lo], preferred_element_type=_F32)
        gate_b = jnp.dot(xn, wg_ref[:, hi], preferred_element_type=_F32)
        y_a = jnp.dot(bz, wco_ref[:, lo], preferred_element_type=_F32)
        val = jnp.dot(yg, wglu_ref[:, lo], preferred_element_type=_F32)
        glu_gate = jnp.dot(yg, wglu_ref[:, hi], preferred_element_type=_F32)
        y_b = val * _sigmoid(glu_gate)
        merged[rows, lo] = (_sigmoid(gate_a) * y_a + _sigmoid(gate_b) * y_b).astype(_BF16)
    x1 = x + jnp.dot(merged[rows, :], wout_ref[...], preferred_element_type=_F32)
    x1_ref[rows, :] = x1
    h = _rmsnorm(x1, gf_ref[...])
    h_ref[rows, :] = _pack_bf16_halves(h)
    return h.astype(_BF16)


def _top_k_rows(rows, logits_tok, br_ref, ti_ref, tg_ref):
    tm = rows.stop - rows.start
    logits = jnp.transpose(logits_tok)[:N_EXPERTS, :] + br_ref[...]
    erow = lax.broadcasted_iota(jnp.int32, (N_EXPERTS, tm), 0).astype(_F32)
    neg_inf = jnp.float32(-jnp.inf)
    work = logits
    vals, idxs = [], []
    for _ in range(TOP_K):
        m = jnp.max(work, axis=0, keepdims=True)
        idx = jnp.min(jnp.where(work == m, erow, float(N_EXPERTS)), axis=0, keepdims=True)
        vals.append(m)
        idxs.append(idx)
        work = jnp.where(erow == idx, neg_inf, work)
    exps = [jnp.exp(v - vals[0]) for v in vals]
    denom = exps[0] + exps[1] + exps[2] + exps[3]
    sel = jnp.zeros((N_EXPERTS, tm), _F32)
    for k in range(TOP_K):
        ti_ref[k:k + 1, rows] = idxs[k].astype(jnp.int32)
        tg_ref[k:k + 1, rows] = exps[k] / denom
        tg_ref[TOP_K + k:TOP_K + k + 1, rows] = jnp.zeros((1, tm), _F32)
        sel = sel + (erow == idxs[k]).astype(_F32)
    return sel, idxs


def _rank_rows(rows, sel, idxs, rk_ref, base):
    tm = rows.stop - rows.start
    erow = lax.broadcasted_iota(jnp.int32, (N_EXPERTS, tm), 0).astype(_F32)
    row = lax.broadcasted_iota(jnp.int32, (tm, tm), 0)
    col = lax.broadcasted_iota(jnp.int32, (tm, tm), 1)
    earlier = (row < col).astype(_BF16)
    before = jnp.dot(sel.astype(_BF16), earlier, preferred_element_type=_F32) + base[...]
    for k in range(TOP_K):
        rk = jnp.sum(jnp.where(erow == idxs[k], before, 0.0), axis=0, keepdims=True)
        rk_ref[k:k + 1, rows] = rk.astype(jnp.int32)
    base[...] = base[...] + jnp.sum(sel, axis=1, keepdims=True)


def _mix_route(x2, bz, yg, g_mix, w_gates, w_conv_out, w_glu, w_out, g_ffn, w_router, b_router):
    t = x2.shape[0]
    tm = MIX_TILE
    tok = lambda i: (i, 0)
    tok_lanes = lambda i: (0, i)
    fixed = lambda i: (0, 0)
    weight = lambda shape: pl.BlockSpec(shape, fixed, pipeline_mode=pl.Buffered(1))
    return pl.pallas_call(
        _mix_route_kernel,
        out_shape=(jax.ShapeDtypeStruct((t, D_MODEL), _F32),
                   jax.ShapeDtypeStruct((t, D_MODEL // 2), jnp.uint32),
                   jax.ShapeDtypeStruct((TOP_K, t), jnp.int32),
                   jax.ShapeDtypeStruct((2 * TOP_K, t), _F32),
                   jax.ShapeDtypeStruct((TOP_K, t), jnp.int32),
                   jax.ShapeDtypeStruct((N_EXPERTS, 1), jnp.int32)),
        grid=(t // tm,),
        in_specs=[pl.BlockSpec((tm, D_MODEL), tok),
                  pl.BlockSpec((tm, D_CONV), tok),
                  pl.BlockSpec((tm, D_SSM), tok),
                  pl.BlockSpec((1, D_MODEL), fixed),
                  weight((D_MODEL, 2 * D_MODEL)),
                  weight((D_CONV, D_MODEL)),
                  weight((D_SSM, 2 * D_MODEL)),
                  weight((D_MODEL, D_MODEL)),
                  pl.BlockSpec((1, D_MODEL), fixed),
                  weight((D_MODEL, LANES)),
                  pl.BlockSpec((N_EXPERTS, 1), fixed)],
        out_specs=(pl.BlockSpec((tm, D_MODEL), tok),
                   pl.BlockSpec((tm, D_MODEL // 2), tok),
                   pl.BlockSpec((TOP_K, tm), tok_lanes),
                   pl.BlockSpec((2 * TOP_K, tm), tok_lanes),
                   pl.BlockSpec((TOP_K, tm), tok_lanes),
                   pl.BlockSpec((N_EXPERTS, 1), fixed)),
        scratch_shapes=[pltpu.VMEM((N_EXPERTS, 1), _F32),
                        pltpu.VMEM((tm, D_MODEL), _BF16)],
        compiler_params=pltpu.CompilerParams(
            dimension_semantics=("arbitrary",), vmem_limit_bytes=VMEM_LIMIT_BYTES),
        name="mix_route",
    )(x2, bz, yg, g_mix, w_gates, w_conv_out, w_glu, w_out, g_ffn, w_router, b_router)


def _expert_ffn_kernel(be_ref, nr_ref, slot_ref, next_ref, x_ref, wgu_hbm, bgu_ref, wd_hbm, bd_ref,
                       y_ref, stage_gu, stage_d, wgu_b, wd_b, sem_gu, sem_d):
    b = pl.program_id(0)
    expert = be_ref[b]
    live = nr_ref[b] > 0

    def weight_copies(e, slot):
        return (pltpu.make_async_copy(wgu_hbm.at[e], stage_gu.at[slot], sem_gu.at[slot]),
                pltpu.make_async_copy(wd_hbm.at[e], stage_d.at[slot], sem_d.at[slot]))

    @pl.when(live & ((b == 0) | (be_ref[jnp.maximum(b - 1, 0)] != expert)))
    def _():
        slot = slot_ref[b]

        @pl.when(b == 0)
        def _():
            for cp in weight_copies(expert, slot):
                cp.start()

        for cp in weight_copies(expert, slot):
            cp.wait()
        wgu_b[...] = stage_gu[slot].astype(_BF16)
        wd_b[...] = stage_d[slot].astype(_BF16)

        @pl.when(next_ref[b] < N_EXPERTS)
        def _():
            for cp in weight_copies(next_ref[b], 1 - slot):
                cp.start()

    def ffn_rows(n_rows):
        xw = x_ref[:n_rows, :]
        valid = lax.broadcasted_iota(jnp.int32, xw.shape, 0) < nr_ref[b]
        x = _unpack_bf16_halves(jnp.where(valid, xw, jnp.uint32(0))).astype(_BF16)
        hgu = jnp.dot(x, wgu_b[...], preferred_element_type=_F32) + bgu_ref[0]
        g = jnp.minimum(hgu[:, :D_FF], SWIGLU_LIMIT)
        up = jnp.clip(hgu[:, D_FF:], -SWIGLU_LIMIT, SWIGLU_LIMIT)
        act = (up + 1.0) * (g * _sigmoid(SWIGLU_ALPHA * g))
        y = jnp.dot(act.astype(_BF16), wd_b[...], preferred_element_type=_F32) + bd_ref[0]
        y_ref[:n_rows, :] = _pack_bf16_halves(y)
        if n_rows < MOE_BLOCK:
            y_ref[n_rows:, :] = jnp.zeros((MOE_BLOCK - n_rows, y_ref.shape[1]), y_ref.dtype)

    for height in range(FFN_ROW_STEP, MOE_BLOCK + 1, FFN_ROW_STEP):
        @pl.when((nr_ref[b] > height - FFN_ROW_STEP) & (nr_ref[b] <= height))
        def _():
            ffn_rows(height)

    @pl.when(jnp.logical_not(live))
    def _():
        y_ref[...] = jnp.zeros_like(y_ref)


def _expert_ffn(block_e, block_rows, block_slot, block_next, x_rows, w_gate_up, b_gate_up, w_down,
                b_down):
    n_rows = x_rows.shape[0]
    n_blocks = n_rows // MOE_BLOCK

    def bias_map(b, be, nr, sl, nx):
        return (be[b], 0, 0)

    def row_map(b, be, nr, sl, nx):
        return (b, 0)

    grid_spec = pltpu.PrefetchScalarGridSpec(
        num_scalar_prefetch=4,
        grid=(n_blocks,),
        in_specs=[pl.BlockSpec((MOE_BLOCK, D_MODEL // 2), row_map),
                  pl.BlockSpec(memory_space=pl.ANY),
                  pl.BlockSpec((1, 1, 2 * D_FF), bias_map),
                  pl.BlockSpec(memory_space=pl.ANY),
                  pl.BlockSpec((1, 1, D_MODEL), bias_map)],
        out_specs=pl.BlockSpec((MOE_BLOCK, D_MODEL // 2), row_map),
        scratch_shapes=[pltpu.VMEM((2, D_MODEL, 2 * D_FF), _F32),
                        pltpu.VMEM((2, D_FF, D_MODEL), _F32),
                        pltpu.VMEM((D_MODEL, 2 * D_FF), _BF16),
                        pltpu.VMEM((D_FF, D_MODEL), _BF16),
                        pltpu.SemaphoreType.DMA((2,)),
                        pltpu.SemaphoreType.DMA((2,))],
    )
    return pl.pallas_call(
        _expert_ffn_kernel,
        out_shape=jax.ShapeDtypeStruct((n_rows, D_MODEL // 2), jnp.uint32),
        grid_spec=grid_spec,
        compiler_params=pltpu.CompilerParams(
            dimension_semantics=("arbitrary",), vmem_limit_bytes=VMEM_LIMIT_BYTES),
        name="expert_ffn",
    )(block_e, block_rows, block_slot, block_next, x_rows, w_gate_up, b_gate_up, w_down, b_down)


def _sc_workers():
    info = plsc.get_sparse_core_info()
    return info.num_cores, info.num_cores * info.num_subcores


def _dispatch(h_packed, dest_flat, n_rows):
    t, width = h_packed.shape
    n_cores, n_workers = _sc_workers()
    n_chunks = t // (n_workers * SC_ROWS)
    chunks_per_k = t // SC_ROWS
    assert n_chunks % 2 == 0

    @functools.partial(
        pl.kernel, mesh=plsc.VectorSubcoreMesh(core_axis_name="c", subcore_axis_name="s"),
        out_type=jax.ShapeDtypeStruct((n_rows, width), h_packed.dtype),
        scratch_types=[pltpu.VMEM((TOP_K, n_chunks, SC_ROWS), jnp.int32),
                       pltpu.VMEM((2, SC_ROWS, width), h_packed.dtype),
                       pltpu.SemaphoreType.DMA((2,)),
                       pltpu.SemaphoreType.DMA((2,))])
    def scatter_rows(h_hbm, dest_hbm, out_hbm, idx_v, buf, lsem, ssem):
        wid = lax.axis_index("s") * n_cores + lax.axis_index("c")
        c0 = wid * n_chunks
        for k in range(TOP_K):
            pltpu.sync_copy(dest_hbm.at[pl.ds(k * chunks_per_k + c0, n_chunks)], idx_v.at[k])

        def load(c, b):
            return pltpu.make_async_copy(h_hbm.at[pl.ds((c0 + c) * SC_ROWS, SC_ROWS)], buf.at[b],
                                         lsem.at[b])

        def scatters(c, b):
            return [pltpu.make_async_copy(buf.at[b], out_hbm.at[idx_v.at[k, c]], ssem.at[b])
                    for k in range(TOP_K)]

        load(0, 0).start()

        @pl.loop(0, n_chunks, step=2)
        def _(ci):
            for b in range(2):
                c = ci + b

                @pl.when(c >= 1)
                def _():
                    for cp in scatters(c - 1, 1 - b):
                        cp.wait()

                @pl.when(c + 1 < n_chunks)
                def _():
                    load(c + 1, 1 - b).start()

                load(c, b).wait()
                for cp in scatters(c, b):
                    cp.start()

        for cp in scatters(n_chunks - 1, 1):
            cp.wait()

    return scatter_rows(h_packed, dest_flat.reshape(TOP_K * chunks_per_k, SC_ROWS))


def _collect(y_rows, dest_flat):
    n_idx = dest_flat.shape[0]
    width = y_rows.shape[1]
    rows, depth = COLLECT_ROWS, COLLECT_RING
    n_cores, n_workers = _sc_workers()
    n_chunks = n_idx // (n_workers * rows)
    assert n_chunks % depth == 0

    @functools.partial(
        pl.kernel, mesh=plsc.VectorSubcoreMesh(core_axis_name="c", subcore_axis_name="s"),
        out_type=jax.ShapeDtypeStruct((n_idx, width), y_rows.dtype),
        scratch_types=[pltpu.VMEM((n_chunks, rows), jnp.int32),
                       pltpu.VMEM((depth, rows, width), y_rows.dtype),
                       pltpu.SemaphoreType.DMA((depth,)),
                       pltpu.SemaphoreType.DMA((depth,))])
    def gather_rows(y_hbm, dest_hbm, out_hbm, idx_v, buf, gsem, wsem):
        wid = lax.axis_index("s") * n_cores + lax.axis_index("c")
        c0 = wid * n_chunks
        pltpu.sync_copy(dest_hbm.at[pl.ds(c0, n_chunks)], idx_v)

        def gather(c, b):
            return pltpu.make_async_copy(y_hbm.at[idx_v.at[c]], buf.at[b], gsem.at[b])

        def write(c, b):
            return pltpu.make_async_copy(buf.at[b], out_hbm.at[pl.ds((c0 + c) * rows, rows)],
                                         wsem.at[b])

        for c in range(depth - 1):
            gather(c, c).start()

        @pl.loop(0, n_chunks, step=depth)
        def _(ci):
            for b in range(depth):
                c = ci + b
                gather(c, b).wait()
                write(c, b).start()
                prev = (b - 1) % depth

                @pl.when(c >= 1)
                def _():
                    write(c - 1, prev).wait()

                @pl.when(c + depth - 1 < n_chunks)
                def _():
                    gather(c + depth - 1, prev).start()

        write(n_chunks - 1, (n_chunks - 1) % depth).wait()

    return gather_rows(y_rows, dest_flat.reshape(n_idx // rows, rows))


def _combine_kernel(x1_ref, ya_ref, tg_ref, g_ref, o_ref):
    acc = x1_ref[...]
    tg = jnp.transpose(tg_ref[...])
    for k in range(TOP_K):
        acc = acc + tg[:, k:k + 1] * _unpack_bf16_halves(ya_ref[k])
    o_ref[...] = _rmsnorm(acc, g_ref[...])


def _combine(x1, y_assign, top_g, g_final):
    t = x1.shape[0]
    tm = TOKEN_TILE
    return pl.pallas_call(
        _combine_kernel,
        out_shape=jax.ShapeDtypeStruct((t, D_MODEL), _F32),
        grid=(t // tm,),
        in_specs=[pl.BlockSpec((tm, D_MODEL), lambda i: (i, 0)),
                  pl.BlockSpec((TOP_K, tm, D_MODEL // 2), lambda i: (0, i, 0)),
                  pl.BlockSpec((2 * TOP_K, tm), lambda i: (0, i)),
                  pl.BlockSpec((1, D_MODEL), lambda i: (0, 0))],
        out_specs=pl.BlockSpec((tm, D_MODEL), lambda i: (i, 0)),
        compiler_params=pltpu.CompilerParams(
            dimension_semantics=("arbitrary",), vmem_limit_bytes=VMEM_LIMIT_BYTES),
        name="combine",
    )(x1, y_assign, top_g, g_final)


def _block_plan(counts, n_blocks):
    padded = ((counts + MOE_BLOCK - 1) // MOE_BLOCK) * MOE_BLOCK
    pad_end = jnp.cumsum(padded)
    pad_start = pad_end - padded
    block_start = (jnp.arange(n_blocks, dtype=jnp.int32) * MOE_BLOCK)[:, None]
    eidx = jnp.arange(N_EXPERTS, dtype=jnp.int32)
    owns = (pad_start[None, :] <= block_start) & (block_start < pad_end[None, :])
    has_blocks = (padded > 0).astype(jnp.int32)
    ordinal = jnp.cumsum(has_blocks) - has_blocks
    later = (eidx[None, :] > eidx[:, None]) & (padded[None, :] > 0)
    next_expert = jnp.min(jnp.where(later, eidx[None, :], N_EXPERTS), axis=1)

    def per_block(per_expert):
        return jnp.sum(jnp.where(owns, per_expert, 0), axis=1).astype(jnp.int32)

    block_e = per_block(eidx[None, :])
    block_rows = per_block(jnp.clip((pad_start + counts)[None, :] - block_start, 0, MOE_BLOCK))
    block_slot = per_block((ordinal % 2)[None, :])
    block_next = per_block(next_expert[None, :])
    return pad_start, (block_e, block_rows, block_slot, block_next)


def kernel(x, norm_mix_g, w_in, conv_w, w_conv_out, ssm_lam_re, ssm_lam_im, ssm_log_dt, ssm_b_re, ssm_b_im, ssm_c_re, ssm_c_im, ssm_d, w_glu, w_out, norm_ffn_g, w_router, b_router, w_gate_up, b_gate_up, w_down, b_down, norm_f_g):
    bsz, seq, d = x.shape
    t = bsz * seq
    x2 = x.reshape(t, d)
    assert seq % TOKEN_TILE == 0 and seq % SSM_TIME_TILE == 0 and w_in.shape[0] == 1

    w_in_b = w_in[0].astype(_BF16)
    n_bcvu = 3 * D_CONV + D_SSM
    g_mix = norm_mix_g[0].reshape(1, d)

    bz, u = _in_proj(x2, g_mix, w_in_b[:, :n_bcvu], conv_w[0], seq)

    tables = _ssm_tables(ssm_lam_re[0], ssm_lam_im[0], ssm_log_dt[0], ssm_b_re[0], ssm_b_im[0],
                         ssm_c_re[0], ssm_c_im[0], ssm_d[0])
    yg = _ssm(u.reshape(bsz, seq, D_SSM), tables).reshape(t, D_SSM)

    x1, h_packed, top_i, top_g, rank, counts = _mix_route(
        x2, bz, yg, g_mix, w_in_b[:, n_bcvu:], w_conv_out[0].astype(_BF16),
        w_glu[0].astype(_BF16), w_out[0].astype(_BF16), norm_ffn_g[0].reshape(1, d),
        jnp.pad(w_router[0], ((0, 0), (0, LANES - N_EXPERTS))).astype(_BF16),
        b_router[0].reshape(N_EXPERTS, 1))

    n_rows = t * TOP_K + N_EXPERTS * MOE_BLOCK
    pad_start, block_plan = _block_plan(counts[:, 0], n_rows // MOE_BLOCK)
    expert_ids = jnp.arange(N_EXPERTS, dtype=jnp.int32)[:, None, None]
    row_start = jnp.sum(jnp.where(top_i[None] == expert_ids, pad_start[:, None, None], 0), axis=0)
    dest = (row_start + rank).reshape(TOP_K * t)

    x_rows = _dispatch(h_packed, dest, n_rows)
    y_rows = _expert_ffn(*block_plan, x_rows, w_gate_up[0],
                         b_gate_up[0].reshape(N_EXPERTS, 1, 2 * D_FF), w_down[0],
                         b_down[0].reshape(N_EXPERTS, 1, D_MODEL))
    y_assign = _collect(y_rows, dest).reshape(TOP_K, t, D_MODEL // 2)
    out = _combine(x1, y_assign, top_g, norm_f_g.reshape(1, d))
    return out.reshape(bsz, seq, d)
```

```python
import functools

import jax
import jax.numpy as jnp
from jax import lax
from jax.experimental import pallas as pl
from jax.experimental.pallas import tpu as pltpu
from jax.experimental.pallas import tpu_sc as plsc

D_MODEL = 1024
D_CONV = 512
CONV_WIDTH = 3
D_SSM = 512
SSM_GROUP = 16
N_SSM_GROUPS = 32
SSM_STATE = 64
N_EXPERTS = 32
TOP_K = 4
D_FF = 1024
SWIGLU_LIMIT = 7.0
SWIGLU_ALPHA = 1.702
RMS_EPS = 1e-6

LANES = 128
SUBLANES = 8
MXU_DIM = 256
CHUNK = 16
SLAB_GROUPS = LANES // SSM_GROUP
N_SLABS = N_SSM_GROUPS // SLAB_GROUPS
SLAB_STATE = SLAB_GROUPS * SSM_STATE
FLAT = CHUNK * LANES
SSM_TIME_TILE = 256
TOKEN_TILE = 1024
IN_TILE = 1024
IN_CHAIN = 512
MIX_TILE = 1024
ROW_CHAIN = 256
MOE_BLOCK = 1024
FFN_ROW_STEP = 256
SC_ROWS = 64
COLLECT_ROWS = 32
COLLECT_RING = 4
VMEM_LIMIT_BYTES = 56 * 1024 * 1024

_BF16 = jnp.bfloat16
_F32 = jnp.float32


def _rmsnorm(xf, g):
    return xf * lax.rsqrt(jnp.mean(xf * xf, axis=-1, keepdims=True) + RMS_EPS) * g


def _sigmoid(v):
    return 1.0 / (1.0 + jnp.exp(-v))


def _pack_bf16_halves(v):
    n = v.shape[1] // 2
    bits = pltpu.bitcast(v.astype(_BF16).astype(_F32), jnp.uint32)
    return (bits[:, :n] >> 16) | (bits[:, n:] & jnp.uint32(0xFFFF0000))


def _unpack_bf16_halves(w):
    return jnp.concatenate([pltpu.bitcast(w << 16, _F32),
                            pltpu.bitcast(w & jnp.uint32(0xFFFF0000), _F32)], axis=1)


def _in_proj_kernel(tiles_per_seq, x_ref, g_ref, w_ref, cw_ref, bz_ref, u_ref, hbuf):
    tm = x_ref.shape[0]
    halo = SUBLANES

    @pl.when(pl.program_id(0) % tiles_per_seq == 0)
    def _():
        hbuf[0:halo, :] = jnp.zeros((halo, D_CONV), _F32)

    cw = cw_ref[...]
    for r0 in range(0, tm, IN_CHAIN):
        rows = slice(r0, r0 + IN_CHAIN)
        xn = _rmsnorm(x_ref[rows, :], g_ref[...]).astype(_BF16)
        cv = jnp.dot(xn, w_ref[:, D_CONV:3 * D_CONV], preferred_element_type=_F32)
        hbuf[halo + r0:halo + r0 + IN_CHAIN, :] = cv[:, :D_CONV] * cv[:, D_CONV:]
        u_ref[rows, :] = jnp.dot(xn, w_ref[:, 3 * D_CONV:], preferred_element_type=_F32)
        z = cw[CONV_WIDTH - 1:CONV_WIDTH, :] * hbuf[halo + r0:halo + r0 + IN_CHAIN, :]
        for lag in range(1, CONV_WIDTH):
            z = z + (cw[CONV_WIDTH - 1 - lag:CONV_WIDTH - lag, :]
                     * hbuf[halo + r0 - lag:halo + r0 - lag + IN_CHAIN, :])
        b_gate = jnp.dot(xn, w_ref[:, :D_CONV], preferred_element_type=_F32)
        bz_ref[rows, :] = (b_gate * z).astype(_BF16)
    hbuf[0:halo, :] = hbuf[tm:tm + halo, :]


def _in_proj(x2, g, w_bcvu, conv_w, seq):
    t = x2.shape[0]
    tm = IN_TILE
    assert SUBLANES >= CONV_WIDTH - 1 and seq % tm == 0
    return pl.pallas_call(
        functools.partial(_in_proj_kernel, seq // tm),
        out_shape=(jax.ShapeDtypeStruct((t, D_CONV), _BF16),
                   jax.ShapeDtypeStruct((t, D_SSM), _F32)),
        grid=(t // tm,),
        in_specs=[pl.BlockSpec((tm, D_MODEL), lambda i: (i, 0)),
                  pl.BlockSpec((1, D_MODEL), lambda i: (0, 0)),
                  pl.BlockSpec((D_MODEL, 3 * D_CONV + D_SSM), lambda i: (0, 0)),
                  pl.BlockSpec((CONV_WIDTH, D_CONV), lambda i: (0, 0))],
        out_specs=(pl.BlockSpec((tm, D_CONV), lambda i: (i, 0)),
                   pl.BlockSpec((tm, D_SSM), lambda i: (i, 0))),
        scratch_shapes=[pltpu.VMEM((tm + SUBLANES, D_CONV), _F32)],
        compiler_params=pltpu.CompilerParams(
            dimension_semantics=("arbitrary",), vmem_limit_bytes=VMEM_LIMIT_BYTES),
        name="in_proj",
    )(x2, g, w_bcvu, conv_w)


def _ssm_prep_kernel(lr_ref, lc_ref, bm_ref, cm_ref, d_ref, toep_ref, bst_ref, cst_ref, a_ref):
    def discretise(lre, lim, log_dt):
        dt = jnp.exp(log_dt)
        mag = jnp.exp(lre * dt)
        return mag * jnp.cos(lim * dt), mag * jnp.sin(lim * dt)

    def powers(are, aim):
        pre, pim = [jnp.ones_like(are)], [jnp.zeros_like(are)]
        for _ in range(CHUNK):
            pre, pim = (pre + [pre[-1] * are - pim[-1] * aim],
                        pim + [pre[-1] * aim + pim[-1] * are])
        return pre, pim

    lr = lr_ref[0]
    lre, lim = lr[0:1, :], lr[1:2, :]
    are, aim = discretise(lre, lim, lr[2:3, :])
    pre, pim = powers(are, aim)
    den = lre * lre + lim * lim
    q_re = ((are - 1.0) * lre + aim * lim) / den
    q_im = (aim * lre - (are - 1.0) * lim) / den
    bb_re = q_re * bm_ref[0, 0] - q_im * bm_ref[0, 1]
    bb_im = q_re * bm_ref[0, 1] + q_im * bm_ref[0, 0]
    cm_re, cm_im = cm_ref[0, 0], cm_ref[0, 1]

    def split_bf16(v):
        v_hi = v.astype(_BF16)
        return v_hi, (v - v_hi.astype(_F32)).astype(_BF16)

    c_hi, c_lo = split_bf16(jnp.concatenate([cm_re, -cm_im], axis=0))
    kblk = []
    for k in range(CHUNK):
        ab_re = bb_re * pre[k] - bb_im * pim[k]
        ab_im = bb_re * pim[k] + bb_im * pre[k]
        rows = slice((CHUNK - 1 - k) * LANES, (CHUNK - k) * LANES)
        bst_ref[0, rows, :SLAB_STATE] = ab_re.astype(_BF16)
        bst_ref[0, rows, SLAB_STATE:] = ab_im.astype(_BF16)
        ab_hi, ab_lo = split_bf16(jnp.concatenate([ab_re, ab_im], axis=1))
        kblk.append(jnp.dot(ab_hi, c_hi, preferred_element_type=_F32)
                    + (jnp.dot(ab_lo, c_hi, preferred_element_type=_F32)
                       + jnp.dot(ab_hi, c_lo, preferred_element_type=_F32)))
    r = lax.broadcasted_iota(jnp.int32, (LANES, LANES), 0)
    c = lax.broadcasted_iota(jnp.int32, (LANES, LANES), 1)
    kblk[0] = kblk[0] + jnp.where(r == c, jnp.broadcast_to(d_ref[0], (LANES, LANES)), 0.0)
    kblk = [kb.astype(_BF16) for kb in kblk]
    zeros = jnp.zeros((LANES, LANES), _BF16)
    for sp in range(CHUNK):
        for s in range(CHUNK):
            toep_ref[0, sp * LANES:(sp + 1) * LANES, s * LANES:(s + 1) * LANES] = (
                kblk[s - sp] if s >= sp else zeros)

    lc = lc_ref[0]
    cre, cim = discretise(lc[:, 0:1], lc[:, 1:2], lc[:, 2:3])
    qre, qim = powers(cre, cim)
    for s in range(CHUNK):
        cols = slice(s * LANES, (s + 1) * LANES)
        cst_ref[0, :SLAB_STATE, cols] = (cm_re * qre[s + 1] - cm_im * qim[s + 1]).astype(_BF16)
        cst_ref[0, SLAB_STATE:, cols] = (-(cm_re * qim[s + 1] + cm_im * qre[s + 1])).astype(_BF16)
    a_ref[0, 0:1, :] = pre[CHUNK]
    a_ref[0, 1:2, :] = pim[CHUNK]


def _ssm_tables(lam_re, lam_im, log_dt, b_re, b_im, c_re, c_im, d_skip):
    sg = (N_SLABS, SLAB_GROUPS)
    eye = jnp.eye(SLAB_GROUPS, dtype=_F32)
    lam = jnp.stack([lam_re, lam_im, jnp.broadcast_to(log_dt[:, None], lam_re.shape)], axis=0)
    lam_row = lam.reshape(3, N_SLABS, SLAB_STATE).transpose(1, 0, 2)
    lam_col = lam_row.transpose(0, 2, 1)

    def b_blockdiag(b):
        bt = b.reshape(*sg, SSM_STATE, SSM_GROUP).transpose(0, 1, 3, 2)
        return (bt[:, :, :, None, :] * eye[None, :, None, :, None]).reshape(N_SLABS, LANES, SLAB_STATE)

    def c_blockdiag(c):
        ct = c.reshape(*sg, SSM_GROUP, SSM_STATE).transpose(0, 1, 3, 2)
        return (ct[:, :, :, None, :] * eye[None, :, None, :, None]).reshape(N_SLABS, SLAB_STATE, LANES)

    bm = jnp.stack([b_blockdiag(b_re), b_blockdiag(b_im)], axis=1)
    cm = jnp.stack([c_blockdiag(c_re), c_blockdiag(c_im)], axis=1)
    d = d_skip.reshape(N_SLABS, 1, LANES)
    slab3 = lambda sl: (sl, 0, 0)
    slab4 = lambda sl: (sl, 0, 0, 0)
    return pl.pallas_call(
        _ssm_prep_kernel,
        out_shape=(jax.ShapeDtypeStruct((N_SLABS, FLAT, FLAT), _BF16),
                   jax.ShapeDtypeStruct((N_SLABS, FLAT, 2 * SLAB_STATE), _BF16),
                   jax.ShapeDtypeStruct((N_SLABS, 2 * SLAB_STATE, FLAT), _BF16),
                   jax.ShapeDtypeStruct((N_SLABS, 2, SLAB_STATE), _F32)),
        grid=(N_SLABS,),
        in_specs=[pl.BlockSpec((1, 3, SLAB_STATE), slab3),
                  pl.BlockSpec((1, SLAB_STATE, 3), slab3),
                  pl.BlockSpec((1, 2, LANES, SLAB_STATE), slab4),
                  pl.BlockSpec((1, 2, SLAB_STATE, LANES), slab4),
                  pl.BlockSpec((1, 1, LANES), slab3)],
        out_specs=(pl.BlockSpec((1, FLAT, FLAT), slab3),
                   pl.BlockSpec((1, FLAT, 2 * SLAB_STATE), slab3),
                   pl.BlockSpec((1, 2 * SLAB_STATE, FLAT), slab3),
                   pl.BlockSpec((1, 2, SLAB_STATE), slab3)),
        compiler_params=pltpu.CompilerParams(
            dimension_semantics=("arbitrary",), vmem_limit_bytes=VMEM_LIMIT_BYTES),
        name="ssm_prep",
    )(lam_row, lam_col, bm, cm, d)


def _ssm_kernel(u_ref, toep_ref, bst_ref, cst_ref, a_ref, y_ref, uflat, s_scr, xc_scr, carry, ytoep):
    nb, tt, _ = u_ref.shape
    nch = tt // CHUNK
    n = nb * nch

    @pl.when(pl.program_id(1) == 0)
    def _():
        carry[...] = jnp.zeros_like(carry)

    for s in range(CHUNK):
        part = u_ref[:, pl.ds(s, nch, stride=CHUNK), :]
        uflat[:, s * LANES:(s + 1) * LANES] = part.reshape(n, LANES).astype(_BF16)

    n_cb = FLAT // MXU_DIM

    def toeplitz(cb):
        kk = (cb + 1) * MXU_DIM
        cols = slice(cb * MXU_DIM, kk)
        ytoep[:, cols] = jnp.dot(uflat[:, :kk], toep_ref[0, :kk, cols], preferred_element_type=_F32)

    for cb in range(n_cb // 2):
        toeplitz(cb)

    nblk = SLAB_STATE // LANES
    loc_all = jnp.dot(uflat[...], bst_ref[0], preferred_element_type=_F32)
    for cb in range(n_cb // 2, n_cb):
        toeplitz(cb)
    for kb in range(2 * nblk):
        s_scr[kb] = loc_all[:, kb * LANES:(kb + 1) * LANES]

    a = a_ref[0]
    are = [jnp.broadcast_to(a[0:1, kb * LANES:(kb + 1) * LANES], (nb, LANES)) for kb in range(nblk)]
    aim = [jnp.broadcast_to(a[1:2, kb * LANES:(kb + 1) * LANES], (nb, LANES)) for kb in range(nblk)]
    xr = [carry[kb] for kb in range(nblk)]
    xi = [carry[nblk + kb] for kb in range(nblk)]
    for j in range(nch):
        rows = pl.ds(j, nb, stride=nch)
        for kb in range(nblk):
            xc_scr[kb, rows, :] = xr[kb]
            xc_scr[nblk + kb, rows, :] = xi[kb]
            nr = are[kb] * xr[kb] - aim[kb] * xi[kb] + s_scr[kb, rows, :]
            ni = are[kb] * xi[kb] + aim[kb] * xr[kb] + s_scr[nblk + kb, rows, :]
            xr[kb], xi[kb] = nr, ni
    for kb in range(nblk):
        carry[kb] = xr[kb]
        carry[nblk + kb] = xi[kb]

    xc = jnp.concatenate([xc_scr[kb] for kb in range(2 * nblk)], axis=1).astype(_BF16)
    for cb in range(n_cb):
        cols = slice(cb * MXU_DIM, (cb + 1) * MXU_DIM)
        y = ytoep[:, cols] + jnp.dot(xc, cst_ref[0, :, cols], preferred_element_type=_F32)
        y = jax.nn.gelu(y)
        for h in range(MXU_DIM // LANES):
            s = cb * (MXU_DIM // LANES) + h
            y_ref[:, pl.ds(s, nch, stride=CHUNK), :] = (
                y[:, h * LANES:(h + 1) * LANES].reshape(nb, nch, LANES))


def _ssm(u3, tables):
    toep, bst, cst, a_chunk = tables
    nb, seq, _ = u3.shape
    tt = SSM_TIME_TILE
    n = nb * (tt // CHUNK)
    return pl.pallas_call(
        _ssm_kernel,
        out_shape=jax.ShapeDtypeStruct(u3.shape, _F32),
        grid=(N_SLABS, seq // tt),
        in_specs=[pl.BlockSpec((nb, tt, LANES), lambda sl, ti: (0, ti, sl)),
                  pl.BlockSpec((1, FLAT, FLAT), lambda sl, ti: (sl, 0, 0)),
                  pl.BlockSpec((1, FLAT, 2 * SLAB_STATE), lambda sl, ti: (sl, 0, 0)),
                  pl.BlockSpec((1, 2 * SLAB_STATE, FLAT), lambda sl, ti: (sl, 0, 0)),
                  pl.BlockSpec((1, 2, SLAB_STATE), lambda sl, ti: (sl, 0, 0))],
        out_specs=pl.BlockSpec((nb, tt, LANES), lambda sl, ti: (0, ti, sl)),
        scratch_shapes=[pltpu.VMEM((n, FLAT), _BF16),
                        pltpu.VMEM((2 * SLAB_STATE // LANES, n, LANES), _F32),
                        pltpu.VMEM((2 * SLAB_STATE // LANES, n, LANES), _F32),
                        pltpu.VMEM((2 * SLAB_STATE // LANES, nb, LANES), _F32),
                        pltpu.VMEM((n, FLAT), _F32)],
        compiler_params=pltpu.CompilerParams(
            dimension_semantics=("arbitrary", "arbitrary"), vmem_limit_bytes=VMEM_LIMIT_BYTES),
        name="ssm",
    )(u3, toep, bst, cst, a_chunk)


def _mix_route_kernel(x_ref, bz_ref, yg_ref, gm_ref, wg_ref, wco_ref, wglu_ref, wout_ref,
                      gf_ref, wr_ref, br_ref,
                      x1_ref, h_ref, ti_ref, tg_ref, rk_ref, cnt_ref, base, merged):
    tm = x_ref.shape[0]

    @pl.when(pl.program_id(0) == 0)
    def _():
        base[...] = jnp.zeros_like(base)

    chains = [slice(r0, r0 + ROW_CHAIN) for r0 in range(0, tm, ROW_CHAIN)]
    hs, picks = [], []

    def route(j):
        logits_tok = jnp.dot(hs[j], wr_ref[...], preferred_element_type=_F32)
        picks.append(_top_k_rows(chains[j], logits_tok, br_ref, ti_ref, tg_ref))

    def rank(j):
        _rank_rows(chains[j], *picks[j], rk_ref, base)

    for j, rows in enumerate(chains):
        hs.append(_mix_rows(rows, x_ref, bz_ref, yg_ref, gm_ref, wg_ref, wco_ref, wglu_ref,
                            wout_ref, gf_ref, x1_ref, h_ref, merged))
        if j >= 1:
            route(j - 1)
        if j >= 2:
            rank(j - 2)
    last = len(chains) - 1
    route(last)
    for j in range(max(last - 1, 0), last + 1):
        rank(j)
    cnt_ref[...] = base[...].astype(jnp.int32)


def _mix_rows(rows, x_ref, bz_ref, yg_ref, gm_ref, wg_ref, wco_ref, wglu_ref, wout_ref,
              gf_ref, x1_ref, h_ref, merged):
    x = x_ref[rows, :]
    xn = _rmsnorm(x, gm_ref[...]).astype(_BF16)
    bz = bz_ref[rows, :]
    yg = yg_ref[rows, :].astype(_BF16)
    for c in range(D_MODEL // MXU_DIM):
        lo = slice(c * MXU_DIM, (c + 1) * MXU_DIM)
        hi = slice(D_MODEL + c * MXU_DIM, D_MODEL + (c + 1) * MXU_DIM)
        gate_a = jnp.dot(xn, wg_ref[:, lo], preferred_element_type=_F32)
        gate_b = jnp.dot(xn, wg_ref[:, hi], preferred_element_type=_F32)
        y_a = jnp.dot(bz, wco_ref[:, lo], preferred_element_type=_F32)
        val = jnp.dot(yg, wglu_ref[:, lo], preferred_element_type=_F32)
        glu_gate = jnp.dot(yg, wglu_ref[:, hi], preferred_element_type=_F32)
        y_b = val * _sigmoid(glu_gate)
        merged[rows, lo] = (_sigmoid(gate_a) * y_a + _sigmoid(gate_b) * y_b).astype(_BF16)
    x1 = x + jnp.dot(merged[rows, :], wout_ref[...], preferred_element_type=_F32)
    x1_ref[rows, :] = x1
    h = _rmsnorm(x1, gf_ref[...])
    h_ref[rows, :] = _pack_bf16_halves(h)
    return h.astype(_BF16)


def _top_k_rows(rows, logits_tok, br_ref, ti_ref, tg_ref):
    tm = rows.stop - rows.start
    logits = jnp.transpose(logits_tok)[:N_EXPERTS, :] + br_ref[...]
    erow = lax.broadcasted_iota(jnp.int32, (N_EXPERTS, tm), 0).astype(_F32)
    neg_inf = jnp.float32(-jnp.inf)
    work = logits
    vals, idxs = [], []
    for _ in range(TOP_K):
        m = jnp.max(work, axis=0, keepdims=True)
        idx = jnp.min(jnp.where(work == m, erow, float(N_EXPERTS)), axis=0, keepdims=True)
        vals.append(m)
        idxs.append(idx)
        work = jnp.where(erow == idx, neg_inf, work)
    exps = [jnp.exp(v - vals[0]) for v in vals]
    denom = exps[0] + exps[1] + exps[2] + exps[3]
    sel = jnp.zeros((N_EXPERTS, tm), _F32)
    for k in range(TOP_K):
        ti_ref[k:k + 1, rows] = idxs[k].astype(jnp.int32)
        tg_ref[k:k + 1, rows] = exps[k] / denom
        tg_ref[TOP_K + k:TOP_K + k + 1, rows] = jnp.zeros((1, tm), _F32)
        sel = sel + (erow == idxs[k]).astype(_F32)
    return sel, idxs


def _rank_rows(rows, sel, idxs, rk_ref, base):
    tm = rows.stop - rows.start
    erow = lax.broadcasted_iota(jnp.int32, (N_EXPERTS, tm), 0).astype(_F32)
    row = lax.broadcasted_iota(jnp.int32, (tm, tm), 0)
    col = lax.broadcasted_iota(jnp.int32, (tm, tm), 1)
    earlier = (row < col).astype(_BF16)
    before = jnp.dot(sel.astype(_BF16), earlier, preferred_element_type=_F32) + base[...]
    for k in range(TOP_K):
        rk = jnp.sum(jnp.where(erow == idxs[k], before, 0.0), axis=0, keepdims=True)
        rk_ref[k:k + 1, rows] = rk.astype(jnp.int32)
    base[...] = base[...] + jnp.sum(sel, axis=1, keepdims=True)


def _mix_route(x2, bz, yg, g_mix, w_gates, w_conv_out, w_glu, w_out, g_ffn, w_router, b_router):
    t = x2.shape[0]
    tm = MIX_TILE
    tok = lambda i: (i, 0)
    tok_lanes = lambda i: (0, i)
    fixed = lambda i: (0, 0)
    weight = lambda shape: pl.BlockSpec(shape, fixed, pipeline_mode=pl.Buffered(1))
    return pl.pallas_call(
        _mix_route_kernel,
        out_shape=(jax.ShapeDtypeStruct((t, D_MODEL), _F32),
                   jax.ShapeDtypeStruct((t, D_MODEL // 2), jnp.uint32),
                   jax.ShapeDtypeStruct((TOP_K, t), jnp.int32),
                   jax.ShapeDtypeStruct((2 * TOP_K, t), _F32),
                   jax.ShapeDtypeStruct((TOP_K, t), jnp.int32),
                   jax.ShapeDtypeStruct((N_EXPERTS, 1), jnp.int32)),
        grid=(t // tm,),
        in_specs=[pl.BlockSpec((tm, D_MODEL), tok),
                  pl.BlockSpec((tm, D_CONV), tok),
                  pl.BlockSpec((tm, D_SSM), tok),
                  pl.BlockSpec((1, D_MODEL), fixed),
                  weight((D_MODEL, 2 * D_MODEL)),
                  weight((D_CONV, D_MODEL)),
                  weight((D_SSM, 2 * D_MODEL)),
                  weight((D_MODEL, D_MODEL)),
                  pl.BlockSpec((1, D_MODEL), fixed),
                  weight((D_MODEL, LANES)),
                  pl.BlockSpec((N_EXPERTS, 1), fixed)],
        out_specs=(pl.BlockSpec((tm, D_MODEL), tok),
                   pl.BlockSpec((tm, D_MODEL // 2), tok),
                   pl.BlockSpec((TOP_K, tm), tok_lanes),
                   pl.BlockSpec((2 * TOP_K, tm), tok_lanes),
                   pl.BlockSpec((TOP_K, tm), tok_lanes),
                   pl.BlockSpec((N_EXPERTS, 1), fixed)),
        scratch_shapes=[pltpu.VMEM((N_EXPERTS, 1), _F32),
                        pltpu.VMEM((tm, D_MODEL), _BF16)],
        compiler_params=pltpu.CompilerParams(
            dimension_semantics=("arbitrary",), vmem_limit_bytes=VMEM_LIMIT_BYTES),
        name="mix_route",
    )(x2, bz, yg, g_mix, w_gates, w_conv_out, w_glu, w_out, g_ffn, w_router, b_router)


def _expert_ffn_kernel(be_ref, nr_ref, slot_ref, next_ref, x_ref, wgu_hbm, bgu_ref, wd_hbm, bd_ref,
                       y_ref, stage_gu, stage_d, wgu_b, wd_b, sem_gu, sem_d):
    b = pl.program_id(0)
    expert = be_ref[b]
    live = nr_ref[b] > 0

    def weight_copies(e, slot):
        return (pltpu.make_async_copy(wgu_hbm.at[e], stage_gu.at[slot], sem_gu.at[slot]),
                pltpu.make_async_copy(wd_hbm.at[e], stage_d.at[slot], sem_d.at[slot]))

    @pl.when(live & ((b == 0) | (be_ref[jnp.maximum(b - 1, 0)] != expert)))
    def _():
        slot = slot_ref[b]

        @pl.when(b == 0)
        def _():
            for cp in weight_copies(expert, slot):
                cp.start()

        for cp in weight_copies(expert, slot):
            cp.wait()
        wgu_b[...] = stage_gu[slot].astype(_BF16)
        wd_b[...] = stage_d[slot].astype(_BF16)

        @pl.when(next_ref[b] < N_EXPERTS)
        def _():
            for cp in weight_copies(next_ref[b], 1 - slot):
                cp.start()

    def ffn_rows(n_rows):
        xw = x_ref[:n_rows, :]
        valid = lax.broadcasted_iota(jnp.int32, xw.shape, 0) < nr_ref[b]
        x = _unpack_bf16_halves(jnp.where(valid, xw, jnp.uint32(0))).astype(_BF16)
        hgu = jnp.dot(x, wgu_b[...], preferred_element_type=_F32) + bgu_ref[0]
        g = jnp.minimum(hgu[:, :D_FF], SWIGLU_LIMIT)
        up = jnp.clip(hgu[:, D_FF:], -SWIGLU_LIMIT, SWIGLU_LIMIT)
        act = (up + 1.0) * (g * _sigmoid(SWIGLU_ALPHA * g))
        y = jnp.dot(act.astype(_BF16), wd_b[...], preferred_element_type=_F32) + bd_ref[0]
        y_ref[:n_rows, :] = _pack_bf16_halves(y)
        if n_rows < MOE_BLOCK:
            y_ref[n_rows:, :] = jnp.zeros((MOE_BLOCK - n_rows, y_ref.shape[1]), y_ref.dtype)

    for height in range(FFN_ROW_STEP, MOE_BLOCK + 1, FFN_ROW_STEP):
        @pl.when((nr_ref[b] > height - FFN_ROW_STEP) & (nr_ref[b] <= height))
        def _():
            ffn_rows(height)

    @pl.when(jnp.logical_not(live))
    def _():
        y_ref[...] = jnp.zeros_like(y_ref)


def _expert_ffn(block_e, block_rows, block_slot, block_next, x_rows, w_gate_up, b_gate_up, w_down,
                b_down):
    n_rows = x_rows.shape[0]
    n_blocks = n_rows // MOE_BLOCK

    def bias_map(b, be, nr, sl, nx):
        return (be[b], 0, 0)

    def row_map(b, be, nr, sl, nx):
        return (b, 0)

    grid_spec = pltpu.PrefetchScalarGridSpec(
        num_scalar_prefetch=4,
        grid=(n_blocks,),
        in_specs=[pl.BlockSpec((MOE_BLOCK, D_MODEL // 2), row_map),
                  pl.BlockSpec(memory_space=pl.ANY),
                  pl.BlockSpec((1, 1, 2 * D_FF), bias_map),
                  pl.BlockSpec(memory_space=pl.ANY),
                  pl.BlockSpec((1, 1, D_MODEL), bias_map)],
        out_specs=pl.BlockSpec((MOE_BLOCK, D_MODEL // 2), row_map),
        scratch_shapes=[pltpu.VMEM((2, D_MODEL, 2 * D_FF), _F32),
                        pltpu.VMEM((2, D_FF, D_MODEL), _F32),
                        pltpu.VMEM((D_MODEL, 2 * D_FF), _BF16),
                        pltpu.VMEM((D_FF, D_MODEL), _BF16),
                        pltpu.SemaphoreType.DMA((2,)),
                        pltpu.SemaphoreType.DMA((2,))],
    )
    return pl.pallas_call(
        _expert_ffn_kernel,
        out_shape=jax.ShapeDtypeStruct((n_rows, D_MODEL // 2), jnp.uint32),
        grid_spec=grid_spec,
        compiler_params=pltpu.CompilerParams(
            dimension_semantics=("arbitrary",), vmem_limit_bytes=VMEM_LIMIT_BYTES),
        name="expert_ffn",
    )(block_e, block_rows, block_slot, block_next, x_rows, w_gate_up, b_gate_up, w_down, b_down)


def _sc_workers():
    info = plsc.get_sparse_core_info()
    return info.num_cores, info.num_cores * info.num_subcores


def _dispatch(h_packed, dest_flat, n_rows):
    t, width = h_packed.shape
    n_cores, n_workers = _sc_workers()
    n_chunks = t // (n_workers * SC_ROWS)
    chunks_per_k = t // SC_ROWS
    assert n_chunks % 2 == 0

    @functools.partial(
        pl.kernel, mesh=plsc.VectorSubcoreMesh(core_axis_name="c", subcore_axis_name="s"),
        out_type=jax.ShapeDtypeStruct((n_rows, width), h_packed.dtype),
        scratch_types=[pltpu.VMEM((TOP_K, n_chunks, SC_ROWS), jnp.int32),
                       pltpu.VMEM((2, SC_ROWS, width), h_packed.dtype),
                       pltpu.SemaphoreType.DMA((2,)),
                       pltpu.SemaphoreType.DMA((2,))])
    def scatter_rows(h_hbm, dest_hbm, out_hbm, idx_v, buf, lsem, ssem):
        wid = lax.axis_index("s") * n_cores + lax.axis_index("c")
        c0 = wid * n_chunks
        for k in range(TOP_K):
            pltpu.sync_copy(dest_hbm.at[pl.ds(k * chunks_per_k + c0, n_chunks)], idx_v.at[k])

        def load(c, b):
            return pltpu.make_async_copy(h_hbm.at[pl.ds((c0 + c) * SC_ROWS, SC_ROWS)], buf.at[b],
                                         lsem.at[b])

        def scatters(c, b):
            return [pltpu.make_async_copy(buf.at[b], out_hbm.at[idx_v.at[k, c]], ssem.at[b])
                    for k in range(TOP_K)]

        load(0, 0).start()

        @pl.loop(0, n_chunks, step=2)
        def _(ci):
            for b in range(2):
                c = ci + b

                @pl.when(c >= 1)
                def _():
                    for cp in scatters(c - 1, 1 - b):
                        cp.wait()

                @pl.when(c + 1 < n_chunks)
                def _():
                    load(c + 1, 1 - b).start()

                load(c, b).wait()
                for cp in scatters(c, b):
                    cp.start()

        for cp in scatters(n_chunks - 1, 1):
            cp.wait()

    return scatter_rows(h_packed, dest_flat.reshape(TOP_K * chunks_per_k, SC_ROWS))


def _collect(y_rows, dest_flat):
    n_idx = dest_flat.shape[0]
    width = y_rows.shape[1]
    rows, depth = COLLECT_ROWS, COLLECT_RING
    n_cores, n_workers = _sc_workers()
    n_chunks = n_idx // (n_workers * rows)
    assert n_chunks % depth == 0

    @functools.partial(
        pl.kernel, mesh=plsc.VectorSubcoreMesh(core_axis_name="c", subcore_axis_name="s"),
        out_type=jax.ShapeDtypeStruct((n_idx, width), y_rows.dtype),
        scratch_types=[pltpu.VMEM((n_chunks, rows), jnp.int32),
                       pltpu.VMEM((depth, rows, width), y_rows.dtype),
                       pltpu.SemaphoreType.DMA((depth,)),
                       pltpu.SemaphoreType.DMA((depth,))])
    def gather_rows(y_hbm, dest_hbm, out_hbm, idx_v, buf, gsem, wsem):
        wid = lax.axis_index("s") * n_cores + lax.axis_index("c")
        c0 = wid * n_chunks
        pltpu.sync_copy(dest_hbm.at[pl.ds(c0, n_chunks)], idx_v)

        def gather(c, b):
            return pltpu.make_async_copy(y_hbm.at[idx_v.at[c]], buf.at[b], gsem.at[b])

        def write(c, b):
            return pltpu.make_async_copy(buf.at[b], out_hbm.at[pl.ds((c0 + c) * rows, rows)],
                                         wsem.at[b])

        for c in range(depth - 1):
            gather(c, c).start()

        @pl.loop(0, n_chunks, step=depth)
        def _(ci):
            for b in range(depth):
                c = ci + b
                gather(c, b).wait()
                write(c, b).start()
                prev = (b - 1) % depth

                @pl.when(c >= 1)
                def _():
                    write(c - 1, prev).wait()

                @pl.when(c + depth - 1 < n_chunks)
                def _():
                    gather(c + depth - 1, prev).start()

        write(n_chunks - 1, (n_chunks - 1) % depth).wait()

    return gather_rows(y_rows, dest_flat.reshape(n_idx // rows, rows))


def _combine_kernel(x1_ref, ya_ref, tg_ref, g_ref, o_ref):
    acc = x1_ref[...]
    tg = jnp.transpose(tg_ref[...])
    for k in range(TOP_K):
        acc = acc + tg[:, k:k + 1] * _unpack_bf16_halves(ya_ref[k])
    o_ref[...] = _rmsnorm(acc, g_ref[...])


def _combine(x1, y_assign, top_g, g_final):
    t = x1.shape[0]
    tm = TOKEN_TILE
    return pl.pallas_call(
        _combine_kernel,
        out_shape=jax.ShapeDtypeStruct((t, D_MODEL), _F32),
        grid=(t // tm,),
        in_specs=[pl.BlockSpec((tm, D_MODEL), lambda i: (i, 0)),
                  pl.BlockSpec((TOP_K, tm, D_MODEL // 2), lambda i: (0, i, 0)),
                  pl.BlockSpec((2 * TOP_K, tm), lambda i: (0, i)),
                  pl.BlockSpec((1, D_MODEL), lambda i: (0, 0))],
        out_specs=pl.BlockSpec((tm, D_MODEL), lambda i: (i, 0)),
        compiler_params=pltpu.CompilerParams(
            dimension_semantics=("arbitrary",), vmem_limit_bytes=VMEM_LIMIT_BYTES),
        name="combine",
    )(x1, y_assign, top_g, g_final)


def _block_plan(counts, n_blocks):
    padded = ((counts + MOE_BLOCK - 1) // MOE_BLOCK) * MOE_BLOCK
    pad_end = jnp.cumsum(padded)
    pad_start = pad_end - padded
    block_start = (jnp.arange(n_blocks, dtype=jnp.int32) * MOE_BLOCK)[:, None]
    eidx = jnp.arange(N_EXPERTS, dtype=jnp.int32)
    owns = (pad_start[None, :] <= block_start) & (block_start < pad_end[None, :])
    has_blocks = (padded > 0).astype(jnp.int32)
    ordinal = jnp.cumsum(has_blocks) - has_blocks
    later = (eidx[None, :] > eidx[:, None]) & (padded[None, :] > 0)
    next_expert = jnp.min(jnp.where(later, eidx[None, :], N_EXPERTS), axis=1)

    def per_block(per_expert):
        return jnp.sum(jnp.where(owns, per_expert, 0), axis=1).astype(jnp.int32)

    block_e = per_block(eidx[None, :])
    block_rows = per_block(jnp.clip((pad_start + counts)[None, :] - block_start, 0, MOE_BLOCK))
    block_slot = per_block((ordinal % 2)[None, :])
    block_next = per_block(next_expert[None, :])
    return pad_start, (block_e, block_rows, block_slot, block_next)


def kernel(x, norm_mix_g, w_in, conv_w, w_conv_out, ssm_lam_re, ssm_lam_im, ssm_log_dt, ssm_b_re, ssm_b_im, ssm_c_re, ssm_c_im, ssm_d, w_glu, w_out, norm_ffn_g, w_router, b_router, w_gate_up, b_gate_up, w_down, b_down, norm_f_g):
    bsz, seq, d = x.shape
    t = bsz * seq
    x2 = x.reshape(t, d)
    assert seq % TOKEN_TILE == 0 and seq % SSM_TIME_TILE == 0 and w_in.shape[0] == 1

    w_in_b = w_in[0].astype(_BF16)
    n_bcvu = 3 * D_CONV + D_SSM
    g_mix = norm_mix_g[0].reshape(1, d)

    bz, u = _in_proj(x2, g_mix, w_in_b[:, :n_bcvu], conv_w[0], seq)

    tables = _ssm_tables(ssm_lam_re[0], ssm_lam_im[0], ssm_log_dt[0], ssm_b_re[0], ssm_b_im[0],
                         ssm_c_re[0], ssm_c_im[0], ssm_d[0])
    yg = _ssm(u.reshape(bsz, seq, D_SSM), tables).reshape(t, D_SSM)

    x1, h_packed, top_i, top_g, rank, counts = _mix_route(
        x2, bz, yg, g_mix, w_in_b[:, n_bcvu:], w_conv_out[0].astype(_BF16),
        w_glu[0].astype(_BF16), w_out[0].astype(_BF16), norm_ffn_g[0].reshape(1, d),
        jnp.pad(w_router[0], ((0, 0), (0, LANES - N_EXPERTS))).astype(_BF16),
        b_router[0].reshape(N_EXPERTS, 1))

    n_rows = t * TOP_K + N_EXPERTS * MOE_BLOCK
    pad_start, block_plan = _block_plan(counts[:, 0], n_rows // MOE_BLOCK)
    expert_ids = jnp.arange(N_EXPERTS, dtype=jnp.int32)[:, None, None]
    row_start = jnp.sum(jnp.where(top_i[None] == expert_ids, pad_start[:, None, None], 0), axis=0)
    dest = (row_start + rank).reshape(TOP_K * t)

    x_rows = _dispatch(h_packed, dest, n_rows)
    y_rows = _expert_ffn(*block_plan, x_rows, w_gate_up[0],
                         b_gate_up[0].reshape(N_EXPERTS, 1, 2 * D_FF), w_down[0],
                         b_down[0].reshape(N_EXPERTS, 1, D_MODEL))
    y_assign = _collect(y_rows, dest).reshape(TOP_K, t, D_MODEL // 2)
    out = _combine(x1, y_assign, top_g, norm_f_g.reshape(1, d))
    return out.reshape(bsz, seq, d)
```

```python
import functools

import jax
import jax.numpy as jnp
from jax import lax
from jax.experimental import pallas as pl
from jax.experimental.pallas import tpu as pltpu
from jax.experimental.pallas import tpu_sc as plsc

D_MODEL = 1024
D_CONV = 512
CONV_WIDTH = 3
D_SSM = 512
SSM_GROUP = 16
N_SSM_GROUPS = 32
SSM_STATE = 64
N_EXPERTS = 32
TOP_K = 4
D_FF = 1024
SWIGLU_LIMIT = 7.0
SWIGLU_ALPHA = 1.702
RMS_EPS = 1e-6

LANES = 128
SUBLANES = 8
MXU_DIM = 256
CHUNK = 16
SLAB_GROUPS = LANES // SSM_GROUP
N_SLABS = N_SSM_GROUPS // SLAB_GROUPS
SLAB_STATE = SLAB_GROUPS * SSM_STATE
FLAT = CHUNK * LANES
SSM_TIME_TILE = 256
TOKEN_TILE = 1024
IN_TILE = 1024
IN_CHAIN = 512
MIX_TILE = 1024
ROW_CHAIN = 256
MOE_GROUPS = 2
MOE_BLOCK = 1024
FFN_ROW_STEP = 256
SC_ROWS = 64
COLLECT_ROWS = 32
COLLECT_RING = 4
VMEM_LIMIT_BYTES = 56 * 1024 * 1024

_BF16 = jnp.bfloat16
_F32 = jnp.float32


def _rmsnorm(xf, g):
    return xf * lax.rsqrt(jnp.mean(xf * xf, axis=-1, keepdims=True) + RMS_EPS) * g


def _sigmoid(v):
    return 1.0 / (1.0 + jnp.exp(-v))


def _pack_bf16_halves(v):
    n = v.shape[1] // 2
    bits = pltpu.bitcast(v.astype(_BF16).astype(_F32), jnp.uint32)
    return (bits[:, :n] >> 16) | (bits[:, n:] & jnp.uint32(0xFFFF0000))


def _unpack_bf16_halves(w):
    return jnp.concatenate([pltpu.bitcast(w << 16, _F32),
                            pltpu.bitcast(w & jnp.uint32(0xFFFF0000), _F32)], axis=1)


def _in_proj_kernel(tiles_per_seq, x_ref, g_ref, w_ref, cw_ref, bz_ref, u_ref, hbuf):
    tm = x_ref.shape[0]
    halo = SUBLANES

    @pl.when(pl.program_id(0) % tiles_per_seq == 0)
    def _():
        hbuf[0:halo, :] = jnp.zeros((halo, D_CONV), _F32)

    cw = cw_ref[...]
    for r0 in range(0, tm, IN_CHAIN):
        rows = slice(r0, r0 + IN_CHAIN)
        xn = _rmsnorm(x_ref[rows, :], g_ref[...]).astype(_BF16)
        cv = jnp.dot(xn, w_ref[:, D_CONV:3 * D_CONV], preferred_element_type=_F32)
        hbuf[halo + r0:halo + r0 + IN_CHAIN, :] = cv[:, :D_CONV] * cv[:, D_CONV:]
        u_ref[rows, :] = jnp.dot(xn, w_ref[:, 3 * D_CONV:], preferred_element_type=_F32)
        z = cw[CONV_WIDTH - 1:CONV_WIDTH, :] * hbuf[halo + r0:halo + r0 + IN_CHAIN, :]
        for lag in range(1, CONV_WIDTH):
            z = z + (cw[CONV_WIDTH - 1 - lag:CONV_WIDTH - lag, :]
                     * hbuf[halo + r0 - lag:halo + r0 - lag + IN_CHAIN, :])
        b_gate = jnp.dot(xn, w_ref[:, :D_CONV], preferred_element_type=_F32)
        bz_ref[rows, :] = (b_gate * z).astype(_BF16)
    hbuf[0:halo, :] = hbuf[tm:tm + halo, :]


def _in_proj(x2, g, w_bcvu, conv_w, seq):
    t = x2.shape[0]
    tm = IN_TILE
    assert SUBLANES >= CONV_WIDTH - 1 and seq % tm == 0
    return pl.pallas_call(
        functools.partial(_in_proj_kernel, seq // tm),
        out_shape=(jax.ShapeDtypeStruct((t, D_CONV), _BF16),
                   jax.ShapeDtypeStruct((t, D_SSM), _F32)),
        grid=(t // tm,),
        in_specs=[pl.BlockSpec((tm, D_MODEL), lambda i: (i, 0)),
                  pl.BlockSpec((1, D_MODEL), lambda i: (0, 0)),
                  pl.BlockSpec((D_MODEL, 3 * D_CONV + D_SSM), lambda i: (0, 0)),
                  pl.BlockSpec((CONV_WIDTH, D_CONV), lambda i: (0, 0))],
        out_specs=(pl.BlockSpec((tm, D_CONV), lambda i: (i, 0)),
                   pl.BlockSpec((tm, D_SSM), lambda i: (i, 0))),
        scratch_shapes=[pltpu.VMEM((tm + SUBLANES, D_CONV), _F32)],
        compiler_params=pltpu.CompilerParams(
            dimension_semantics=("arbitrary",), vmem_limit_bytes=VMEM_LIMIT_BYTES),
        name="in_proj",
    )(x2, g, w_bcvu, conv_w)


def _ssm_prep_kernel(lr_ref, lc_ref, bm_ref, cm_ref, d_ref, toep_ref, bst_ref, cst_ref, a_ref):
    def discretise(lre, lim, log_dt):
        dt = jnp.exp(log_dt)
        mag = jnp.exp(lre * dt)
        return mag * jnp.cos(lim * dt), mag * jnp.sin(lim * dt)

    def powers(are, aim):
        pre, pim = [jnp.ones_like(are)], [jnp.zeros_like(are)]
        for _ in range(CHUNK):
            pre, pim = (pre + [pre[-1] * are - pim[-1] * aim],
                        pim + [pre[-1] * aim + pim[-1] * are])
        return pre, pim

    lr = lr_ref[0]
    lre, lim = lr[0:1, :], lr[1:2, :]
    are, aim = discretise(lre, lim, lr[2:3, :])
    pre, pim = powers(are, aim)
    den = lre * lre + lim * lim
    q_re = ((are - 1.0) * lre + aim * lim) / den
    q_im = (aim * lre - (are - 1.0) * lim) / den
    bb_re = q_re * bm_ref[0, 0] - q_im * bm_ref[0, 1]
    bb_im = q_re * bm_ref[0, 1] + q_im * bm_ref[0, 0]
    cm_re, cm_im = cm_ref[0, 0], cm_ref[0, 1]

    def split_bf16(v):
        v_hi = v.astype(_BF16)
        return v_hi, (v - v_hi.astype(_F32)).astype(_BF16)

    c_hi, c_lo = split_bf16(jnp.concatenate([cm_re, -cm_im], axis=0))
    kblk = []
    for k in range(CHUNK):
        ab_re = bb_re * pre[k] - bb_im * pim[k]
        ab_im = bb_re * pim[k] + bb_im * pre[k]
        rows = slice((CHUNK - 1 - k) * LANES, (CHUNK - k) * LANES)
        bst_ref[0, rows, :SLAB_STATE] = ab_re.astype(_BF16)
        bst_ref[0, rows, SLAB_STATE:] = ab_im.astype(_BF16)
        ab_hi, ab_lo = split_bf16(jnp.concatenate([ab_re, ab_im], axis=1))
        kblk.append(jnp.dot(ab_hi, c_hi, preferred_element_type=_F32)
                    + (jnp.dot(ab_lo, c_hi, preferred_element_type=_F32)
                       + jnp.dot(ab_hi, c_lo, preferred_element_type=_F32)))
    r = lax.broadcasted_iota(jnp.int32, (LANES, LANES), 0)
    c = lax.broadcasted_iota(jnp.int32, (LANES, LANES), 1)
    kblk[0] = kblk[0] + jnp.where(r == c, jnp.broadcast_to(d_ref[0], (LANES, LANES)), 0.0)
    kblk = [kb.astype(_BF16) for kb in kblk]
    zeros = jnp.zeros((LANES, LANES), _BF16)
    for sp in range(CHUNK):
        for s in range(CHUNK):
            toep_ref[0, sp * LANES:(sp + 1) * LANES, s * LANES:(s + 1) * LANES] = (
                kblk[s - sp] if s >= sp else zeros)

    lc = lc_ref[0]
    cre, cim = discretise(lc[:, 0:1], lc[:, 1:2], lc[:, 2:3])
    qre, qim = powers(cre, cim)
    for s in range(CHUNK):
        cols = slice(s * LANES, (s + 1) * LANES)
        cst_ref[0, :SLAB_STATE, cols] = (cm_re * qre[s + 1] - cm_im * qim[s + 1]).astype(_BF16)
        cst_ref[0, SLAB_STATE:, cols] = (-(cm_re * qim[s + 1] + cm_im * qre[s + 1])).astype(_BF16)
    a_ref[0, 0:1, :] = pre[CHUNK]
    a_ref[0, 1:2, :] = pim[CHUNK]


def _ssm_tables(lam_re, lam_im, log_dt, b_re, b_im, c_re, c_im, d_skip):
    sg = (N_SLABS, SLAB_GROUPS)
    eye = jnp.eye(SLAB_GROUPS, dtype=_F32)
    lam = jnp.stack([lam_re, lam_im, jnp.broadcast_to(log_dt[:, None], lam_re.shape)], axis=0)
    lam_row = lam.reshape(3, N_SLABS, SLAB_STATE).transpose(1, 0, 2)
    lam_col = lam_row.transpose(0, 2, 1)

    def b_blockdiag(b):
        bt = b.reshape(*sg, SSM_STATE, SSM_GROUP).transpose(0, 1, 3, 2)
        return (bt[:, :, :, None, :] * eye[None, :, None, :, None]).reshape(N_SLABS, LANES, SLAB_STATE)

    def c_blockdiag(c):
        ct = c.reshape(*sg, SSM_GROUP, SSM_STATE).transpose(0, 1, 3, 2)
        return (ct[:, :, :, None, :] * eye[None, :, None, :, None]).reshape(N_SLABS, SLAB_STATE, LANES)

    bm = jnp.stack([b_blockdiag(b_re), b_blockdiag(b_im)], axis=1)
    cm = jnp.stack([c_blockdiag(c_re), c_blockdiag(c_im)], axis=1)
    d = d_skip.reshape(N_SLABS, 1, LANES)
    slab3 = lambda sl: (sl, 0, 0)
    slab4 = lambda sl: (sl, 0, 0, 0)
    return pl.pallas_call(
        _ssm_prep_kernel,
        out_shape=(jax.ShapeDtypeStruct((N_SLABS, FLAT, FLAT), _BF16),
                   jax.ShapeDtypeStruct((N_SLABS, FLAT, 2 * SLAB_STATE), _BF16),
                   jax.ShapeDtypeStruct((N_SLABS, 2 * SLAB_STATE, FLAT), _BF16),
                   jax.ShapeDtypeStruct((N_SLABS, 2, SLAB_STATE), _F32)),
        grid=(N_SLABS,),
        in_specs=[pl.BlockSpec((1, 3, SLAB_STATE), slab3),
                  pl.BlockSpec((1, SLAB_STATE, 3), slab3),
                  pl.BlockSpec((1, 2, LANES, SLAB_STATE), slab4),
                  pl.BlockSpec((1, 2, SLAB_STATE, LANES), slab4),
                  pl.BlockSpec((1, 1, LANES), slab3)],
        out_specs=(pl.BlockSpec((1, FLAT, FLAT), slab3),
                   pl.BlockSpec((1, FLAT, 2 * SLAB_STATE), slab3),
                   pl.BlockSpec((1, 2 * SLAB_STATE, FLAT), slab3),
                   pl.BlockSpec((1, 2, SLAB_STATE), slab3)),
        compiler_params=pltpu.CompilerParams(
            dimension_semantics=("arbitrary",), vmem_limit_bytes=VMEM_LIMIT_BYTES),
        name="ssm_prep",
    )(lam_row, lam_col, bm, cm, d)


def _ssm_kernel(u_ref, toep_ref, bst_ref, cst_ref, a_ref, y_ref, uflat, s_scr, xc_scr, carry, ytoep):
    nb, tt, _ = u_ref.shape
    nch = tt // CHUNK
    n = nb * nch

    @pl.when(pl.program_id(1) == 0)
    def _():
        carry[...] = jnp.zeros_like(carry)

    for s in range(CHUNK):
        part = u_ref[:, pl.ds(s, nch, stride=CHUNK), :]
        uflat[:, s * LANES:(s + 1) * LANES] = part.reshape(n, LANES).astype(_BF16)

    n_cb = FLAT // MXU_DIM

    def toeplitz(cb):
        kk = (cb + 1) * MXU_DIM
        cols = slice(cb * MXU_DIM, kk)
        ytoep[:, cols] = jnp.dot(uflat[:, :kk], toep_ref[0, :kk, cols], preferred_element_type=_F32)

    for cb in range(n_cb // 2):
        toeplitz(cb)

    nblk = SLAB_STATE // LANES
    loc_all = jnp.dot(uflat[...], bst_ref[0], preferred_element_type=_F32)
    for cb in range(n_cb // 2, n_cb):
        toeplitz(cb)
    for kb in range(2 * nblk):
        s_scr[kb] = loc_all[:, kb * LANES:(kb + 1) * LANES]

    a = a_ref[0]
    are = [jnp.broadcast_to(a[0:1, kb * LANES:(kb + 1) * LANES], (nb, LANES)) for kb in range(nblk)]
    aim = [jnp.broadcast_to(a[1:2, kb * LANES:(kb + 1) * LANES], (nb, LANES)) for kb in range(nblk)]
    xr = [carry[kb] for kb in range(nblk)]
    xi = [carry[nblk + kb] for kb in range(nblk)]
    for j in range(nch):
        rows = pl.ds(j, nb, stride=nch)
        for kb in range(nblk):
            xc_scr[kb, rows, :] = xr[kb]
            xc_scr[nblk + kb, rows, :] = xi[kb]
            nr = are[kb] * xr[kb] - aim[kb] * xi[kb] + s_scr[kb, rows, :]
            ni = are[kb] * xi[kb] + aim[kb] * xr[kb] + s_scr[nblk + kb, rows, :]
            xr[kb], xi[kb] = nr, ni
    for kb in range(nblk):
        carry[kb] = xr[kb]
        carry[nblk + kb] = xi[kb]

    xc = jnp.concatenate([xc_scr[kb] for kb in range(2 * nblk)], axis=1).astype(_BF16)
    for cb in range(n_cb):
        cols = slice(cb * MXU_DIM, (cb + 1) * MXU_DIM)
        y = ytoep[:, cols] + jnp.dot(xc, cst_ref[0, :, cols], preferred_element_type=_F32)
        y = jax.nn.gelu(y)
        for h in range(MXU_DIM // LANES):
            s = cb * (MXU_DIM // LANES) + h
            y_ref[:, pl.ds(s, nch, stride=CHUNK), :] = (
                y[:, h * LANES:(h + 1) * LANES].reshape(nb, nch, LANES))


def _ssm(u3, tables):
    toep, bst, cst, a_chunk = tables
    nb, seq, _ = u3.shape
    tt = SSM_TIME_TILE
    n = nb * (tt // CHUNK)
    return pl.pallas_call(
        _ssm_kernel,
        out_shape=jax.ShapeDtypeStruct(u3.shape, _F32),
        grid=(N_SLABS, seq // tt),
        in_specs=[pl.BlockSpec((nb, tt, LANES), lambda sl, ti: (0, ti, sl)),
                  pl.BlockSpec((1, FLAT, FLAT), lambda sl, ti: (sl, 0, 0)),
                  pl.BlockSpec((1, FLAT, 2 * SLAB_STATE), lambda sl, ti: (sl, 0, 0)),
                  pl.BlockSpec((1, 2 * SLAB_STATE, FLAT), lambda sl, ti: (sl, 0, 0)),
                  pl.BlockSpec((1, 2, SLAB_STATE), lambda sl, ti: (sl, 0, 0))],
        out_specs=pl.BlockSpec((nb, tt, LANES), lambda sl, ti: (0, ti, sl)),
        scratch_shapes=[pltpu.VMEM((n, FLAT), _BF16),
                        pltpu.VMEM((2 * SLAB_STATE // LANES, n, LANES), _F32),
                        pltpu.VMEM((2 * SLAB_STATE // LANES, n, LANES), _F32),
                        pltpu.VMEM((2 * SLAB_STATE // LANES, nb, LANES), _F32),
                        pltpu.VMEM((n, FLAT), _F32)],
        compiler_params=pltpu.CompilerParams(
            dimension_semantics=("arbitrary", "arbitrary"), vmem_limit_bytes=VMEM_LIMIT_BYTES),
        name="ssm",
    )(u3, toep, bst, cst, a_chunk)


def _mix_route_kernel(tiles_per_group, x_ref, bz_ref, yg_ref, gm_ref, wg_ref, wco_ref, wglu_ref, wout_ref,
                      gf_ref, wr_ref, br_ref,
                      x1_ref, h_ref, ti_ref, tg_ref, rk_ref, cnt_ref, base, merged):
    tm = x_ref.shape[0]
    i = pl.program_id(0)

    @pl.when(i % tiles_per_group == 0)
    def _():
        base[...] = jnp.zeros_like(base)

    chains = [slice(r0, r0 + ROW_CHAIN) for r0 in range(0, tm, ROW_CHAIN)]
    hs, picks = [], []

    def route(j):
        logits_tok = jnp.dot(hs[j], wr_ref[...], preferred_element_type=_F32)
        picks.append(_top_k_rows(chains[j], logits_tok, br_ref, ti_ref, tg_ref))

    def rank(j):
        _rank_rows(chains[j], *picks[j], rk_ref, base)

    for j, rows in enumerate(chains):
        hs.append(_mix_rows(rows, x_ref, bz_ref, yg_ref, gm_ref, wg_ref, wco_ref, wglu_ref,
                            wout_ref, gf_ref, x1_ref, h_ref, merged))
        if j >= 1:
            route(j - 1)
        if j >= 2:
            rank(j - 2)
    last = len(chains) - 1
    route(last)
    for j in range(max(last - 1, 0), last + 1):
        rank(j)
    cnt_ref[i // tiles_per_group] = base[...].astype(jnp.int32)


def _mix_rows(rows, x_ref, bz_ref, yg_ref, gm_ref, wg_ref, wco_ref, wglu_ref, wout_ref,
              gf_ref, x1_ref, h_ref, merged):
    x = x_ref[rows, :]
    xn = _rmsnorm(x, gm_ref[...]).astype(_BF16)
    bz = bz_ref[rows, :]
    yg = yg_ref[rows, :].astype(_BF16)
    for c in range(D_MODEL // MXU_DIM):
        lo = slice(c * MXU_DIM, (c + 1) * MXU_DIM)
        hi = slice(D_MODEL + c * MXU_DIM, D_MODEL + (c + 1) * MXU_DIM)
        gate_a = jnp.dot(xn, wg_ref[:, lo], preferred_element_type=_F32)
        gate_b = jnp.dot(xn, wg_ref[:, hi], preferred_element_type=_F32)
        y_a = jnp.dot(bz, wco_ref[:, lo], preferred_element_type=_F32)
        val = jnp.dot(yg, wglu_ref[:, lo], preferred_element_type=_F32)
        glu_gate = jnp.dot(yg, wglu_ref[:, hi], preferred_element_type=_F32)
        y_b = val * _sigmoid(glu_gate)
        merged[rows, lo] = (_sigmoid(gate_a) * y_a + _sigmoid(gate_b) * y_b).astype(_BF16)
    x1 = x + jnp.dot(merged[rows, :], wout_ref[...], preferred_element_type=_F32)
    x1_ref[rows, :] = x1
    h = _rmsnorm(x1, gf_ref[...])
    h_ref[rows, :] = _pack_bf16_halves(h)
    return h.astype(_BF16)


def _top_k_rows(rows, logits_tok, br_ref, ti_ref, tg_ref):
    tm = rows.stop - rows.start
    logits = jnp.transpose(logits_tok)[:N_EXPERTS, :] + br_ref[...]
    erow = lax.broadcasted_iota(jnp.int32, (N_EXPERTS, tm), 0).astype(_F32)
    neg_inf = jnp.float32(-jnp.inf)
    work = logits
    vals, idxs = [], []
    for _ in range(TOP_K):
        m = jnp.max(work, axis=0, keepdims=True)
        idx = jnp.min(jnp.where(work == m, erow, float(N_EXPERTS)), axis=0, keepdims=True)
        vals.append(m)
        idxs.append(idx)
        work = jnp.where(erow == idx, neg_inf, work)
    exps = [jnp.exp(v - vals[0]) for v in vals]
    denom = exps[0] + exps[1] + exps[2] + exps[3]
    sel = jnp.zeros((N_EXPERTS, tm), _F32)
    for k in range(TOP_K):
        ti_ref[k:k + 1, rows] = idxs[k].astype(jnp.int32)
        tg_ref[k:k + 1, rows] = exps[k] / denom
        tg_ref[TOP_K + k:TOP_K + k + 1, rows] = jnp.zeros((1, tm), _F32)
        sel = sel + (erow == idxs[k]).astype(_F32)
    return sel, idxs


def _rank_rows(rows, sel, idxs, rk_ref, base):
    tm = rows.stop - rows.start
    erow = lax.broadcasted_iota(jnp.int32, (N_EXPERTS, tm), 0).astype(_F32)
    row = lax.broadcasted_iota(jnp.int32, (tm, tm), 0)
    col = lax.broadcasted_iota(jnp.int32, (tm, tm), 1)
    earlier = (row < col).astype(_BF16)
    before = jnp.dot(sel.astype(_BF16), earlier, preferred_element_type=_F32) + base[...]
    for k in range(TOP_K):
        rk = jnp.sum(jnp.where(erow == idxs[k], before, 0.0), axis=0, keepdims=True)
        rk_ref[k:k + 1, rows] = rk.astype(jnp.int32)
    base[...] = base[...] + jnp.sum(sel, axis=1, keepdims=True)


def _mix_route(x2, bz, yg, g_mix, w_gates, w_conv_out, w_glu, w_out, g_ffn, w_router, b_router):
    t = x2.shape[0]
    tm = MIX_TILE
    tok = lambda i: (i, 0)
    tok_lanes = lambda i: (0, i)
    fixed = lambda i: (0, 0)
    weight = lambda shape: pl.BlockSpec(shape, fixed, pipeline_mode=pl.Buffered(1))
    steps = t // tm
    assert steps % MOE_GROUPS == 0
    return pl.pallas_call(
        functools.partial(_mix_route_kernel, steps // MOE_GROUPS),
        out_shape=(jax.ShapeDtypeStruct((t, D_MODEL), _F32),
                   jax.ShapeDtypeStruct((t, D_MODEL // 2), jnp.uint32),
                   jax.ShapeDtypeStruct((TOP_K, t), jnp.int32),
                   jax.ShapeDtypeStruct((2 * TOP_K, t), _F32),
                   jax.ShapeDtypeStruct((TOP_K, t), jnp.int32),
                   jax.ShapeDtypeStruct((MOE_GROUPS, N_EXPERTS, 1), jnp.int32)),
        grid=(steps,),
        in_specs=[pl.BlockSpec((tm, D_MODEL), tok),
                  pl.BlockSpec((tm, D_CONV), tok),
                  pl.BlockSpec((tm, D_SSM), tok),
                  pl.BlockSpec((1, D_MODEL), fixed),
                  weight((D_MODEL, 2 * D_MODEL)),
                  weight((D_CONV, D_MODEL)),
                  weight((D_SSM, 2 * D_MODEL)),
                  weight((D_MODEL, D_MODEL)),
                  pl.BlockSpec((1, D_MODEL), fixed),
                  weight((D_MODEL, LANES)),
                  pl.BlockSpec((N_EXPERTS, 1), fixed)],
        out_specs=(pl.BlockSpec((tm, D_MODEL), tok),
                   pl.BlockSpec((tm, D_MODEL // 2), tok),
                   pl.BlockSpec((TOP_K, tm), tok_lanes),
                   pl.BlockSpec((2 * TOP_K, tm), tok_lanes),
                   pl.BlockSpec((TOP_K, tm), tok_lanes),
                   pl.BlockSpec((MOE_GROUPS, N_EXPERTS, 1), lambda i: (0, 0, 0))),
        scratch_shapes=[pltpu.VMEM((N_EXPERTS, 1), _F32),
                        pltpu.VMEM((tm, D_MODEL), _BF16)],
        compiler_params=pltpu.CompilerParams(
            dimension_semantics=("arbitrary",), vmem_limit_bytes=VMEM_LIMIT_BYTES),
        name="mix_route",
    )(x2, bz, yg, g_mix, w_gates, w_conv_out, w_glu, w_out, g_ffn, w_router, b_router)


def _expert_ffn_kernel(be_ref, nr_ref, slot_ref, next_ref, x_ref, wgu_hbm, bgu_ref, wd_hbm, bd_ref,
                       y_ref, stage_gu, stage_d, wgu_b, wd_b, sem_gu, sem_d):
    b = pl.program_id(0)
    expert = be_ref[b]
    live = nr_ref[b] > 0

    def weight_copies(e, slot):
        return (pltpu.make_async_copy(wgu_hbm.at[e], stage_gu.at[slot], sem_gu.at[slot]),
                pltpu.make_async_copy(wd_hbm.at[e], stage_d.at[slot], sem_d.at[slot]))

    @pl.when(live & ((b == 0) | (be_ref[jnp.maximum(b - 1, 0)] != expert)))
    def _():
        slot = slot_ref[b]

        @pl.when(b == 0)
        def _():
            for cp in weight_copies(expert, slot):
                cp.start()

        for cp in weight_copies(expert, slot):
            cp.wait()
        wgu_b[...] = stage_gu[slot].astype(_BF16)
        wd_b[...] = stage_d[slot].astype(_BF16)

        @pl.when(next_ref[b] < N_EXPERTS)
        def _():
            for cp in weight_copies(next_ref[b], 1 - slot):
                cp.start()

    def ffn_rows(n_rows):
        xw = x_ref[:n_rows, :]
        valid = lax.broadcasted_iota(jnp.int32, xw.shape, 0) < nr_ref[b]
        x = _unpack_bf16_halves(jnp.where(valid, xw, jnp.uint32(0))).astype(_BF16)
        hgu = jnp.dot(x, wgu_b[...], preferred_element_type=_F32) + bgu_ref[0]
        g = jnp.minimum(hgu[:, :D_FF], SWIGLU_LIMIT)
        up = jnp.clip(hgu[:, D_FF:], -SWIGLU_LIMIT, SWIGLU_LIMIT)
        act = (up + 1.0) * (g * _sigmoid(SWIGLU_ALPHA * g))
        y = jnp.dot(act.astype(_BF16), wd_b[...], preferred_element_type=_F32) + bd_ref[0]
        y_ref[:n_rows, :] = _pack_bf16_halves(y)
        if n_rows < MOE_BLOCK:
            y_ref[n_rows:, :] = jnp.zeros((MOE_BLOCK - n_rows, y_ref.shape[1]), y_ref.dtype)

    for height in range(FFN_ROW_STEP, MOE_BLOCK + 1, FFN_ROW_STEP):
        @pl.when((nr_ref[b] > height - FFN_ROW_STEP) & (nr_ref[b] <= height))
        def _():
            ffn_rows(height)

    @pl.when(jnp.logical_not(live))
    def _():
        y_ref[...] = jnp.zeros_like(y_ref)


def _expert_ffn(block_e, block_rows, block_slot, block_next, x_rows, w_gate_up, b_gate_up, w_down,
                b_down):
    n_rows = x_rows.shape[0]
    n_blocks = n_rows // MOE_BLOCK

    def bias_map(b, be, nr, sl, nx):
        return (be[b], 0, 0)

    def row_map(b, be, nr, sl, nx):
        return (b, 0)

    grid_spec = pltpu.PrefetchScalarGridSpec(
        num_scalar_prefetch=4,
        grid=(n_blocks,),
        in_specs=[pl.BlockSpec((MOE_BLOCK, D_MODEL // 2), row_map),
                  pl.BlockSpec(memory_space=pl.ANY),
                  pl.BlockSpec((1, 1, 2 * D_FF), bias_map),
                  pl.BlockSpec(memory_space=pl.ANY),
                  pl.BlockSpec((1, 1, D_MODEL), bias_map)],
        out_specs=pl.BlockSpec((MOE_BLOCK, D_MODEL // 2), row_map),
        scratch_shapes=[pltpu.VMEM((2, D_MODEL, 2 * D_FF), _F32),
                        pltpu.VMEM((2, D_FF, D_MODEL), _F32),
                        pltpu.VMEM((D_MODEL, 2 * D_FF), _BF16),
                        pltpu.VMEM((D_FF, D_MODEL), _BF16),
                        pltpu.SemaphoreType.DMA((2,)),
                        pltpu.SemaphoreType.DMA((2,))],
    )
    return pl.pallas_call(
        _expert_ffn_kernel,
        out_shape=jax.ShapeDtypeStruct((n_rows, D_MODEL // 2), jnp.uint32),
        grid_spec=grid_spec,
        compiler_params=pltpu.CompilerParams(
            dimension_semantics=("arbitrary",), vmem_limit_bytes=VMEM_LIMIT_BYTES),
        name="expert_ffn",
    )(block_e, block_rows, block_slot, block_next, x_rows, w_gate_up, b_gate_up, w_down, b_down)


def _sc_workers():
    info = plsc.get_sparse_core_info()
    return info.num_cores, info.num_cores * info.num_subcores


def _dispatch(h_packed, dest_flat, n_rows, first_token):
    width = h_packed.shape[1]
    t = dest_flat.shape[0] // TOP_K
    n_cores, n_workers = _sc_workers()
    n_chunks = t // (n_workers * SC_ROWS)
    chunks_per_k = t // SC_ROWS
    first_chunk = first_token // SC_ROWS
    assert n_chunks % 2 == 0 and first_token % SC_ROWS == 0

    @functools.partial(
        pl.kernel, mesh=plsc.VectorSubcoreMesh(core_axis_name="c", subcore_axis_name="s"),
        out_type=jax.ShapeDtypeStruct((n_rows, width), h_packed.dtype),
        scratch_types=[pltpu.VMEM((TOP_K, n_chunks, SC_ROWS), jnp.int32),
                       pltpu.VMEM((2, SC_ROWS, width), h_packed.dtype),
                       pltpu.SemaphoreType.DMA((2,)),
                       pltpu.SemaphoreType.DMA((2,))])
    def scatter_rows(h_hbm, dest_hbm, out_hbm, idx_v, buf, lsem, ssem):
        wid = lax.axis_index("s") * n_cores + lax.axis_index("c")
        c0 = wid * n_chunks
        for k in range(TOP_K):
            pltpu.sync_copy(dest_hbm.at[pl.ds(k * chunks_per_k + c0, n_chunks)], idx_v.at[k])

        def load(c, b):
            return pltpu.make_async_copy(
                h_hbm.at[pl.ds((first_chunk + c0 + c) * SC_ROWS, SC_ROWS)], buf.at[b], lsem.at[b])

        def scatters(c, b):
            return [pltpu.make_async_copy(buf.at[b], out_hbm.at[idx_v.at[k, c]], ssem.at[b])
                    for k in range(TOP_K)]

        load(0, 0).start()

        @pl.loop(0, n_chunks, step=2)
        def _(ci):
            for b in range(2):
                c = ci + b

                @pl.when(c >= 1)
                def _():
                    for cp in scatters(c - 1, 1 - b):
                        cp.wait()

                @pl.when(c + 1 < n_chunks)
                def _():
                    load(c + 1, 1 - b).start()

                load(c, b).wait()
                for cp in scatters(c, b):
                    cp.start()

        for cp in scatters(n_chunks - 1, 1):
            cp.wait()

    return scatter_rows(h_packed, dest_flat.reshape(TOP_K * chunks_per_k, SC_ROWS))


def _collect(y_rows, dest_flat):
    n_idx = dest_flat.shape[0]
    width = y_rows.shape[1]
    rows, depth = COLLECT_ROWS, COLLECT_RING
    n_cores, n_workers = _sc_workers()
    n_chunks = n_idx // (n_workers * rows)
    assert n_chunks % depth == 0

    @functools.partial(
        pl.kernel, mesh=plsc.VectorSubcoreMesh(core_axis_name="c", subcore_axis_name="s"),
        out_type=jax.ShapeDtypeStruct((n_idx, width), y_rows.dtype),
        scratch_types=[pltpu.VMEM((n_chunks, rows), jnp.int32),
                       pltpu.VMEM((depth, rows, width), y_rows.dtype),
                       pltpu.SemaphoreType.DMA((depth,)),
                       pltpu.SemaphoreType.DMA((depth,))])
    def gather_rows(y_hbm, dest_hbm, out_hbm, idx_v, buf, gsem, wsem):
        wid = lax.axis_index("s") * n_cores + lax.axis_index("c")
        c0 = wid * n_chunks
        pltpu.sync_copy(dest_hbm.at[pl.ds(c0, n_chunks)], idx_v)

        def gather(c, b):
            return pltpu.make_async_copy(y_hbm.at[idx_v.at[c]], buf.at[b], gsem.at[b])

        def write(c, b):
            return pltpu.make_async_copy(buf.at[b], out_hbm.at[pl.ds((c0 + c) * rows, rows)],
                                         wsem.at[b])

        for c in range(depth - 1):
            gather(c, c).start()

        @pl.loop(0, n_chunks, step=depth)
        def _(ci):
            for b in range(depth):
                c = ci + b
                gather(c, b).wait()
                write(c, b).start()
                prev = (b - 1) % depth

                @pl.when(c >= 1)
                def _():
                    write(c - 1, prev).wait()

                @pl.when(c + depth - 1 < n_chunks)
                def _():
                    gather(c + depth - 1, prev).start()

        write(n_chunks - 1, (n_chunks - 1) % depth).wait()

    return gather_rows(y_rows, dest_flat.reshape(n_idx // rows, rows))


def _combine_kernel(x1_ref, ya_ref, tg_ref, g_ref, *rest):
    o_ref = rest[-1]
    acc = x1_ref[...]
    tg = jnp.transpose(tg_ref[...])
    for k in range(TOP_K):
        acc = acc + tg[:, k:k + 1] * _unpack_bf16_halves(ya_ref[k])
    o_ref[...] = _rmsnorm(acc, g_ref[...])


def _combine(x1, y_group, top_g, g_final, group, out_prev):
    t = x1.shape[0]
    tm = TOKEN_TILE
    steps = y_group.shape[1] // tm
    first = group * steps
    in_specs = [pl.BlockSpec((tm, D_MODEL), lambda i: (first + i, 0)),
                pl.BlockSpec((TOP_K, tm, D_MODEL // 2), lambda i: (0, i, 0)),
                pl.BlockSpec((2 * TOP_K, tm), lambda i: (0, first + i)),
                pl.BlockSpec((1, D_MODEL), lambda i: (0, 0))]
    args = [x1, y_group, top_g, g_final]
    aliases = {}
    if out_prev is not None:
        in_specs.append(pl.BlockSpec(memory_space=pl.ANY))
        args.append(out_prev)
        aliases = {len(args) - 1: 0}
    return pl.pallas_call(
        _combine_kernel,
        out_shape=jax.ShapeDtypeStruct((t, D_MODEL), _F32),
        grid=(steps,),
        in_specs=in_specs,
        out_specs=pl.BlockSpec((tm, D_MODEL), lambda i: (first + i, 0)),
        input_output_aliases=aliases,
        compiler_params=pltpu.CompilerParams(
            dimension_semantics=("arbitrary",), vmem_limit_bytes=VMEM_LIMIT_BYTES),
        name=f"combine{group}",
    )(*args)


def _block_plan(counts, n_blocks):
    padded = ((counts + MOE_BLOCK - 1) // MOE_BLOCK) * MOE_BLOCK
    pad_end = jnp.cumsum(padded)
    pad_start = pad_end - padded
    block_start = (jnp.arange(n_blocks, dtype=jnp.int32) * MOE_BLOCK)[:, None]
    eidx = jnp.arange(N_EXPERTS, dtype=jnp.int32)
    owns = (pad_start[None, :] <= block_start) & (block_start < pad_end[None, :])
    has_blocks = (padded > 0).astype(jnp.int32)
    ordinal = jnp.cumsum(has_blocks) - has_blocks
    later = (eidx[None, :] > eidx[:, None]) & (padded[None, :] > 0)
    next_expert = jnp.min(jnp.where(later, eidx[None, :], N_EXPERTS), axis=1)

    def per_block(per_expert):
        return jnp.sum(jnp.where(owns, per_expert, 0), axis=1).astype(jnp.int32)

    block_e = per_block(eidx[None, :])
    block_rows = per_block(jnp.clip((pad_start + counts)[None, :] - block_start, 0, MOE_BLOCK))
    block_slot = per_block((ordinal % 2)[None, :])
    block_next = per_block(next_expert[None, :])
    return pad_start, (block_e, block_rows, block_slot, block_next)


def kernel(x, norm_mix_g, w_in, conv_w, w_conv_out, ssm_lam_re, ssm_lam_im, ssm_log_dt, ssm_b_re, ssm_b_im, ssm_c_re, ssm_c_im, ssm_d, w_glu, w_out, norm_ffn_g, w_router, b_router, w_gate_up, b_gate_up, w_down, b_down, norm_f_g):
    bsz, seq, d = x.shape
    t = bsz * seq
    x2 = x.reshape(t, d)
    assert seq % TOKEN_TILE == 0 and seq % SSM_TIME_TILE == 0 and w_in.shape[0] == 1

    w_in_b = w_in[0].astype(_BF16)
    n_bcvu = 3 * D_CONV + D_SSM
    g_mix = norm_mix_g[0].reshape(1, d)

    bz, u = _in_proj(x2, g_mix, w_in_b[:, :n_bcvu], conv_w[0], seq)

    tables = _ssm_tables(ssm_lam_re[0], ssm_lam_im[0], ssm_log_dt[0], ssm_b_re[0], ssm_b_im[0],
                         ssm_c_re[0], ssm_c_im[0], ssm_d[0])
    yg = _ssm(u.reshape(bsz, seq, D_SSM), tables).reshape(t, D_SSM)

    x1, h_packed, top_i, top_g, rank, counts = _mix_route(
        x2, bz, yg, g_mix, w_in_b[:, n_bcvu:], w_conv_out[0].astype(_BF16),
        w_glu[0].astype(_BF16), w_out[0].astype(_BF16), norm_ffn_g[0].reshape(1, d),
        jnp.pad(w_router[0], ((0, 0), (0, LANES - N_EXPERTS))).astype(_BF16),
        b_router[0].reshape(N_EXPERTS, 1))

    tg = t // MOE_GROUPS
    n_rows = tg * TOP_K + N_EXPERTS * MOE_BLOCK
    expert_ids = jnp.arange(N_EXPERTS, dtype=jnp.int32)[:, None, None]
    biases = (b_gate_up[0].reshape(N_EXPERTS, 1, 2 * D_FF), b_down[0].reshape(N_EXPERTS, 1, D_MODEL))
    dests, y_rows = [], []
    for g in range(MOE_GROUPS):
        tokens = slice(g * tg, (g + 1) * tg)
        pad_start, block_plan = _block_plan(counts[g, :, 0], n_rows // MOE_BLOCK)
        row_start = jnp.sum(jnp.where(top_i[None, :, tokens] == expert_ids, pad_start[:, None, None], 0),
                            axis=0)
        dests.append((row_start + rank[:, tokens]).reshape(TOP_K * tg))
        x_rows = _dispatch(h_packed, dests[g], n_rows, g * tg)
        y_rows.append(_expert_ffn(*block_plan, x_rows, w_gate_up[0], biases[0], w_down[0], biases[1]))
    out = None
    for g in range(MOE_GROUPS):
        y_group = _collect(y_rows[g], dests[g]).reshape(TOP_K, tg, D_MODEL // 2)
        out = _combine(x1, y_group, top_g, norm_f_g.reshape(1, d), g, out)
    return out.reshape(bsz, seq, d)
```

```python
import functools

import jax
import jax.numpy as jnp
from jax import lax
from jax.experimental import pallas as pl
from jax.experimental.pallas import tpu as pltpu
from jax.experimental.pallas import tpu_sc as plsc

D_MODEL = 1024
D_CONV = 512
CONV_WIDTH = 3
D_SSM = 512
SSM_GROUP = 16
N_SSM_GROUPS = 32
SSM_STATE = 64
N_EXPERTS = 32
TOP_K = 4
D_FF = 1024
SWIGLU_LIMIT = 7.0
SWIGLU_ALPHA = 1.702
RMS_EPS = 1e-6

LANES = 128
SUBLANES = 8
MXU_DIM = 256
CHUNK = 16
SLAB_GROUPS = LANES // SSM_GROUP
N_SLABS = N_SSM_GROUPS // SLAB_GROUPS
SLAB_STATE = SLAB_GROUPS * SSM_STATE
FLAT = CHUNK * LANES
SSM_TIME_TILE = 256
TOKEN_TILE = 1024
IN_TILE = 1024
IN_CHAIN = 512
MIX_TILE = 1024
ROW_CHAIN = 256
MOE_BLOCK = 1024
FFN_ROW_STEP = 256
SC_ROWS = 64
COLLECT_ROWS = 32
COLLECT_RING = 4
VMEM_LIMIT_BYTES = 56 * 1024 * 1024

_BF16 = jnp.bfloat16
_F32 = jnp.float32


def _rmsnorm(xf, g):
    return xf * lax.rsqrt(jnp.mean(xf * xf, axis=-1, keepdims=True) + RMS_EPS) * g


def _sigmoid(v):
    return 0.5 * jnp.tanh(0.5 * v) + 0.5


def _pack_bf16_halves(v):
    n = v.shape[1] // 2
    bits = pltpu.bitcast(v.astype(_BF16).astype(_F32), jnp.uint32)
    return (bits[:, :n] >> 16) | (bits[:, n:] & jnp.uint32(0xFFFF0000))


def _unpack_bf16_halves(w):
    return jnp.concatenate([pltpu.bitcast(w << 16, _F32),
                            pltpu.bitcast(w & jnp.uint32(0xFFFF0000), _F32)], axis=1)


def _in_proj_kernel(tiles_per_seq, x_ref, g_ref, w_ref, cw_ref, bz_ref, u_ref, hbuf):
    tm = x_ref.shape[0]
    halo = SUBLANES

    @pl.when(pl.program_id(0) % tiles_per_seq == 0)
    def _():
        hbuf[0:halo, :] = jnp.zeros((halo, D_CONV), _F32)

    cw = cw_ref[...]
    for r0 in range(0, tm, IN_CHAIN):
        rows = slice(r0, r0 + IN_CHAIN)
        xn = _rmsnorm(x_ref[rows, :], g_ref[...]).astype(_BF16)
        cv = jnp.dot(xn, w_ref[:, D_CONV:3 * D_CONV], preferred_element_type=_F32)
        hbuf[halo + r0:halo + r0 + IN_CHAIN, :] = cv[:, :D_CONV] * cv[:, D_CONV:]
        u_ref[rows, :] = jnp.dot(xn, w_ref[:, 3 * D_CONV:], preferred_element_type=_F32)
        z = cw[CONV_WIDTH - 1:CONV_WIDTH, :] * hbuf[halo + r0:halo + r0 + IN_CHAIN, :]
        for lag in range(1, CONV_WIDTH):
            z = z + (cw[CONV_WIDTH - 1 - lag:CONV_WIDTH - lag, :]
                     * hbuf[halo + r0 - lag:halo + r0 - lag + IN_CHAIN, :])
        b_gate = jnp.dot(xn, w_ref[:, :D_CONV], preferred_element_type=_F32)
        bz_ref[rows, :] = (b_gate * z).astype(_BF16)
    hbuf[0:halo, :] = hbuf[tm:tm + halo, :]


def _in_proj(x2, g, w_bcvu, conv_w, seq):
    t = x2.shape[0]
    tm = IN_TILE
    assert SUBLANES >= CONV_WIDTH - 1 and seq % tm == 0
    return pl.pallas_call(
        functools.partial(_in_proj_kernel, seq // tm),
        out_shape=(jax.ShapeDtypeStruct((t, D_CONV), _BF16),
                   jax.ShapeDtypeStruct((t, D_SSM), _F32)),
        grid=(t // tm,),
        in_specs=[pl.BlockSpec((tm, D_MODEL), lambda i: (i, 0)),
                  pl.BlockSpec((1, D_MODEL), lambda i: (0, 0)),
                  pl.BlockSpec((D_MODEL, 3 * D_CONV + D_SSM), lambda i: (0, 0)),
                  pl.BlockSpec((CONV_WIDTH, D_CONV), lambda i: (0, 0))],
        out_specs=(pl.BlockSpec((tm, D_CONV), lambda i: (i, 0)),
                   pl.BlockSpec((tm, D_SSM), lambda i: (i, 0))),
        scratch_shapes=[pltpu.VMEM((tm + SUBLANES, D_CONV), _F32)],
        compiler_params=pltpu.CompilerParams(
            dimension_semantics=("arbitrary",), vmem_limit_bytes=VMEM_LIMIT_BYTES),
        name="in_proj",
    )(x2, g, w_bcvu, conv_w)


def _ssm_prep_kernel(lr_ref, lc_ref, bm_ref, cm_ref, d_ref, toep_ref, bst_ref, cst_ref, a_ref):
    def discretise(lre, lim, log_dt):
        dt = jnp.exp(log_dt)
        mag = jnp.exp(lre * dt)
        return mag * jnp.cos(lim * dt), mag * jnp.sin(lim * dt)

    def powers(are, aim):
        pre, pim = [jnp.ones_like(are)], [jnp.zeros_like(are)]
        for _ in range(CHUNK):
            pre, pim = (pre + [pre[-1] * are - pim[-1] * aim],
                        pim + [pre[-1] * aim + pim[-1] * are])
        return pre, pim

    lr = lr_ref[0]
    lre, lim = lr[0:1, :], lr[1:2, :]
    are, aim = discretise(lre, lim, lr[2:3, :])
    pre, pim = powers(are, aim)
    den = lre * lre + lim * lim
    q_re = ((are - 1.0) * lre + aim * lim) / den
    q_im = (aim * lre - (are - 1.0) * lim) / den
    bb_re = q_re * bm_ref[0, 0] - q_im * bm_ref[0, 1]
    bb_im = q_re * bm_ref[0, 1] + q_im * bm_ref[0, 0]
    cm_re, cm_im = cm_ref[0, 0], cm_ref[0, 1]

    def split_bf16(v):
        v_hi = v.astype(_BF16)
        return v_hi, (v - v_hi.astype(_F32)).astype(_BF16)

    c_hi, c_lo = split_bf16(jnp.concatenate([cm_re, -cm_im], axis=0))
    kblk = []
    for k in range(CHUNK):
        ab_re = bb_re * pre[k] - bb_im * pim[k]
        ab_im = bb_re * pim[k] + bb_im * pre[k]
        rows = slice((CHUNK - 1 - k) * LANES, (CHUNK - k) * LANES)
        bst_ref[0, rows, :SLAB_STATE] = ab_re.astype(_BF16)
        bst_ref[0, rows, SLAB_STATE:] = ab_im.astype(_BF16)
        ab_hi, ab_lo = split_bf16(jnp.concatenate([ab_re, ab_im], axis=1))
        kblk.append(jnp.dot(ab_hi, c_hi, preferred_element_type=_F32)
                    + (jnp.dot(ab_lo, c_hi, preferred_element_type=_F32)
                       + jnp.dot(ab_hi, c_lo, preferred_element_type=_F32)))
    r = lax.broadcasted_iota(jnp.int32, (LANES, LANES), 0)
    c = lax.broadcasted_iota(jnp.int32, (LANES, LANES), 1)
    kblk[0] = kblk[0] + jnp.where(r == c, jnp.broadcast_to(d_ref[0], (LANES, LANES)), 0.0)
    kblk = [kb.astype(_BF16) for kb in kblk]
    zeros = jnp.zeros((LANES, LANES), _BF16)
    for sp in range(CHUNK):
        for s in range(CHUNK):
            toep_ref[0, sp * LANES:(sp + 1) * LANES, s * LANES:(s + 1) * LANES] = (
                kblk[s - sp] if s >= sp else zeros)

    lc = lc_ref[0]
    cre, cim = discretise(lc[:, 0:1], lc[:, 1:2], lc[:, 2:3])
    qre, qim = powers(cre, cim)
    for s in range(CHUNK):
        cols = slice(s * LANES, (s + 1) * LANES)
        cst_ref[0, :SLAB_STATE, cols] = (cm_re * qre[s + 1] - cm_im * qim[s + 1]).astype(_BF16)
        cst_ref[0, SLAB_STATE:, cols] = (-(cm_re * qim[s + 1] + cm_im * qre[s + 1])).astype(_BF16)
    a_ref[0, 0:1, :] = pre[CHUNK]
    a_ref[0, 1:2, :] = pim[CHUNK]


def _ssm_tables(lam_re, lam_im, log_dt, b_re, b_im, c_re, c_im, d_skip):
    sg = (N_SLABS, SLAB_GROUPS)
    eye = jnp.eye(SLAB_GROUPS, dtype=_F32)
    lam = jnp.stack([lam_re, lam_im, jnp.broadcast_to(log_dt[:, None], lam_re.shape)], axis=0)
    lam_row = lam.reshape(3, N_SLABS, SLAB_STATE).transpose(1, 0, 2)
    lam_col = lam_row.transpose(0, 2, 1)

    def b_blockdiag(b):
        bt = b.reshape(*sg, SSM_STATE, SSM_GROUP).transpose(0, 1, 3, 2)
        return (bt[:, :, :, None, :] * eye[None, :, None, :, None]).reshape(N_SLABS, LANES, SLAB_STATE)

    def c_blockdiag(c):
        ct = c.reshape(*sg, SSM_GROUP, SSM_STATE).transpose(0, 1, 3, 2)
        return (ct[:, :, :, None, :] * eye[None, :, None, :, None]).reshape(N_SLABS, SLAB_STATE, LANES)

    bm = jnp.stack([b_blockdiag(b_re), b_blockdiag(b_im)], axis=1)
    cm = jnp.stack([c_blockdiag(c_re), c_blockdiag(c_im)], axis=1)
    d = d_skip.reshape(N_SLABS, 1, LANES)
    slab3 = lambda sl: (sl, 0, 0)
    slab4 = lambda sl: (sl, 0, 0, 0)
    return pl.pallas_call(
        _ssm_prep_kernel,
        out_shape=(jax.ShapeDtypeStruct((N_SLABS, FLAT, FLAT), _BF16),
                   jax.ShapeDtypeStruct((N_SLABS, FLAT, 2 * SLAB_STATE), _BF16),
                   jax.ShapeDtypeStruct((N_SLABS, 2 * SLAB_STATE, FLAT), _BF16),
                   jax.ShapeDtypeStruct((N_SLABS, 2, SLAB_STATE), _F32)),
        grid=(N_SLABS,),
        in_specs=[pl.BlockSpec((1, 3, SLAB_STATE), slab3),
                  pl.BlockSpec((1, SLAB_STATE, 3), slab3),
                  pl.BlockSpec((1, 2, LANES, SLAB_STATE), slab4),
                  pl.BlockSpec((1, 2, SLAB_STATE, LANES), slab4),
                  pl.BlockSpec((1, 1, LANES), slab3)],
        out_specs=(pl.BlockSpec((1, FLAT, FLAT), slab3),
                   pl.BlockSpec((1, FLAT, 2 * SLAB_STATE), slab3),
                   pl.BlockSpec((1, 2 * SLAB_STATE, FLAT), slab3),
                   pl.BlockSpec((1, 2, SLAB_STATE), slab3)),
        compiler_params=pltpu.CompilerParams(
            dimension_semantics=("arbitrary",), vmem_limit_bytes=VMEM_LIMIT_BYTES),
        name="ssm_prep",
    )(lam_row, lam_col, bm, cm, d)


def _ssm_kernel(u_ref, toep_ref, bst_ref, cst_ref, a_ref, y_ref, uflat, s_scr, xc_scr, carry, ytoep):
    nb, tt, _ = u_ref.shape
    nch = tt // CHUNK
    n = nb * nch

    @pl.when(pl.program_id(1) == 0)
    def _():
        carry[...] = jnp.zeros_like(carry)

    for s in range(CHUNK):
        part = u_ref[:, pl.ds(s, nch, stride=CHUNK), :]
        uflat[:, s * LANES:(s + 1) * LANES] = part.reshape(n, LANES).astype(_BF16)

    n_cb = FLAT // MXU_DIM

    def toeplitz(cb):
        kk = (cb + 1) * MXU_DIM
        cols = slice(cb * MXU_DIM, kk)
        ytoep[:, cols] = jnp.dot(uflat[:, :kk], toep_ref[0, :kk, cols], preferred_element_type=_F32)

    for cb in range(n_cb // 2):
        toeplitz(cb)

    nblk = SLAB_STATE // LANES
    loc_all = jnp.dot(uflat[...], bst_ref[0], preferred_element_type=_F32)
    for cb in range(n_cb // 2, n_cb):
        toeplitz(cb)
    for kb in range(2 * nblk):
        s_scr[kb] = loc_all[:, kb * LANES:(kb + 1) * LANES]

    a = a_ref[0]
    are = [jnp.broadcast_to(a[0:1, kb * LANES:(kb + 1) * LANES], (nb, LANES)) for kb in range(nblk)]
    aim = [jnp.broadcast_to(a[1:2, kb * LANES:(kb + 1) * LANES], (nb, LANES)) for kb in range(nblk)]
    xr = [carry[kb] for kb in range(nblk)]
    xi = [carry[nblk + kb] for kb in range(nblk)]
    for j in range(nch):
        rows = pl.ds(j, nb, stride=nch)
        for kb in range(nblk):
            xc_scr[kb, rows, :] = xr[kb]
            xc_scr[nblk + kb, rows, :] = xi[kb]
            nr = are[kb] * xr[kb] - aim[kb] * xi[kb] + s_scr[kb, rows, :]
            ni = are[kb] * xi[kb] + aim[kb] * xr[kb] + s_scr[nblk + kb, rows, :]
            xr[kb], xi[kb] = nr, ni
    for kb in range(nblk):
        carry[kb] = xr[kb]
        carry[nblk + kb] = xi[kb]

    xc = jnp.concatenate([xc_scr[kb] for kb in range(2 * nblk)], axis=1).astype(_BF16)
    for cb in range(n_cb):
        cols = slice(cb * MXU_DIM, (cb + 1) * MXU_DIM)
        y = ytoep[:, cols] + jnp.dot(xc, cst_ref[0, :, cols], preferred_element_type=_F32)
        y = jax.nn.gelu(y)
        for h in range(MXU_DIM // LANES):
            s = cb * (MXU_DIM // LANES) + h
            y_ref[:, pl.ds(s, nch, stride=CHUNK), :] = (
                y[:, h * LANES:(h + 1) * LANES].reshape(nb, nch, LANES))


def _ssm(u3, tables):
    toep, bst, cst, a_chunk = tables
    nb, seq, _ = u3.shape
    tt = SSM_TIME_TILE
    n = nb * (tt // CHUNK)
    return pl.pallas_call(
        _ssm_kernel,
        out_shape=jax.ShapeDtypeStruct(u3.shape, _F32),
        grid=(N_SLABS, seq // tt),
        in_specs=[pl.BlockSpec((nb, tt, LANES), lambda sl, ti: (0, ti, sl)),
                  pl.BlockSpec((1, FLAT, FLAT), lambda sl, ti: (sl, 0, 0)),
                  pl.BlockSpec((1, FLAT, 2 * SLAB_STATE), lambda sl, ti: (sl, 0, 0)),
                  pl.BlockSpec((1, 2 * SLAB_STATE, FLAT), lambda sl, ti: (sl, 0, 0)),
                  pl.BlockSpec((1, 2, SLAB_STATE), lambda sl, ti: (sl, 0, 0))],
        out_specs=pl.BlockSpec((nb, tt, LANES), lambda sl, ti: (0, ti, sl)),
        scratch_shapes=[pltpu.VMEM((n, FLAT), _BF16),
                        pltpu.VMEM((2 * SLAB_STATE // LANES, n, LANES), _F32),
                        pltpu.VMEM((2 * SLAB_STATE // LANES, n, LANES), _F32),
                        pltpu.VMEM((2 * SLAB_STATE // LANES, nb, LANES), _F32),
                        pltpu.VMEM((n, FLAT), _F32)],
        compiler_params=pltpu.CompilerParams(
            dimension_semantics=("arbitrary", "arbitrary"), vmem_limit_bytes=VMEM_LIMIT_BYTES),
        name="ssm",
    )(u3, toep, bst, cst, a_chunk)


def _mix_route_kernel(x_ref, bz_ref, yg_ref, gm_ref, wg_ref, wco_ref, wglu_ref, wout_ref,
                      gf_ref, wr_ref, br_ref,
                      x1_ref, h_ref, ti_ref, tg_ref, rk_ref, cnt_ref, base, merged):
    tm = x_ref.shape[0]

    @pl.when(pl.program_id(0) == 0)
    def _():
        base[...] = jnp.zeros_like(base)

    chains = [slice(r0, r0 + ROW_CHAIN) for r0 in range(0, tm, ROW_CHAIN)]
    hs, picks = [], []

    def route(j):
        logits_tok = jnp.dot(hs[j], wr_ref[...], preferred_element_type=_F32)
        picks.append(_top_k_rows(chains[j], logits_tok, br_ref, ti_ref, tg_ref))

    def rank(j):
        _rank_rows(chains[j], *picks[j], rk_ref, base)

    for j, rows in enumerate(chains):
        hs.append(_mix_rows(rows, x_ref, bz_ref, yg_ref, gm_ref, wg_ref, wco_ref, wglu_ref,
                            wout_ref, gf_ref, x1_ref, h_ref, merged))
        if j >= 1:
            route(j - 1)
        if j >= 2:
            rank(j - 2)
    last = len(chains) - 1
    route(last)
    for j in range(max(last - 1, 0), last + 1):
        rank(j)
    cnt_ref[...] = base[...].astype(jnp.int32)


def _mix_rows(rows, x_ref, bz_ref, yg_ref, gm_ref, wg_ref, wco_ref, wglu_ref, wout_ref,
              gf_ref, x1_ref, h_ref, merged):
    x = x_ref[rows, :]
    xn = _rmsnorm(x, gm_ref[...]).astype(_BF16)
    bz = bz_ref[rows, :]
    yg = yg_ref[rows, :].astype(_BF16)
    for c in range(D_MODEL // MXU_DIM):
        lo = slice(c * MXU_DIM, (c + 1) * MXU_DIM)
        hi = slice(D_MODEL + c * MXU_DIM, D_MODEL + (c + 1) * MXU_DIM)
        gate_a = jnp.dot(xn, wg_ref[:, lo], preferred_element_type=_F32)
        gate_b = jnp.dot(xn, wg_ref[:, hi], preferred_element_type=_F32)
        y_a = jnp.dot(bz, wco_ref[:, lo], preferred_element_type=_F32)
        val = jnp.dot(yg, wglu_ref[:, lo], preferred_element_type=_F32)
        glu_gate = jnp.dot(yg, wglu_ref[:, hi], preferred_element_type=_F32)
        y_b = val * _sigmoid(glu_gate)
        merged[rows, lo] = (_sigmoid(gate_a) * y_a + _sigmoid(gate_b) * y_b).astype(_BF16)
    x1 = x + jnp.dot(merged[rows, :], wout_ref[...], preferred_element_type=_F32)
    x1_ref[rows, :] = x1
    h = _rmsnorm(x1, gf_ref[...])
    h_ref[rows, :] = _pack_bf16_halves(h)
    return h.astype(_BF16)


def _top_k_rows(rows, logits_tok, br_ref, ti_ref, tg_ref):
    tm = rows.stop - rows.start
    logits = jnp.transpose(logits_tok)[:N_EXPERTS, :] + br_ref[...]
    erow = lax.broadcasted_iota(jnp.int32, (N_EXPERTS, tm), 0).astype(_F32)
    neg_inf = jnp.float32(-jnp.inf)
    work = logits
    vals, idxs = [], []
    for _ in range(TOP_K):
        m = jnp.max(work, axis=0, keepdims=True)
        idx = jnp.min(jnp.where(work == m, erow, float(N_EXPERTS)), axis=0, keepdims=True)
        vals.append(m)
        idxs.append(idx)
        work = jnp.where(erow == idx, neg_inf, work)
    exps = [jnp.exp(v - vals[0]) for v in vals]
    denom = exps[0] + exps[1] + exps[2] + exps[3]
    sel = jnp.zeros((N_EXPERTS, tm), _F32)
    for k in range(TOP_K):
        ti_ref[k:k + 1, rows] = idxs[k].astype(jnp.int32)
        tg_ref[k:k + 1, rows] = exps[k] / denom
        tg_ref[TOP_K + k:TOP_K + k + 1, rows] = jnp.zeros((1, tm), _F32)
        sel = sel + (erow == idxs[k]).astype(_F32)
    return sel, idxs


def _rank_rows(rows, sel, idxs, rk_ref, base):
    tm = rows.stop - rows.start
    erow = lax.broadcasted_iota(jnp.int32, (N_EXPERTS, tm), 0).astype(_F32)
    row = lax.broadcasted_iota(jnp.int32, (tm, tm), 0)
    col = lax.broadcasted_iota(jnp.int32, (tm, tm), 1)
    earlier = (row < col).astype(_BF16)
    before = jnp.dot(sel.astype(_BF16), earlier, preferred_element_type=_F32) + base[...]
    for k in range(TOP_K):
        rk = jnp.sum(jnp.where(erow == idxs[k], before, 0.0), axis=0, keepdims=True)
        rk_ref[k:k + 1, rows] = rk.astype(jnp.int32)
    base[...] = base[...] + jnp.sum(sel, axis=1, keepdims=True)


def _mix_route(x2, bz, yg, g_mix, w_gates, w_conv_out, w_glu, w_out, g_ffn, w_router, b_router):
    t = x2.shape[0]
    tm = MIX_TILE
    tok = lambda i: (i, 0)
    tok_lanes = lambda i: (0, i)
    fixed = lambda i: (0, 0)
    weight = lambda shape: pl.BlockSpec(shape, fixed, pipeline_mode=pl.Buffered(1))
    return pl.pallas_call(
        _mix_route_kernel,
        out_shape=(jax.ShapeDtypeStruct((t, D_MODEL), _F32),
                   jax.ShapeDtypeStruct((t, D_MODEL // 2), jnp.uint32),
                   jax.ShapeDtypeStruct((TOP_K, t), jnp.int32),
                   jax.ShapeDtypeStruct((2 * TOP_K, t), _F32),
                   jax.ShapeDtypeStruct((TOP_K, t), jnp.int32),
                   jax.ShapeDtypeStruct((N_EXPERTS, 1), jnp.int32)),
        grid=(t // tm,),
        in_specs=[pl.BlockSpec((tm, D_MODEL), tok),
                  pl.BlockSpec((tm, D_CONV), tok),
                  pl.BlockSpec((tm, D_SSM), tok),
                  pl.BlockSpec((1, D_MODEL), fixed),
                  weight((D_MODEL, 2 * D_MODEL)),
                  weight((D_CONV, D_MODEL)),
                  weight((D_SSM, 2 * D_MODEL)),
                  weight((D_MODEL, D_MODEL)),
                  pl.BlockSpec((1, D_MODEL), fixed),
                  weight((D_MODEL, LANES)),
                  pl.BlockSpec((N_EXPERTS, 1), fixed)],
        out_specs=(pl.BlockSpec((tm, D_MODEL), tok),
                   pl.BlockSpec((tm, D_MODEL // 2), tok),
                   pl.BlockSpec((TOP_K, tm), tok_lanes),
                   pl.BlockSpec((2 * TOP_K, tm), tok_lanes),
                   pl.BlockSpec((TOP_K, tm), tok_lanes),
                   pl.BlockSpec((N_EXPERTS, 1), fixed)),
        scratch_shapes=[pltpu.VMEM((N_EXPERTS, 1), _F32),
                        pltpu.VMEM((tm, D_MODEL), _BF16)],
        compiler_params=pltpu.CompilerParams(
            dimension_semantics=("arbitrary",), vmem_limit_bytes=VMEM_LIMIT_BYTES),
        name="mix_route",
    )(x2, bz, yg, g_mix, w_gates, w_conv_out, w_glu, w_out, g_ffn, w_router, b_router)


def _expert_ffn_kernel(be_ref, nr_ref, slot_ref, next_ref, x_ref, wgu_hbm, bgu_ref, wd_hbm, bd_ref,
                       y_ref, stage_gu, stage_d, wgu_b, wd_b, sem_gu, sem_d):
    b = pl.program_id(0)
    expert = be_ref[b]
    live = nr_ref[b] > 0

    def weight_copies(e, slot):
        return (pltpu.make_async_copy(wgu_hbm.at[e], stage_gu.at[slot], sem_gu.at[slot]),
                pltpu.make_async_copy(wd_hbm.at[e], stage_d.at[slot], sem_d.at[slot]))

    @pl.when(live & ((b == 0) | (be_ref[jnp.maximum(b - 1, 0)] != expert)))
    def _():
        slot = slot_ref[b]

        @pl.when(b == 0)
        def _():
            for cp in weight_copies(expert, slot):
                cp.start()

        for cp in weight_copies(expert, slot):
            cp.wait()
        wgu_b[...] = stage_gu[slot].astype(_BF16)
        wd_b[...] = stage_d[slot].astype(_BF16)

        @pl.when(next_ref[b] < N_EXPERTS)
        def _():
            for cp in weight_copies(next_ref[b], 1 - slot):
                cp.start()

    def ffn_rows(n_rows):
        xw = x_ref[:n_rows, :]
        valid = lax.broadcasted_iota(jnp.int32, xw.shape, 0) < nr_ref[b]
        x = _unpack_bf16_halves(jnp.where(valid, xw, jnp.uint32(0))).astype(_BF16)
        hgu = jnp.dot(x, wgu_b[...], preferred_element_type=_F32) + bgu_ref[0]
        g = jnp.minimum(hgu[:, :D_FF], SWIGLU_LIMIT)
        up = jnp.clip(hgu[:, D_FF:], -SWIGLU_LIMIT, SWIGLU_LIMIT)
        act = (up + 1.0) * (g * _sigmoid(SWIGLU_ALPHA * g))
        y = jnp.dot(act.astype(_BF16), wd_b[...], preferred_element_type=_F32) + bd_ref[0]
        y_ref[:n_rows, :] = _pack_bf16_halves(y)
        if n_rows < MOE_BLOCK:
            y_ref[n_rows:, :] = jnp.zeros((MOE_BLOCK - n_rows, y_ref.shape[1]), y_ref.dtype)

    for height in range(FFN_ROW_STEP, MOE_BLOCK + 1, FFN_ROW_STEP):
        @pl.when((nr_ref[b] > height - FFN_ROW_STEP) & (nr_ref[b] <= height))
        def _():
            ffn_rows(height)

    @pl.when(jnp.logical_not(live))
    def _():
        y_ref[...] = jnp.zeros_like(y_ref)


def _expert_ffn(block_e, block_rows, block_slot, block_next, x_rows, w_gate_up, b_gate_up, w_down,
                b_down):
    n_rows = x_rows.shape[0]
    n_blocks = n_rows // MOE_BLOCK

    def bias_map(b, be, nr, sl, nx):
        return (be[b], 0, 0)

    def row_map(b, be, nr, sl, nx):
        return (b, 0)

    grid_spec = pltpu.PrefetchScalarGridSpec(
        num_scalar_prefetch=4,
        grid=(n_blocks,),
        in_specs=[pl.BlockSpec((MOE_BLOCK, D_MODEL // 2), row_map),
                  pl.BlockSpec(memory_space=pl.ANY),
                  pl.BlockSpec((1, 1, 2 * D_FF), bias_map),
                  pl.BlockSpec(memory_space=pl.ANY),
                  pl.BlockSpec((1, 1, D_MODEL), bias_map)],
        out_specs=pl.BlockSpec((MOE_BLOCK, D_MODEL // 2), row_map),
        scratch_shapes=[pltpu.VMEM((2, D_MODEL, 2 * D_FF), _F32),
                        pltpu.VMEM((2, D_FF, D_MODEL), _F32),
                        pltpu.VMEM((D_MODEL, 2 * D_FF), _BF16),
                        pltpu.VMEM((D_FF, D_MODEL), _BF16),
                        pltpu.SemaphoreType.DMA((2,)),
                        pltpu.SemaphoreType.DMA((2,))],
    )
    return pl.pallas_call(
        _expert_ffn_kernel,
        out_shape=jax.ShapeDtypeStruct((n_rows, D_MODEL // 2), jnp.uint32),
        grid_spec=grid_spec,
        compiler_params=pltpu.CompilerParams(
            dimension_semantics=("arbitrary",), vmem_limit_bytes=VMEM_LIMIT_BYTES),
        name="expert_ffn",
    )(block_e, block_rows, block_slot, block_next, x_rows, w_gate_up, b_gate_up, w_down, b_down)


def _sc_workers():
    info = plsc.get_sparse_core_info()
    return info.num_cores, info.num_cores * info.num_subcores


def _dispatch(h_packed, dest_flat, n_rows):
    t, width = h_packed.shape
    n_cores, n_workers = _sc_workers()
    n_chunks = t // (n_workers * SC_ROWS)
    chunks_per_k = t // SC_ROWS
    assert n_chunks % 2 == 0

    @functools.partial(
        pl.kernel, mesh=plsc.VectorSubcoreMesh(core_axis_name="c", subcore_axis_name="s"),
        out_type=jax.ShapeDtypeStruct((n_rows, width), h_packed.dtype),
        scratch_types=[pltpu.VMEM((TOP_K, n_chunks, SC_ROWS), jnp.int32),
                       pltpu.VMEM((2, SC_ROWS, width), h_packed.dtype),
                       pltpu.SemaphoreType.DMA((2,)),
                       pltpu.SemaphoreType.DMA((2,))])
    def scatter_rows(h_hbm, dest_hbm, out_hbm, idx_v, buf, lsem, ssem):
        wid = lax.axis_index("s") * n_cores + lax.axis_index("c")
        c0 = wid * n_chunks
        for k in range(TOP_K):
            pltpu.sync_copy(dest_hbm.at[pl.ds(k * chunks_per_k + c0, n_chunks)], idx_v.at[k])

        def load(c, b):
            return pltpu.make_async_copy(h_hbm.at[pl.ds((c0 + c) * SC_ROWS, SC_ROWS)], buf.at[b],
                                         lsem.at[b])

        def scatters(c, b):
            return [pltpu.make_async_copy(buf.at[b], out_hbm.at[idx_v.at[k, c]], ssem.at[b])
                    for k in range(TOP_K)]

        load(0, 0).start()

        @pl.loop(0, n_chunks, step=2)
        def _(ci):
            for b in range(2):
                c = ci + b

                @pl.when(c >= 1)
                def _():
                    for cp in scatters(c - 1, 1 - b):
                        cp.wait()

                @pl.when(c + 1 < n_chunks)
                def _():
                    load(c + 1, 1 - b).start()

                load(c, b).wait()
                for cp in scatters(c, b):
                    cp.start()

        for cp in scatters(n_chunks - 1, 1):
            cp.wait()

    return scatter_rows(h_packed, dest_flat.reshape(TOP_K * chunks_per_k, SC_ROWS))


def _collect(y_rows, dest_flat):
    n_idx = dest_flat.shape[0]
    width = y_rows.shape[1]
    rows, depth = COLLECT_ROWS, COLLECT_RING
    n_cores, n_workers = _sc_workers()
    n_chunks = n_idx // (n_workers * rows)
    assert n_chunks % depth == 0

    @functools.partial(
        pl.kernel, mesh=plsc.VectorSubcoreMesh(core_axis_name="c", subcore_axis_name="s"),
        out_type=jax.ShapeDtypeStruct((n_idx, width), y_rows.dtype),
        scratch_types=[pltpu.VMEM((n_chunks, rows), jnp.int32),
                       pltpu.VMEM((depth, rows, width), y_rows.dtype),
                       pltpu.SemaphoreType.DMA((depth,)),
                       pltpu.SemaphoreType.DMA((depth,))])
    def gather_rows(y_hbm, dest_hbm, out_hbm, idx_v, buf, gsem, wsem):
        wid = lax.axis_index("s") * n_cores + lax.axis_index("c")
        c0 = wid * n_chunks
        pltpu.sync_copy(dest_hbm.at[pl.ds(c0, n_chunks)], idx_v)

        def gather(c, b):
            return pltpu.make_async_copy(y_hbm.at[idx_v.at[c]], buf.at[b], gsem.at[b])

        def write(c, b):
            return pltpu.make_async_copy(buf.at[b], out_hbm.at[pl.ds((c0 + c) * rows, rows)],
                                         wsem.at[b])

        for c in range(depth - 1):
            gather(c, c).start()

        @pl.loop(0, n_chunks, step=depth)
        def _(ci):
            for b in range(depth):
                c = ci + b
                gather(c, b).wait()
                write(c, b).start()
                prev = (b - 1) % depth

                @pl.when(c >= 1)
                def _():
                    write(c - 1, prev).wait()

                @pl.when(c + depth - 1 < n_chunks)
                def _():
                    gather(c + depth - 1, prev).start()

        write(n_chunks - 1, (n_chunks - 1) % depth).wait()

    return gather_rows(y_rows, dest_flat.reshape(n_idx // rows, rows))


def _combine_kernel(x1_ref, ya_ref, tg_ref, g_ref, o_ref):
    acc = x1_ref[...]
    tg = jnp.transpose(tg_ref[...])
    for k in range(TOP_K):
        acc = acc + tg[:, k:k + 1] * _unpack_bf16_halves(ya_ref[k])
    o_ref[...] = _rmsnorm(acc, g_ref[...])


def _combine(x1, y_assign, top_g, g_final):
    t = x1.shape[0]
    tm = TOKEN_TILE
    return pl.pallas_call(
        _combine_kernel,
        out_shape=jax.ShapeDtypeStruct((t, D_MODEL), _F32),
        grid=(t // tm,),
        in_specs=[pl.BlockSpec((tm, D_MODEL), lambda i: (i, 0)),
                  pl.BlockSpec((TOP_K, tm, D_MODEL // 2), lambda i: (0, i, 0)),
                  pl.BlockSpec((2 * TOP_K, tm), lambda i: (0, i)),
                  pl.BlockSpec((1, D_MODEL), lambda i: (0, 0))],
        out_specs=pl.BlockSpec((tm, D_MODEL), lambda i: (i, 0)),
        compiler_params=pltpu.CompilerParams(
            dimension_semantics=("arbitrary",), vmem_limit_bytes=VMEM_LIMIT_BYTES),
        name="combine",
    )(x1, y_assign, top_g, g_final)


def _block_plan(counts, n_blocks):
    padded = ((counts + MOE_BLOCK - 1) // MOE_BLOCK) * MOE_BLOCK
    pad_end = jnp.cumsum(padded)
    pad_start = pad_end - padded
    block_start = (jnp.arange(n_blocks, dtype=jnp.int32) * MOE_BLOCK)[:, None]
    eidx = jnp.arange(N_EXPERTS, dtype=jnp.int32)
    owns = (pad_start[None, :] <= block_start) & (block_start < pad_end[None, :])
    has_blocks = (padded > 0).astype(jnp.int32)
    ordinal = jnp.cumsum(has_blocks) - has_blocks
    later = (eidx[None, :] > eidx[:, None]) & (padded[None, :] > 0)
    next_expert = jnp.min(jnp.where(later, eidx[None, :], N_EXPERTS), axis=1)

    def per_block(per_expert):
        return jnp.sum(jnp.where(owns, per_expert, 0), axis=1).astype(jnp.int32)

    block_e = per_block(eidx[None, :])
    block_rows = per_block(jnp.clip((pad_start + counts)[None, :] - block_start, 0, MOE_BLOCK))
    block_slot = per_block((ordinal % 2)[None, :])
    block_next = per_block(next_expert[None, :])
    return pad_start, (block_e, block_rows, block_slot, block_next)


def kernel(x, norm_mix_g, w_in, conv_w, w_conv_out, ssm_lam_re, ssm_lam_im, ssm_log_dt, ssm_b_re, ssm_b_im, ssm_c_re, ssm_c_im, ssm_d, w_glu, w_out, norm_ffn_g, w_router, b_router, w_gate_up, b_gate_up, w_down, b_down, norm_f_g):
    bsz, seq, d = x.shape
    t = bsz * seq
    x2 = x.reshape(t, d)
    assert seq % TOKEN_TILE == 0 and seq % SSM_TIME_TILE == 0 and w_in.shape[0] == 1

    w_in_b = w_in[0].astype(_BF16)
    n_bcvu = 3 * D_CONV + D_SSM
    g_mix = norm_mix_g[0].reshape(1, d)

    bz, u = _in_proj(x2, g_mix, w_in_b[:, :n_bcvu], conv_w[0], seq)

    tables = _ssm_tables(ssm_lam_re[0], ssm_lam_im[0], ssm_log_dt[0], ssm_b_re[0], ssm_b_im[0],
                         ssm_c_re[0], ssm_c_im[0], ssm_d[0])
    yg = _ssm(u.reshape(bsz, seq, D_SSM), tables).reshape(t, D_SSM)

    x1, h_packed, top_i, top_g, rank, counts = _mix_route(
        x2, bz, yg, g_mix, w_in_b[:, n_bcvu:], w_conv_out[0].astype(_BF16),
        w_glu[0].astype(_BF16), w_out[0].astype(_BF16), norm_ffn_g[0].reshape(1, d),
        jnp.pad(w_router[0], ((0, 0), (0, LANES - N_EXPERTS))).astype(_BF16),
        b_router[0].reshape(N_EXPERTS, 1))

    n_rows = t * TOP_K + N_EXPERTS * MOE_BLOCK
    pad_start, block_plan = _block_plan(counts[:, 0], n_rows // MOE_BLOCK)
    expert_ids = jnp.arange(N_EXPERTS, dtype=jnp.int32)[:, None, None]
    row_start = jnp.sum(jnp.where(top_i[None] == expert_ids, pad_start[:, None, None], 0), axis=0)
    dest = (row_start + rank).reshape(TOP_K * t)

    x_rows = _dispatch(h_packed, dest, n_rows)
    y_rows = _expert_ffn(*block_plan, x_rows, w_gate_up[0],
                         b_gate_up[0].reshape(N_EXPERTS, 1, 2 * D_FF), w_down[0],
                         b_down[0].reshape(N_EXPERTS, 1, D_MODEL))
    y_assign = _collect(y_rows, dest).reshape(TOP_K, t, D_MODEL // 2)
    out = _combine(x1, y_assign, top_g, norm_f_g.reshape(1, d))
    return out.reshape(bsz, seq, d)
```

```python
import functools

import jax
import jax.numpy as jnp
from jax import lax
from jax.experimental import pallas as pl
from jax.experimental.pallas import tpu as pltpu
from jax.experimental.pallas import tpu_sc as plsc

D_MODEL = 1024
D_CONV = 512
CONV_WIDTH = 3
D_SSM = 512
SSM_GROUP = 16
N_SSM_GROUPS = 32
SSM_STATE = 64
N_EXPERTS = 32
TOP_K = 4
D_FF = 1024
SWIGLU_LIMIT = 7.0
SWIGLU_ALPHA = 1.702
RMS_EPS = 1e-6

LANES = 128
SUBLANES = 8
MXU_DIM = 256
CHUNK = 16
SLAB_GROUPS = LANES // SSM_GROUP
N_SLABS = N_SSM_GROUPS // SLAB_GROUPS
SLAB_STATE = SLAB_GROUPS * SSM_STATE
FLAT = CHUNK * LANES
SSM_TIME_TILE = 256
TOKEN_TILE = 1024
IN_TILE = 1024
IN_CHAIN = 512
MIX_TILE = 1024
ROW_CHAIN = 256
MOE_BLOCK = 1024
FFN_ROW_STEP = 256
SC_ROWS = 64
COLLECT_ROWS = 32
COLLECT_RING = 4
VMEM_LIMIT_BYTES = 56 * 1024 * 1024

_BF16 = jnp.bfloat16
_F32 = jnp.float32


def _rmsnorm(xf, g):
    return xf * lax.rsqrt(jnp.mean(xf * xf, axis=-1, keepdims=True) + RMS_EPS) * g


def _sigmoid(v):
    return 0.5 * jnp.tanh(0.5 * v) + 0.5


def _pack_bf16_halves(v):
    n = v.shape[1] // 2
    packed = pltpu.pack_elementwise([v[:, :n], v[:, n:]], packed_dtype=_BF16)
    return pltpu.bitcast(packed, jnp.uint32)


def _unpack_bf16_halves(w):
    return jnp.concatenate(
        [pltpu.unpack_elementwise(w, index=i, packed_dtype=_BF16, unpacked_dtype=_F32) for i in range(2)],
        axis=1)


def _in_proj_kernel(tiles_per_seq, x_ref, g_ref, w_ref, cw_ref, bz_ref, u_ref, hbuf):
    tm = x_ref.shape[0]
    halo = SUBLANES

    @pl.when(pl.program_id(0) % tiles_per_seq == 0)
    def _():
        hbuf[0:halo, :] = jnp.zeros((halo, D_CONV), _F32)

    cw = cw_ref[...]
    for r0 in range(0, tm, IN_CHAIN):
        rows = slice(r0, r0 + IN_CHAIN)
        xn = _rmsnorm(x_ref[rows, :], g_ref[...]).astype(_BF16)
        cv = jnp.dot(xn, w_ref[:, D_CONV:3 * D_CONV], preferred_element_type=_F32)
        hbuf[halo + r0:halo + r0 + IN_CHAIN, :] = cv[:, :D_CONV] * cv[:, D_CONV:]
        u_ref[rows, :] = jnp.dot(xn, w_ref[:, 3 * D_CONV:], preferred_element_type=_F32)
        z = cw[CONV_WIDTH - 1:CONV_WIDTH, :] * hbuf[halo + r0:halo + r0 + IN_CHAIN, :]
        for lag in range(1, CONV_WIDTH):
            z = z + (cw[CONV_WIDTH - 1 - lag:CONV_WIDTH - lag, :]
                     * hbuf[halo + r0 - lag:halo + r0 - lag + IN_CHAIN, :])
        b_gate = jnp.dot(xn, w_ref[:, :D_CONV], preferred_element_type=_F32)
        bz_ref[rows, :] = (b_gate * z).astype(_BF16)
    hbuf[0:halo, :] = hbuf[tm:tm + halo, :]


def _in_proj(x2, g, w_bcvu, conv_w, seq):
    t = x2.shape[0]
    tm = IN_TILE
    assert SUBLANES >= CONV_WIDTH - 1 and seq % tm == 0
    return pl.pallas_call(
        functools.partial(_in_proj_kernel, seq // tm),
        out_shape=(jax.ShapeDtypeStruct((t, D_CONV), _BF16),
                   jax.ShapeDtypeStruct((t, D_SSM), _F32)),
        grid=(t // tm,),
        in_specs=[pl.BlockSpec((tm, D_MODEL), lambda i: (i, 0)),
                  pl.BlockSpec((1, D_MODEL), lambda i: (0, 0)),
                  pl.BlockSpec((D_MODEL, 3 * D_CONV + D_SSM), lambda i: (0, 0)),
                  pl.BlockSpec((CONV_WIDTH, D_CONV), lambda i: (0, 0))],
        out_specs=(pl.BlockSpec((tm, D_CONV), lambda i: (i, 0)),
                   pl.BlockSpec((tm, D_SSM), lambda i: (i, 0))),
        scratch_shapes=[pltpu.VMEM((tm + SUBLANES, D_CONV), _F32)],
        compiler_params=pltpu.CompilerParams(
            dimension_semantics=("arbitrary",), vmem_limit_bytes=VMEM_LIMIT_BYTES),
        name="in_proj",
    )(x2, g, w_bcvu, conv_w)


def _ssm_prep_kernel(lr_ref, lc_ref, bm_ref, cm_ref, d_ref, toep_ref, bst_ref, cst_ref, a_ref):
    def discretise(lre, lim, log_dt):
        dt = jnp.exp(log_dt)
        mag = jnp.exp(lre * dt)
        return mag * jnp.cos(lim * dt), mag * jnp.sin(lim * dt)

    def powers(are, aim):
        pre, pim = [jnp.ones_like(are)], [jnp.zeros_like(are)]
        for _ in range(CHUNK):
            pre, pim = (pre + [pre[-1] * are - pim[-1] * aim],
                        pim + [pre[-1] * aim + pim[-1] * are])
        return pre, pim

    lr = lr_ref[0]
    lre, lim = lr[0:1, :], lr[1:2, :]
    are, aim = discretise(lre, lim, lr[2:3, :])
    pre, pim = powers(are, aim)
    den = lre * lre + lim * lim
    q_re = ((are - 1.0) * lre + aim * lim) / den
    q_im = (aim * lre - (are - 1.0) * lim) / den
    bb_re = q_re * bm_ref[0, 0] - q_im * bm_ref[0, 1]
    bb_im = q_re * bm_ref[0, 1] + q_im * bm_ref[0, 0]
    cm_re, cm_im = cm_ref[0, 0], cm_ref[0, 1]

    def split_bf16(v):
        v_hi = v.astype(_BF16)
        return v_hi, (v - v_hi.astype(_F32)).astype(_BF16)

    c_hi, c_lo = split_bf16(jnp.concatenate([cm_re, -cm_im], axis=0))
    kblk = []
    for k in range(CHUNK):
        ab_re = bb_re * pre[k] - bb_im * pim[k]
        ab_im = bb_re * pim[k] + bb_im * pre[k]
        rows = slice((CHUNK - 1 - k) * LANES, (CHUNK - k) * LANES)
        bst_ref[0, rows, :SLAB_STATE] = ab_re.astype(_BF16)
        bst_ref[0, rows, SLAB_STATE:] = ab_im.astype(_BF16)
        ab_hi, ab_lo = split_bf16(jnp.concatenate([ab_re, ab_im], axis=1))
        kblk.append(jnp.dot(ab_hi, c_hi, preferred_element_type=_F32)
                    + (jnp.dot(ab_lo, c_hi, preferred_element_type=_F32)
                       + jnp.dot(ab_hi, c_lo, preferred_element_type=_F32)))
    r = lax.broadcasted_iota(jnp.int32, (LANES, LANES), 0)
    c = lax.broadcasted_iota(jnp.int32, (LANES, LANES), 1)
    kblk[0] = kblk[0] + jnp.where(r == c, jnp.broadcast_to(d_ref[0], (LANES, LANES)), 0.0)
    kblk = [kb.astype(_BF16) for kb in kblk]
    zeros = jnp.zeros((LANES, LANES), _BF16)
    for sp in range(CHUNK):
        for s in range(CHUNK):
            toep_ref[0, sp * LANES:(sp + 1) * LANES, s * LANES:(s + 1) * LANES] = (
                kblk[s - sp] if s >= sp else zeros)

    lc = lc_ref[0]
    cre, cim = discretise(lc[:, 0:1], lc[:, 1:2], lc[:, 2:3])
    qre, qim = powers(cre, cim)
    for s in range(CHUNK):
        cols = slice(s * LANES, (s + 1) * LANES)
        cst_ref[0, :SLAB_STATE, cols] = (cm_re * qre[s + 1] - cm_im * qim[s + 1]).astype(_BF16)
        cst_ref[0, SLAB_STATE:, cols] = (-(cm_re * qim[s + 1] + cm_im * qre[s + 1])).astype(_BF16)
    a_ref[0, 0:1, :] = pre[CHUNK]
    a_ref[0, 1:2, :] = pim[CHUNK]


def _ssm_tables(lam_re, lam_im, log_dt, b_re, b_im, c_re, c_im, d_skip):
    sg = (N_SLABS, SLAB_GROUPS)
    eye = jnp.eye(SLAB_GROUPS, dtype=_F32)
    lam = jnp.stack([lam_re, lam_im, jnp.broadcast_to(log_dt[:, None], lam_re.shape)], axis=0)
    lam_row = lam.reshape(3, N_SLABS, SLAB_STATE).transpose(1, 0, 2)
    lam_col = lam_row.transpose(0, 2, 1)

    def b_blockdiag(b):
        bt = b.reshape(*sg, SSM_STATE, SSM_GROUP).transpose(0, 1, 3, 2)
        return (bt[:, :, :, None, :] * eye[None, :, None, :, None]).reshape(N_SLABS, LANES, SLAB_STATE)

    def c_blockdiag(c):
        ct = c.reshape(*sg, SSM_GROUP, SSM_STATE).transpose(0, 1, 3, 2)
        return (ct[:, :, :, None, :] * eye[None, :, None, :, None]).reshape(N_SLABS, SLAB_STATE, LANES)

    bm = jnp.stack([b_blockdiag(b_re), b_blockdiag(b_im)], axis=1)
    cm = jnp.stack([c_blockdiag(c_re), c_blockdiag(c_im)], axis=1)
    d = d_skip.reshape(N_SLABS, 1, LANES)
    slab3 = lambda sl: (sl, 0, 0)
    slab4 = lambda sl: (sl, 0, 0, 0)
    return pl.pallas_call(
        _ssm_prep_kernel,
        out_shape=(jax.ShapeDtypeStruct((N_SLABS, FLAT, FLAT), _BF16),
                   jax.ShapeDtypeStruct((N_SLABS, FLAT, 2 * SLAB_STATE), _BF16),
                   jax.ShapeDtypeStruct((N_SLABS, 2 * SLAB_STATE, FLAT), _BF16),
                   jax.ShapeDtypeStruct((N_SLABS, 2, SLAB_STATE), _F32)),
        grid=(N_SLABS,),
        in_specs=[pl.BlockSpec((1, 3, SLAB_STATE), slab3),
                  pl.BlockSpec((1, SLAB_STATE, 3), slab3),
                  pl.BlockSpec((1, 2, LANES, SLAB_STATE), slab4),
                  pl.BlockSpec((1, 2, SLAB_STATE, LANES), slab4),
                  pl.BlockSpec((1, 1, LANES), slab3)],
        out_specs=(pl.BlockSpec((1, FLAT, FLAT), slab3),
                   pl.BlockSpec((1, FLAT, 2 * SLAB_STATE), slab3),
                   pl.BlockSpec((1, 2 * SLAB_STATE, FLAT), slab3),
                   pl.BlockSpec((1, 2, SLAB_STATE), slab3)),
        compiler_params=pltpu.CompilerParams(
            dimension_semantics=("arbitrary",), vmem_limit_bytes=VMEM_LIMIT_BYTES),
        name="ssm_prep",
    )(lam_row, lam_col, bm, cm, d)


def _ssm_kernel(u_ref, toep_ref, bst_ref, cst_ref, a_ref, y_ref, uflat, s_scr, xc_scr, carry, ytoep):
    nb, tt, _ = u_ref.shape
    nch = tt // CHUNK
    n = nb * nch

    @pl.when(pl.program_id(1) == 0)
    def _():
        carry[...] = jnp.zeros_like(carry)

    for s in range(CHUNK):
        part = u_ref[:, pl.ds(s, nch, stride=CHUNK), :]
        uflat[:, s * LANES:(s + 1) * LANES] = part.reshape(n, LANES).astype(_BF16)

    n_cb = FLAT // MXU_DIM

    def toeplitz(cb):
        kk = (cb + 1) * MXU_DIM
        cols = slice(cb * MXU_DIM, kk)
        ytoep[:, cols] = jnp.dot(uflat[:, :kk], toep_ref[0, :kk, cols], preferred_element_type=_F32)

    for cb in range(n_cb // 2):
        toeplitz(cb)

    nblk = SLAB_STATE // LANES
    loc_all = jnp.dot(uflat[...], bst_ref[0], preferred_element_type=_F32)
    for cb in range(n_cb // 2, n_cb):
        toeplitz(cb)
    for kb in range(2 * nblk):
        s_scr[kb] = loc_all[:, kb * LANES:(kb + 1) * LANES]

    a = a_ref[0]
    are = [jnp.broadcast_to(a[0:1, kb * LANES:(kb + 1) * LANES], (nb, LANES)) for kb in range(nblk)]
    aim = [jnp.broadcast_to(a[1:2, kb * LANES:(kb + 1) * LANES], (nb, LANES)) for kb in range(nblk)]
    xr = [carry[kb] for kb in range(nblk)]
    xi = [carry[nblk + kb] for kb in range(nblk)]
    for j in range(nch):
        rows = pl.ds(j, nb, stride=nch)
        for kb in range(nblk):
            xc_scr[kb, rows, :] = xr[kb]
            xc_scr[nblk + kb, rows, :] = xi[kb]
            nr = are[kb] * xr[kb] - aim[kb] * xi[kb] + s_scr[kb, rows, :]
            ni = are[kb] * xi[kb] + aim[kb] * xr[kb] + s_scr[nblk + kb, rows, :]
            xr[kb], xi[kb] = nr, ni
    for kb in range(nblk):
        carry[kb] = xr[kb]
        carry[nblk + kb] = xi[kb]

    xc = jnp.concatenate([xc_scr[kb] for kb in range(2 * nblk)], axis=1).astype(_BF16)
    for cb in range(n_cb):
        cols = slice(cb * MXU_DIM, (cb + 1) * MXU_DIM)
        y = ytoep[:, cols] + jnp.dot(xc, cst_ref[0, :, cols], preferred_element_type=_F32)
        y = jax.nn.gelu(y)
        for h in range(MXU_DIM // LANES):
            s = cb * (MXU_DIM // LANES) + h
            y_ref[:, pl.ds(s, nch, stride=CHUNK), :] = (
                y[:, h * LANES:(h + 1) * LANES].reshape(nb, nch, LANES))


def _ssm(u3, tables):
    toep, bst, cst, a_chunk = tables
    nb, seq, _ = u3.shape
    tt = SSM_TIME_TILE
    n = nb * (tt // CHUNK)
    return pl.pallas_call(
        _ssm_kernel,
        out_shape=jax.ShapeDtypeStruct(u3.shape, _F32),
        grid=(N_SLABS, seq // tt),
        in_specs=[pl.BlockSpec((nb, tt, LANES), lambda sl, ti: (0, ti, sl)),
                  pl.BlockSpec((1, FLAT, FLAT), lambda sl, ti: (sl, 0, 0)),
                  pl.BlockSpec((1, FLAT, 2 * SLAB_STATE), lambda sl, ti: (sl, 0, 0)),
                  pl.BlockSpec((1, 2 * SLAB_STATE, FLAT), lambda sl, ti: (sl, 0, 0)),
                  pl.BlockSpec((1, 2, SLAB_STATE), lambda sl, ti: (sl, 0, 0))],
        out_specs=pl.BlockSpec((nb, tt, LANES), lambda sl, ti: (0, ti, sl)),
        scratch_shapes=[pltpu.VMEM((n, FLAT), _BF16),
                        pltpu.VMEM((2 * SLAB_STATE // LANES, n, LANES), _F32),
                        pltpu.VMEM((2 * SLAB_STATE // LANES, n, LANES), _F32),
                        pltpu.VMEM((2 * SLAB_STATE // LANES, nb, LANES), _F32),
                        pltpu.VMEM((n, FLAT), _F32)],
        compiler_params=pltpu.CompilerParams(
            dimension_semantics=("arbitrary", "arbitrary"), vmem_limit_bytes=VMEM_LIMIT_BYTES),
        name="ssm",
    )(u3, toep, bst, cst, a_chunk)


def _mix_route_kernel(x_ref, bz_ref, yg_ref, gm_ref, wg_ref, wco_ref, wglu_ref, wout_ref,
                      gf_ref, wr_ref, br_ref,
                      x1_ref, h_ref, ti_ref, tg_ref, rk_ref, cnt_ref, base, merged):
    tm = x_ref.shape[0]

    @pl.when(pl.program_id(0) == 0)
    def _():
        base[...] = jnp.zeros_like(base)

    chains = [slice(r0, r0 + ROW_CHAIN) for r0 in range(0, tm, ROW_CHAIN)]
    hs, picks = [], []

    def route(j):
        logits_tok = jnp.dot(hs[j], wr_ref[...], preferred_element_type=_F32)
        picks.append(_top_k_rows(chains[j], logits_tok, br_ref, ti_ref, tg_ref))

    def rank(j):
        _rank_rows(chains[j], *picks[j], rk_ref, base)

    for j, rows in enumerate(chains):
        hs.append(_mix_rows(rows, x_ref, bz_ref, yg_ref, gm_ref, wg_ref, wco_ref, wglu_ref,
                            wout_ref, gf_ref, x1_ref, h_ref, merged))
        if j >= 1:
            route(j - 1)
        if j >= 2:
            rank(j - 2)
    last = len(chains) - 1
    route(last)
    for j in range(max(last - 1, 0), last + 1):
        rank(j)
    cnt_ref[...] = base[...].astype(jnp.int32)


def _mix_rows(rows, x_ref, bz_ref, yg_ref, gm_ref, wg_ref, wco_ref, wglu_ref, wout_ref,
              gf_ref, x1_ref, h_ref, merged):
    x = x_ref[rows, :]
    xn = _rmsnorm(x, gm_ref[...]).astype(_BF16)
    bz = bz_ref[rows, :]
    yg = yg_ref[rows, :].astype(_BF16)
    for c in range(D_MODEL // MXU_DIM):
        lo = slice(c * MXU_DIM, (c + 1) * MXU_DIM)
        hi = slice(D_MODEL + c * MXU_DIM, D_MODEL + (c + 1) * MXU_DIM)
        gate_a = jnp.dot(xn, wg_ref[:, lo], preferred_element_type=_F32)
        gate_b = jnp.dot(xn, wg_ref[:, hi], preferred_element_type=_F32)
        y_a = jnp.dot(bz, wco_ref[:, lo], preferred_element_type=_F32)
        val = jnp.dot(yg, wglu_ref[:, lo], preferred_element_type=_F32)
        glu_gate = jnp.dot(yg, wglu_ref[:, hi], preferred_element_type=_F32)
        y_b = val * _sigmoid(glu_gate)
        merged[rows, lo] = (_sigmoid(gate_a) * y_a + _sigmoid(gate_b) * y_b).astype(_BF16)
    x1 = x + jnp.dot(merged[rows, :], wout_ref[...], preferred_element_type=_F32)
    x1_ref[rows, :] = x1
    h = _rmsnorm(x1, gf_ref[...])
    h_ref[rows, :] = _pack_bf16_halves(h)
    return h.astype(_BF16)


def _top_k_rows(rows, logits_tok, br_ref, ti_ref, tg_ref):
    tm = rows.stop - rows.start
    logits = jnp.transpose(logits_tok)[:N_EXPERTS, :] + br_ref[...]
    erow = lax.broadcasted_iota(jnp.int32, (N_EXPERTS, tm), 0).astype(_F32)
    neg_inf = jnp.float32(-jnp.inf)
    work = logits
    vals, idxs = [], []
    for _ in range(TOP_K):
        m = jnp.max(work, axis=0, keepdims=True)
        idx = jnp.min(jnp.where(work == m, erow, float(N_EXPERTS)), axis=0, keepdims=True)
        vals.append(m)
        idxs.append(idx)
        work = jnp.where(erow == idx, neg_inf, work)
    exps = [jnp.exp(v - vals[0]) for v in vals]
    denom = exps[0] + exps[1] + exps[2] + exps[3]
    sel = jnp.zeros((N_EXPERTS, tm), _F32)
    for k in range(TOP_K):
        ti_ref[k:k + 1, rows] = idxs[k].astype(jnp.int32)
        tg_ref[k:k + 1, rows] = exps[k] / denom
        tg_ref[TOP_K + k:TOP_K + k + 1, rows] = jnp.zeros((1, tm), _F32)
        sel = sel + (erow == idxs[k]).astype(_F32)
    return sel, idxs


def _rank_rows(rows, sel, idxs, rk_ref, base):
    tm = rows.stop - rows.start
    erow = lax.broadcasted_iota(jnp.int32, (N_EXPERTS, tm), 0).astype(_F32)
    row = lax.broadcasted_iota(jnp.int32, (tm, tm), 0)
    col = lax.broadcasted_iota(jnp.int32, (tm, tm), 1)
    earlier = (row < col).astype(_BF16)
    before = jnp.dot(sel.astype(_BF16), earlier, preferred_element_type=_F32) + base[...]
    for k in range(TOP_K):
        rk = jnp.sum(jnp.where(erow == idxs[k], before, 0.0), axis=0, keepdims=True)
        rk_ref[k:k + 1, rows] = rk.astype(jnp.int32)
    base[...] = base[...] + jnp.sum(sel, axis=1, keepdims=True)


def _mix_route(x2, bz, yg, g_mix, w_gates, w_conv_out, w_glu, w_out, g_ffn, w_router, b_router):
    t = x2.shape[0]
    tm = MIX_TILE
    tok = lambda i: (i, 0)
    tok_lanes = lambda i: (0, i)
    fixed = lambda i: (0, 0)
    weight = lambda shape: pl.BlockSpec(shape, fixed, pipeline_mode=pl.Buffered(1))
    return pl.pallas_call(
        _mix_route_kernel,
        out_shape=(jax.ShapeDtypeStruct((t, D_MODEL), _F32),
                   jax.ShapeDtypeStruct((t, D_MODEL // 2), jnp.uint32),
                   jax.ShapeDtypeStruct((TOP_K, t), jnp.int32),
                   jax.ShapeDtypeStruct((2 * TOP_K, t), _F32),
                   jax.ShapeDtypeStruct((TOP_K, t), jnp.int32),
                   jax.ShapeDtypeStruct((N_EXPERTS, 1), jnp.int32)),
        grid=(t // tm,),
        in_specs=[pl.BlockSpec((tm, D_MODEL), tok),
                  pl.BlockSpec((tm, D_CONV), tok),
                  pl.BlockSpec((tm, D_SSM), tok),
                  pl.BlockSpec((1, D_MODEL), fixed),
                  weight((D_MODEL, 2 * D_MODEL)),
                  weight((D_CONV, D_MODEL)),
                  weight((D_SSM, 2 * D_MODEL)),
                  weight((D_MODEL, D_MODEL)),
                  pl.BlockSpec((1, D_MODEL), fixed),
                  weight((D_MODEL, LANES)),
                  pl.BlockSpec((N_EXPERTS, 1), fixed)],
        out_specs=(pl.BlockSpec((tm, D_MODEL), tok),
                   pl.BlockSpec((tm, D_MODEL // 2), tok),
                   pl.BlockSpec((TOP_K, tm), tok_lanes),
                   pl.BlockSpec((2 * TOP_K, tm), tok_lanes),
                   pl.BlockSpec((TOP_K, tm), tok_lanes),
                   pl.BlockSpec((N_EXPERTS, 1), fixed)),
        scratch_shapes=[pltpu.VMEM((N_EXPERTS, 1), _F32),
                        pltpu.VMEM((tm, D_MODEL), _BF16)],
        compiler_params=pltpu.CompilerParams(
            dimension_semantics=("arbitrary",), vmem_limit_bytes=VMEM_LIMIT_BYTES),
        name="mix_route",
    )(x2, bz, yg, g_mix, w_gates, w_conv_out, w_glu, w_out, g_ffn, w_router, b_router)


def _expert_ffn_kernel(be_ref, nr_ref, slot_ref, next_ref, x_ref, wgu_hbm, bgu_ref, wd_hbm, bd_ref,
                       y_ref, stage_gu, stage_d, wgu_b, wd_b, sem_gu, sem_d):
    b = pl.program_id(0)
    expert = be_ref[b]
    live = nr_ref[b] > 0

    def weight_copies(e, slot):
        return (pltpu.make_async_copy(wgu_hbm.at[e], stage_gu.at[slot], sem_gu.at[slot]),
                pltpu.make_async_copy(wd_hbm.at[e], stage_d.at[slot], sem_d.at[slot]))

    @pl.when(live & ((b == 0) | (be_ref[jnp.maximum(b - 1, 0)] != expert)))
    def _():
        slot = slot_ref[b]

        @pl.when(b == 0)
        def _():
            for cp in weight_copies(expert, slot):
                cp.start()

        for cp in weight_copies(expert, slot):
            cp.wait()
        wgu_b[...] = stage_gu[slot].astype(_BF16)
        wd_b[...] = stage_d[slot].astype(_BF16)

        @pl.when(next_ref[b] < N_EXPERTS)
        def _():
            for cp in weight_copies(next_ref[b], 1 - slot):
                cp.start()

    def ffn_rows(n_rows):
        xw = x_ref[:n_rows, :]
        valid = lax.broadcasted_iota(jnp.int32, xw.shape, 0) < nr_ref[b]
        x = _unpack_bf16_halves(jnp.where(valid, xw, jnp.uint32(0))).astype(_BF16)
        hgu = jnp.dot(x, wgu_b[...], preferred_element_type=_F32) + bgu_ref[0]
        g = jnp.minimum(hgu[:, :D_FF], SWIGLU_LIMIT)
        up = jnp.clip(hgu[:, D_FF:], -SWIGLU_LIMIT, SWIGLU_LIMIT)
        act = (up + 1.0) * (g * _sigmoid(SWIGLU_ALPHA * g))
        y = jnp.dot(act.astype(_BF16), wd_b[...], preferred_element_type=_F32) + bd_ref[0]
        y_ref[:n_rows, :] = _pack_bf16_halves(y)
        if n_rows < MOE_BLOCK:
            y_ref[n_rows:, :] = jnp.zeros((MOE_BLOCK - n_rows, y_ref.shape[1]), y_ref.dtype)

    for height in range(FFN_ROW_STEP, MOE_BLOCK + 1, FFN_ROW_STEP):
        @pl.when((nr_ref[b] > height - FFN_ROW_STEP) & (nr_ref[b] <= height))
        def _():
            ffn_rows(height)

    @pl.when(jnp.logical_not(live))
    def _():
        y_ref[...] = jnp.zeros_like(y_ref)


def _expert_ffn(block_e, block_rows, block_slot, block_next, x_rows, w_gate_up, b_gate_up, w_down,
                b_down):
    n_rows = x_rows.shape[0]
    n_blocks = n_rows // MOE_BLOCK

    def bias_map(b, be, nr, sl, nx):
        return (be[b], 0, 0)

    def row_map(b, be, nr, sl, nx):
        return (b, 0)

    grid_spec = pltpu.PrefetchScalarGridSpec(
        num_scalar_prefetch=4,
        grid=(n_blocks,),
        in_specs=[pl.BlockSpec((MOE_BLOCK, D_MODEL // 2), row_map),
                  pl.BlockSpec(memory_space=pl.ANY),
                  pl.BlockSpec((1, 1, 2 * D_FF), bias_map),
                  pl.BlockSpec(memory_space=pl.ANY),
                  pl.BlockSpec((1, 1, D_MODEL), bias_map)],
        out_specs=pl.BlockSpec((MOE_BLOCK, D_MODEL // 2), row_map),
        scratch_shapes=[pltpu.VMEM((2, D_MODEL, 2 * D_FF), _F32),
                        pltpu.VMEM((2, D_FF, D_MODEL), _F32),
                        pltpu.VMEM((D_MODEL, 2 * D_FF), _BF16),
                        pltpu.VMEM((D_FF, D_MODEL), _BF16),
                        pltpu.SemaphoreType.DMA((2,)),
                        pltpu.SemaphoreType.DMA((2,))],
    )
    return pl.pallas_call(
        _expert_ffn_kernel,
        out_shape=jax.ShapeDtypeStruct((n_rows, D_MODEL // 2), jnp.uint32),
        grid_spec=grid_spec,
        compiler_params=pltpu.CompilerParams(
            dimension_semantics=("arbitrary",), vmem_limit_bytes=VMEM_LIMIT_BYTES),
        name="expert_ffn",
    )(block_e, block_rows, block_slot, block_next, x_rows, w_gate_up, b_gate_up, w_down, b_down)


def _sc_workers():
    info = plsc.get_sparse_core_info()
    return info.num_cores, info.num_cores * info.num_subcores


def _dispatch(h_packed, dest_flat, n_rows):
    t, width = h_packed.shape
    n_cores, n_workers = _sc_workers()
    n_chunks = t // (n_workers * SC_ROWS)
    chunks_per_k = t // SC_ROWS
    assert n_chunks % 2 == 0

    @functools.partial(
        pl.kernel, mesh=plsc.VectorSubcoreMesh(core_axis_name="c", subcore_axis_name="s"),
        out_type=jax.ShapeDtypeStruct((n_rows, width), h_packed.dtype),
        scratch_types=[pltpu.VMEM((TOP_K, n_chunks, SC_ROWS), jnp.int32),
                       pltpu.VMEM((2, SC_ROWS, width), h_packed.dtype),
                       pltpu.SemaphoreType.DMA((2,)),
                       pltpu.SemaphoreType.DMA((2,))])
    def scatter_rows(h_hbm, dest_hbm, out_hbm, idx_v, buf, lsem, ssem):
        wid = lax.axis_index("s") * n_cores + lax.axis_index("c")
        c0 = wid * n_chunks
        for k in range(TOP_K):
            pltpu.sync_copy(dest_hbm.at[pl.ds(k * chunks_per_k + c0, n_chunks)], idx_v.at[k])

        def load(c, b):
            return pltpu.make_async_copy(h_hbm.at[pl.ds((c0 + c) * SC_ROWS, SC_ROWS)], buf.at[b],
                                         lsem.at[b])

        def scatters(c, b):
            return [pltpu.make_async_copy(buf.at[b], out_hbm.at[idx_v.at[k, c]], ssem.at[b])
                    for k in range(TOP_K)]

        load(0, 0).start()

        @pl.loop(0, n_chunks, step=2)
        def _(ci):
            for b in range(2):
                c = ci + b

                @pl.when(c >= 1)
                def _():
                    for cp in scatters(c - 1, 1 - b):
                        cp.wait()

                @pl.when(c + 1 < n_chunks)
                def _():
                    load(c + 1, 1 - b).start()

                load(c, b).wait()
                for cp in scatters(c, b):
                    cp.start()

        for cp in scatters(n_chunks - 1, 1):
            cp.wait()

    return scatter_rows(h_packed, dest_flat.reshape(TOP_K * chunks_per_k, SC_ROWS))


def _collect(y_rows, dest_flat):
    n_idx = dest_flat.shape[0]
    width = y_rows.shape[1]
    rows, depth = COLLECT_ROWS, COLLECT_RING
    n_cores, n_workers = _sc_workers()
    n_chunks = n_idx // (n_workers * rows)
    assert n_chunks % depth == 0

    @functools.partial(
        pl.kernel, mesh=plsc.VectorSubcoreMesh(core_axis_name="c", subcore_axis_name="s"),
        out_type=jax.ShapeDtypeStruct((n_idx, width), y_rows.dtype),
        scratch_types=[pltpu.VMEM((n_chunks, rows), jnp.int32),
                       pltpu.VMEM((depth, rows, width), y_rows.dtype),
                       pltpu.SemaphoreType.DMA((depth,)),
                       pltpu.SemaphoreType.DMA((depth,))])
    def gather_rows(y_hbm, dest_hbm, out_hbm, idx_v, buf, gsem, wsem):
        wid = lax.axis_index("s") * n_cores + lax.axis_index("c")
        c0 = wid * n_chunks
        pltpu.sync_copy(dest_hbm.at[pl.ds(c0, n_chunks)], idx_v)

        def gather(c, b):
            return pltpu.make_async_copy(y_hbm.at[idx_v.at[c]], buf.at[b], gsem.at[b])

        def write(c, b):
            return pltpu.make_async_copy(buf.at[b], out_hbm.at[pl.ds((c0 + c) * rows, rows)],
                                         wsem.at[b])

        for c in range(depth - 1):
            gather(c, c).start()

        @pl.loop(0, n_chunks, step=depth)
        def _(ci):
            for b in range(depth):
                c = ci + b
                gather(c, b).wait()
                write(c, b).start()
                prev = (b - 1) % depth

                @pl.when(c >= 1)
                def _():
                    write(c - 1, prev).wait()

                @pl.when(c + depth - 1 < n_chunks)
                def _():
                    gather(c + depth - 1, prev).start()

        write(n_chunks - 1, (n_chunks - 1) % depth).wait()

    return gather_rows(y_rows, dest_flat.reshape(n_idx // rows, rows))


def _combine_kernel(x1_ref, ya_ref, tg_ref, g_ref, o_ref):
    acc = x1_ref[...]
    tg = jnp.transpose(tg_ref[...])
    for k in range(TOP_K):
        acc = acc + tg[:, k:k + 1] * _unpack_bf16_halves(ya_ref[k])
    o_ref[...] = _rmsnorm(acc, g_ref[...])


def _combine(x1, y_assign, top_g, g_final):
    t = x1.shape[0]
    tm = TOKEN_TILE
    return pl.pallas_call(
        _combine_kernel,
        out_shape=jax.ShapeDtypeStruct((t, D_MODEL), _F32),
        grid=(t // tm,),
        in_specs=[pl.BlockSpec((tm, D_MODEL), lambda i: (i, 0)),
                  pl.BlockSpec((TOP_K, tm, D_MODEL // 2), lambda i: (0, i, 0)),
                  pl.BlockSpec((2 * TOP_K, tm), lambda i: (0, i)),
                  pl.BlockSpec((1, D_MODEL), lambda i: (0, 0))],
        out_specs=pl.BlockSpec((tm, D_MODEL), lambda i: (i, 0)),
        compiler_params=pltpu.CompilerParams(
            dimension_semantics=("arbitrary",), vmem_limit_bytes=VMEM_LIMIT_BYTES),
        name="combine",
    )(x1, y_assign, top_g, g_final)


def _block_plan(counts, n_blocks):
    padded = ((counts + MOE_BLOCK - 1) // MOE_BLOCK) * MOE_BLOCK
    pad_end = jnp.cumsum(padded)
    pad_start = pad_end - padded
    block_start = (jnp.arange(n_blocks, dtype=jnp.int32) * MOE_BLOCK)[:, None]
    eidx = jnp.arange(N_EXPERTS, dtype=jnp.int32)
    owns = (pad_start[None, :] <= block_start) & (block_start < pad_end[None, :])
    has_blocks = (padded > 0).astype(jnp.int32)
    ordinal = jnp.cumsum(has_blocks) - has_blocks
    later = (eidx[None, :] > eidx[:, None]) & (padded[None, :] > 0)
    next_expert = jnp.min(jnp.where(later, eidx[None, :], N_EXPERTS), axis=1)

    def per_block(per_expert):
        return jnp.sum(jnp.where(owns, per_expert, 0), axis=1).astype(jnp.int32)

    block_e = per_block(eidx[None, :])
    block_rows = per_block(jnp.clip((pad_start + counts)[None, :] - block_start, 0, MOE_BLOCK))
    block_slot = per_block((ordinal % 2)[None, :])
    block_next = per_block(next_expert[None, :])
    return pad_start, (block_e, block_rows, block_slot, block_next)


def kernel(x, norm_mix_g, w_in, conv_w, w_conv_out, ssm_lam_re, ssm_lam_im, ssm_log_dt, ssm_b_re, ssm_b_im, ssm_c_re, ssm_c_im, ssm_d, w_glu, w_out, norm_ffn_g, w_router, b_router, w_gate_up, b_gate_up, w_down, b_down, norm_f_g):
    bsz, seq, d = x.shape
    t = bsz * seq
    x2 = x.reshape(t, d)
    assert seq % TOKEN_TILE == 0 and seq % SSM_TIME_TILE == 0 and w_in.shape[0] == 1

    w_in_b = w_in[0].astype(_BF16)
    n_bcvu = 3 * D_CONV + D_SSM
    g_mix = norm_mix_g[0].reshape(1, d)

    bz, u = _in_proj(x2, g_mix, w_in_b[:, :n_bcvu], conv_w[0], seq)

    tables = _ssm_tables(ssm_lam_re[0], ssm_lam_im[0], ssm_log_dt[0], ssm_b_re[0], ssm_b_im[0],
                         ssm_c_re[0], ssm_c_im[0], ssm_d[0])
    yg = _ssm(u.reshape(bsz, seq, D_SSM), tables).reshape(t, D_SSM)

    x1, h_packed, top_i, top_g, rank, counts = _mix_route(
        x2, bz, yg, g_mix, w_in_b[:, n_bcvu:], w_conv_out[0].astype(_BF16),
        w_glu[0].astype(_BF16), w_out[0].astype(_BF16), norm_ffn_g[0].reshape(1, d),
        jnp.pad(w_router[0], ((0, 0), (0, LANES - N_EXPERTS))).astype(_BF16),
        b_router[0].reshape(N_EXPERTS, 1))

    n_rows = t * TOP_K + N_EXPERTS * MOE_BLOCK
    pad_start, block_plan = _block_plan(counts[:, 0], n_rows // MOE_BLOCK)
    expert_ids = jnp.arange(N_EXPERTS, dtype=jnp.int32)[:, None, None]
    row_start = jnp.sum(jnp.where(top_i[None] == expert_ids, pad_start[:, None, None], 0), axis=0)
    dest = (row_start + rank).reshape(TOP_K * t)

    x_rows = _dispatch(h_packed, dest, n_rows)
    y_rows = _expert_ffn(*block_plan, x_rows, w_gate_up[0],
                         b_gate_up[0].reshape(N_EXPERTS, 1, 2 * D_FF), w_down[0],
                         b_down[0].reshape(N_EXPERTS, 1, D_MODEL))
    y_assign = _collect(y_rows, dest).reshape(TOP_K, t, D_MODEL // 2)
    out = _combine(x1, y_assign, top_g, norm_f_g.reshape(1, d))
    return out.reshape(bsz, seq, d)
```

```python
import functools

import jax
import jax.numpy as jnp
from jax import lax
from jax.experimental import pallas as pl
from jax.experimental.pallas import tpu as pltpu
from jax.experimental.pallas import tpu_sc as plsc

D_MODEL = 1024
D_CONV = 512
CONV_WIDTH = 3
D_SSM = 512
SSM_GROUP = 16
N_SSM_GROUPS = 32
SSM_STATE = 64
N_EXPERTS = 32
TOP_K = 4
D_FF = 1024
SWIGLU_LIMIT = 7.0
SWIGLU_ALPHA = 1.702
RMS_EPS = 1e-6

LANES = 128
SUBLANES = 8
MXU_DIM = 256
CHUNK = 8
SLAB_GROUPS = LANES // SSM_GROUP
N_SLABS = N_SSM_GROUPS // SLAB_GROUPS
SLAB_STATE = SLAB_GROUPS * SSM_STATE
FLAT = CHUNK * LANES
SSM_TIME_TILE = 128
TOKEN_TILE = 1024
IN_TILE = 1024
IN_CHAIN = 512
MIX_TILE = 1024
ROW_CHAIN = 256
MOE_BLOCK = 1024
FFN_ROW_STEP = 256
SC_ROWS = 64
COLLECT_ROWS = 32
COLLECT_RING = 4
VMEM_LIMIT_BYTES = 56 * 1024 * 1024

_BF16 = jnp.bfloat16
_F32 = jnp.float32


def _rmsnorm(xf, g):
    return xf * lax.rsqrt(jnp.mean(xf * xf, axis=-1, keepdims=True) + RMS_EPS) * g


def _sigmoid(v):
    return 0.5 * jnp.tanh(0.5 * v) + 0.5


def _pack_bf16_halves(v):
    n = v.shape[1] // 2
    bits = pltpu.bitcast(v.astype(_BF16).astype(_F32), jnp.uint32)
    return (bits[:, :n] >> 16) | (bits[:, n:] & jnp.uint32(0xFFFF0000))


def _unpack_bf16_halves(w):
    return jnp.concatenate([pltpu.bitcast(w << 16, _F32),
                            pltpu.bitcast(w & jnp.uint32(0xFFFF0000), _F32)], axis=1)


def _in_proj_kernel(tiles_per_seq, x_ref, g_ref, w_ref, cw_ref, bz_ref, u_ref, hbuf):
    tm = x_ref.shape[0]
    halo = SUBLANES

    @pl.when(pl.program_id(0) % tiles_per_seq == 0)
    def _():
        hbuf[0:halo, :] = jnp.zeros((halo, D_CONV), _F32)

    cw = cw_ref[...]
    for r0 in range(0, tm, IN_CHAIN):
        rows = slice(r0, r0 + IN_CHAIN)
        xn = _rmsnorm(x_ref[rows, :], g_ref[...]).astype(_BF16)
        cv = jnp.dot(xn, w_ref[:, D_CONV:3 * D_CONV], preferred_element_type=_F32)
        hbuf[halo + r0:halo + r0 + IN_CHAIN, :] = cv[:, :D_CONV] * cv[:, D_CONV:]
        u_ref[rows, :] = jnp.dot(xn, w_ref[:, 3 * D_CONV:], preferred_element_type=_F32)
        z = cw[CONV_WIDTH - 1:CONV_WIDTH, :] * hbuf[halo + r0:halo + r0 + IN_CHAIN, :]
        for lag in range(1, CONV_WIDTH):
            z = z + (cw[CONV_WIDTH - 1 - lag:CONV_WIDTH - lag, :]
                     * hbuf[halo + r0 - lag:halo + r0 - lag + IN_CHAIN, :])
        b_gate = jnp.dot(xn, w_ref[:, :D_CONV], preferred_element_type=_F32)
        bz_ref[rows, :] = (b_gate * z).astype(_BF16)
    hbuf[0:halo, :] = hbuf[tm:tm + halo, :]


def _in_proj(x2, g, w_bcvu, conv_w, seq):
    t = x2.shape[0]
    tm = IN_TILE
    assert SUBLANES >= CONV_WIDTH - 1 and seq % tm == 0
    return pl.pallas_call(
        functools.partial(_in_proj_kernel, seq // tm),
        out_shape=(jax.ShapeDtypeStruct((t, D_CONV), _BF16),
                   jax.ShapeDtypeStruct((t, D_SSM), _F32)),
        grid=(t // tm,),
        in_specs=[pl.BlockSpec((tm, D_MODEL), lambda i: (i, 0)),
                  pl.BlockSpec((1, D_MODEL), lambda i: (0, 0)),
                  pl.BlockSpec((D_MODEL, 3 * D_CONV + D_SSM), lambda i: (0, 0)),
                  pl.BlockSpec((CONV_WIDTH, D_CONV), lambda i: (0, 0))],
        out_specs=(pl.BlockSpec((tm, D_CONV), lambda i: (i, 0)),
                   pl.BlockSpec((tm, D_SSM), lambda i: (i, 0))),
        scratch_shapes=[pltpu.VMEM((tm + SUBLANES, D_CONV), _F32)],
        compiler_params=pltpu.CompilerParams(
            dimension_semantics=("arbitrary",), vmem_limit_bytes=VMEM_LIMIT_BYTES),
        name="in_proj",
    )(x2, g, w_bcvu, conv_w)


def _ssm_prep_kernel(lr_ref, lc_ref, bm_ref, cm_ref, d_ref, toep_ref, bst_ref, cst_ref, a_ref):
    def discretise(lre, lim, log_dt):
        dt = jnp.exp(log_dt)
        mag = jnp.exp(lre * dt)
        return mag * jnp.cos(lim * dt), mag * jnp.sin(lim * dt)

    def powers(are, aim):
        pre, pim = [jnp.ones_like(are)], [jnp.zeros_like(are)]
        for _ in range(CHUNK):
            pre, pim = (pre + [pre[-1] * are - pim[-1] * aim],
                        pim + [pre[-1] * aim + pim[-1] * are])
        return pre, pim

    lr = lr_ref[0]
    lre, lim = lr[0:1, :], lr[1:2, :]
    are, aim = discretise(lre, lim, lr[2:3, :])
    pre, pim = powers(are, aim)
    den = lre * lre + lim * lim
    q_re = ((are - 1.0) * lre + aim * lim) / den
    q_im = (aim * lre - (are - 1.0) * lim) / den
    bb_re = q_re * bm_ref[0, 0] - q_im * bm_ref[0, 1]
    bb_im = q_re * bm_ref[0, 1] + q_im * bm_ref[0, 0]
    cm_re, cm_im = cm_ref[0, 0], cm_ref[0, 1]

    def split_bf16(v):
        v_hi = v.astype(_BF16)
        return v_hi, (v - v_hi.astype(_F32)).astype(_BF16)

    c_hi, c_lo = split_bf16(jnp.concatenate([cm_re, -cm_im], axis=0))
    kblk = []
    for k in range(CHUNK):
        ab_re = bb_re * pre[k] - bb_im * pim[k]
        ab_im = bb_re * pim[k] + bb_im * pre[k]
        rows = slice((CHUNK - 1 - k) * LANES, (CHUNK - k) * LANES)
        bst_ref[0, rows, :SLAB_STATE] = ab_re.astype(_BF16)
        bst_ref[0, rows, SLAB_STATE:] = ab_im.astype(_BF16)
        ab_hi, ab_lo = split_bf16(jnp.concatenate([ab_re, ab_im], axis=1))
        kblk.append(jnp.dot(ab_hi, c_hi, preferred_element_type=_F32)
                    + (jnp.dot(ab_lo, c_hi, preferred_element_type=_F32)
                       + jnp.dot(ab_hi, c_lo, preferred_element_type=_F32)))
    r = lax.broadcasted_iota(jnp.int32, (LANES, LANES), 0)
    c = lax.broadcasted_iota(jnp.int32, (LANES, LANES), 1)
    kblk[0] = kblk[0] + jnp.where(r == c, jnp.broadcast_to(d_ref[0], (LANES, LANES)), 0.0)
    kblk = [kb.astype(_BF16) for kb in kblk]
    zeros = jnp.zeros((LANES, LANES), _BF16)
    for sp in range(CHUNK):
        for s in range(CHUNK):
            toep_ref[0, sp * LANES:(sp + 1) * LANES, s * LANES:(s + 1) * LANES] = (
                kblk[s - sp] if s >= sp else zeros)

    lc = lc_ref[0]
    cre, cim = discretise(lc[:, 0:1], lc[:, 1:2], lc[:, 2:3])
    qre, qim = powers(cre, cim)
    for s in range(CHUNK):
        cols = slice(s * LANES, (s + 1) * LANES)
        cst_ref[0, :SLAB_STATE, cols] = (cm_re * qre[s + 1] - cm_im * qim[s + 1]).astype(_BF16)
        cst_ref[0, SLAB_STATE:, cols] = (-(cm_re * qim[s + 1] + cm_im * qre[s + 1])).astype(_BF16)
    a_ref[0, 0:1, :] = pre[CHUNK]
    a_ref[0, 1:2, :] = pim[CHUNK]


def _ssm_tables(lam_re, lam_im, log_dt, b_re, b_im, c_re, c_im, d_skip):
    sg = (N_SLABS, SLAB_GROUPS)
    eye = jnp.eye(SLAB_GROUPS, dtype=_F32)
    lam = jnp.stack([lam_re, lam_im, jnp.broadcast_to(log_dt[:, None], lam_re.shape)], axis=0)
    lam_row = lam.reshape(3, N_SLABS, SLAB_STATE).transpose(1, 0, 2)
    lam_col = lam_row.transpose(0, 2, 1)

    def b_blockdiag(b):
        bt = b.reshape(*sg, SSM_STATE, SSM_GROUP).transpose(0, 1, 3, 2)
        return (bt[:, :, :, None, :] * eye[None, :, None, :, None]).reshape(N_SLABS, LANES, SLAB_STATE)

    def c_blockdiag(c):
        ct = c.reshape(*sg, SSM_GROUP, SSM_STATE).transpose(0, 1, 3, 2)
        return (ct[:, :, :, None, :] * eye[None, :, None, :, None]).reshape(N_SLABS, SLAB_STATE, LANES)

    bm = jnp.stack([b_blockdiag(b_re), b_blockdiag(b_im)], axis=1)
    cm = jnp.stack([c_blockdiag(c_re), c_blockdiag(c_im)], axis=1)
    d = d_skip.reshape(N_SLABS, 1, LANES)
    slab3 = lambda sl: (sl, 0, 0)
    slab4 = lambda sl: (sl, 0, 0, 0)
    return pl.pallas_call(
        _ssm_prep_kernel,
        out_shape=(jax.ShapeDtypeStruct((N_SLABS, FLAT, FLAT), _BF16),
                   jax.ShapeDtypeStruct((N_SLABS, FLAT, 2 * SLAB_STATE), _BF16),
                   jax.ShapeDtypeStruct((N_SLABS, 2 * SLAB_STATE, FLAT), _BF16),
                   jax.ShapeDtypeStruct((N_SLABS, 2, SLAB_STATE), _F32)),
        grid=(N_SLABS,),
        in_specs=[pl.BlockSpec((1, 3, SLAB_STATE), slab3),
                  pl.BlockSpec((1, SLAB_STATE, 3), slab3),
                  pl.BlockSpec((1, 2, LANES, SLAB_STATE), slab4),
                  pl.BlockSpec((1, 2, SLAB_STATE, LANES), slab4),
                  pl.BlockSpec((1, 1, LANES), slab3)],
        out_specs=(pl.BlockSpec((1, FLAT, FLAT), slab3),
                   pl.BlockSpec((1, FLAT, 2 * SLAB_STATE), slab3),
                   pl.BlockSpec((1, 2 * SLAB_STATE, FLAT), slab3),
                   pl.BlockSpec((1, 2, SLAB_STATE), slab3)),
        compiler_params=pltpu.CompilerParams(
            dimension_semantics=("arbitrary",), vmem_limit_bytes=VMEM_LIMIT_BYTES),
        name="ssm_prep",
    )(lam_row, lam_col, bm, cm, d)


def _ssm_kernel(u_ref, toep_ref, bst_ref, cst_ref, a_ref, y_ref, uflat, s_scr, xc_scr, carry, ytoep):
    nb, tt, _ = u_ref.shape
    nch = tt // CHUNK
    n = nb * nch

    @pl.when(pl.program_id(1) == 0)
    def _():
        carry[...] = jnp.zeros_like(carry)

    for s in range(CHUNK):
        part = u_ref[:, pl.ds(s, nch, stride=CHUNK), :]
        uflat[:, s * LANES:(s + 1) * LANES] = part.reshape(n, LANES).astype(_BF16)

    n_cb = FLAT // MXU_DIM

    def toeplitz(cb):
        kk = (cb + 1) * MXU_DIM
        cols = slice(cb * MXU_DIM, kk)
        ytoep[:, cols] = jnp.dot(uflat[:, :kk], toep_ref[0, :kk, cols], preferred_element_type=_F32)

    for cb in range(n_cb // 2):
        toeplitz(cb)

    nblk = SLAB_STATE // LANES
    loc_all = jnp.dot(uflat[...], bst_ref[0], preferred_element_type=_F32)
    for cb in range(n_cb // 2, n_cb):
        toeplitz(cb)
    for kb in range(2 * nblk):
        s_scr[kb] = loc_all[:, kb * LANES:(kb + 1) * LANES]

    a = a_ref[0]
    are = [jnp.broadcast_to(a[0:1, kb * LANES:(kb + 1) * LANES], (nb, LANES)) for kb in range(nblk)]
    aim = [jnp.broadcast_to(a[1:2, kb * LANES:(kb + 1) * LANES], (nb, LANES)) for kb in range(nblk)]
    xr = [carry[kb] for kb in range(nblk)]
    xi = [carry[nblk + kb] for kb in range(nblk)]
    for j in range(nch):
        rows = pl.ds(j, nb, stride=nch)
        for kb in range(nblk):
            xc_scr[kb, rows, :] = xr[kb]
            xc_scr[nblk + kb, rows, :] = xi[kb]
            nr = are[kb] * xr[kb] - aim[kb] * xi[kb] + s_scr[kb, rows, :]
            ni = are[kb] * xi[kb] + aim[kb] * xr[kb] + s_scr[nblk + kb, rows, :]
            xr[kb], xi[kb] = nr, ni
    for kb in range(nblk):
        carry[kb] = xr[kb]
        carry[nblk + kb] = xi[kb]

    xc = jnp.concatenate([xc_scr[kb] for kb in range(2 * nblk)], axis=1).astype(_BF16)
    for cb in range(n_cb):
        cols = slice(cb * MXU_DIM, (cb + 1) * MXU_DIM)
        y = ytoep[:, cols] + jnp.dot(xc, cst_ref[0, :, cols], preferred_element_type=_F32)
        y = jax.nn.gelu(y)
        for h in range(MXU_DIM // LANES):
            s = cb * (MXU_DIM // LANES) + h
            y_ref[:, pl.ds(s, nch, stride=CHUNK), :] = (
                y[:, h * LANES:(h + 1) * LANES].reshape(nb, nch, LANES))


def _ssm(u3, tables):
    toep, bst, cst, a_chunk = tables
    nb, seq, _ = u3.shape
    tt = SSM_TIME_TILE
    n = nb * (tt // CHUNK)
    return pl.pallas_call(
        _ssm_kernel,
        out_shape=jax.ShapeDtypeStruct(u3.shape, _F32),
        grid=(N_SLABS, seq // tt),
        in_specs=[pl.BlockSpec((nb, tt, LANES), lambda sl, ti: (0, ti, sl)),
                  pl.BlockSpec((1, FLAT, FLAT), lambda sl, ti: (sl, 0, 0)),
                  pl.BlockSpec((1, FLAT, 2 * SLAB_STATE), lambda sl, ti: (sl, 0, 0)),
                  pl.BlockSpec((1, 2 * SLAB_STATE, FLAT), lambda sl, ti: (sl, 0, 0)),
                  pl.BlockSpec((1, 2, SLAB_STATE), lambda sl, ti: (sl, 0, 0))],
        out_specs=pl.BlockSpec((nb, tt, LANES), lambda sl, ti: (0, ti, sl)),
        scratch_shapes=[pltpu.VMEM((n, FLAT), _BF16),
                        pltpu.VMEM((2 * SLAB_STATE // LANES, n, LANES), _F32),
                        pltpu.VMEM((2 * SLAB_STATE // LANES, n, LANES), _F32),
                        pltpu.VMEM((2 * SLAB_STATE // LANES, nb, LANES), _F32),
                        pltpu.VMEM((n, FLAT), _F32)],
        compiler_params=pltpu.CompilerParams(
            dimension_semantics=("arbitrary", "arbitrary"), vmem_limit_bytes=VMEM_LIMIT_BYTES),
        name="ssm",
    )(u3, toep, bst, cst, a_chunk)


def _mix_route_kernel(x_ref, bz_ref, yg_ref, gm_ref, wg_ref, wco_ref, wglu_ref, wout_ref,
                      gf_ref, wr_ref, br_ref,
                      x1_ref, h_ref, ti_ref, tg_ref, rk_ref, cnt_ref, base, merged):
    tm = x_ref.shape[0]

    @pl.when(pl.program_id(0) == 0)
    def _():
        base[...] = jnp.zeros_like(base)

    chains = [slice(r0, r0 + ROW_CHAIN) for r0 in range(0, tm, ROW_CHAIN)]
    hs, picks = [], []

    def route(j):
        logits_tok = jnp.dot(hs[j], wr_ref[...], preferred_element_type=_F32)
        picks.append(_top_k_rows(chains[j], logits_tok, br_ref, ti_ref, tg_ref))

    def rank(j):
        _rank_rows(chains[j], *picks[j], rk_ref, base)

    for j, rows in enumerate(chains):
        hs.append(_mix_rows(rows, x_ref, bz_ref, yg_ref, gm_ref, wg_ref, wco_ref, wglu_ref,
                            wout_ref, gf_ref, x1_ref, h_ref, merged))
        if j >= 1:
            route(j - 1)
        if j >= 2:
            rank(j - 2)
    last = len(chains) - 1
    route(last)
    for j in range(max(last - 1, 0), last + 1):
        rank(j)
    cnt_ref[...] = base[...].astype(jnp.int32)


def _mix_rows(rows, x_ref, bz_ref, yg_ref, gm_ref, wg_ref, wco_ref, wglu_ref, wout_ref,
              gf_ref, x1_ref, h_ref, merged):
    x = x_ref[rows, :]
    xn = _rmsnorm(x, gm_ref[...]).astype(_BF16)
    bz = bz_ref[rows, :]
    yg = yg_ref[rows, :].astype(_BF16)
    for c in range(D_MODEL // MXU_DIM):
        lo = slice(c * MXU_DIM, (c + 1) * MXU_DIM)
        hi = slice(D_MODEL + c * MXU_DIM, D_MODEL + (c + 1) * MXU_DIM)
        gate_a = jnp.dot(xn, wg_ref[:, lo], preferred_element_type=_F32)
        gate_b = jnp.dot(xn, wg_ref[:, hi], preferred_element_type=_F32)
        y_a = jnp.dot(bz, wco_ref[:, lo], preferred_element_type=_F32)
        val = jnp.dot(yg, wglu_ref[:, lo], preferred_element_type=_F32)
        glu_gate = jnp.dot(yg, wglu_ref[:, hi], preferred_element_type=_F32)
        y_b = val * _sigmoid(glu_gate)
        merged[rows, lo] = (_sigmoid(gate_a) * y_a + _sigmoid(gate_b) * y_b).astype(_BF16)
    x1 = x + jnp.dot(merged[rows, :], wout_ref[...], preferred_element_type=_F32)
    x1_ref[rows, :] = x1
    h = _rmsnorm(x1, gf_ref[...])
    h_ref[rows, :] = _pack_bf16_halves(h)
    return h.astype(_BF16)


def _top_k_rows(rows, logits_tok, br_ref, ti_ref, tg_ref):
    tm = rows.stop - rows.start
    logits = jnp.transpose(logits_tok)[:N_EXPERTS, :] + br_ref[...]
    erow = lax.broadcasted_iota(jnp.int32, (N_EXPERTS, tm), 0).astype(_F32)
    neg_inf = jnp.float32(-jnp.inf)
    work = logits
    vals, idxs = [], []
    for _ in range(TOP_K):
        m = jnp.max(work, axis=0, keepdims=True)
        idx = jnp.min(jnp.where(work == m, erow, float(N_EXPERTS)), axis=0, keepdims=True)
        vals.append(m)
        idxs.append(idx)
        work = jnp.where(erow == idx, neg_inf, work)
    exps = [jnp.exp(v - vals[0]) for v in vals]
    denom = exps[0] + exps[1] + exps[2] + exps[3]
    sel = jnp.zeros((N_EXPERTS, tm), _F32)
    for k in range(TOP_K):
        ti_ref[k:k + 1, rows] = idxs[k].astype(jnp.int32)
        tg_ref[k:k + 1, rows] = exps[k] / denom
        tg_ref[TOP_K + k:TOP_K + k + 1, rows] = jnp.zeros((1, tm), _F32)
        sel = sel + (erow == idxs[k]).astype(_F32)
    return sel, idxs


def _rank_rows(rows, sel, idxs, rk_ref, base):
    tm = rows.stop - rows.start
    erow = lax.broadcasted_iota(jnp.int32, (N_EXPERTS, tm), 0).astype(_F32)
    row = lax.broadcasted_iota(jnp.int32, (tm, tm), 0)
    col = lax.broadcasted_iota(jnp.int32, (tm, tm), 1)
    earlier = (row < col).astype(_BF16)
    before = jnp.dot(sel.astype(_BF16), earlier, preferred_element_type=_F32) + base[...]
    for k in range(TOP_K):
        rk = jnp.sum(jnp.where(erow == idxs[k], before, 0.0), axis=0, keepdims=True)
        rk_ref[k:k + 1, rows] = rk.astype(jnp.int32)
    base[...] = base[...] + jnp.sum(sel, axis=1, keepdims=True)


def _mix_route(x2, bz, yg, g_mix, w_gates, w_conv_out, w_glu, w_out, g_ffn, w_router, b_router):
    t = x2.shape[0]
    tm = MIX_TILE
    tok = lambda i: (i, 0)
    tok_lanes = lambda i: (0, i)
    fixed = lambda i: (0, 0)
    weight = lambda shape: pl.BlockSpec(shape, fixed, pipeline_mode=pl.Buffered(1))
    return pl.pallas_call(
        _mix_route_kernel,
        out_shape=(jax.ShapeDtypeStruct((t, D_MODEL), _F32),
                   jax.ShapeDtypeStruct((t, D_MODEL // 2), jnp.uint32),
                   jax.ShapeDtypeStruct((TOP_K, t), jnp.int32),
                   jax.ShapeDtypeStruct((2 * TOP_K, t), _F32),
                   jax.ShapeDtypeStruct((TOP_K, t), jnp.int32),
                   jax.ShapeDtypeStruct((N_EXPERTS, 1), jnp.int32)),
        grid=(t // tm,),
        in_specs=[pl.BlockSpec((tm, D_MODEL), tok),
                  pl.BlockSpec((tm, D_CONV), tok),
                  pl.BlockSpec((tm, D_SSM), tok),
                  pl.BlockSpec((1, D_MODEL), fixed),
                  weight((D_MODEL, 2 * D_MODEL)),
                  weight((D_CONV, D_MODEL)),
                  weight((D_SSM, 2 * D_MODEL)),
                  weight((D_MODEL, D_MODEL)),
                  pl.BlockSpec((1, D_MODEL), fixed),
                  weight((D_MODEL, LANES)),
                  pl.BlockSpec((N_EXPERTS, 1), fixed)],
        out_specs=(pl.BlockSpec((tm, D_MODEL), tok),
                   pl.BlockSpec((tm, D_MODEL // 2), tok),
                   pl.BlockSpec((TOP_K, tm), tok_lanes),
                   pl.BlockSpec((2 * TOP_K, tm), tok_lanes),
                   pl.BlockSpec((TOP_K, tm), tok_lanes),
                   pl.BlockSpec((N_EXPERTS, 1), fixed)),
        scratch_shapes=[pltpu.VMEM((N_EXPERTS, 1), _F32),
                        pltpu.VMEM((tm, D_MODEL), _BF16)],
        compiler_params=pltpu.CompilerParams(
            dimension_semantics=("arbitrary",), vmem_limit_bytes=VMEM_LIMIT_BYTES),
        name="mix_route",
    )(x2, bz, yg, g_mix, w_gates, w_conv_out, w_glu, w_out, g_ffn, w_router, b_router)


def _expert_ffn_kernel(be_ref, nr_ref, slot_ref, next_ref, x_ref, wgu_hbm, bgu_ref, wd_hbm, bd_ref,
                       y_ref, stage_gu, stage_d, wgu_b, wd_b, sem_gu, sem_d):
    b = pl.program_id(0)
    expert = be_ref[b]
    live = nr_ref[b] > 0

    def weight_copies(e, slot):
        return (pltpu.make_async_copy(wgu_hbm.at[e], stage_gu.at[slot], sem_gu.at[slot]),
                pltpu.make_async_copy(wd_hbm.at[e], stage_d.at[slot], sem_d.at[slot]))

    @pl.when(live & ((b == 0) | (be_ref[jnp.maximum(b - 1, 0)] != expert)))
    def _():
        slot = slot_ref[b]

        @pl.when(b == 0)
        def _():
            for cp in weight_copies(expert, slot):
                cp.start()

        for cp in weight_copies(expert, slot):
            cp.wait()
        wgu_b[...] = stage_gu[slot].astype(_BF16)
        wd_b[...] = stage_d[slot].astype(_BF16)

        @pl.when(next_ref[b] < N_EXPERTS)
        def _():
            for cp in weight_copies(next_ref[b], 1 - slot):
                cp.start()

    def ffn_rows(n_rows):
        xw = x_ref[:n_rows, :]
        valid = lax.broadcasted_iota(jnp.int32, xw.shape, 0) < nr_ref[b]
        x = _unpack_bf16_halves(jnp.where(valid, xw, jnp.uint32(0))).astype(_BF16)
        hgu = jnp.dot(x, wgu_b[...], preferred_element_type=_F32) + bgu_ref[0]
        g = jnp.minimum(hgu[:, :D_FF], SWIGLU_LIMIT)
        up = jnp.clip(hgu[:, D_FF:], -SWIGLU_LIMIT, SWIGLU_LIMIT)
        act = (up + 1.0) * (g * _sigmoid(SWIGLU_ALPHA * g))
        y = jnp.dot(act.astype(_BF16), wd_b[...], preferred_element_type=_F32) + bd_ref[0]
        y_ref[:n_rows, :] = _pack_bf16_halves(y)
        if n_rows < MOE_BLOCK:
            y_ref[n_rows:, :] = jnp.zeros((MOE_BLOCK - n_rows, y_ref.shape[1]), y_ref.dtype)

    for height in range(FFN_ROW_STEP, MOE_BLOCK + 1, FFN_ROW_STEP):
        @pl.when((nr_ref[b] > height - FFN_ROW_STEP) & (nr_ref[b] <= height))
        def _():
            ffn_rows(height)

    @pl.when(jnp.logical_not(live))
    def _():
        y_ref[...] = jnp.zeros_like(y_ref)


def _expert_ffn(block_e, block_rows, block_slot, block_next, x_rows, w_gate_up, b_gate_up, w_down,
                b_down):
    n_rows = x_rows.shape[0]
    n_blocks = n_rows // MOE_BLOCK

    def bias_map(b, be, nr, sl, nx):
        return (be[b], 0, 0)

    def row_map(b, be, nr, sl, nx):
        return (b, 0)

    grid_spec = pltpu.PrefetchScalarGridSpec(
        num_scalar_prefetch=4,
        grid=(n_blocks,),
        in_specs=[pl.BlockSpec((MOE_BLOCK, D_MODEL // 2), row_map),
                  pl.BlockSpec(memory_space=pl.ANY),
                  pl.BlockSpec((1, 1, 2 * D_FF), bias_map),
                  pl.BlockSpec(memory_space=pl.ANY),
                  pl.BlockSpec((1, 1, D_MODEL), bias_map)],
        out_specs=pl.BlockSpec((MOE_BLOCK, D_MODEL // 2), row_map),
        scratch_shapes=[pltpu.VMEM((2, D_MODEL, 2 * D_FF), _F32),
                        pltpu.VMEM((2, D_FF, D_MODEL), _F32),
                        pltpu.VMEM((D_MODEL, 2 * D_FF), _BF16),
                        pltpu.VMEM((D_FF, D_MODEL), _BF16),
                        pltpu.SemaphoreType.DMA((2,)),
                        pltpu.SemaphoreType.DMA((2,))],
    )
    return pl.pallas_call(
        _expert_ffn_kernel,
        out_shape=jax.ShapeDtypeStruct((n_rows, D_MODEL // 2), jnp.uint32),
        grid_spec=grid_spec,
        compiler_params=pltpu.CompilerParams(
            dimension_semantics=("arbitrary",), vmem_limit_bytes=VMEM_LIMIT_BYTES),
        name="expert_ffn",
    )(block_e, block_rows, block_slot, block_next, x_rows, w_gate_up, b_gate_up, w_down, b_down)


def _sc_workers():
    info = plsc.get_sparse_core_info()
    return info.num_cores, info.num_cores * info.num_subcores


def _dispatch(h_packed, dest_flat, n_rows):
    t, width = h_packed.shape
    n_cores, n_workers = _sc_workers()
    n_chunks = t // (n_workers * SC_ROWS)
    chunks_per_k = t // SC_ROWS
    assert n_chunks % 2 == 0

    @functools.partial(
        pl.kernel, mesh=plsc.VectorSubcoreMesh(core_axis_name="c", subcore_axis_name="s"),
        out_type=jax.ShapeDtypeStruct((n_rows, width), h_packed.dtype),
        scratch_types=[pltpu.VMEM((TOP_K, n_chunks, SC_ROWS), jnp.int32),
                       pltpu.VMEM((2, SC_ROWS, width), h_packed.dtype),
                       pltpu.SemaphoreType.DMA((2,)),
                       pltpu.SemaphoreType.DMA((2,))])
    def scatter_rows(h_hbm, dest_hbm, out_hbm, idx_v, buf, lsem, ssem):
        wid = lax.axis_index("s") * n_cores + lax.axis_index("c")
        c0 = wid * n_chunks
        for k in range(TOP_K):
            pltpu.sync_copy(dest_hbm.at[pl.ds(k * chunks_per_k + c0, n_chunks)], idx_v.at[k])

        def load(c, b):
            return pltpu.make_async_copy(h_hbm.at[pl.ds((c0 + c) * SC_ROWS, SC_ROWS)], buf.at[b],
                                         lsem.at[b])

        def scatters(c, b):
            return [pltpu.make_async_copy(buf.at[b], out_hbm.at[idx_v.at[k, c]], ssem.at[b])
                    for k in range(TOP_K)]

        load(0, 0).start()

        @pl.loop(0, n_chunks, step=2)
        def _(ci):
            for b in range(2):
                c = ci + b

                @pl.when(c >= 1)
                def _():
                    for cp in scatters(c - 1, 1 - b):
                        cp.wait()

                @pl.when(c + 1 < n_chunks)
                def _():
                    load(c + 1, 1 - b).start()

                load(c, b).wait()
                for cp in scatters(c, b):
                    cp.start()

        for cp in scatters(n_chunks - 1, 1):
            cp.wait()

    return scatter_rows(h_packed, dest_flat.reshape(TOP_K * chunks_per_k, SC_ROWS))


def _collect(y_rows, dest_flat):
    n_idx = dest_flat.shape[0]
    width = y_rows.shape[1]
    rows, depth = COLLECT_ROWS, COLLECT_RING
    n_cores, n_workers = _sc_workers()
    n_chunks = n_idx // (n_workers * rows)
    assert n_chunks % depth == 0

    @functools.partial(
        pl.kernel, mesh=plsc.VectorSubcoreMesh(core_axis_name="c", subcore_axis_name="s"),
        out_type=jax.ShapeDtypeStruct((n_idx, width), y_rows.dtype),
        scratch_types=[pltpu.VMEM((n_chunks, rows), jnp.int32),
                       pltpu.VMEM((depth, rows, width), y_rows.dtype),
                       pltpu.SemaphoreType.DMA((depth,)),
                       pltpu.SemaphoreType.DMA((depth,))])
    def gather_rows(y_hbm, dest_hbm, out_hbm, idx_v, buf, gsem, wsem):
        wid = lax.axis_index("s") * n_cores + lax.axis_index("c")
        c0 = wid * n_chunks
        pltpu.sync_copy(dest_hbm.at[pl.ds(c0, n_chunks)], idx_v)

        def gather(c, b):
            return pltpu.make_async_copy(y_hbm.at[idx_v.at[c]], buf.at[b], gsem.at[b])

        def write(c, b):
            return pltpu.make_async_copy(buf.at[b], out_hbm.at[pl.ds((c0 + c) * rows, rows)],
                                         wsem.at[b])

        for c in range(depth - 1):
            gather(c, c).start()

        @pl.loop(0, n_chunks, step=depth)
        def _(ci):
            for b in range(depth):
                c = ci + b
                gather(c, b).wait()
                write(c, b).start()
                prev = (b - 1) % depth

                @pl.when(c >= 1)
                def _():
                    write(c - 1, prev).wait()

                @pl.when(c + depth - 1 < n_chunks)
                def _():
                    gather(c + depth - 1, prev).start()

        write(n_chunks - 1, (n_chunks - 1) % depth).wait()

    return gather_rows(y_rows, dest_flat.reshape(n_idx // rows, rows))


def _combine_kernel(x1_ref, ya_ref, tg_ref, g_ref, o_ref):
    acc = x1_ref[...]
    tg = jnp.transpose(tg_ref[...])
    for k in range(TOP_K):
        acc = acc + tg[:, k:k + 1] * _unpack_bf16_halves(ya_ref[k])
    o_ref[...] = _rmsnorm(acc, g_ref[...])


def _combine(x1, y_assign, top_g, g_final):
    t = x1.shape[0]
    tm = TOKEN_TILE
    return pl.pallas_call(
        _combine_kernel,
        out_shape=jax.ShapeDtypeStruct((t, D_MODEL), _F32),
        grid=(t // tm,),
        in_specs=[pl.BlockSpec((tm, D_MODEL), lambda i: (i, 0)),
                  pl.BlockSpec((TOP_K, tm, D_MODEL // 2), lambda i: (0, i, 0)),
                  pl.BlockSpec((2 * TOP_K, tm), lambda i: (0, i)),
                  pl.BlockSpec((1, D_MODEL), lambda i: (0, 0))],
        out_specs=pl.BlockSpec((tm, D_MODEL), lambda i: (i, 0)),
        compiler_params=pltpu.CompilerParams(
            dimension_semantics=("arbitrary",), vmem_limit_bytes=VMEM_LIMIT_BYTES),
        name="combine",
    )(x1, y_assign, top_g, g_final)


def _block_plan(counts, n_blocks):
    padded = ((counts + MOE_BLOCK - 1) // MOE_BLOCK) * MOE_BLOCK
    pad_end = jnp.cumsum(padded)
    pad_start = pad_end - padded
    block_start = (jnp.arange(n_blocks, dtype=jnp.int32) * MOE_BLOCK)[:, None]
    eidx = jnp.arange(N_EXPERTS, dtype=jnp.int32)
    owns = (pad_start[None, :] <= block_start) & (block_start < pad_end[None, :])
    has_blocks = (padded > 0).astype(jnp.int32)
    ordinal = jnp.cumsum(has_blocks) - has_blocks
    later = (eidx[None, :] > eidx[:, None]) & (padded[None, :] > 0)
    next_expert = jnp.min(jnp.where(later, eidx[None, :], N_EXPERTS), axis=1)

    def per_block(per_expert):
        return jnp.sum(jnp.where(owns, per_expert, 0), axis=1).astype(jnp.int32)

    block_e = per_block(eidx[None, :])
    block_rows = per_block(jnp.clip((pad_start + counts)[None, :] - block_start, 0, MOE_BLOCK))
    block_slot = per_block((ordinal % 2)[None, :])
    block_next = per_block(next_expert[None, :])
    return pad_start, (block_e, block_rows, block_slot, block_next)


def kernel(x, norm_mix_g, w_in, conv_w, w_conv_out, ssm_lam_re, ssm_lam_im, ssm_log_dt, ssm_b_re, ssm_b_im, ssm_c_re, ssm_c_im, ssm_d, w_glu, w_out, norm_ffn_g, w_router, b_router, w_gate_up, b_gate_up, w_down, b_down, norm_f_g):
    bsz, seq, d = x.shape
    t = bsz * seq
    x2 = x.reshape(t, d)
    assert seq % TOKEN_TILE == 0 and seq % SSM_TIME_TILE == 0 and w_in.shape[0] == 1

    w_in_b = w_in[0].astype(_BF16)
    n_bcvu = 3 * D_CONV + D_SSM
    g_mix = norm_mix_g[0].reshape(1, d)

    bz, u = _in_proj(x2, g_mix, w_in_b[:, :n_bcvu], conv_w[0], seq)

    tables = _ssm_tables(ssm_lam_re[0], ssm_lam_im[0], ssm_log_dt[0], ssm_b_re[0], ssm_b_im[0],
                         ssm_c_re[0], ssm_c_im[0], ssm_d[0])
    yg = _ssm(u.reshape(bsz, seq, D_SSM), tables).reshape(t, D_SSM)

    x1, h_packed, top_i, top_g, rank, counts = _mix_route(
        x2, bz, yg, g_mix, w_in_b[:, n_bcvu:], w_conv_out[0].astype(_BF16),
        w_glu[0].astype(_BF16), w_out[0].astype(_BF16), norm_ffn_g[0].reshape(1, d),
        jnp.pad(w_router[0], ((0, 0), (0, LANES - N_EXPERTS))).astype(_BF16),
        b_router[0].reshape(N_EXPERTS, 1))

    n_rows = t * TOP_K + N_EXPERTS * MOE_BLOCK
    pad_start, block_plan = _block_plan(counts[:, 0], n_rows // MOE_BLOCK)
    expert_ids = jnp.arange(N_EXPERTS, dtype=jnp.int32)[:, None, None]
    row_start = jnp.sum(jnp.where(top_i[None] == expert_ids, pad_start[:, None, None], 0), axis=0)
    dest = (row_start + rank).reshape(TOP_K * t)

    x_rows = _dispatch(h_packed, dest, n_rows)
    y_rows = _expert_ffn(*block_plan, x_rows, w_gate_up[0],
                         b_gate_up[0].reshape(N_EXPERTS, 1, 2 * D_FF), w_down[0],
                         b_down[0].reshape(N_EXPERTS, 1, D_MODEL))
    y_assign = _collect(y_rows, dest).reshape(TOP_K, t, D_MODEL // 2)
    out = _combine(x1, y_assign, top_g, norm_f_g.reshape(1, d))
    return out.reshape(bsz, seq, d)
```

```python
import functools

import jax
import jax.numpy as jnp
from jax import lax
from jax.experimental import pallas as pl
from jax.experimental.pallas import tpu as pltpu
from jax.experimental.pallas import tpu_sc as plsc

D_MODEL = 1024
D_CONV = 512
CONV_WIDTH = 3
D_SSM = 512
SSM_GROUP = 16
N_SSM_GROUPS = 32
SSM_STATE = 64
N_EXPERTS = 32
TOP_K = 4
D_FF = 1024
SWIGLU_LIMIT = 7.0
SWIGLU_ALPHA = 1.702
RMS_EPS = 1e-6

LANES = 128
SUBLANES = 8
MXU_DIM = 256
CHUNK = 8
SLAB_GROUPS = LANES // SSM_GROUP
N_SLABS = N_SSM_GROUPS // SLAB_GROUPS
SLAB_STATE = SLAB_GROUPS * SSM_STATE
FLAT = CHUNK * LANES
SSM_TIME_TILE = 128
TOKEN_TILE = 1024
IN_TILE = 2048
IN_CHAIN = 512
MIX_TILE = 1024
ROW_CHAIN = 256
MOE_BLOCK = 1024
FFN_ROW_STEP = 256
SC_ROWS = 64
COLLECT_ROWS = 32
COLLECT_RING = 4
VMEM_LIMIT_BYTES = 56 * 1024 * 1024

_BF16 = jnp.bfloat16
_F32 = jnp.float32


def _rmsnorm(xf, g):
    return xf * lax.rsqrt(jnp.mean(xf * xf, axis=-1, keepdims=True) + RMS_EPS) * g


def _sigmoid(v):
    return 0.5 * jnp.tanh(0.5 * v) + 0.5


def _pack_bf16_halves(v):
    n = v.shape[1] // 2
    bits = pltpu.bitcast(v.astype(_BF16).astype(_F32), jnp.uint32)
    return (bits[:, :n] >> 16) | (bits[:, n:] & jnp.uint32(0xFFFF0000))


def _unpack_bf16_halves(w):
    return jnp.concatenate([pltpu.bitcast(w << 16, _F32),
                            pltpu.bitcast(w & jnp.uint32(0xFFFF0000), _F32)], axis=1)


def _in_proj_kernel(tiles_per_seq, x_ref, g_ref, w_ref, cw_ref, bz_ref, u_ref, hbuf):
    tm = x_ref.shape[0]
    halo = SUBLANES

    @pl.when(pl.program_id(0) % tiles_per_seq == 0)
    def _():
        hbuf[0:halo, :] = jnp.zeros((halo, D_CONV), _F32)

    cw = cw_ref[...]
    for r0 in range(0, tm, IN_CHAIN):
        rows = slice(r0, r0 + IN_CHAIN)
        xn = _rmsnorm(x_ref[rows, :], g_ref[...]).astype(_BF16)
        cv = jnp.dot(xn, w_ref[:, D_CONV:3 * D_CONV], preferred_element_type=_F32)
        hbuf[halo + r0:halo + r0 + IN_CHAIN, :] = cv[:, :D_CONV] * cv[:, D_CONV:]
        u_ref[rows, :] = jnp.dot(xn, w_ref[:, 3 * D_CONV:], preferred_element_type=_F32)
        z = cw[CONV_WIDTH - 1:CONV_WIDTH, :] * hbuf[halo + r0:halo + r0 + IN_CHAIN, :]
        for lag in range(1, CONV_WIDTH):
            z = z + (cw[CONV_WIDTH - 1 - lag:CONV_WIDTH - lag, :]
                     * hbuf[halo + r0 - lag:halo + r0 - lag + IN_CHAIN, :])
        b_gate = jnp.dot(xn, w_ref[:, :D_CONV], preferred_element_type=_F32)
        bz_ref[rows, :] = (b_gate * z).astype(_BF16)
    hbuf[0:halo, :] = hbuf[tm:tm + halo, :]


def _in_proj(x2, g, w_bcvu, conv_w, seq):
    t = x2.shape[0]
    tm = IN_TILE
    assert SUBLANES >= CONV_WIDTH - 1 and seq % tm == 0
    return pl.pallas_call(
        functools.partial(_in_proj_kernel, seq // tm),
        out_shape=(jax.ShapeDtypeStruct((t, D_CONV), _BF16),
                   jax.ShapeDtypeStruct((t, D_SSM), _F32)),
        grid=(t // tm,),
        in_specs=[pl.BlockSpec((tm, D_MODEL), lambda i: (i, 0)),
                  pl.BlockSpec((1, D_MODEL), lambda i: (0, 0)),
                  pl.BlockSpec((D_MODEL, 3 * D_CONV + D_SSM), lambda i: (0, 0)),
                  pl.BlockSpec((CONV_WIDTH, D_CONV), lambda i: (0, 0))],
        out_specs=(pl.BlockSpec((tm, D_CONV), lambda i: (i, 0)),
                   pl.BlockSpec((tm, D_SSM), lambda i: (i, 0))),
        scratch_shapes=[pltpu.VMEM((tm + SUBLANES, D_CONV), _F32)],
        compiler_params=pltpu.CompilerParams(
            dimension_semantics=("arbitrary",), vmem_limit_bytes=VMEM_LIMIT_BYTES),
        name="in_proj",
    )(x2, g, w_bcvu, conv_w)


def _ssm_prep_kernel(lr_ref, lc_ref, bm_ref, cm_ref, d_ref, toep_ref, bst_ref, cst_ref, a_ref):
    def discretise(lre, lim, log_dt):
        dt = jnp.exp(log_dt)
        mag = jnp.exp(lre * dt)
        return mag * jnp.cos(lim * dt), mag * jnp.sin(lim * dt)

    def powers(are, aim):
        pre, pim = [jnp.ones_like(are)], [jnp.zeros_like(are)]
        for _ in range(CHUNK):
            pre, pim = (pre + [pre[-1] * are - pim[-1] * aim],
                        pim + [pre[-1] * aim + pim[-1] * are])
        return pre, pim

    lr = lr_ref[0]
    lre, lim = lr[0:1, :], lr[1:2, :]
    are, aim = discretise(lre, lim, lr[2:3, :])
    pre, pim = powers(are, aim)
    den = lre * lre + lim * lim
    q_re = ((are - 1.0) * lre + aim * lim) / den
    q_im = (aim * lre - (are - 1.0) * lim) / den
    bb_re = q_re * bm_ref[0, 0] - q_im * bm_ref[0, 1]
    bb_im = q_re * bm_ref[0, 1] + q_im * bm_ref[0, 0]
    cm_re, cm_im = cm_ref[0, 0], cm_ref[0, 1]

    def split_bf16(v):
        v_hi = v.astype(_BF16)
        return v_hi, (v - v_hi.astype(_F32)).astype(_BF16)

    c_hi, c_lo = split_bf16(jnp.concatenate([cm_re, -cm_im], axis=0))
    kblk = []
    for k in range(CHUNK):
        ab_re = bb_re * pre[k] - bb_im * pim[k]
        ab_im = bb_re * pim[k] + bb_im * pre[k]
        rows = slice((CHUNK - 1 - k) * LANES, (CHUNK - k) * LANES)
        bst_ref[0, rows, :SLAB_STATE] = ab_re.astype(_BF16)
        bst_ref[0, rows, SLAB_STATE:] = ab_im.astype(_BF16)
        ab_hi, ab_lo = split_bf16(jnp.concatenate([ab_re, ab_im], axis=1))
        kblk.append(jnp.dot(ab_hi, c_hi, preferred_element_type=_F32)
                    + (jnp.dot(ab_lo, c_hi, preferred_element_type=_F32)
                       + jnp.dot(ab_hi, c_lo, preferred_element_type=_F32)))
    r = lax.broadcasted_iota(jnp.int32, (LANES, LANES), 0)
    c = lax.broadcasted_iota(jnp.int32, (LANES, LANES), 1)
    kblk[0] = kblk[0] + jnp.where(r == c, jnp.broadcast_to(d_ref[0], (LANES, LANES)), 0.0)
    kblk = [kb.astype(_BF16) for kb in kblk]
    zeros = jnp.zeros((LANES, LANES), _BF16)
    for sp in range(CHUNK):
        for s in range(CHUNK):
            toep_ref[0, sp * LANES:(sp + 1) * LANES, s * LANES:(s + 1) * LANES] = (
                kblk[s - sp] if s >= sp else zeros)

    lc = lc_ref[0]
    cre, cim = discretise(lc[:, 0:1], lc[:, 1:2], lc[:, 2:3])
    qre, qim = powers(cre, cim)
    for s in range(CHUNK):
        cols = slice(s * LANES, (s + 1) * LANES)
        cst_ref[0, :SLAB_STATE, cols] = (cm_re * qre[s + 1] - cm_im * qim[s + 1]).astype(_BF16)
        cst_ref[0, SLAB_STATE:, cols] = (-(cm_re * qim[s + 1] + cm_im * qre[s + 1])).astype(_BF16)
    a_ref[0, 0:1, :] = pre[CHUNK]
    a_ref[0, 1:2, :] = pim[CHUNK]


def _ssm_tables(lam_re, lam_im, log_dt, b_re, b_im, c_re, c_im, d_skip):
    sg = (N_SLABS, SLAB_GROUPS)
    eye = jnp.eye(SLAB_GROUPS, dtype=_F32)
    lam = jnp.stack([lam_re, lam_im, jnp.broadcast_to(log_dt[:, None], lam_re.shape)], axis=0)
    lam_row = lam.reshape(3, N_SLABS, SLAB_STATE).transpose(1, 0, 2)
    lam_col = lam_row.transpose(0, 2, 1)

    def b_blockdiag(b):
        bt = b.reshape(*sg, SSM_STATE, SSM_GROUP).transpose(0, 1, 3, 2)
        return (bt[:, :, :, None, :] * eye[None, :, None, :, None]).reshape(N_SLABS, LANES, SLAB_STATE)

    def c_blockdiag(c):
        ct = c.reshape(*sg, SSM_GROUP, SSM_STATE).transpose(0, 1, 3, 2)
        return (ct[:, :, :, None, :] * eye[None, :, None, :, None]).reshape(N_SLABS, SLAB_STATE, LANES)

    bm = jnp.stack([b_blockdiag(b_re), b_blockdiag(b_im)], axis=1)
    cm = jnp.stack([c_blockdiag(c_re), c_blockdiag(c_im)], axis=1)
    d = d_skip.reshape(N_SLABS, 1, LANES)
    slab3 = lambda sl: (sl, 0, 0)
    slab4 = lambda sl: (sl, 0, 0, 0)
    return pl.pallas_call(
        _ssm_prep_kernel,
        out_shape=(jax.ShapeDtypeStruct((N_SLABS, FLAT, FLAT), _BF16),
                   jax.ShapeDtypeStruct((N_SLABS, FLAT, 2 * SLAB_STATE), _BF16),
                   jax.ShapeDtypeStruct((N_SLABS, 2 * SLAB_STATE, FLAT), _BF16),
                   jax.ShapeDtypeStruct((N_SLABS, 2, SLAB_STATE), _F32)),
        grid=(N_SLABS,),
        in_specs=[pl.BlockSpec((1, 3, SLAB_STATE), slab3),
                  pl.BlockSpec((1, SLAB_STATE, 3), slab3),
                  pl.BlockSpec((1, 2, LANES, SLAB_STATE), slab4),
                  pl.BlockSpec((1, 2, SLAB_STATE, LANES), slab4),
                  pl.BlockSpec((1, 1, LANES), slab3)],
        out_specs=(pl.BlockSpec((1, FLAT, FLAT), slab3),
                   pl.BlockSpec((1, FLAT, 2 * SLAB_STATE), slab3),
                   pl.BlockSpec((1, 2 * SLAB_STATE, FLAT), slab3),
                   pl.BlockSpec((1, 2, SLAB_STATE), slab3)),
        compiler_params=pltpu.CompilerParams(
            dimension_semantics=("arbitrary",), vmem_limit_bytes=VMEM_LIMIT_BYTES),
        name="ssm_prep",
    )(lam_row, lam_col, bm, cm, d)


def _ssm_kernel(u_ref, toep_ref, bst_ref, cst_ref, a_ref, y_ref, uflat, s_scr, xc_scr, carry, ytoep):
    nb, tt, _ = u_ref.shape
    nch = tt // CHUNK
    n = nb * nch

    @pl.when(pl.program_id(1) == 0)
    def _():
        carry[...] = jnp.zeros_like(carry)

    for s in range(CHUNK):
        part = u_ref[:, pl.ds(s, nch, stride=CHUNK), :]
        uflat[:, s * LANES:(s + 1) * LANES] = part.reshape(n, LANES).astype(_BF16)

    n_cb = FLAT // MXU_DIM

    def toeplitz(cb):
        kk = (cb + 1) * MXU_DIM
        cols = slice(cb * MXU_DIM, kk)
        ytoep[:, cols] = jnp.dot(uflat[:, :kk], toep_ref[0, :kk, cols], preferred_element_type=_F32)

    nblk = SLAB_STATE // LANES
    loc_all = jnp.dot(uflat[...], bst_ref[0], preferred_element_type=_F32)
    for cb in range(n_cb):
        toeplitz(cb)
    for kb in range(2 * nblk):
        s_scr[kb] = loc_all[:, kb * LANES:(kb + 1) * LANES]

    a = a_ref[0]
    are = [jnp.broadcast_to(a[0:1, kb * LANES:(kb + 1) * LANES], (nb, LANES)) for kb in range(nblk)]
    aim = [jnp.broadcast_to(a[1:2, kb * LANES:(kb + 1) * LANES], (nb, LANES)) for kb in range(nblk)]
    xr = [carry[kb] for kb in range(nblk)]
    xi = [carry[nblk + kb] for kb in range(nblk)]
    for j in range(nch):
        rows = pl.ds(j, nb, stride=nch)
        for kb in range(nblk):
            xc_scr[kb, rows, :] = xr[kb]
            xc_scr[nblk + kb, rows, :] = xi[kb]
            nr = are[kb] * xr[kb] - aim[kb] * xi[kb] + s_scr[kb, rows, :]
            ni = are[kb] * xi[kb] + aim[kb] * xr[kb] + s_scr[nblk + kb, rows, :]
            xr[kb], xi[kb] = nr, ni
    for kb in range(nblk):
        carry[kb] = xr[kb]
        carry[nblk + kb] = xi[kb]

    xc = jnp.concatenate([xc_scr[kb] for kb in range(2 * nblk)], axis=1).astype(_BF16)
    for cb in range(n_cb):
        cols = slice(cb * MXU_DIM, (cb + 1) * MXU_DIM)
        y = ytoep[:, cols] + jnp.dot(xc, cst_ref[0, :, cols], preferred_element_type=_F32)
        y = jax.nn.gelu(y)
        for h in range(MXU_DIM // LANES):
            s = cb * (MXU_DIM // LANES) + h
            y_ref[:, pl.ds(s, nch, stride=CHUNK), :] = (
                y[:, h * LANES:(h + 1) * LANES].reshape(nb, nch, LANES))


def _ssm(u3, tables):
    toep, bst, cst, a_chunk = tables
    nb, seq, _ = u3.shape
    tt = SSM_TIME_TILE
    n = nb * (tt // CHUNK)
    return pl.pallas_call(
        _ssm_kernel,
        out_shape=jax.ShapeDtypeStruct(u3.shape, _F32),
        grid=(N_SLABS, seq // tt),
        in_specs=[pl.BlockSpec((nb, tt, LANES), lambda sl, ti: (0, ti, sl)),
                  pl.BlockSpec((1, FLAT, FLAT), lambda sl, ti: (sl, 0, 0)),
                  pl.BlockSpec((1, FLAT, 2 * SLAB_STATE), lambda sl, ti: (sl, 0, 0)),
                  pl.BlockSpec((1, 2 * SLAB_STATE, FLAT), lambda sl, ti: (sl, 0, 0)),
                  pl.BlockSpec((1, 2, SLAB_STATE), lambda sl, ti: (sl, 0, 0))],
        out_specs=pl.BlockSpec((nb, tt, LANES), lambda sl, ti: (0, ti, sl)),
        scratch_shapes=[pltpu.VMEM((n, FLAT), _BF16),
                        pltpu.VMEM((2 * SLAB_STATE // LANES, n, LANES), _F32),
                        pltpu.VMEM((2 * SLAB_STATE // LANES, n, LANES), _F32),
                        pltpu.VMEM((2 * SLAB_STATE // LANES, nb, LANES), _F32),
                        pltpu.VMEM((n, FLAT), _F32)],
        compiler_params=pltpu.CompilerParams(
            dimension_semantics=("arbitrary", "arbitrary"), vmem_limit_bytes=VMEM_LIMIT_BYTES),
        name="ssm",
    )(u3, toep, bst, cst, a_chunk)


def _mix_route_kernel(x_ref, bz_ref, yg_ref, gm_ref, wg_ref, wco_ref, wglu_ref, wout_ref,
                      gf_ref, wr_ref, br_ref,
                      x1_ref, h_ref, ti_ref, tg_ref, rk_ref, cnt_ref, base, merged):
    tm = x_ref.shape[0]

    @pl.when(pl.program_id(0) == 0)
    def _():
        base[...] = jnp.zeros_like(base)

    chains = [slice(r0, r0 + ROW_CHAIN) for r0 in range(0, tm, ROW_CHAIN)]
    hs, picks = [], []

    def route(j):
        logits_tok = jnp.dot(hs[j], wr_ref[...], preferred_element_type=_F32)
        picks.append(_top_k_rows(chains[j], logits_tok, br_ref, ti_ref, tg_ref))

    def rank(j):
        _rank_rows(chains[j], *picks[j], rk_ref, base)

    for j, rows in enumerate(chains):
        hs.append(_mix_rows(rows, x_ref, bz_ref, yg_ref, gm_ref, wg_ref, wco_ref, wglu_ref,
                            wout_ref, gf_ref, x1_ref, h_ref, merged))
        if j >= 1:
            route(j - 1)
        if j >= 2:
            rank(j - 2)
    last = len(chains) - 1
    route(last)
    for j in range(max(last - 1, 0), last + 1):
        rank(j)
    cnt_ref[...] = base[...].astype(jnp.int32)


def _mix_rows(rows, x_ref, bz_ref, yg_ref, gm_ref, wg_ref, wco_ref, wglu_ref, wout_ref,
              gf_ref, x1_ref, h_ref, merged):
    x = x_ref[rows, :]
    xn = _rmsnorm(x, gm_ref[...]).astype(_BF16)
    bz = bz_ref[rows, :]
    yg = yg_ref[rows, :].astype(_BF16)
    for c in range(D_MODEL // MXU_DIM):
        lo = slice(c * MXU_DIM, (c + 1) * MXU_DIM)
        hi = slice(D_MODEL + c * MXU_DIM, D_MODEL + (c + 1) * MXU_DIM)
        gate_a = jnp.dot(xn, wg_ref[:, lo], preferred_element_type=_F32)
        gate_b = jnp.dot(xn, wg_ref[:, hi], preferred_element_type=_F32)
        y_a = jnp.dot(bz, wco_ref[:, lo], preferred_element_type=_F32)
        val = jnp.dot(yg, wglu_ref[:, lo], preferred_element_type=_F32)
        glu_gate = jnp.dot(yg, wglu_ref[:, hi], preferred_element_type=_F32)
        y_b = val * _sigmoid(glu_gate)
        merged[rows, lo] = (_sigmoid(gate_a) * y_a + _sigmoid(gate_b) * y_b).astype(_BF16)
    x1 = x + jnp.dot(merged[rows, :], wout_ref[...], preferred_element_type=_F32)
    x1_ref[rows, :] = x1
    h = _rmsnorm(x1, gf_ref[...])
    h_ref[rows, :] = _pack_bf16_halves(h)
    return h.astype(_BF16)


def _top_k_rows(rows, logits_tok, br_ref, ti_ref, tg_ref):
    tm = rows.stop - rows.start
    logits = jnp.transpose(logits_tok)[:N_EXPERTS, :] + br_ref[...]
    erow = lax.broadcasted_iota(jnp.int32, (N_EXPERTS, tm), 0).astype(_F32)
    neg_inf = jnp.float32(-jnp.inf)
    work = logits
    vals, idxs = [], []
    for _ in range(TOP_K):
        m = jnp.max(work, axis=0, keepdims=True)
        idx = jnp.min(jnp.where(work == m, erow, float(N_EXPERTS)), axis=0, keepdims=True)
        vals.append(m)
        idxs.append(idx)
        work = jnp.where(erow == idx, neg_inf, work)
    exps = [jnp.exp(v - vals[0]) for v in vals]
    denom = exps[0] + exps[1] + exps[2] + exps[3]
    sel = jnp.zeros((N_EXPERTS, tm), _F32)
    for k in range(TOP_K):
        ti_ref[k:k + 1, rows] = idxs[k].astype(jnp.int32)
        tg_ref[k:k + 1, rows] = exps[k] / denom
        tg_ref[TOP_K + k:TOP_K + k + 1, rows] = jnp.zeros((1, tm), _F32)
        sel = sel + (erow == idxs[k]).astype(_F32)
    return sel, idxs


def _rank_rows(rows, sel, idxs, rk_ref, base):
    tm = rows.stop - rows.start
    erow = lax.broadcasted_iota(jnp.int32, (N_EXPERTS, tm), 0).astype(_F32)
    row = lax.broadcasted_iota(jnp.int32, (tm, tm), 0)
    col = lax.broadcasted_iota(jnp.int32, (tm, tm), 1)
    earlier = (row < col).astype(_BF16)
    before = jnp.dot(sel.astype(_BF16), earlier, preferred_element_type=_F32) + base[...]
    for k in range(TOP_K):
        rk = jnp.sum(jnp.where(erow == idxs[k], before, 0.0), axis=0, keepdims=True)
        rk_ref[k:k + 1, rows] = rk.astype(jnp.int32)
    base[...] = base[...] + jnp.sum(sel, axis=1, keepdims=True)


def _mix_route(x2, bz, yg, g_mix, w_gates, w_conv_out, w_glu, w_out, g_ffn, w_router, b_router):
    t = x2.shape[0]
    tm = MIX_TILE
    tok = lambda i: (i, 0)
    tok_lanes = lambda i: (0, i)
    fixed = lambda i: (0, 0)
    weight = lambda shape: pl.BlockSpec(shape, fixed, pipeline_mode=pl.Buffered(1))
    return pl.pallas_call(
        _mix_route_kernel,
        out_shape=(jax.ShapeDtypeStruct((t, D_MODEL), _F32),
                   jax.ShapeDtypeStruct((t, D_MODEL // 2), jnp.uint32),
                   jax.ShapeDtypeStruct((TOP_K, t), jnp.int32),
                   jax.ShapeDtypeStruct((2 * TOP_K, t), _F32),
                   jax.ShapeDtypeStruct((TOP_K, t), jnp.int32),
                   jax.ShapeDtypeStruct((N_EXPERTS, 1), jnp.int32)),
        grid=(t // tm,),
        in_specs=[pl.BlockSpec((tm, D_MODEL), tok),
                  pl.BlockSpec((tm, D_CONV), tok),
                  pl.BlockSpec((tm, D_SSM), tok),
                  pl.BlockSpec((1, D_MODEL), fixed),
                  weight((D_MODEL, 2 * D_MODEL)),
                  weight((D_CONV, D_MODEL)),
                  weight((D_SSM, 2 * D_MODEL)),
                  weight((D_MODEL, D_MODEL)),
                  pl.BlockSpec((1, D_MODEL), fixed),
                  weight((D_MODEL, LANES)),
                  pl.BlockSpec((N_EXPERTS, 1), fixed)],
        out_specs=(pl.BlockSpec((tm, D_MODEL), tok),
                   pl.BlockSpec((tm, D_MODEL // 2), tok),
                   pl.BlockSpec((TOP_K, tm), tok_lanes),
                   pl.BlockSpec((2 * TOP_K, tm), tok_lanes),
                   pl.BlockSpec((TOP_K, tm), tok_lanes),
                   pl.BlockSpec((N_EXPERTS, 1), fixed)),
        scratch_shapes=[pltpu.VMEM((N_EXPERTS, 1), _F32),
                        pltpu.VMEM((tm, D_MODEL), _BF16)],
        compiler_params=pltpu.CompilerParams(
            dimension_semantics=("arbitrary",), vmem_limit_bytes=VMEM_LIMIT_BYTES),
        name="mix_route",
    )(x2, bz, yg, g_mix, w_gates, w_conv_out, w_glu, w_out, g_ffn, w_router, b_router)


def _expert_ffn_kernel(be_ref, nr_ref, slot_ref, next_ref, x_ref, wgu_hbm, bgu_ref, wd_hbm, bd_ref,
                       y_ref, stage_gu, stage_d, wgu_b, wd_b, sem_gu, sem_d):
    b = pl.program_id(0)
    expert = be_ref[b]
    live = nr_ref[b] > 0

    def weight_copies(e, slot):
        return (pltpu.make_async_copy(wgu_hbm.at[e], stage_gu.at[slot], sem_gu.at[slot]),
                pltpu.make_async_copy(wd_hbm.at[e], stage_d.at[slot], sem_d.at[slot]))

    @pl.when(live & ((b == 0) | (be_ref[jnp.maximum(b - 1, 0)] != expert)))
    def _():
        slot = slot_ref[b]

        @pl.when(b == 0)
        def _():
            for cp in weight_copies(expert, slot):
                cp.start()

        for cp in weight_copies(expert, slot):
            cp.wait()
        wgu_b[...] = stage_gu[slot].astype(_BF16)
        wd_b[...] = stage_d[slot].astype(_BF16)

        @pl.when(next_ref[b] < N_EXPERTS)
        def _():
            for cp in weight_copies(next_ref[b], 1 - slot):
                cp.start()

    def ffn_rows(n_rows):
        xw = x_ref[:n_rows, :]
        valid = lax.broadcasted_iota(jnp.int32, xw.shape, 0) < nr_ref[b]
        x = _unpack_bf16_halves(jnp.where(valid, xw, jnp.uint32(0))).astype(_BF16)
        hgu = jnp.dot(x, wgu_b[...], preferred_element_type=_F32) + bgu_ref[0]
        g = jnp.minimum(hgu[:, :D_FF], SWIGLU_LIMIT)
        up = jnp.clip(hgu[:, D_FF:], -SWIGLU_LIMIT, SWIGLU_LIMIT)
        act = (up + 1.0) * (g * _sigmoid(SWIGLU_ALPHA * g))
        y = jnp.dot(act.astype(_BF16), wd_b[...], preferred_element_type=_F32) + bd_ref[0]
        y_ref[:n_rows, :] = _pack_bf16_halves(y)
        if n_rows < MOE_BLOCK:
            y_ref[n_rows:, :] = jnp.zeros((MOE_BLOCK - n_rows, y_ref.shape[1]), y_ref.dtype)

    for height in range(FFN_ROW_STEP, MOE_BLOCK + 1, FFN_ROW_STEP):
        @pl.when((nr_ref[b] > height - FFN_ROW_STEP) & (nr_ref[b] <= height))
        def _():
            ffn_rows(height)

    @pl.when(jnp.logical_not(live))
    def _():
        y_ref[...] = jnp.zeros_like(y_ref)


def _expert_ffn(block_e, block_rows, block_slot, block_next, x_rows, w_gate_up, b_gate_up, w_down,
                b_down):
    n_rows = x_rows.shape[0]
    n_blocks = n_rows // MOE_BLOCK

    def bias_map(b, be, nr, sl, nx):
        return (be[b], 0, 0)

    def row_map(b, be, nr, sl, nx):
        return (b, 0)

    grid_spec = pltpu.PrefetchScalarGridSpec(
        num_scalar_prefetch=4,
        grid=(n_blocks,),
        in_specs=[pl.BlockSpec((MOE_BLOCK, D_MODEL // 2), row_map),
                  pl.BlockSpec(memory_space=pl.ANY),
                  pl.BlockSpec((1, 1, 2 * D_FF), bias_map),
                  pl.BlockSpec(memory_space=pl.ANY),
                  pl.BlockSpec((1, 1, D_MODEL), bias_map)],
        out_specs=pl.BlockSpec((MOE_BLOCK, D_MODEL // 2), row_map),
        scratch_shapes=[pltpu.VMEM((2, D_MODEL, 2 * D_FF), _F32),
                        pltpu.VMEM((2, D_FF, D_MODEL), _F32),
                        pltpu.VMEM((D_MODEL, 2 * D_FF), _BF16),
                        pltpu.VMEM((D_FF, D_MODEL), _BF16),
                        pltpu.SemaphoreType.DMA((2,)),
                        pltpu.SemaphoreType.DMA((2,))],
    )
    return pl.pallas_call(
        _expert_ffn_kernel,
        out_shape=jax.ShapeDtypeStruct((n_rows, D_MODEL // 2), jnp.uint32),
        grid_spec=grid_spec,
        compiler_params=pltpu.CompilerParams(
            dimension_semantics=("arbitrary",), vmem_limit_bytes=VMEM_LIMIT_BYTES),
        name="expert_ffn",
    )(block_e, block_rows, block_slot, block_next, x_rows, w_gate_up, b_gate_up, w_down, b_down)


def _sc_workers():
    info = plsc.get_sparse_core_info()
    return info.num_cores, info.num_cores * info.num_subcores


def _dispatch(h_packed, dest_flat, n_rows):
    t, width = h_packed.shape
    n_cores, n_workers = _sc_workers()
    n_chunks = t // (n_workers * SC_ROWS)
    chunks_per_k = t // SC_ROWS
    assert n_chunks % 2 == 0

    @functools.partial(
        pl.kernel, mesh=plsc.VectorSubcoreMesh(core_axis_name="c", subcore_axis_name="s"),
        out_type=jax.ShapeDtypeStruct((n_rows, width), h_packed.dtype),
        scratch_types=[pltpu.VMEM((TOP_K, n_chunks, SC_ROWS), jnp.int32),
                       pltpu.VMEM((2, SC_ROWS, width), h_packed.dtype),
                       pltpu.SemaphoreType.DMA((2,)),
                       pltpu.SemaphoreType.DMA((2,))])
    def scatter_rows(h_hbm, dest_hbm, out_hbm, idx_v, buf, lsem, ssem):
        wid = lax.axis_index("s") * n_cores + lax.axis_index("c")
        c0 = wid * n_chunks
        for k in range(TOP_K):
            pltpu.sync_copy(dest_hbm.at[pl.ds(k * chunks_per_k + c0, n_chunks)], idx_v.at[k])

        def load(c, b):
            return pltpu.make_async_copy(h_hbm.at[pl.ds((c0 + c) * SC_ROWS, SC_ROWS)], buf.at[b],
                                         lsem.at[b])

        def scatters(c, b):
            return [pltpu.make_async_copy(buf.at[b], out_hbm.at[idx_v.at[k, c]], ssem.at[b])
                    for k in range(TOP_K)]

        load(0, 0).start()

        @pl.loop(0, n_chunks, step=2)
        def _(ci):
            for b in range(2):
                c = ci + b

                @pl.when(c >= 1)
                def _():
                    for cp in scatters(c - 1, 1 - b):
                        cp.wait()

                @pl.when(c + 1 < n_chunks)
                def _():
                    load(c + 1, 1 - b).start()

                load(c, b).wait()
                for cp in scatters(c, b):
                    cp.start()

        for cp in scatters(n_chunks - 1, 1):
            cp.wait()

    return scatter_rows(h_packed, dest_flat.reshape(TOP_K * chunks_per_k, SC_ROWS))


def _collect(y_rows, dest_flat):
    n_idx = dest_flat.shape[0]
    width = y_rows.shape[1]
    rows, depth = COLLECT_ROWS, COLLECT_RING
    n_cores, n_workers = _sc_workers()
    n_chunks = n_idx // (n_workers * rows)
    assert n_chunks % depth == 0

    @functools.partial(
        pl.kernel, mesh=plsc.VectorSubcoreMesh(core_axis_name="c", subcore_axis_name="s"),
        out_type=jax.ShapeDtypeStruct((n_idx, width), y_rows.dtype),
        scratch_types=[pltpu.VMEM((n_chunks, rows), jnp.int32),
                       pltpu.VMEM((depth, rows, width), y_rows.dtype),
                       pltpu.SemaphoreType.DMA((depth,)),
                       pltpu.SemaphoreType.DMA((depth,))])
    def gather_rows(y_hbm, dest_hbm, out_hbm, idx_v, buf, gsem, wsem):
        wid = lax.axis_index("s") * n_cores + lax.axis_index("c")
        c0 = wid * n_chunks
        pltpu.sync_copy(dest_hbm.at[pl.ds(c0, n_chunks)], idx_v)

        def gather(c, b):
            return pltpu.make_async_copy(y_hbm.at[idx_v.at[c]], buf.at[b], gsem.at[b])

        def write(c, b):
            return pltpu.make_async_copy(buf.at[b], out_hbm.at[pl.ds((c0 + c) * rows, rows)],
                                         wsem.at[b])

        for c in range(depth - 1):
            gather(c, c).start()

        @pl.loop(0, n_chunks, step=depth)
        def _(ci):
            for b in range(depth):
                c = ci + b
                gather(c, b).wait()
                write(c, b).start()
                prev = (b - 1) % depth

                @pl.when(c >= 1)
                def _():
                    write(c - 1, prev).wait()

                @pl.when(c + depth - 1 < n_chunks)
                def _():
                    gather(c + depth - 1, prev).start()

        write(n_chunks - 1, (n_chunks - 1) % depth).wait()

    return gather_rows(y_rows, dest_flat.reshape(n_idx // rows, rows))


def _combine_kernel(x1_ref, ya_ref, tg_ref, g_ref, o_ref):
    acc = x1_ref[...]
    tg = jnp.transpose(tg_ref[...])
    for k in range(TOP_K):
        acc = acc + tg[:, k:k + 1] * _unpack_bf16_halves(ya_ref[k])
    o_ref[...] = _rmsnorm(acc, g_ref[...])


def _combine(x1, y_assign, top_g, g_final):
    t = x1.shape[0]
    tm = TOKEN_TILE
    return pl.pallas_call(
        _combine_kernel,
        out_shape=jax.ShapeDtypeStruct((t, D_MODEL), _F32),
        grid=(t // tm,),
        in_specs=[pl.BlockSpec((tm, D_MODEL), lambda i: (i, 0)),
                  pl.BlockSpec((TOP_K, tm, D_MODEL // 2), lambda i: (0, i, 0)),
                  pl.BlockSpec((2 * TOP_K, tm), lambda i: (0, i)),
                  pl.BlockSpec((1, D_MODEL), lambda i: (0, 0))],
        out_specs=pl.BlockSpec((tm, D_MODEL), lambda i: (i, 0)),
        compiler_params=pltpu.CompilerParams(
            dimension_semantics=("arbitrary",), vmem_limit_bytes=VMEM_LIMIT_BYTES),
        name="combine",
    )(x1, y_assign, top_g, g_final)


def _block_plan(counts, n_blocks):
    padded = ((counts + MOE_BLOCK - 1) // MOE_BLOCK) * MOE_BLOCK
    pad_end = jnp.cumsum(padded)
    pad_start = pad_end - padded
    block_start = (jnp.arange(n_blocks, dtype=jnp.int32) * MOE_BLOCK)[:, None]
    eidx = jnp.arange(N_EXPERTS, dtype=jnp.int32)
    owns = (pad_start[None, :] <= block_start) & (block_start < pad_end[None, :])
    has_blocks = (padded > 0).astype(jnp.int32)
    ordinal = jnp.cumsum(has_blocks) - has_blocks
    later = (eidx[None, :] > eidx[:, None]) & (padded[None, :] > 0)
    next_expert = jnp.min(jnp.where(later, eidx[None, :], N_EXPERTS), axis=1)

    def per_block(per_expert):
        return jnp.sum(jnp.where(owns, per_expert, 0), axis=1).astype(jnp.int32)

    block_e = per_block(eidx[None, :])
    block_rows = per_block(jnp.clip((pad_start + counts)[None, :] - block_start, 0, MOE_BLOCK))
    block_slot = per_block((ordinal % 2)[None, :])
    block_next = per_block(next_expert[None, :])
    return pad_start, (block_e, block_rows, block_slot, block_next)


def kernel(x, norm_mix_g, w_in, conv_w, w_conv_out, ssm_lam_re, ssm_lam_im, ssm_log_dt, ssm_b_re, ssm_b_im, ssm_c_re, ssm_c_im, ssm_d, w_glu, w_out, norm_ffn_g, w_router, b_router, w_gate_up, b_gate_up, w_down, b_down, norm_f_g):
    bsz, seq, d = x.shape
    t = bsz * seq
    x2 = x.reshape(t, d)
    assert seq % TOKEN_TILE == 0 and seq % SSM_TIME_TILE == 0 and w_in.shape[0] == 1

    w_in_b = w_in[0].astype(_BF16)
    n_bcvu = 3 * D_CONV + D_SSM
    g_mix = norm_mix_g[0].reshape(1, d)

    bz, u = _in_proj(x2, g_mix, w_in_b[:, :n_bcvu], conv_w[0], seq)

    tables = _ssm_tables(ssm_lam_re[0], ssm_lam_im[0], ssm_log_dt[0], ssm_b_re[0], ssm_b_im[0],
                         ssm_c_re[0], ssm_c_im[0], ssm_d[0])
    yg = _ssm(u.reshape(bsz, seq, D_SSM), tables).reshape(t, D_SSM)

    x1, h_packed, top_i, top_g, rank, counts = _mix_route(
        x2, bz, yg, g_mix, w_in_b[:, n_bcvu:], w_conv_out[0].astype(_BF16),
        w_glu[0].astype(_BF16), w_out[0].astype(_BF16), norm_ffn_g[0].reshape(1, d),
        jnp.pad(w_router[0], ((0, 0), (0, LANES - N_EXPERTS))).astype(_BF16),
        b_router[0].reshape(N_EXPERTS, 1))

    n_rows = t * TOP_K + N_EXPERTS * MOE_BLOCK
    pad_start, block_plan = _block_plan(counts[:, 0], n_rows // MOE_BLOCK)
    expert_ids = jnp.arange(N_EXPERTS, dtype=jnp.int32)[:, None, None]
    row_start = jnp.sum(jnp.where(top_i[None] == expert_ids, pad_start[:, None, None], 0), axis=0)
    dest = (row_start + rank).reshape(TOP_K * t)

    x_rows = _dispatch(h_packed, dest, n_rows)
    y_rows = _expert_ffn(*block_plan, x_rows, w_gate_up[0],
                         b_gate_up[0].reshape(N_EXPERTS, 1, 2 * D_FF), w_down[0],
                         b_down[0].reshape(N_EXPERTS, 1, D_MODEL))
    y_assign = _collect(y_rows, dest).reshape(TOP_K, t, D_MODEL // 2)
    out = _combine(x1, y_assign, top_g, norm_f_g.reshape(1, d))
    return out.reshape(bsz, seq, d)
```

```python
import functools

import jax
import jax.numpy as jnp
from jax import lax
from jax.experimental import pallas as pl
from jax.experimental.pallas import tpu as pltpu
from jax.experimental.pallas import tpu_sc as plsc

D_MODEL = 1024
D_CONV = 512
CONV_WIDTH = 3
D_SSM = 512
SSM_GROUP = 16
N_SSM_GROUPS = 32
SSM_STATE = 64
N_EXPERTS = 32
TOP_K = 4
D_FF = 1024
SWIGLU_LIMIT = 7.0
SWIGLU_ALPHA = 1.702
RMS_EPS = 1e-6

LANES = 128
SUBLANES = 8
MXU_DIM = 256
CHUNK = 8
SLAB_GROUPS = LANES // SSM_GROUP
N_SLABS = N_SSM_GROUPS // SLAB_GROUPS
SLAB_STATE = SLAB_GROUPS * SSM_STATE
FLAT = CHUNK * LANES
SSM_TIME_TILE = 128
TOKEN_TILE = 1024
IN_TILE = 1024
IN_CHAIN = 512
MIX_TILE = 1024
ROW_CHAIN = 256
MOE_BLOCK = 1024
FFN_ROW_STEP = 256
SC_ROWS = 64
COLLECT_ROWS = 32
COLLECT_RING = 4
VMEM_LIMIT_BYTES = 56 * 1024 * 1024

_BF16 = jnp.bfloat16
_F32 = jnp.float32


def _rmsnorm(xf, g):
    return xf * lax.rsqrt(jnp.mean(xf * xf, axis=-1, keepdims=True) + RMS_EPS) * g


def _sigmoid(v):
    return 0.5 * jnp.tanh(0.5 * v) + 0.5


def _pack_bf16_halves(v):
    n = v.shape[1] // 2
    bits = pltpu.bitcast(v.astype(_BF16).astype(_F32), jnp.uint32)
    return (bits[:, :n] >> 16) | (bits[:, n:] & jnp.uint32(0xFFFF0000))


def _unpack_bf16_halves(w):
    return jnp.concatenate([pltpu.bitcast(w << 16, _F32),
                            pltpu.bitcast(w & jnp.uint32(0xFFFF0000), _F32)], axis=1)


def _in_proj_kernel(tiles_per_seq, x_ref, g_ref, w_ref, cw_ref, bz_ref, u_ref, hbuf):
    tm = x_ref.shape[0]
    halo = SUBLANES

    @pl.when(pl.program_id(0) % tiles_per_seq == 0)
    def _():
        hbuf[0:halo, :] = jnp.zeros((halo, D_CONV), _F32)

    cw = cw_ref[...]
    for r0 in range(0, tm, IN_CHAIN):
        rows = slice(r0, r0 + IN_CHAIN)
        xn = _rmsnorm(x_ref[rows, :], g_ref[...]).astype(_BF16)
        cv = jnp.dot(xn, w_ref[:, D_CONV:3 * D_CONV], preferred_element_type=_F32)
        hbuf[halo + r0:halo + r0 + IN_CHAIN, :] = cv[:, :D_CONV] * cv[:, D_CONV:]
        u_ref[rows, :] = jnp.dot(xn, w_ref[:, 3 * D_CONV:], preferred_element_type=_F32)
        z = cw[CONV_WIDTH - 1:CONV_WIDTH, :] * hbuf[halo + r0:halo + r0 + IN_CHAIN, :]
        for lag in range(1, CONV_WIDTH):
            z = z + (cw[CONV_WIDTH - 1 - lag:CONV_WIDTH - lag, :]
                     * hbuf[halo + r0 - lag:halo + r0 - lag + IN_CHAIN, :])
        b_gate = jnp.dot(xn, w_ref[:, :D_CONV], preferred_element_type=_F32)
        bz_ref[rows, :] = (b_gate * z).astype(_BF16)
    hbuf[0:halo, :] = hbuf[tm:tm + halo, :]


def _in_proj(x2, g, w_bcvu, conv_w, seq):
    t = x2.shape[0]
    tm = IN_TILE
    assert SUBLANES >= CONV_WIDTH - 1 and seq % tm == 0
    return pl.pallas_call(
        functools.partial(_in_proj_kernel, seq // tm),
        out_shape=(jax.ShapeDtypeStruct((t, D_CONV), _BF16),
                   jax.ShapeDtypeStruct((t, D_SSM), _F32)),
        grid=(t // tm,),
        in_specs=[pl.BlockSpec((tm, D_MODEL), lambda i: (i, 0)),
                  pl.BlockSpec((1, D_MODEL), lambda i: (0, 0)),
                  pl.BlockSpec((D_MODEL, 3 * D_CONV + D_SSM), lambda i: (0, 0)),
                  pl.BlockSpec((CONV_WIDTH, D_CONV), lambda i: (0, 0))],
        out_specs=(pl.BlockSpec((tm, D_CONV), lambda i: (i, 0)),
                   pl.BlockSpec((tm, D_SSM), lambda i: (i, 0))),
        scratch_shapes=[pltpu.VMEM((tm + SUBLANES, D_CONV), _F32)],
        compiler_params=pltpu.CompilerParams(
            dimension_semantics=("arbitrary",), vmem_limit_bytes=VMEM_LIMIT_BYTES),
        name="in_proj",
    )(x2, g, w_bcvu, conv_w)


def _ssm_prep_kernel(lr_ref, lc_ref, bm_ref, cm_ref, d_ref, toep_ref, bst_ref, cst_ref, a_ref):
    def discretise(lre, lim, log_dt):
        dt = jnp.exp(log_dt)
        mag = jnp.exp(lre * dt)
        return mag * jnp.cos(lim * dt), mag * jnp.sin(lim * dt)

    def powers(are, aim):
        pre, pim = [jnp.ones_like(are)], [jnp.zeros_like(are)]
        for _ in range(CHUNK):
            pre, pim = (pre + [pre[-1] * are - pim[-1] * aim],
                        pim + [pre[-1] * aim + pim[-1] * are])
        return pre, pim

    lr = lr_ref[0]
    lre, lim = lr[0:1, :], lr[1:2, :]
    are, aim = discretise(lre, lim, lr[2:3, :])
    pre, pim = powers(are, aim)
    den = lre * lre + lim * lim
    q_re = ((are - 1.0) * lre + aim * lim) / den
    q_im = (aim * lre - (are - 1.0) * lim) / den
    bb_re = q_re * bm_ref[0, 0] - q_im * bm_ref[0, 1]
    bb_im = q_re * bm_ref[0, 1] + q_im * bm_ref[0, 0]
    cm_re, cm_im = cm_ref[0, 0], cm_ref[0, 1]

    def split_bf16(v):
        v_hi = v.astype(_BF16)
        return v_hi, (v - v_hi.astype(_F32)).astype(_BF16)

    c_hi, c_lo = split_bf16(jnp.concatenate([cm_re, -cm_im], axis=0))
    kblk = []
    for k in range(CHUNK):
        ab_re = bb_re * pre[k] - bb_im * pim[k]
        ab_im = bb_re * pim[k] + bb_im * pre[k]
        rows = slice((CHUNK - 1 - k) * LANES, (CHUNK - k) * LANES)
        bst_ref[0, rows, :SLAB_STATE] = ab_re.astype(_BF16)
        bst_ref[0, rows, SLAB_STATE:] = ab_im.astype(_BF16)
        ab_hi, ab_lo = split_bf16(jnp.concatenate([ab_re, ab_im], axis=1))
        kblk.append(jnp.dot(ab_hi, c_hi, preferred_element_type=_F32)
                    + (jnp.dot(ab_lo, c_hi, preferred_element_type=_F32)
                       + jnp.dot(ab_hi, c_lo, preferred_element_type=_F32)))
    r = lax.broadcasted_iota(jnp.int32, (LANES, LANES), 0)
    c = lax.broadcasted_iota(jnp.int32, (LANES, LANES), 1)
    kblk[0] = kblk[0] + jnp.where(r == c, jnp.broadcast_to(d_ref[0], (LANES, LANES)), 0.0)
    kblk = [kb.astype(_BF16) for kb in kblk]
    zeros = jnp.zeros((LANES, LANES), _BF16)
    for sp in range(CHUNK):
        for s in range(CHUNK):
            toep_ref[0, sp * LANES:(sp + 1) * LANES, s * LANES:(s + 1) * LANES] = (
                kblk[s - sp] if s >= sp else zeros)

    lc = lc_ref[0]
    cre, cim = discretise(lc[:, 0:1], lc[:, 1:2], lc[:, 2:3])
    qre, qim = powers(cre, cim)
    for s in range(CHUNK):
        cols = slice(s * LANES, (s + 1) * LANES)
        cst_ref[0, :SLAB_STATE, cols] = (cm_re * qre[s + 1] - cm_im * qim[s + 1]).astype(_BF16)
        cst_ref[0, SLAB_STATE:, cols] = (-(cm_re * qim[s + 1] + cm_im * qre[s + 1])).astype(_BF16)
    a_ref[0, 0:1, :] = pre[CHUNK]
    a_ref[0, 1:2, :] = pim[CHUNK]


def _ssm_tables(lam_re, lam_im, log_dt, b_re, b_im, c_re, c_im, d_skip):
    sg = (N_SLABS, SLAB_GROUPS)
    eye = jnp.eye(SLAB_GROUPS, dtype=_F32)
    lam = jnp.stack([lam_re, lam_im, jnp.broadcast_to(log_dt[:, None], lam_re.shape)], axis=0)
    lam_row = lam.reshape(3, N_SLABS, SLAB_STATE).transpose(1, 0, 2)
    lam_col = lam_row.transpose(0, 2, 1)

    def b_blockdiag(b):
        bt = b.reshape(*sg, SSM_STATE, SSM_GROUP).transpose(0, 1, 3, 2)
        return (bt[:, :, :, None, :] * eye[None, :, None, :, None]).reshape(N_SLABS, LANES, SLAB_STATE)

    def c_blockdiag(c):
        ct = c.reshape(*sg, SSM_GROUP, SSM_STATE).transpose(0, 1, 3, 2)
        return (ct[:, :, :, None, :] * eye[None, :, None, :, None]).reshape(N_SLABS, SLAB_STATE, LANES)

    bm = jnp.stack([b_blockdiag(b_re), b_blockdiag(b_im)], axis=1)
    cm = jnp.stack([c_blockdiag(c_re), c_blockdiag(c_im)], axis=1)
    d = d_skip.reshape(N_SLABS, 1, LANES)
    slab3 = lambda sl: (sl, 0, 0)
    slab4 = lambda sl: (sl, 0, 0, 0)
    return pl.pallas_call(
        _ssm_prep_kernel,
        out_shape=(jax.ShapeDtypeStruct((N_SLABS, FLAT, FLAT), _BF16),
                   jax.ShapeDtypeStruct((N_SLABS, FLAT, 2 * SLAB_STATE), _BF16),
                   jax.ShapeDtypeStruct((N_SLABS, 2 * SLAB_STATE, FLAT), _BF16),
                   jax.ShapeDtypeStruct((N_SLABS, 2, SLAB_STATE), _F32)),
        grid=(N_SLABS,),
        in_specs=[pl.BlockSpec((1, 3, SLAB_STATE), slab3),
                  pl.BlockSpec((1, SLAB_STATE, 3), slab3),
                  pl.BlockSpec((1, 2, LANES, SLAB_STATE), slab4),
                  pl.BlockSpec((1, 2, SLAB_STATE, LANES), slab4),
                  pl.BlockSpec((1, 1, LANES), slab3)],
        out_specs=(pl.BlockSpec((1, FLAT, FLAT), slab3),
                   pl.BlockSpec((1, FLAT, 2 * SLAB_STATE), slab3),
                   pl.BlockSpec((1, 2 * SLAB_STATE, FLAT), slab3),
                   pl.BlockSpec((1, 2, SLAB_STATE), slab3)),
        compiler_params=pltpu.CompilerParams(
            dimension_semantics=("arbitrary",), vmem_limit_bytes=VMEM_LIMIT_BYTES),
        name="ssm_prep",
    )(lam_row, lam_col, bm, cm, d)


def _ssm_kernel(u_ref, toep_ref, bst_ref, cst_ref, a_ref, y_ref, uflat, s_scr, xc_scr, carry, ytoep):
    nb, tt, _ = u_ref.shape
    nch = tt // CHUNK
    n = nb * nch

    @pl.when(pl.program_id(1) == 0)
    def _():
        carry[...] = jnp.zeros_like(carry)

    for s in range(CHUNK):
        part = u_ref[:, pl.ds(s, nch, stride=CHUNK), :]
        uflat[:, s * LANES:(s + 1) * LANES] = part.reshape(n, LANES).astype(_BF16)

    n_cb = FLAT // MXU_DIM

    def toeplitz(cb):
        kk = (cb + 1) * MXU_DIM
        cols = slice(cb * MXU_DIM, kk)
        ytoep[:, cols] = jnp.dot(uflat[:, :kk], toep_ref[0, :kk, cols], preferred_element_type=_F32)

    for cb in range(n_cb // 2):
        toeplitz(cb)

    nblk = SLAB_STATE // LANES
    loc_all = jnp.dot(uflat[...], bst_ref[0], preferred_element_type=_F32)
    for cb in range(n_cb // 2, n_cb):
        toeplitz(cb)
    for kb in range(2 * nblk):
        s_scr[kb] = loc_all[:, kb * LANES:(kb + 1) * LANES]

    a = a_ref[0]
    are = [jnp.broadcast_to(a[0:1, kb * LANES:(kb + 1) * LANES], (nb, LANES)) for kb in range(nblk)]
    aim = [jnp.broadcast_to(a[1:2, kb * LANES:(kb + 1) * LANES], (nb, LANES)) for kb in range(nblk)]
    xr = [carry[kb] for kb in range(nblk)]
    xi = [carry[nblk + kb] for kb in range(nblk)]
    for j in range(nch):
        rows = pl.ds(j, nb, stride=nch)
        for kb in range(nblk):
            xc_scr[kb, rows, :] = xr[kb]
            xc_scr[nblk + kb, rows, :] = xi[kb]
            nr = are[kb] * xr[kb] - aim[kb] * xi[kb] + s_scr[kb, rows, :]
            ni = are[kb] * xi[kb] + aim[kb] * xr[kb] + s_scr[nblk + kb, rows, :]
            xr[kb], xi[kb] = nr, ni
    for kb in range(nblk):
        carry[kb] = xr[kb]
        carry[nblk + kb] = xi[kb]

    xc = jnp.concatenate([xc_scr[kb] for kb in range(2 * nblk)], axis=1).astype(_BF16)
    for cb in range(n_cb):
        cols = slice(cb * MXU_DIM, (cb + 1) * MXU_DIM)
        y = ytoep[:, cols] + jnp.dot(xc, cst_ref[0, :, cols], preferred_element_type=_F32)
        y = jax.nn.gelu(y)
        for h in range(MXU_DIM // LANES):
            s = cb * (MXU_DIM // LANES) + h
            y_ref[:, pl.ds(s, nch, stride=CHUNK), :] = (
                y[:, h * LANES:(h + 1) * LANES].reshape(nb, nch, LANES))


def _ssm(u3, tables):
    toep, bst, cst, a_chunk = tables
    nb, seq, _ = u3.shape
    tt = SSM_TIME_TILE
    n = nb * (tt // CHUNK)
    return pl.pallas_call(
        _ssm_kernel,
        out_shape=jax.ShapeDtypeStruct(u3.shape, _F32),
        grid=(N_SLABS, seq // tt),
        in_specs=[pl.BlockSpec((nb, tt, LANES), lambda sl, ti: (0, ti, sl)),
                  pl.BlockSpec((1, FLAT, FLAT), lambda sl, ti: (sl, 0, 0)),
                  pl.BlockSpec((1, FLAT, 2 * SLAB_STATE), lambda sl, ti: (sl, 0, 0)),
                  pl.BlockSpec((1, 2 * SLAB_STATE, FLAT), lambda sl, ti: (sl, 0, 0)),
                  pl.BlockSpec((1, 2, SLAB_STATE), lambda sl, ti: (sl, 0, 0))],
        out_specs=pl.BlockSpec((nb, tt, LANES), lambda sl, ti: (0, ti, sl)),
        scratch_shapes=[pltpu.VMEM((n, FLAT), _BF16),
                        pltpu.VMEM((2 * SLAB_STATE // LANES, n, LANES), _F32),
                        pltpu.VMEM((2 * SLAB_STATE // LANES, n, LANES), _F32),
                        pltpu.VMEM((2 * SLAB_STATE // LANES, nb, LANES), _F32),
                        pltpu.VMEM((n, FLAT), _F32)],
        compiler_params=pltpu.CompilerParams(
            dimension_semantics=("arbitrary", "arbitrary"), vmem_limit_bytes=VMEM_LIMIT_BYTES),
        name="ssm",
    )(u3, toep, bst, cst, a_chunk)


def _mix_route_kernel(x_ref, bz_ref, yg_ref, gm_ref, wg_ref, wco_ref, wglu_ref, wout_ref,
                      gf_ref, wr_ref, br_ref,
                      x1_ref, h_ref, ti_ref, tg_ref, rk_ref, cnt_ref, base, merged):
    tm = x_ref.shape[0]

    @pl.when(pl.program_id(0) == 0)
    def _():
        base[...] = jnp.zeros_like(base)

    chains = [slice(r0, r0 + ROW_CHAIN) for r0 in range(0, tm, ROW_CHAIN)]
    hs, picks = [], []

    def route(j):
        logits_tok = jnp.dot(hs[j], wr_ref[...], preferred_element_type=_F32)
        picks.append(_top_k_rows(chains[j], logits_tok, br_ref, ti_ref, tg_ref))

    def rank(j):
        _rank_rows(chains[j], *picks[j], rk_ref, base)

    for j, rows in enumerate(chains):
        hs.append(_mix_rows(rows, x_ref, bz_ref, yg_ref, gm_ref, wg_ref, wco_ref, wglu_ref,
                            wout_ref, gf_ref, x1_ref, h_ref, merged))
        if j >= 1:
            route(j - 1)
        if j >= 2:
            rank(j - 2)
    last = len(chains) - 1
    route(last)
    for j in range(max(last - 1, 0), last + 1):
        rank(j)
    cnt_ref[...] = base[...].astype(jnp.int32)


def _mix_rows(rows, x_ref, bz_ref, yg_ref, gm_ref, wg_ref, wco_ref, wglu_ref, wout_ref,
              gf_ref, x1_ref, h_ref, merged):
    x = x_ref[rows, :]
    xn = _rmsnorm(x, gm_ref[...]).astype(_BF16)
    bz = bz_ref[rows, :]
    yg = yg_ref[rows, :].astype(_BF16)
    for c in range(D_MODEL // MXU_DIM):
        lo = slice(c * MXU_DIM, (c + 1) * MXU_DIM)
        hi = slice(D_MODEL + c * MXU_DIM, D_MODEL + (c + 1) * MXU_DIM)
        gate_a = jnp.dot(xn, wg_ref[:, lo], preferred_element_type=_F32)
        gate_b = jnp.dot(xn, wg_ref[:, hi], preferred_element_type=_F32)
        y_a = jnp.dot(bz, wco_ref[:, lo], preferred_element_type=_F32)
        val = jnp.dot(yg, wglu_ref[:, lo], preferred_element_type=_F32)
        glu_gate = jnp.dot(yg, wglu_ref[:, hi], preferred_element_type=_F32)
        y_b = val * _sigmoid(glu_gate)
        merged[rows, lo] = (_sigmoid(gate_a) * y_a + _sigmoid(gate_b) * y_b).astype(_BF16)
    x1 = x + jnp.dot(merged[rows, :], wout_ref[...], preferred_element_type=_F32)
    x1_ref[rows, :] = x1
    h = _rmsnorm(x1, gf_ref[...])
    h_ref[rows, :] = _pack_bf16_halves(h)
    return h.astype(_BF16)


def _top_k_rows(rows, logits_tok, br_ref, ti_ref, tg_ref):
    tm = rows.stop - rows.start
    logits = jnp.transpose(logits_tok)[:N_EXPERTS, :] + br_ref[...]
    erow = lax.broadcasted_iota(jnp.int32, (N_EXPERTS, tm), 0).astype(_F32)
    neg_inf = jnp.float32(-jnp.inf)
    work = logits
    vals, idxs = [], []
    for _ in range(TOP_K):
        m = jnp.max(work, axis=0, keepdims=True)
        idx = jnp.min(jnp.where(work == m, erow, float(N_EXPERTS)), axis=0, keepdims=True)
        vals.append(m)
        idxs.append(idx)
        work = jnp.where(erow == idx, neg_inf, work)
    exps = [jnp.exp(v - vals[0]) for v in vals]
    denom = exps[0] + exps[1] + exps[2] + exps[3]
    sel = jnp.zeros((N_EXPERTS, tm), _F32)
    for k in range(TOP_K):
        ti_ref[k:k + 1, rows] = idxs[k].astype(jnp.int32)
        tg_ref[k:k + 1, rows] = exps[k] / denom
        tg_ref[TOP_K + k:TOP_K + k + 1, rows] = jnp.zeros((1, tm), _F32)
        sel = sel + (erow == idxs[k]).astype(_F32)
    return sel, idxs


def _rank_rows(rows, sel, idxs, rk_ref, base):
    tm = rows.stop - rows.start
    erow = lax.broadcasted_iota(jnp.int32, (N_EXPERTS, tm), 0).astype(_F32)
    row = lax.broadcasted_iota(jnp.int32, (tm, tm), 0)
    col = lax.broadcasted_iota(jnp.int32, (tm, tm), 1)
    earlier = (row < col).astype(_BF16)
    before = jnp.dot(sel.astype(_BF16), earlier, preferred_element_type=_F32) + base[...]
    for k in range(TOP_K):
        rk = jnp.sum(jnp.where(erow == idxs[k], before, 0.0), axis=0, keepdims=True)
        rk_ref[k:k + 1, rows] = rk.astype(jnp.int32)
    base[...] = base[...] + jnp.sum(sel, axis=1, keepdims=True)


def _mix_route(x2, bz, yg, g_mix, w_gates, w_conv_out, w_glu, w_out, g_ffn, w_router, b_router):
    t = x2.shape[0]
    tm = MIX_TILE
    tok = lambda i: (i, 0)
    tok_lanes = lambda i: (0, i)
    fixed = lambda i: (0, 0)
    weight = lambda shape: pl.BlockSpec(shape, fixed, pipeline_mode=pl.Buffered(1))
    return pl.pallas_call(
        _mix_route_kernel,
        out_shape=(jax.ShapeDtypeStruct((t, D_MODEL), _F32),
                   jax.ShapeDtypeStruct((t, D_MODEL // 2), jnp.uint32),
                   jax.ShapeDtypeStruct((TOP_K, t), jnp.int32),
                   jax.ShapeDtypeStruct((2 * TOP_K, t), _F32),
                   jax.ShapeDtypeStruct((TOP_K, t), jnp.int32),
                   jax.ShapeDtypeStruct((N_EXPERTS, 1), jnp.int32)),
        grid=(t // tm,),
        in_specs=[pl.BlockSpec((tm, D_MODEL), tok),
                  pl.BlockSpec((tm, D_CONV), tok),
                  pl.BlockSpec((tm, D_SSM), tok),
                  pl.BlockSpec((1, D_MODEL), fixed),
                  weight((D_MODEL, 2 * D_MODEL)),
                  weight((D_CONV, D_MODEL)),
                  weight((D_SSM, 2 * D_MODEL)),
                  weight((D_MODEL, D_MODEL)),
                  pl.BlockSpec((1, D_MODEL), fixed),
                  weight((D_MODEL, LANES)),
                  pl.BlockSpec((N_EXPERTS, 1), fixed)],
        out_specs=(pl.BlockSpec((tm, D_MODEL), tok),
                   pl.BlockSpec((tm, D_MODEL // 2), tok),
                   pl.BlockSpec((TOP_K, tm), tok_lanes),
                   pl.BlockSpec((2 * TOP_K, tm), tok_lanes),
                   pl.BlockSpec((TOP_K, tm), tok_lanes),
                   pl.BlockSpec((N_EXPERTS, 1), fixed)),
        scratch_shapes=[pltpu.VMEM((N_EXPERTS, 1), _F32),
                        pltpu.VMEM((tm, D_MODEL), _BF16)],
        compiler_params=pltpu.CompilerParams(
            dimension_semantics=("arbitrary",), vmem_limit_bytes=VMEM_LIMIT_BYTES),
        name="mix_route",
    )(x2, bz, yg, g_mix, w_gates, w_conv_out, w_glu, w_out, g_ffn, w_router, b_router)


def _expert_ffn_kernel(be_ref, nr_ref, slot_ref, next_ref, x_ref, wgu_hbm, bgu_ref, wd_hbm, bd_ref,
                       y_ref, stage_gu, stage_d, wgu_b, wd_b, sem_gu, sem_d):
    b = pl.program_id(0)
    expert = be_ref[b]
    live = nr_ref[b] > 0

    def weight_copies(e, slot):
        return (pltpu.make_async_copy(wgu_hbm.at[e], stage_gu.at[slot], sem_gu.at[slot]),
                pltpu.make_async_copy(wd_hbm.at[e], stage_d.at[slot], sem_d.at[slot]))

    first = live & ((b == 0) | (be_ref[jnp.maximum(b - 1, 0)] != expert))
    slot = slot_ref[b]

    @pl.when(first)
    def _():
        @pl.when(b == 0)
        def _():
            for cp in weight_copies(expert, slot):
                cp.start()

        for cp in weight_copies(expert, slot):
            cp.wait()

        @pl.when(next_ref[b] < N_EXPERTS)
        def _():
            for cp in weight_copies(next_ref[b], 1 - slot):
                cp.start()

    def ffn_rows(n_rows, round_weights):
        xw = x_ref[:n_rows, :]
        valid = lax.broadcasted_iota(jnp.int32, xw.shape, 0) < nr_ref[b]
        x = _unpack_bf16_halves(jnp.where(valid, xw, jnp.uint32(0))).astype(_BF16)
        if round_weights:
            halves = []
            for cols in (slice(0, D_FF), slice(D_FF, 2 * D_FF)):
                wgu_b[:, cols] = stage_gu[slot, :, cols].astype(_BF16)
                halves.append(jnp.dot(x, wgu_b[:, cols], preferred_element_type=_F32)
                              + bgu_ref[0][:, cols])
            wd_b[...] = stage_d[slot].astype(_BF16)
            g, up = halves
        else:
            hgu = jnp.dot(x, wgu_b[...], preferred_element_type=_F32) + bgu_ref[0]
            g, up = hgu[:, :D_FF], hgu[:, D_FF:]
        g = jnp.minimum(g, SWIGLU_LIMIT)
        up = jnp.clip(up, -SWIGLU_LIMIT, SWIGLU_LIMIT)
        act = (up + 1.0) * (g * _sigmoid(SWIGLU_ALPHA * g))
        y = jnp.dot(act.astype(_BF16), wd_b[...], preferred_element_type=_F32) + bd_ref[0]
        y_ref[:n_rows, :] = _pack_bf16_halves(y)
        if n_rows < MOE_BLOCK:
            y_ref[n_rows:, :] = jnp.zeros((MOE_BLOCK - n_rows, y_ref.shape[1]), y_ref.dtype)

    for height in range(FFN_ROW_STEP, MOE_BLOCK + 1, FFN_ROW_STEP):
        in_height = (nr_ref[b] > height - FFN_ROW_STEP) & (nr_ref[b] <= height)
        for round_weights in (True, False):
            @pl.when(in_height & (first if round_weights else jnp.logical_not(first)))
            def _():
                ffn_rows(height, round_weights)

    @pl.when(jnp.logical_not(live))
    def _():
        y_ref[...] = jnp.zeros_like(y_ref)


def _expert_ffn(block_e, block_rows, block_slot, block_next, x_rows, w_gate_up, b_gate_up, w_down,
                b_down):
    n_rows = x_rows.shape[0]
    n_blocks = n_rows // MOE_BLOCK

    def bias_map(b, be, nr, sl, nx):
        return (be[b], 0, 0)

    def row_map(b, be, nr, sl, nx):
        return (b, 0)

    grid_spec = pltpu.PrefetchScalarGridSpec(
        num_scalar_prefetch=4,
        grid=(n_blocks,),
        in_specs=[pl.BlockSpec((MOE_BLOCK, D_MODEL // 2), row_map),
                  pl.BlockSpec(memory_space=pl.ANY),
                  pl.BlockSpec((1, 1, 2 * D_FF), bias_map),
                  pl.BlockSpec(memory_space=pl.ANY),
                  pl.BlockSpec((1, 1, D_MODEL), bias_map)],
        out_specs=pl.BlockSpec((MOE_BLOCK, D_MODEL // 2), row_map),
        scratch_shapes=[pltpu.VMEM((2, D_MODEL, 2 * D_FF), _F32),
                        pltpu.VMEM((2, D_FF, D_MODEL), _F32),
                        pltpu.VMEM((D_MODEL, 2 * D_FF), _BF16),
                        pltpu.VMEM((D_FF, D_MODEL), _BF16),
                        pltpu.SemaphoreType.DMA((2,)),
                        pltpu.SemaphoreType.DMA((2,))],
    )
    return pl.pallas_call(
        _expert_ffn_kernel,
        out_shape=jax.ShapeDtypeStruct((n_rows, D_MODEL // 2), jnp.uint32),
        grid_spec=grid_spec,
        compiler_params=pltpu.CompilerParams(
            dimension_semantics=("arbitrary",), vmem_limit_bytes=VMEM_LIMIT_BYTES),
        name="expert_ffn",
    )(block_e, block_rows, block_slot, block_next, x_rows, w_gate_up, b_gate_up, w_down, b_down)


def _sc_workers():
    info = plsc.get_sparse_core_info()
    return info.num_cores, info.num_cores * info.num_subcores


def _dispatch(h_packed, dest_flat, n_rows):
    t, width = h_packed.shape
    n_cores, n_workers = _sc_workers()
    n_chunks = t // (n_workers * SC_ROWS)
    chunks_per_k = t // SC_ROWS
    assert n_chunks % 2 == 0

    @functools.partial(
        pl.kernel, mesh=plsc.VectorSubcoreMesh(core_axis_name="c", subcore_axis_name="s"),
        out_type=jax.ShapeDtypeStruct((n_rows, width), h_packed.dtype),
        scratch_types=[pltpu.VMEM((TOP_K, n_chunks, SC_ROWS), jnp.int32),
                       pltpu.VMEM((2, SC_ROWS, width), h_packed.dtype),
                       pltpu.SemaphoreType.DMA((2,)),
                       pltpu.SemaphoreType.DMA((2,))])
    def scatter_rows(h_hbm, dest_hbm, out_hbm, idx_v, buf, lsem, ssem):
        wid = lax.axis_index("s") * n_cores + lax.axis_index("c")
        c0 = wid * n_chunks
        for k in range(TOP_K):
            pltpu.sync_copy(dest_hbm.at[pl.ds(k * chunks_per_k + c0, n_chunks)], idx_v.at[k])

        def load(c, b):
            return pltpu.make_async_copy(h_hbm.at[pl.ds((c0 + c) * SC_ROWS, SC_ROWS)], buf.at[b],
                                         lsem.at[b])

        def scatters(c, b):
            return [pltpu.make_async_copy(buf.at[b], out_hbm.at[idx_v.at[k, c]], ssem.at[b])
                    for k in range(TOP_K)]

        load(0, 0).start()

        @pl.loop(0, n_chunks, step=2)
        def _(ci):
            for b in range(2):
                c = ci + b

                @pl.when(c >= 1)
                def _():
                    for cp in scatters(c - 1, 1 - b):
                        cp.wait()

                @pl.when(c + 1 < n_chunks)
                def _():
                    load(c + 1, 1 - b).start()

                load(c, b).wait()
                for cp in scatters(c, b):
                    cp.start()

        for cp in scatters(n_chunks - 1, 1):
            cp.wait()

    return scatter_rows(h_packed, dest_flat.reshape(TOP_K * chunks_per_k, SC_ROWS))


def _collect(y_rows, dest_flat):
    n_idx = dest_flat.shape[0]
    width = y_rows.shape[1]
    rows, depth = COLLECT_ROWS, COLLECT_RING
    n_cores, n_workers = _sc_workers()
    n_chunks = n_idx // (n_workers * rows)
    assert n_chunks % depth == 0

    @functools.partial(
        pl.kernel, mesh=plsc.VectorSubcoreMesh(core_axis_name="c", subcore_axis_name="s"),
        out_type=jax.ShapeDtypeStruct((n_idx, width), y_rows.dtype),
        scratch_types=[pltpu.VMEM((n_chunks, rows), jnp.int32),
                       pltpu.VMEM((depth, rows, width), y_rows.dtype),
                       pltpu.SemaphoreType.DMA((depth,)),
                       pltpu.SemaphoreType.DMA((depth,))])
    def gather_rows(y_hbm, dest_hbm, out_hbm, idx_v, buf, gsem, wsem):
        wid = lax.axis_index("s") * n_cores + lax.axis_index("c")
        c0 = wid * n_chunks
        pltpu.sync_copy(dest_hbm.at[pl.ds(c0, n_chunks)], idx_v)

        def gather(c, b):
            return pltpu.make_async_copy(y_hbm.at[idx_v.at[c]], buf.at[b], gsem.at[b])

        def write(c, b):
            return pltpu.make_async_copy(buf.at[b], out_hbm.at[pl.ds((c0 + c) * rows, rows)],
                                         wsem.at[b])

        for c in range(depth - 1):
            gather(c, c).start()

        @pl.loop(0, n_chunks, step=depth)
        def _(ci):
            for b in range(depth):
                c = ci + b
                gather(c, b).wait()
                write(c, b).start()
                prev = (b - 1) % depth

                @pl.when(c >= 1)
                def _():
                    write(c - 1, prev).wait()

                @pl.when(c + depth - 1 < n_chunks)
                def _():
                    gather(c + depth - 1, prev).start()

        write(n_chunks - 1, (n_chunks - 1) % depth).wait()

    return gather_rows(y_rows, dest_flat.reshape(n_idx // rows, rows))


def _combine_kernel(x1_ref, ya_ref, tg_ref, g_ref, o_ref):
    acc = x1_ref[...]
    tg = jnp.transpose(tg_ref[...])
    for k in range(TOP_K):
        acc = acc + tg[:, k:k + 1] * _unpack_bf16_halves(ya_ref[k])
    o_ref[...] = _rmsnorm(acc, g_ref[...])


def _combine(x1, y_assign, top_g, g_final):
    t = x1.shape[0]
    tm = TOKEN_TILE
    return pl.pallas_call(
        _combine_kernel,
        out_shape=jax.ShapeDtypeStruct((t, D_MODEL), _F32),
        grid=(t // tm,),
        in_specs=[pl.BlockSpec((tm, D_MODEL), lambda i: (i, 0)),
                  pl.BlockSpec((TOP_K, tm, D_MODEL // 2), lambda i: (0, i, 0)),
                  pl.BlockSpec((2 * TOP_K, tm), lambda i: (0, i)),
                  pl.BlockSpec((1, D_MODEL), lambda i: (0, 0))],
        out_specs=pl.BlockSpec((tm, D_MODEL), lambda i: (i, 0)),
        compiler_params=pltpu.CompilerParams(
            dimension_semantics=("arbitrary",), vmem_limit_bytes=VMEM_LIMIT_BYTES),
        name="combine",
    )(x1, y_assign, top_g, g_final)


def _block_plan(counts, n_blocks):
    padded = ((counts + MOE_BLOCK - 1) // MOE_BLOCK) * MOE_BLOCK
    pad_end = jnp.cumsum(padded)
    pad_start = pad_end - padded
    block_start = (jnp.arange(n_blocks, dtype=jnp.int32) * MOE_BLOCK)[:, None]
    eidx = jnp.arange(N_EXPERTS, dtype=jnp.int32)
    owns = (pad_start[None, :] <= block_start) & (block_start < pad_end[None, :])
    has_blocks = (padded > 0).astype(jnp.int32)
    ordinal = jnp.cumsum(has_blocks) - has_blocks
    later = (eidx[None, :] > eidx[:, None]) & (padded[None, :] > 0)
    next_expert = jnp.min(jnp.where(later, eidx[None, :], N_EXPERTS), axis=1)

    def per_block(per_expert):
        return jnp.sum(jnp.where(owns, per_expert, 0), axis=1).astype(jnp.int32)

    block_e = per_block(eidx[None, :])
    block_rows = per_block(jnp.clip((pad_start + counts)[None, :] - block_start, 0, MOE_BLOCK))
    block_slot = per_block((ordinal % 2)[None, :])
    block_next = per_block(next_expert[None, :])
    return pad_start, (block_e, block_rows, block_slot, block_next)


def kernel(x, norm_mix_g, w_in, conv_w, w_conv_out, ssm_lam_re, ssm_lam_im, ssm_log_dt, ssm_b_re, ssm_b_im, ssm_c_re, ssm_c_im, ssm_d, w_glu, w_out, norm_ffn_g, w_router, b_router, w_gate_up, b_gate_up, w_down, b_down, norm_f_g):
    bsz, seq, d = x.shape
    t = bsz * seq
    x2 = x.reshape(t, d)
    assert seq % TOKEN_TILE == 0 and seq % SSM_TIME_TILE == 0 and w_in.shape[0] == 1

    w_in_b = w_in[0].astype(_BF16)
    n_bcvu = 3 * D_CONV + D_SSM
    g_mix = norm_mix_g[0].reshape(1, d)

    bz, u = _in_proj(x2, g_mix, w_in_b[:, :n_bcvu], conv_w[0], seq)

    tables = _ssm_tables(ssm_lam_re[0], ssm_lam_im[0], ssm_log_dt[0], ssm_b_re[0], ssm_b_im[0],
                         ssm_c_re[0], ssm_c_im[0], ssm_d[0])
    yg = _ssm(u.reshape(bsz, seq, D_SSM), tables).reshape(t, D_SSM)

    x1, h_packed, top_i, top_g, rank, counts = _mix_route(
        x2, bz, yg, g_mix, w_in_b[:, n_bcvu:], w_conv_out[0].astype(_BF16),
        w_glu[0].astype(_BF16), w_out[0].astype(_BF16), norm_ffn_g[0].reshape(1, d),
        jnp.pad(w_router[0], ((0, 0), (0, LANES - N_EXPERTS))).astype(_BF16),
        b_router[0].reshape(N_EXPERTS, 1))

    n_rows = t * TOP_K + N_EXPERTS * MOE_BLOCK
    pad_start, block_plan = _block_plan(counts[:, 0], n_rows // MOE_BLOCK)
    expert_ids = jnp.arange(N_EXPERTS, dtype=jnp.int32)[:, None, None]
    row_start = jnp.sum(jnp.where(top_i[None] == expert_ids, pad_start[:, None, None], 0), axis=0)
    dest = (row_start + rank).reshape(TOP_K * t)

    x_rows = _dispatch(h_packed, dest, n_rows)
    y_rows = _expert_ffn(*block_plan, x_rows, w_gate_up[0],
                         b_gate_up[0].reshape(N_EXPERTS, 1, 2 * D_FF), w_down[0],
                         b_down[0].reshape(N_EXPERTS, 1, D_MODEL))
    y_assign = _collect(y_rows, dest).reshape(TOP_K, t, D_MODEL // 2)
    out = _combine(x1, y_assign, top_g, norm_f_g.reshape(1, d))
    return out.reshape(bsz, seq, d)
```

```python
import functools

import jax
import jax.numpy as jnp
from jax import lax
from jax.experimental import pallas as pl
from jax.experimental.pallas import tpu as pltpu
from jax.experimental.pallas import tpu_sc as plsc

D_MODEL = 1024
D_CONV = 512
CONV_WIDTH = 3
D_SSM = 512
SSM_GROUP = 16
N_SSM_GROUPS = 32
SSM_STATE = 64
N_EXPERTS = 32
TOP_K = 4
D_FF = 1024
SWIGLU_LIMIT = 7.0
SWIGLU_ALPHA = 1.702
RMS_EPS = 1e-6

LANES = 128
SUBLANES = 8
MXU_DIM = 256
CHUNK = 8
SLAB_GROUPS = LANES // SSM_GROUP
N_SLABS = N_SSM_GROUPS // SLAB_GROUPS
SLAB_STATE = SLAB_GROUPS * SSM_STATE
FLAT = CHUNK * LANES
SSM_TIME_TILE = 128
TOKEN_TILE = 1024
IN_TILE = 2048
IN_CHAIN = 512
MIX_TILE = 1024
ROW_CHAIN = 256
MOE_BLOCK = 1024
FFN_ROW_STEP = 256
SC_ROWS = 64
COLLECT_ROWS = 32
COLLECT_RING = 4
VMEM_LIMIT_BYTES = 56 * 1024 * 1024

_BF16 = jnp.bfloat16
_F32 = jnp.float32


def _rmsnorm(xf, g):
    return xf * lax.rsqrt(jnp.mean(xf * xf, axis=-1, keepdims=True) + RMS_EPS) * g


def _sigmoid(v):
    return 0.5 * jnp.tanh(0.5 * v) + 0.5


def _pack_bf16_halves(v):
    n = v.shape[1] // 2
    bits = pltpu.bitcast(v.astype(_BF16).astype(_F32), jnp.uint32)
    return (bits[:, :n] >> 16) | (bits[:, n:] & jnp.uint32(0xFFFF0000))


def _unpack_bf16_halves(w):
    return jnp.concatenate([pltpu.bitcast(w << 16, _F32),
                            pltpu.bitcast(w & jnp.uint32(0xFFFF0000), _F32)], axis=1)


def _in_proj_kernel(tiles_per_seq, x_ref, g_ref, w_ref, cw_ref, bz_ref, u_ref, hbuf):
    tm = x_ref.shape[0]
    halo = SUBLANES

    @pl.when(pl.program_id(0) % tiles_per_seq == 0)
    def _():
        hbuf[0:halo, :] = jnp.zeros((halo, D_CONV), _F32)

    cw = cw_ref[...]
    for r0 in range(0, tm, IN_CHAIN):
        rows = slice(r0, r0 + IN_CHAIN)
        xn = _rmsnorm(x_ref[rows, :], g_ref[...]).astype(_BF16)
        cv = jnp.dot(xn, w_ref[:, D_CONV:3 * D_CONV], preferred_element_type=_F32)
        hbuf[halo + r0:halo + r0 + IN_CHAIN, :] = cv[:, :D_CONV] * cv[:, D_CONV:]
        u_ref[rows, :] = jnp.dot(xn, w_ref[:, 3 * D_CONV:], preferred_element_type=_F32)
        z = cw[CONV_WIDTH - 1:CONV_WIDTH, :] * hbuf[halo + r0:halo + r0 + IN_CHAIN, :]
        for lag in range(1, CONV_WIDTH):
            z = z + (cw[CONV_WIDTH - 1 - lag:CONV_WIDTH - lag, :]
                     * hbuf[halo + r0 - lag:halo + r0 - lag + IN_CHAIN, :])
        b_gate = jnp.dot(xn, w_ref[:, :D_CONV], preferred_element_type=_F32)
        bz_ref[rows, :] = (b_gate * z).astype(_BF16)
    hbuf[0:halo, :] = hbuf[tm:tm + halo, :]


def _in_proj(x2, g, w_bcvu, conv_w, seq):
    t = x2.shape[0]
    tm = IN_TILE
    assert SUBLANES >= CONV_WIDTH - 1 and seq % tm == 0
    return pl.pallas_call(
        functools.partial(_in_proj_kernel, seq // tm),
        out_shape=(jax.ShapeDtypeStruct((t, D_CONV), _BF16),
                   jax.ShapeDtypeStruct((t, D_SSM), _F32)),
        grid=(t // tm,),
        in_specs=[pl.BlockSpec((tm, D_MODEL), lambda i: (i, 0)),
                  pl.BlockSpec((1, D_MODEL), lambda i: (0, 0)),
                  pl.BlockSpec((D_MODEL, 3 * D_CONV + D_SSM), lambda i: (0, 0)),
                  pl.BlockSpec((CONV_WIDTH, D_CONV), lambda i: (0, 0))],
        out_specs=(pl.BlockSpec((tm, D_CONV), lambda i: (i, 0)),
                   pl.BlockSpec((tm, D_SSM), lambda i: (i, 0))),
        scratch_shapes=[pltpu.VMEM((tm + SUBLANES, D_CONV), _F32)],
        compiler_params=pltpu.CompilerParams(
            dimension_semantics=("arbitrary",), vmem_limit_bytes=VMEM_LIMIT_BYTES),
        name="in_proj",
    )(x2, g, w_bcvu, conv_w)


def _ssm_prep_kernel(lr_ref, lc_ref, bm_ref, cm_ref, d_ref, toep_ref, bst_ref, cst_ref, a_ref):
    def discretise(lre, lim, log_dt):
        dt = jnp.exp(log_dt)
        mag = jnp.exp(lre * dt)
        return mag * jnp.cos(lim * dt), mag * jnp.sin(lim * dt)

    def powers(are, aim):
        pre, pim = [jnp.ones_like(are)], [jnp.zeros_like(are)]
        for _ in range(CHUNK):
            pre, pim = (pre + [pre[-1] * are - pim[-1] * aim],
                        pim + [pre[-1] * aim + pim[-1] * are])
        return pre, pim

    lr = lr_ref[0]
    lre, lim = lr[0:1, :], lr[1:2, :]
    are, aim = discretise(lre, lim, lr[2:3, :])
    pre, pim = powers(are, aim)
    den = lre * lre + lim * lim
    q_re = ((are - 1.0) * lre + aim * lim) / den
    q_im = (aim * lre - (are - 1.0) * lim) / den
    bb_re = q_re * bm_ref[0, 0] - q_im * bm_ref[0, 1]
    bb_im = q_re * bm_ref[0, 1] + q_im * bm_ref[0, 0]
    cm_re, cm_im = cm_ref[0, 0], cm_ref[0, 1]

    def split_bf16(v):
        v_hi = v.astype(_BF16)
        return v_hi, (v - v_hi.astype(_F32)).astype(_BF16)

    c_hi, c_lo = split_bf16(jnp.concatenate([cm_re, -cm_im], axis=0))
    kblk = []
    for k in range(CHUNK):
        ab_re = bb_re * pre[k] - bb_im * pim[k]
        ab_im = bb_re * pim[k] + bb_im * pre[k]
        rows = slice((CHUNK - 1 - k) * LANES, (CHUNK - k) * LANES)
        bst_ref[0, rows, :SLAB_STATE] = ab_re.astype(_BF16)
        bst_ref[0, rows, SLAB_STATE:] = ab_im.astype(_BF16)
        ab_hi, ab_lo = split_bf16(jnp.concatenate([ab_re, ab_im], axis=1))
        kblk.append(jnp.dot(ab_hi, c_hi, preferred_element_type=_F32)
                    + (jnp.dot(ab_lo, c_hi, preferred_element_type=_F32)
                       + jnp.dot(ab_hi, c_lo, preferred_element_type=_F32)))
    r = lax.broadcasted_iota(jnp.int32, (LANES, LANES), 0)
    c = lax.broadcasted_iota(jnp.int32, (LANES, LANES), 1)
    kblk[0] = kblk[0] + jnp.where(r == c, jnp.broadcast_to(d_ref[0], (LANES, LANES)), 0.0)
    kblk = [kb.astype(_BF16) for kb in kblk]
    zeros = jnp.zeros((LANES, LANES), _BF16)
    for sp in range(CHUNK):
        for s in range(CHUNK):
            toep_ref[0, sp * LANES:(sp + 1) * LANES, s * LANES:(s + 1) * LANES] = (
                kblk[s - sp] if s >= sp else zeros)

    lc = lc_ref[0]
    cre, cim = discretise(lc[:, 0:1], lc[:, 1:2], lc[:, 2:3])
    qre, qim = powers(cre, cim)
    for s in range(CHUNK):
        cols = slice(s * LANES, (s + 1) * LANES)
        cst_ref[0, :SLAB_STATE, cols] = (cm_re * qre[s + 1] - cm_im * qim[s + 1]).astype(_BF16)
        cst_ref[0, SLAB_STATE:, cols] = (-(cm_re * qim[s + 1] + cm_im * qre[s + 1])).astype(_BF16)
    a_ref[0, 0:1, :] = pre[CHUNK]
    a_ref[0, 1:2, :] = pim[CHUNK]


def _ssm_tables(lam_re, lam_im, log_dt, b_re, b_im, c_re, c_im, d_skip):
    sg = (N_SLABS, SLAB_GROUPS)
    eye = jnp.eye(SLAB_GROUPS, dtype=_F32)
    lam = jnp.stack([lam_re, lam_im, jnp.broadcast_to(log_dt[:, None], lam_re.shape)], axis=0)
    lam_row = lam.reshape(3, N_SLABS, SLAB_STATE).transpose(1, 0, 2)
    lam_col = lam_row.transpose(0, 2, 1)

    def b_blockdiag(b):
        bt = b.reshape(*sg, SSM_STATE, SSM_GROUP).transpose(0, 1, 3, 2)
        return (bt[:, :, :, None, :] * eye[None, :, None, :, None]).reshape(N_SLABS, LANES, SLAB_STATE)

    def c_blockdiag(c):
        ct = c.reshape(*sg, SSM_GROUP, SSM_STATE).transpose(0, 1, 3, 2)
        return (ct[:, :, :, None, :] * eye[None, :, None, :, None]).reshape(N_SLABS, SLAB_STATE, LANES)

    bm = jnp.stack([b_blockdiag(b_re), b_blockdiag(b_im)], axis=1)
    cm = jnp.stack([c_blockdiag(c_re), c_blockdiag(c_im)], axis=1)
    d = d_skip.reshape(N_SLABS, 1, LANES)
    slab3 = lambda sl: (sl, 0, 0)
    slab4 = lambda sl: (sl, 0, 0, 0)
    return pl.pallas_call(
        _ssm_prep_kernel,
        out_shape=(jax.ShapeDtypeStruct((N_SLABS, FLAT, FLAT), _BF16),
                   jax.ShapeDtypeStruct((N_SLABS, FLAT, 2 * SLAB_STATE), _BF16),
                   jax.ShapeDtypeStruct((N_SLABS, 2 * SLAB_STATE, FLAT), _BF16),
                   jax.ShapeDtypeStruct((N_SLABS, 2, SLAB_STATE), _F32)),
        grid=(N_SLABS,),
        in_specs=[pl.BlockSpec((1, 3, SLAB_STATE), slab3),
                  pl.BlockSpec((1, SLAB_STATE, 3), slab3),
                  pl.BlockSpec((1, 2, LANES, SLAB_STATE), slab4),
                  pl.BlockSpec((1, 2, SLAB_STATE, LANES), slab4),
                  pl.BlockSpec((1, 1, LANES), slab3)],
        out_specs=(pl.BlockSpec((1, FLAT, FLAT), slab3),
                   pl.BlockSpec((1, FLAT, 2 * SLAB_STATE), slab3),
                   pl.BlockSpec((1, 2 * SLAB_STATE, FLAT), slab3),
                   pl.BlockSpec((1, 2, SLAB_STATE), slab3)),
        compiler_params=pltpu.CompilerParams(
            dimension_semantics=("arbitrary",), vmem_limit_bytes=VMEM_LIMIT_BYTES),
        name="ssm_prep",
    )(lam_row, lam_col, bm, cm, d)


def _ssm_kernel(u_ref, toep_ref, bst_ref, cst_ref, a_ref, y_ref, uflat, s_scr, xc_scr, carry, ytoep):
    nb, tt, _ = u_ref.shape
    nch = tt // CHUNK
    n = nb * nch

    @pl.when(pl.program_id(1) == 0)
    def _():
        carry[...] = jnp.zeros_like(carry)

    for s in range(CHUNK):
        part = u_ref[:, pl.ds(s, nch, stride=CHUNK), :]
        uflat[:, s * LANES:(s + 1) * LANES] = part.reshape(n, LANES).astype(_BF16)

    n_cb = FLAT // MXU_DIM

    def toeplitz(cb):
        kk = (cb + 1) * MXU_DIM
        cols = slice(cb * MXU_DIM, kk)
        ytoep[:, cols] = jnp.dot(uflat[:, :kk], toep_ref[0, :kk, cols], preferred_element_type=_F32)

    nblk = SLAB_STATE // LANES
    loc_all = jnp.dot(uflat[...], bst_ref[0], preferred_element_type=_F32)
    for cb in range(n_cb):
        toeplitz(cb)
    for kb in range(2 * nblk):
        s_scr[kb] = loc_all[:, kb * LANES:(kb + 1) * LANES]

    a = a_ref[0]
    are = [jnp.broadcast_to(a[0:1, kb * LANES:(kb + 1) * LANES], (nb, LANES)) for kb in range(nblk)]
    aim = [jnp.broadcast_to(a[1:2, kb * LANES:(kb + 1) * LANES], (nb, LANES)) for kb in range(nblk)]
    xr = [carry[kb] for kb in range(nblk)]
    xi = [carry[nblk + kb] for kb in range(nblk)]
    for j in range(nch):
        rows = pl.ds(j, nb, stride=nch)
        for kb in range(nblk):
            xc_scr[kb, rows, :] = xr[kb]
            xc_scr[nblk + kb, rows, :] = xi[kb]
            nr = are[kb] * xr[kb] - aim[kb] * xi[kb] + s_scr[kb, rows, :]
            ni = are[kb] * xi[kb] + aim[kb] * xr[kb] + s_scr[nblk + kb, rows, :]
            xr[kb], xi[kb] = nr, ni
    for kb in range(nblk):
        carry[kb] = xr[kb]
        carry[nblk + kb] = xi[kb]

    xc = jnp.concatenate([xc_scr[kb] for kb in range(2 * nblk)], axis=1).astype(_BF16)
    for cb in range(n_cb):
        cols = slice(cb * MXU_DIM, (cb + 1) * MXU_DIM)
        y = ytoep[:, cols] + jnp.dot(xc, cst_ref[0, :, cols], preferred_element_type=_F32)
        y = jax.nn.gelu(y)
        for h in range(MXU_DIM // LANES):
            s = cb * (MXU_DIM // LANES) + h
            y_ref[:, pl.ds(s, nch, stride=CHUNK), :] = (
                y[:, h * LANES:(h + 1) * LANES].reshape(nb, nch, LANES))


def _ssm(u3, tables):
    toep, bst, cst, a_chunk = tables
    nb, seq, _ = u3.shape
    tt = SSM_TIME_TILE
    n = nb * (tt // CHUNK)
    return pl.pallas_call(
        _ssm_kernel,
        out_shape=jax.ShapeDtypeStruct(u3.shape, _F32),
        grid=(N_SLABS, seq // tt),
        in_specs=[pl.BlockSpec((nb, tt, LANES), lambda sl, ti: (0, ti, sl)),
                  pl.BlockSpec((1, FLAT, FLAT), lambda sl, ti: (sl, 0, 0)),
                  pl.BlockSpec((1, FLAT, 2 * SLAB_STATE), lambda sl, ti: (sl, 0, 0)),
                  pl.BlockSpec((1, 2 * SLAB_STATE, FLAT), lambda sl, ti: (sl, 0, 0)),
                  pl.BlockSpec((1, 2, SLAB_STATE), lambda sl, ti: (sl, 0, 0))],
        out_specs=pl.BlockSpec((nb, tt, LANES), lambda sl, ti: (0, ti, sl)),
        scratch_shapes=[pltpu.VMEM((n, FLAT), _BF16),
                        pltpu.VMEM((2 * SLAB_STATE // LANES, n, LANES), _F32),
                        pltpu.VMEM((2 * SLAB_STATE // LANES, n, LANES), _F32),
                        pltpu.VMEM((2 * SLAB_STATE // LANES, nb, LANES), _F32),
                        pltpu.VMEM((n, FLAT), _F32)],
        compiler_params=pltpu.CompilerParams(
            dimension_semantics=("arbitrary", "arbitrary"), vmem_limit_bytes=VMEM_LIMIT_BYTES),
        name="ssm",
    )(u3, toep, bst, cst, a_chunk)


def _mix_route_kernel(x_ref, bz_ref, yg_ref, gm_ref, wg_ref, wco_ref, wglu_ref, wout_ref,
                      gf_ref, wr_ref, br_ref,
                      x1_ref, h_ref, ti_ref, tg_ref, rk_ref, cnt_ref, base, merged):
    tm = x_ref.shape[0]

    @pl.when(pl.program_id(0) == 0)
    def _():
        base[...] = jnp.zeros_like(base)

    chains = [slice(r0, r0 + ROW_CHAIN) for r0 in range(0, tm, ROW_CHAIN)]
    hs, picks = [], []

    def route(j):
        logits_tok = jnp.dot(hs[j], wr_ref[...], preferred_element_type=_F32)
        picks.append(_top_k_rows(chains[j], logits_tok, br_ref, ti_ref, tg_ref))

    def rank(j):
        _rank_rows(chains[j], *picks[j], rk_ref, base)

    for j, rows in enumerate(chains):
        hs.append(_mix_rows(rows, x_ref, bz_ref, yg_ref, gm_ref, wg_ref, wco_ref, wglu_ref,
                            wout_ref, gf_ref, x1_ref, h_ref, merged))
        if j >= 1:
            route(j - 1)
        if j >= 2:
            rank(j - 2)
    last = len(chains) - 1
    route(last)
    for j in range(max(last - 1, 0), last + 1):
        rank(j)
    cnt_ref[...] = base[...].astype(jnp.int32)


def _mix_rows(rows, x_ref, bz_ref, yg_ref, gm_ref, wg_ref, wco_ref, wglu_ref, wout_ref,
              gf_ref, x1_ref, h_ref, merged):
    x = x_ref[rows, :]
    xn = _rmsnorm(x, gm_ref[...]).astype(_BF16)
    bz = bz_ref[rows, :]
    yg = yg_ref[rows, :].astype(_BF16)
    for c in range(D_MODEL // MXU_DIM):
        lo = slice(c * MXU_DIM, (c + 1) * MXU_DIM)
        hi = slice(D_MODEL + c * MXU_DIM, D_MODEL + (c + 1) * MXU_DIM)
        gate_a = jnp.dot(xn, wg_ref[:, lo], preferred_element_type=_F32)
        gate_b = jnp.dot(xn, wg_ref[:, hi], preferred_element_type=_F32)
        y_a = jnp.dot(bz, wco_ref[:, lo], preferred_element_type=_F32)
        val = jnp.dot(yg, wglu_ref[:, lo], preferred_element_type=_F32)
        glu_gate = jnp.dot(yg, wglu_ref[:, hi], preferred_element_type=_F32)
        y_b = val * _sigmoid(glu_gate)
        merged[rows, lo] = (_sigmoid(gate_a) * y_a + _sigmoid(gate_b) * y_b).astype(_BF16)
    x1 = x + jnp.dot(merged[rows, :], wout_ref[...], preferred_element_type=_F32)
    x1_ref[rows, :] = x1
    h = _rmsnorm(x1, gf_ref[...])
    h_ref[rows, :] = _pack_bf16_halves(h)
    return h.astype(_BF16)


def _top_k_rows(rows, logits_tok, br_ref, ti_ref, tg_ref):
    tm = rows.stop - rows.start
    logits = jnp.transpose(logits_tok)[:N_EXPERTS, :] + br_ref[...]
    erow = lax.broadcasted_iota(jnp.int32, (N_EXPERTS, tm), 0).astype(_F32)
    neg_inf = jnp.float32(-jnp.inf)
    work = logits
    vals, idxs = [], []
    for _ in range(TOP_K):
        m = jnp.max(work, axis=0, keepdims=True)
        idx = jnp.min(jnp.where(work == m, erow, float(N_EXPERTS)), axis=0, keepdims=True)
        vals.append(m)
        idxs.append(idx)
        work = jnp.where(erow == idx, neg_inf, work)
    exps = [jnp.exp(v - vals[0]) for v in vals]
    denom = exps[0] + exps[1] + exps[2] + exps[3]
    sel = jnp.zeros((N_EXPERTS, tm), _F32)
    for k in range(TOP_K):
        ti_ref[k:k + 1, rows] = idxs[k].astype(jnp.int32)
        tg_ref[k:k + 1, rows] = exps[k] / denom
        tg_ref[TOP_K + k:TOP_K + k + 1, rows] = jnp.zeros((1, tm), _F32)
        sel = sel + (erow == idxs[k]).astype(_F32)
    return sel, idxs


def _rank_rows(rows, sel, idxs, rk_ref, base):
    tm = rows.stop - rows.start
    erow = lax.broadcasted_iota(jnp.int32, (N_EXPERTS, tm), 0).astype(_F32)
    row = lax.broadcasted_iota(jnp.int32, (tm, tm), 0)
    col = lax.broadcasted_iota(jnp.int32, (tm, tm), 1)
    earlier = (row < col).astype(_BF16)
    before = jnp.dot(sel.astype(_BF16), earlier, preferred_element_type=_F32) + base[...]
    for k in range(TOP_K):
        rk = jnp.sum(jnp.where(erow == idxs[k], before, 0.0), axis=0, keepdims=True)
        rk_ref[k:k + 1, rows] = rk.astype(jnp.int32)
    base[...] = base[...] + jnp.sum(sel, axis=1, keepdims=True)


def _mix_route(x2, bz, yg, g_mix, w_gates, w_conv_out, w_glu, w_out, g_ffn, w_router, b_router):
    t = x2.shape[0]
    tm = MIX_TILE
    tok = lambda i: (i, 0)
    tok_lanes = lambda i: (0, i)
    fixed = lambda i: (0, 0)
    weight = lambda shape: pl.BlockSpec(shape, fixed, pipeline_mode=pl.Buffered(1))
    return pl.pallas_call(
        _mix_route_kernel,
        out_shape=(jax.ShapeDtypeStruct((t, D_MODEL), _F32),
                   jax.ShapeDtypeStruct((t, D_MODEL // 2), jnp.uint32),
                   jax.ShapeDtypeStruct((TOP_K, t), jnp.int32),
                   jax.ShapeDtypeStruct((2 * TOP_K, t), _F32),
                   jax.ShapeDtypeStruct((TOP_K, t), jnp.int32),
                   jax.ShapeDtypeStruct((N_EXPERTS, 1), jnp.int32)),
        grid=(t // tm,),
        in_specs=[pl.BlockSpec((tm, D_MODEL), tok),
                  pl.BlockSpec((tm, D_CONV), tok),
                  pl.BlockSpec((tm, D_SSM), tok),
                  pl.BlockSpec((1, D_MODEL), fixed),
                  weight((D_MODEL, 2 * D_MODEL)),
                  weight((D_CONV, D_MODEL)),
                  weight((D_SSM, 2 * D_MODEL)),
                  weight((D_MODEL, D_MODEL)),
                  pl.BlockSpec((1, D_MODEL), fixed),
                  weight((D_MODEL, LANES)),
                  pl.BlockSpec((N_EXPERTS, 1), fixed)],
        out_specs=(pl.BlockSpec((tm, D_MODEL), tok),
                   pl.BlockSpec((tm, D_MODEL // 2), tok),
                   pl.BlockSpec((TOP_K, tm), tok_lanes),
                   pl.BlockSpec((2 * TOP_K, tm), tok_lanes),
                   pl.BlockSpec((TOP_K, tm), tok_lanes),
                   pl.BlockSpec((N_EXPERTS, 1), fixed)),
        scratch_shapes=[pltpu.VMEM((N_EXPERTS, 1), _F32),
                        pltpu.VMEM((tm, D_MODEL), _BF16)],
        compiler_params=pltpu.CompilerParams(
            dimension_semantics=("arbitrary",), vmem_limit_bytes=VMEM_LIMIT_BYTES),
        name="mix_route",
    )(x2, bz, yg, g_mix, w_gates, w_conv_out, w_glu, w_out, g_ffn, w_router, b_router)


def _expert_ffn_kernel(be_ref, nr_ref, slot_ref, next_ref, x_ref, wgu_hbm, bgu_ref, wd_hbm, bd_ref,
                       y_ref, stage_gu, stage_d, wgu_b, wd_b, sem_gu, sem_d):
    b = pl.program_id(0)
    expert = be_ref[b]
    live = nr_ref[b] > 0

    def weight_copies(e, slot):
        return (pltpu.make_async_copy(wgu_hbm.at[e], stage_gu.at[slot], sem_gu.at[slot]),
                pltpu.make_async_copy(wd_hbm.at[e], stage_d.at[slot], sem_d.at[slot]))

    first = live & ((b == 0) | (be_ref[jnp.maximum(b - 1, 0)] != expert))
    slot = slot_ref[b]

    @pl.when(first)
    def _():
        @pl.when(b == 0)
        def _():
            for cp in weight_copies(expert, slot):
                cp.start()

        for cp in weight_copies(expert, slot):
            cp.wait()

        @pl.when(next_ref[b] < N_EXPERTS)
        def _():
            for cp in weight_copies(next_ref[b], 1 - slot):
                cp.start()

    def ffn_rows(n_rows, round_weights):
        xw = x_ref[:n_rows, :]
        valid = lax.broadcasted_iota(jnp.int32, xw.shape, 0) < nr_ref[b]
        x = _unpack_bf16_halves(jnp.where(valid, xw, jnp.uint32(0))).astype(_BF16)
        if round_weights:
            halves = []
            for cols in (slice(0, D_FF), slice(D_FF, 2 * D_FF)):
                wgu_b[:, cols] = stage_gu[slot, :, cols].astype(_BF16)
                halves.append(jnp.dot(x, wgu_b[:, cols], preferred_element_type=_F32)
                              + bgu_ref[0][:, cols])
            wd_b[...] = stage_d[slot].astype(_BF16)
            g, up = halves
        else:
            hgu = jnp.dot(x, wgu_b[...], preferred_element_type=_F32) + bgu_ref[0]
            g, up = hgu[:, :D_FF], hgu[:, D_FF:]
        g = jnp.minimum(g, SWIGLU_LIMIT)
        up = jnp.clip(up, -SWIGLU_LIMIT, SWIGLU_LIMIT)
        act = (up + 1.0) * (g * _sigmoid(SWIGLU_ALPHA * g))
        y = jnp.dot(act.astype(_BF16), wd_b[...], preferred_element_type=_F32) + bd_ref[0]
        y_ref[:n_rows, :] = _pack_bf16_halves(y)
        if n_rows < MOE_BLOCK:
            y_ref[n_rows:, :] = jnp.zeros((MOE_BLOCK - n_rows, y_ref.shape[1]), y_ref.dtype)

    for height in range(FFN_ROW_STEP, MOE_BLOCK + 1, FFN_ROW_STEP):
        in_height = (nr_ref[b] > height - FFN_ROW_STEP) & (nr_ref[b] <= height)
        for round_weights in (True, False):
            @pl.when(in_height & (first if round_weights else jnp.logical_not(first)))
            def _():
                ffn_rows(height, round_weights)

    @pl.when(jnp.logical_not(live))
    def _():
        y_ref[...] = jnp.zeros_like(y_ref)


def _expert_ffn(block_e, block_rows, block_slot, block_next, x_rows, w_gate_up, b_gate_up, w_down,
                b_down):
    n_rows = x_rows.shape[0]
    n_blocks = n_rows // MOE_BLOCK

    def bias_map(b, be, nr, sl, nx):
        return (be[b], 0, 0)

    def row_map(b, be, nr, sl, nx):
        return (b, 0)

    grid_spec = pltpu.PrefetchScalarGridSpec(
        num_scalar_prefetch=4,
        grid=(n_blocks,),
        in_specs=[pl.BlockSpec((MOE_BLOCK, D_MODEL // 2), row_map),
                  pl.BlockSpec(memory_space=pl.ANY),
                  pl.BlockSpec((1, 1, 2 * D_FF), bias_map),
                  pl.BlockSpec(memory_space=pl.ANY),
                  pl.BlockSpec((1, 1, D_MODEL), bias_map)],
        out_specs=pl.BlockSpec((MOE_BLOCK, D_MODEL // 2), row_map),
        scratch_shapes=[pltpu.VMEM((2, D_MODEL, 2 * D_FF), _F32),
                        pltpu.VMEM((2, D_FF, D_MODEL), _F32),
                        pltpu.VMEM((D_MODEL, 2 * D_FF), _BF16),
                        pltpu.VMEM((D_FF, D_MODEL), _BF16),
                        pltpu.SemaphoreType.DMA((2,)),
                        pltpu.SemaphoreType.DMA((2,))],
    )
    return pl.pallas_call(
        _expert_ffn_kernel,
        out_shape=jax.ShapeDtypeStruct((n_rows, D_MODEL // 2), jnp.uint32),
        grid_spec=grid_spec,
        compiler_params=pltpu.CompilerParams(
            dimension_semantics=("arbitrary",), vmem_limit_bytes=VMEM_LIMIT_BYTES),
        name="expert_ffn",
    )(block_e, block_rows, block_slot, block_next, x_rows, w_gate_up, b_gate_up, w_down, b_down)


def _sc_workers():
    info = plsc.get_sparse_core_info()
    return info.num_cores, info.num_cores * info.num_subcores


def _dispatch(h_packed, dest_flat, n_rows):
    t, width = h_packed.shape
    n_cores, n_workers = _sc_workers()
    n_chunks = t // (n_workers * SC_ROWS)
    chunks_per_k = t // SC_ROWS
    assert n_chunks % 2 == 0

    @functools.partial(
        pl.kernel, mesh=plsc.VectorSubcoreMesh(core_axis_name="c", subcore_axis_name="s"),
        out_type=jax.ShapeDtypeStruct((n_rows, width), h_packed.dtype),
        scratch_types=[pltpu.VMEM((TOP_K, n_chunks, SC_ROWS), jnp.int32),
                       pltpu.VMEM((2, SC_ROWS, width), h_packed.dtype),
                       pltpu.SemaphoreType.DMA((2,)),
                       pltpu.SemaphoreType.DMA((2,))])
    def scatter_rows(h_hbm, dest_hbm, out_hbm, idx_v, buf, lsem, ssem):
        wid = lax.axis_index("s") * n_cores + lax.axis_index("c")
        c0 = wid * n_chunks
        for k in range(TOP_K):
            pltpu.sync_copy(dest_hbm.at[pl.ds(k * chunks_per_k + c0, n_chunks)], idx_v.at[k])

        def load(c, b):
            return pltpu.make_async_copy(h_hbm.at[pl.ds((c0 + c) * SC_ROWS, SC_ROWS)], buf.at[b],
                                         lsem.at[b])

        def scatters(c, b):
            return [pltpu.make_async_copy(buf.at[b], out_hbm.at[idx_v.at[k, c]], ssem.at[b])
                    for k in range(TOP_K)]

        load(0, 0).start()

        @pl.loop(0, n_chunks, step=2)
        def _(ci):
            for b in range(2):
                c = ci + b

                @pl.when(c >= 1)
                def _():
                    for cp in scatters(c - 1, 1 - b):
                        cp.wait()

                @pl.when(c + 1 < n_chunks)
                def _():
                    load(c + 1, 1 - b).start()

                load(c, b).wait()
                for cp in scatters(c, b):
                    cp.start()

        for cp in scatters(n_chunks - 1, 1):
            cp.wait()

    return scatter_rows(h_packed, dest_flat.reshape(TOP_K * chunks_per_k, SC_ROWS))


def _collect(y_rows, dest_flat):
    n_idx = dest_flat.shape[0]
    width = y_rows.shape[1]
    rows, depth = COLLECT_ROWS, COLLECT_RING
    n_cores, n_workers = _sc_workers()
    n_chunks = n_idx // (n_workers * rows)
    assert n_chunks % depth == 0

    @functools.partial(
        pl.kernel, mesh=plsc.VectorSubcoreMesh(core_axis_name="c", subcore_axis_name="s"),
        out_type=jax.ShapeDtypeStruct((n_idx, width), y_rows.dtype),
        scratch_types=[pltpu.VMEM((n_chunks, rows), jnp.int32),
                       pltpu.VMEM((depth, rows, width), y_rows.dtype),
                       pltpu.SemaphoreType.DMA((depth,)),
                       pltpu.SemaphoreType.DMA((depth,))])
    def gather_rows(y_hbm, dest_hbm, out_hbm, idx_v, buf, gsem, wsem):
        wid = lax.axis_index("s") * n_cores + lax.axis_index("c")
        c0 = wid * n_chunks
        pltpu.sync_copy(dest_hbm.at[pl.ds(c0, n_chunks)], idx_v)

        def gather(c, b):
            return pltpu.make_async_copy(y_hbm.at[idx_v.at[c]], buf.at[b], gsem.at[b])

        def write(c, b):
            return pltpu.make_async_copy(buf.at[b], out_hbm.at[pl.ds((c0 + c) * rows, rows)],
                                         wsem.at[b])

        for c in range(depth - 1):
            gather(c, c).start()

        @pl.loop(0, n_chunks, step=depth)
        def _(ci):
            for b in range(depth):
                c = ci + b
                gather(c, b).wait()
                write(c, b).start()
                prev = (b - 1) % depth

                @pl.when(c >= 1)
                def _():
                    write(c - 1, prev).wait()

                @pl.when(c + depth - 1 < n_chunks)
                def _():
                    gather(c + depth - 1, prev).start()

        write(n_chunks - 1, (n_chunks - 1) % depth).wait()

    return gather_rows(y_rows, dest_flat.reshape(n_idx // rows, rows))


def _combine_kernel(x1_ref, ya_ref, tg_ref, g_ref, o_ref):
    acc = x1_ref[...]
    tg = jnp.transpose(tg_ref[...])
    for k in range(TOP_K):
        acc = acc + tg[:, k:k + 1] * _unpack_bf16_halves(ya_ref[k])
    o_ref[...] = _rmsnorm(acc, g_ref[...])


def _combine(x1, y_assign, top_g, g_final):
    t = x1.shape[0]
    tm = TOKEN_TILE
    return pl.pallas_call(
        _combine_kernel,
        out_shape=jax.ShapeDtypeStruct((t, D_MODEL), _F32),
        grid=(t // tm,),
        in_specs=[pl.BlockSpec((tm, D_MODEL), lambda i: (i, 0)),
                  pl.BlockSpec((TOP_K, tm, D_MODEL // 2), lambda i: (0, i, 0)),
                  pl.BlockSpec((2 * TOP_K, tm), lambda i: (0, i)),
                  pl.BlockSpec((1, D_MODEL), lambda i: (0, 0))],
        out_specs=pl.BlockSpec((tm, D_MODEL), lambda i: (i, 0)),
        compiler_params=pltpu.CompilerParams(
            dimension_semantics=("arbitrary",), vmem_limit_bytes=VMEM_LIMIT_BYTES),
        name="combine",
    )(x1, y_assign, top_g, g_final)


def _block_plan(counts, n_blocks):
    padded = ((counts + MOE_BLOCK - 1) // MOE_BLOCK) * MOE_BLOCK
    pad_end = jnp.cumsum(padded)
    pad_start = pad_end - padded
    block_start = (jnp.arange(n_blocks, dtype=jnp.int32) * MOE_BLOCK)[:, None]
    eidx = jnp.arange(N_EXPERTS, dtype=jnp.int32)
    owns = (pad_start[None, :] <= block_start) & (block_start < pad_end[None, :])
    has_blocks = (padded > 0).astype(jnp.int32)
    ordinal = jnp.cumsum(has_blocks) - has_blocks
    later = (eidx[None, :] > eidx[:, None]) & (padded[None, :] > 0)
    next_expert = jnp.min(jnp.where(later, eidx[None, :], N_EXPERTS), axis=1)

    def per_block(per_expert):
        return jnp.sum(jnp.where(owns, per_expert, 0), axis=1).astype(jnp.int32)

    block_e = per_block(eidx[None, :])
    block_rows = per_block(jnp.clip((pad_start + counts)[None, :] - block_start, 0, MOE_BLOCK))
    block_slot = per_block((ordinal % 2)[None, :])
    block_next = per_block(next_expert[None, :])
    return pad_start, (block_e, block_rows, block_slot, block_next)


def kernel(x, norm_mix_g, w_in, conv_w, w_conv_out, ssm_lam_re, ssm_lam_im, ssm_log_dt, ssm_b_re, ssm_b_im, ssm_c_re, ssm_c_im, ssm_d, w_glu, w_out, norm_ffn_g, w_router, b_router, w_gate_up, b_gate_up, w_down, b_down, norm_f_g):
    bsz, seq, d = x.shape
    t = bsz * seq
    x2 = x.reshape(t, d)
    assert seq % TOKEN_TILE == 0 and seq % SSM_TIME_TILE == 0 and w_in.shape[0] == 1

    w_in_b = w_in[0].astype(_BF16)
    n_bcvu = 3 * D_CONV + D_SSM
    g_mix = norm_mix_g[0].reshape(1, d)

    bz, u = _in_proj(x2, g_mix, w_in_b[:, :n_bcvu], conv_w[0], seq)

    tables = _ssm_tables(ssm_lam_re[0], ssm_lam_im[0], ssm_log_dt[0], ssm_b_re[0], ssm_b_im[0],
                         ssm_c_re[0], ssm_c_im[0], ssm_d[0])
    yg = _ssm(u.reshape(bsz, seq, D_SSM), tables).reshape(t, D_SSM)

    x1, h_packed, top_i, top_g, rank, counts = _mix_route(
        x2, bz, yg, g_mix, w_in_b[:, n_bcvu:], w_conv_out[0].astype(_BF16),
        w_glu[0].astype(_BF16), w_out[0].astype(_BF16), norm_ffn_g[0].reshape(1, d),
        jnp.pad(w_router[0], ((0, 0), (0, LANES - N_EXPERTS))).astype(_BF16),
        b_router[0].reshape(N_EXPERTS, 1))

    n_rows = t * TOP_K + N_EXPERTS * MOE_BLOCK
    pad_start, block_plan = _block_plan(counts[:, 0], n_rows // MOE_BLOCK)
    expert_ids = jnp.arange(N_EXPERTS, dtype=jnp.int32)[:, None, None]
    row_start = jnp.sum(jnp.where(top_i[None] == expert_ids, pad_start[:, None, None], 0), axis=0)
    dest = (row_start + rank).reshape(TOP_K * t)

    x_rows = _dispatch(h_packed, dest, n_rows)
    y_rows = _expert_ffn(*block_plan, x_rows, w_gate_up[0],
                         b_gate_up[0].reshape(N_EXPERTS, 1, 2 * D_FF), w_down[0],
                         b_down[0].reshape(N_EXPERTS, 1, D_MODEL))
    y_assign = _collect(y_rows, dest).reshape(TOP_K, t, D_MODEL // 2)
    out = _combine(x1, y_assign, top_g, norm_f_g.reshape(1, d))
    return out.reshape(bsz, seq, d)
```

```python
import functools

import jax
import jax.numpy as jnp
from jax import lax
from jax.experimental import pallas as pl
from jax.experimental.pallas import tpu as pltpu
from jax.experimental.pallas import tpu_sc as plsc

D_MODEL = 1024
D_CONV = 512
CONV_WIDTH = 3
D_SSM = 512
SSM_GROUP = 16
N_SSM_GROUPS = 32
SSM_STATE = 64
N_EXPERTS = 32
TOP_K = 4
D_FF = 1024
SWIGLU_LIMIT = 7.0
SWIGLU_ALPHA = 1.702
RMS_EPS = 1e-6

LANES = 128
SUBLANES = 8
MXU_DIM = 256
CHUNK = 8
SLAB_GROUPS = LANES // SSM_GROUP
N_SLABS = N_SSM_GROUPS // SLAB_GROUPS
SLAB_STATE = SLAB_GROUPS * SSM_STATE
FLAT = CHUNK * LANES
SSM_TIME_TILE = 128
TOKEN_TILE = 512
COMBINE_BUFFERS = 3
IN_TILE = 1024
IN_CHAIN = 512
MIX_TILE = 1024
ROW_CHAIN = 256
MOE_BLOCK = 1024
FFN_ROW_STEP = 256
SC_ROWS = 64
COLLECT_ROWS = 32
COLLECT_RING = 4
VMEM_LIMIT_BYTES = 56 * 1024 * 1024

_BF16 = jnp.bfloat16
_F32 = jnp.float32


def _rmsnorm(xf, g):
    return xf * lax.rsqrt(jnp.mean(xf * xf, axis=-1, keepdims=True) + RMS_EPS) * g


def _sigmoid(v):
    return 0.5 * jnp.tanh(0.5 * v) + 0.5


def _pack_bf16_halves(v):
    n = v.shape[1] // 2
    bits = pltpu.bitcast(v.astype(_BF16).astype(_F32), jnp.uint32)
    return (bits[:, :n] >> 16) | (bits[:, n:] & jnp.uint32(0xFFFF0000))


def _unpack_bf16_halves(w):
    return jnp.concatenate([pltpu.bitcast(w << 16, _F32),
                            pltpu.bitcast(w & jnp.uint32(0xFFFF0000), _F32)], axis=1)


def _in_proj_kernel(tiles_per_seq, x_ref, g_ref, w_ref, cw_ref, bz_ref, u_ref, hbuf):
    tm = x_ref.shape[0]
    halo = SUBLANES

    @pl.when(pl.program_id(0) % tiles_per_seq == 0)
    def _():
        hbuf[0:halo, :] = jnp.zeros((halo, D_CONV), _F32)

    cw = cw_ref[...]
    for r0 in range(0, tm, IN_CHAIN):
        rows = slice(r0, r0 + IN_CHAIN)
        xn = _rmsnorm(x_ref[rows, :], g_ref[...]).astype(_BF16)
        cv = jnp.dot(xn, w_ref[:, D_CONV:3 * D_CONV], preferred_element_type=_F32)
        hbuf[halo + r0:halo + r0 + IN_CHAIN, :] = cv[:, :D_CONV] * cv[:, D_CONV:]
        u_ref[rows, :] = jnp.dot(xn, w_ref[:, 3 * D_CONV:], preferred_element_type=_F32)
        z = cw[CONV_WIDTH - 1:CONV_WIDTH, :] * hbuf[halo + r0:halo + r0 + IN_CHAIN, :]
        for lag in range(1, CONV_WIDTH):
            z = z + (cw[CONV_WIDTH - 1 - lag:CONV_WIDTH - lag, :]
                     * hbuf[halo + r0 - lag:halo + r0 - lag + IN_CHAIN, :])
        b_gate = jnp.dot(xn, w_ref[:, :D_CONV], preferred_element_type=_F32)
        bz_ref[rows, :] = (b_gate * z).astype(_BF16)
    hbuf[0:halo, :] = hbuf[tm:tm + halo, :]


def _in_proj(x2, g, w_bcvu, conv_w, seq):
    t = x2.shape[0]
    tm = IN_TILE
    assert SUBLANES >= CONV_WIDTH - 1 and seq % tm == 0
    return pl.pallas_call(
        functools.partial(_in_proj_kernel, seq // tm),
        out_shape=(jax.ShapeDtypeStruct((t, D_CONV), _BF16),
                   jax.ShapeDtypeStruct((t, D_SSM), _F32)),
        grid=(t // tm,),
        in_specs=[pl.BlockSpec((tm, D_MODEL), lambda i: (i, 0)),
                  pl.BlockSpec((1, D_MODEL), lambda i: (0, 0)),
                  pl.BlockSpec((D_MODEL, 3 * D_CONV + D_SSM), lambda i: (0, 0)),
                  pl.BlockSpec((CONV_WIDTH, D_CONV), lambda i: (0, 0))],
        out_specs=(pl.BlockSpec((tm, D_CONV), lambda i: (i, 0)),
                   pl.BlockSpec((tm, D_SSM), lambda i: (i, 0))),
        scratch_shapes=[pltpu.VMEM((tm + SUBLANES, D_CONV), _F32)],
        compiler_params=pltpu.CompilerParams(
            dimension_semantics=("arbitrary",), vmem_limit_bytes=VMEM_LIMIT_BYTES),
        name="in_proj",
    )(x2, g, w_bcvu, conv_w)


def _ssm_prep_kernel(lr_ref, lc_ref, bm_ref, cm_ref, d_ref, toep_ref, bst_ref, cst_ref, a_ref):
    def discretise(lre, lim, log_dt):
        dt = jnp.exp(log_dt)
        mag = jnp.exp(lre * dt)
        return mag * jnp.cos(lim * dt), mag * jnp.sin(lim * dt)

    def powers(are, aim):
        pre, pim = [jnp.ones_like(are)], [jnp.zeros_like(are)]
        for _ in range(CHUNK):
            pre, pim = (pre + [pre[-1] * are - pim[-1] * aim],
                        pim + [pre[-1] * aim + pim[-1] * are])
        return pre, pim

    lr = lr_ref[0]
    lre, lim = lr[0:1, :], lr[1:2, :]
    are, aim = discretise(lre, lim, lr[2:3, :])
    pre, pim = powers(are, aim)
    den = lre * lre + lim * lim
    q_re = ((are - 1.0) * lre + aim * lim) / den
    q_im = (aim * lre - (are - 1.0) * lim) / den
    bb_re = q_re * bm_ref[0, 0] - q_im * bm_ref[0, 1]
    bb_im = q_re * bm_ref[0, 1] + q_im * bm_ref[0, 0]
    cm_re, cm_im = cm_ref[0, 0], cm_ref[0, 1]

    def split_bf16(v):
        v_hi = v.astype(_BF16)
        return v_hi, (v - v_hi.astype(_F32)).astype(_BF16)

    c_hi, c_lo = split_bf16(jnp.concatenate([cm_re, -cm_im], axis=0))
    kblk = []
    for k in range(CHUNK):
        ab_re = bb_re * pre[k] - bb_im * pim[k]
        ab_im = bb_re * pim[k] + bb_im * pre[k]
        rows = slice((CHUNK - 1 - k) * LANES, (CHUNK - k) * LANES)
        bst_ref[0, rows, :SLAB_STATE] = ab_re.astype(_BF16)
        bst_ref[0, rows, SLAB_STATE:] = ab_im.astype(_BF16)
        ab_hi, ab_lo = split_bf16(jnp.concatenate([ab_re, ab_im], axis=1))
        kblk.append(jnp.dot(ab_hi, c_hi, preferred_element_type=_F32)
                    + (jnp.dot(ab_lo, c_hi, preferred_element_type=_F32)
                       + jnp.dot(ab_hi, c_lo, preferred_element_type=_F32)))
    r = lax.broadcasted_iota(jnp.int32, (LANES, LANES), 0)
    c = lax.broadcasted_iota(jnp.int32, (LANES, LANES), 1)
    kblk[0] = kblk[0] + jnp.where(r == c, jnp.broadcast_to(d_ref[0], (LANES, LANES)), 0.0)
    kblk = [kb.astype(_BF16) for kb in kblk]
    zeros = jnp.zeros((LANES, LANES), _BF16)
    for sp in range(CHUNK):
        for s in range(CHUNK):
            toep_ref[0, sp * LANES:(sp + 1) * LANES, s * LANES:(s + 1) * LANES] = (
                kblk[s - sp] if s >= sp else zeros)

    lc = lc_ref[0]
    cre, cim = discretise(lc[:, 0:1], lc[:, 1:2], lc[:, 2:3])
    qre, qim = powers(cre, cim)
    for s in range(CHUNK):
        cols = slice(s * LANES, (s + 1) * LANES)
        cst_ref[0, :SLAB_STATE, cols] = (cm_re * qre[s + 1] - cm_im * qim[s + 1]).astype(_BF16)
        cst_ref[0, SLAB_STATE:, cols] = (-(cm_re * qim[s + 1] + cm_im * qre[s + 1])).astype(_BF16)
    a_ref[0, 0:1, :] = pre[CHUNK]
    a_ref[0, 1:2, :] = pim[CHUNK]


def _ssm_tables(lam_re, lam_im, log_dt, b_re, b_im, c_re, c_im, d_skip):
    sg = (N_SLABS, SLAB_GROUPS)
    eye = jnp.eye(SLAB_GROUPS, dtype=_F32)
    lam = jnp.stack([lam_re, lam_im, jnp.broadcast_to(log_dt[:, None], lam_re.shape)], axis=0)
    lam_row = lam.reshape(3, N_SLABS, SLAB_STATE).transpose(1, 0, 2)
    lam_col = lam_row.transpose(0, 2, 1)

    def b_blockdiag(b):
        bt = b.reshape(*sg, SSM_STATE, SSM_GROUP).transpose(0, 1, 3, 2)
        return (bt[:, :, :, None, :] * eye[None, :, None, :, None]).reshape(N_SLABS, LANES, SLAB_STATE)

    def c_blockdiag(c):
        ct = c.reshape(*sg, SSM_GROUP, SSM_STATE).transpose(0, 1, 3, 2)
        return (ct[:, :, :, None, :] * eye[None, :, None, :, None]).reshape(N_SLABS, SLAB_STATE, LANES)

    bm = jnp.stack([b_blockdiag(b_re), b_blockdiag(b_im)], axis=1)
    cm = jnp.stack([c_blockdiag(c_re), c_blockdiag(c_im)], axis=1)
    d = d_skip.reshape(N_SLABS, 1, LANES)
    slab3 = lambda sl: (sl, 0, 0)
    slab4 = lambda sl: (sl, 0, 0, 0)
    return pl.pallas_call(
        _ssm_prep_kernel,
        out_shape=(jax.ShapeDtypeStruct((N_SLABS, FLAT, FLAT), _BF16),
                   jax.ShapeDtypeStruct((N_SLABS, FLAT, 2 * SLAB_STATE), _BF16),
                   jax.ShapeDtypeStruct((N_SLABS, 2 * SLAB_STATE, FLAT), _BF16),
                   jax.ShapeDtypeStruct((N_SLABS, 2, SLAB_STATE), _F32)),
        grid=(N_SLABS,),
        in_specs=[pl.BlockSpec((1, 3, SLAB_STATE), slab3),
                  pl.BlockSpec((1, SLAB_STATE, 3), slab3),
                  pl.BlockSpec((1, 2, LANES, SLAB_STATE), slab4),
                  pl.BlockSpec((1, 2, SLAB_STATE, LANES), slab4),
                  pl.BlockSpec((1, 1, LANES), slab3)],
        out_specs=(pl.BlockSpec((1, FLAT, FLAT), slab3),
                   pl.BlockSpec((1, FLAT, 2 * SLAB_STATE), slab3),
                   pl.BlockSpec((1, 2 * SLAB_STATE, FLAT), slab3),
                   pl.BlockSpec((1, 2, SLAB_STATE), slab3)),
        compiler_params=pltpu.CompilerParams(
            dimension_semantics=("arbitrary",), vmem_limit_bytes=VMEM_LIMIT_BYTES),
        name="ssm_prep",
    )(lam_row, lam_col, bm, cm, d)


def _ssm_kernel(u_ref, toep_ref, bst_ref, cst_ref, a_ref, y_ref, uflat, s_scr, xc_scr, carry, ytoep):
    nb, tt, _ = u_ref.shape
    nch = tt // CHUNK
    n = nb * nch

    @pl.when(pl.program_id(1) == 0)
    def _():
        carry[...] = jnp.zeros_like(carry)

    for s in range(CHUNK):
        part = u_ref[:, pl.ds(s, nch, stride=CHUNK), :]
        uflat[:, s * LANES:(s + 1) * LANES] = part.reshape(n, LANES).astype(_BF16)

    n_cb = FLAT // MXU_DIM

    def toeplitz(cb):
        kk = (cb + 1) * MXU_DIM
        cols = slice(cb * MXU_DIM, kk)
        ytoep[:, cols] = jnp.dot(uflat[:, :kk], toep_ref[0, :kk, cols], preferred_element_type=_F32)

    for cb in range(n_cb // 2):
        toeplitz(cb)

    nblk = SLAB_STATE // LANES
    loc_all = jnp.dot(uflat[...], bst_ref[0], preferred_element_type=_F32)
    for cb in range(n_cb // 2, n_cb):
        toeplitz(cb)
    for kb in range(2 * nblk):
        s_scr[kb] = loc_all[:, kb * LANES:(kb + 1) * LANES]

    a = a_ref[0]
    are = [jnp.broadcast_to(a[0:1, kb * LANES:(kb + 1) * LANES], (nb, LANES)) for kb in range(nblk)]
    aim = [jnp.broadcast_to(a[1:2, kb * LANES:(kb + 1) * LANES], (nb, LANES)) for kb in range(nblk)]
    xr = [carry[kb] for kb in range(nblk)]
    xi = [carry[nblk + kb] for kb in range(nblk)]
    for j in range(nch):
        rows = pl.ds(j, nb, stride=nch)
        for kb in range(nblk):
            xc_scr[kb, rows, :] = xr[kb]
            xc_scr[nblk + kb, rows, :] = xi[kb]
            nr = are[kb] * xr[kb] - aim[kb] * xi[kb] + s_scr[kb, rows, :]
            ni = are[kb] * xi[kb] + aim[kb] * xr[kb] + s_scr[nblk + kb, rows, :]
            xr[kb], xi[kb] = nr, ni
    for kb in range(nblk):
        carry[kb] = xr[kb]
        carry[nblk + kb] = xi[kb]

    xc = jnp.concatenate([xc_scr[kb] for kb in range(2 * nblk)], axis=1).astype(_BF16)
    for cb in range(n_cb):
        cols = slice(cb * MXU_DIM, (cb + 1) * MXU_DIM)
        y = ytoep[:, cols] + jnp.dot(xc, cst_ref[0, :, cols], preferred_element_type=_F32)
        y = jax.nn.gelu(y)
        for h in range(MXU_DIM // LANES):
            s = cb * (MXU_DIM // LANES) + h
            y_ref[:, pl.ds(s, nch, stride=CHUNK), :] = (
                y[:, h * LANES:(h + 1) * LANES].reshape(nb, nch, LANES))


def _ssm(u3, tables):
    toep, bst, cst, a_chunk = tables
    nb, seq, _ = u3.shape
    tt = SSM_TIME_TILE
    n = nb * (tt // CHUNK)
    return pl.pallas_call(
        _ssm_kernel,
        out_shape=jax.ShapeDtypeStruct(u3.shape, _F32),
        grid=(N_SLABS, seq // tt),
        in_specs=[pl.BlockSpec((nb, tt, LANES), lambda sl, ti: (0, ti, sl)),
                  pl.BlockSpec((1, FLAT, FLAT), lambda sl, ti: (sl, 0, 0)),
                  pl.BlockSpec((1, FLAT, 2 * SLAB_STATE), lambda sl, ti: (sl, 0, 0)),
                  pl.BlockSpec((1, 2 * SLAB_STATE, FLAT), lambda sl, ti: (sl, 0, 0)),
                  pl.BlockSpec((1, 2, SLAB_STATE), lambda sl, ti: (sl, 0, 0))],
        out_specs=pl.BlockSpec((nb, tt, LANES), lambda sl, ti: (0, ti, sl)),
        scratch_shapes=[pltpu.VMEM((n, FLAT), _BF16),
                        pltpu.VMEM((2 * SLAB_STATE // LANES, n, LANES), _F32),
                        pltpu.VMEM((2 * SLAB_STATE // LANES, n, LANES), _F32),
                        pltpu.VMEM((2 * SLAB_STATE // LANES, nb, LANES), _F32),
                        pltpu.VMEM((n, FLAT), _F32)],
        compiler_params=pltpu.CompilerParams(
            dimension_semantics=("arbitrary", "arbitrary"), vmem_limit_bytes=VMEM_LIMIT_BYTES),
        name="ssm",
    )(u3, toep, bst, cst, a_chunk)


def _mix_route_kernel(x_ref, bz_ref, yg_ref, gm_ref, wg_ref, wco_ref, wglu_ref, wout_ref,
                      gf_ref, wr_ref, br_ref,
                      x1_ref, h_ref, ti_ref, tg_ref, rk_ref, cnt_ref, base, merged):
    tm = x_ref.shape[0]

    @pl.when(pl.program_id(0) == 0)
    def _():
        base[...] = jnp.zeros_like(base)

    chains = [slice(r0, r0 + ROW_CHAIN) for r0 in range(0, tm, ROW_CHAIN)]
    hs, picks = [], []

    def route(j):
        logits_tok = jnp.dot(hs[j], wr_ref[...], preferred_element_type=_F32)
        picks.append(_top_k_rows(chains[j], logits_tok, br_ref, ti_ref, tg_ref))

    def rank(j):
        _rank_rows(chains[j], *picks[j], rk_ref, base)

    for j, rows in enumerate(chains):
        hs.append(_mix_rows(rows, x_ref, bz_ref, yg_ref, gm_ref, wg_ref, wco_ref, wglu_ref,
                            wout_ref, gf_ref, x1_ref, h_ref, merged))
        if j >= 1:
            route(j - 1)
        if j >= 2:
            rank(j - 2)
    last = len(chains) - 1
    route(last)
    for j in range(max(last - 1, 0), last + 1):
        rank(j)
    cnt_ref[...] = base[...].astype(jnp.int32)


def _mix_rows(rows, x_ref, bz_ref, yg_ref, gm_ref, wg_ref, wco_ref, wglu_ref, wout_ref,
              gf_ref, x1_ref, h_ref, merged):
    x = x_ref[rows, :]
    xn = _rmsnorm(x, gm_ref[...]).astype(_BF16)
    bz = bz_ref[rows, :]
    yg = yg_ref[rows, :].astype(_BF16)
    for c in range(D_MODEL // MXU_DIM):
        lo = slice(c * MXU_DIM, (c + 1) * MXU_DIM)
        hi = slice(D_MODEL + c * MXU_DIM, D_MODEL + (c + 1) * MXU_DIM)
        gate_a = jnp.dot(xn, wg_ref[:, lo], preferred_element_type=_F32)
        gate_b = jnp.dot(xn, wg_ref[:, hi], preferred_element_type=_F32)
        y_a = jnp.dot(bz, wco_ref[:, lo], preferred_element_type=_F32)
        val = jnp.dot(yg, wglu_ref[:, lo], preferred_element_type=_F32)
        glu_gate = jnp.dot(yg, wglu_ref[:, hi], preferred_element_type=_F32)
        y_b = val * _sigmoid(glu_gate)
        merged[rows, lo] = (_sigmoid(gate_a) * y_a + _sigmoid(gate_b) * y_b).astype(_BF16)
    x1 = x + jnp.dot(merged[rows, :], wout_ref[...], preferred_element_type=_F32)
    x1_ref[rows, :] = x1
    h = _rmsnorm(x1, gf_ref[...])
    h_ref[rows, :] = _pack_bf16_halves(h)
    return h.astype(_BF16)


def _top_k_rows(rows, logits_tok, br_ref, ti_ref, tg_ref):
    tm = rows.stop - rows.start
    logits = jnp.transpose(logits_tok)[:N_EXPERTS, :] + br_ref[...]
    erow = lax.broadcasted_iota(jnp.int32, (N_EXPERTS, tm), 0).astype(_F32)
    neg_inf = jnp.float32(-jnp.inf)
    work = logits
    vals, idxs = [], []
    for _ in range(TOP_K):
        m = jnp.max(work, axis=0, keepdims=True)
        idx = jnp.min(jnp.where(work == m, erow, float(N_EXPERTS)), axis=0, keepdims=True)
        vals.append(m)
        idxs.append(idx)
        work = jnp.where(erow == idx, neg_inf, work)
    exps = [jnp.exp(v - vals[0]) for v in vals]
    denom = exps[0] + exps[1] + exps[2] + exps[3]
    sel = jnp.zeros((N_EXPERTS, tm), _F32)
    for k in range(TOP_K):
        ti_ref[k:k + 1, rows] = idxs[k].astype(jnp.int32)
        tg_ref[k:k + 1, rows] = exps[k] / denom
        tg_ref[TOP_K + k:TOP_K + k + 1, rows] = jnp.zeros((1, tm), _F32)
        sel = sel + (erow == idxs[k]).astype(_F32)
    return sel, idxs


def _rank_rows(rows, sel, idxs, rk_ref, base):
    tm = rows.stop - rows.start
    erow = lax.broadcasted_iota(jnp.int32, (N_EXPERTS, tm), 0).astype(_F32)
    row = lax.broadcasted_iota(jnp.int32, (tm, tm), 0)
    col = lax.broadcasted_iota(jnp.int32, (tm, tm), 1)
    earlier = (row < col).astype(_BF16)
    before = jnp.dot(sel.astype(_BF16), earlier, preferred_element_type=_F32) + base[...]
    for k in range(TOP_K):
        rk = jnp.sum(jnp.where(erow == idxs[k], before, 0.0), axis=0, keepdims=True)
        rk_ref[k:k + 1, rows] = rk.astype(jnp.int32)
    base[...] = base[...] + jnp.sum(sel, axis=1, keepdims=True)


def _mix_route(x2, bz, yg, g_mix, w_gates, w_conv_out, w_glu, w_out, g_ffn, w_router, b_router):
    t = x2.shape[0]
    tm = MIX_TILE
    tok = lambda i: (i, 0)
    tok_lanes = lambda i: (0, i)
    fixed = lambda i: (0, 0)
    weight = lambda shape: pl.BlockSpec(shape, fixed, pipeline_mode=pl.Buffered(1))
    return pl.pallas_call(
        _mix_route_kernel,
        out_shape=(jax.ShapeDtypeStruct((t, D_MODEL), _F32),
                   jax.ShapeDtypeStruct((t, D_MODEL // 2), jnp.uint32),
                   jax.ShapeDtypeStruct((TOP_K, t), jnp.int32),
                   jax.ShapeDtypeStruct((2 * TOP_K, t), _F32),
                   jax.ShapeDtypeStruct((TOP_K, t), jnp.int32),
                   jax.ShapeDtypeStruct((N_EXPERTS, 1), jnp.int32)),
        grid=(t // tm,),
        in_specs=[pl.BlockSpec((tm, D_MODEL), tok),
                  pl.BlockSpec((tm, D_CONV), tok),
                  pl.BlockSpec((tm, D_SSM), tok),
                  pl.BlockSpec((1, D_MODEL), fixed),
                  weight((D_MODEL, 2 * D_MODEL)),
                  weight((D_CONV, D_MODEL)),
                  weight((D_SSM, 2 * D_MODEL)),
                  weight((D_MODEL, D_MODEL)),
                  pl.BlockSpec((1, D_MODEL), fixed),
                  weight((D_MODEL, LANES)),
                  pl.BlockSpec((N_EXPERTS, 1), fixed)],
        out_specs=(pl.BlockSpec((tm, D_MODEL), tok),
                   pl.BlockSpec((tm, D_MODEL // 2), tok),
                   pl.BlockSpec((TOP_K, tm), tok_lanes),
                   pl.BlockSpec((2 * TOP_K, tm), tok_lanes),
                   pl.BlockSpec((TOP_K, tm), tok_lanes),
                   pl.BlockSpec((N_EXPERTS, 1), fixed)),
        scratch_shapes=[pltpu.VMEM((N_EXPERTS, 1), _F32),
                        pltpu.VMEM((tm, D_MODEL), _BF16)],
        compiler_params=pltpu.CompilerParams(
            dimension_semantics=("arbitrary",), vmem_limit_bytes=VMEM_LIMIT_BYTES),
        name="mix_route",
    )(x2, bz, yg, g_mix, w_gates, w_conv_out, w_glu, w_out, g_ffn, w_router, b_router)


def _expert_ffn_kernel(be_ref, nr_ref, slot_ref, next_ref, x_ref, wgu_hbm, bgu_ref, wd_hbm, bd_ref,
                       y_ref, stage_gu, stage_d, wgu_b, wd_b, sem_gu, sem_d):
    b = pl.program_id(0)
    expert = be_ref[b]
    live = nr_ref[b] > 0

    def weight_copies(e, slot):
        return (pltpu.make_async_copy(wgu_hbm.at[e], stage_gu.at[slot], sem_gu.at[slot]),
                pltpu.make_async_copy(wd_hbm.at[e], stage_d.at[slot], sem_d.at[slot]))

    first = live & ((b == 0) | (be_ref[jnp.maximum(b - 1, 0)] != expert))
    slot = slot_ref[b]

    @pl.when(first)
    def _():
        @pl.when(b == 0)
        def _():
            for cp in weight_copies(expert, slot):
                cp.start()

        for cp in weight_copies(expert, slot):
            cp.wait()

        @pl.when(next_ref[b] < N_EXPERTS)
        def _():
            for cp in weight_copies(next_ref[b], 1 - slot):
                cp.start()

    def ffn_rows(n_rows, round_weights):
        xw = x_ref[:n_rows, :]
        valid = lax.broadcasted_iota(jnp.int32, xw.shape, 0) < nr_ref[b]
        x = _unpack_bf16_halves(jnp.where(valid, xw, jnp.uint32(0))).astype(_BF16)
        if round_weights:
            halves = []
            for cols in (slice(0, D_FF), slice(D_FF, 2 * D_FF)):
                wgu_b[:, cols] = stage_gu[slot, :, cols].astype(_BF16)
                halves.append(jnp.dot(x, wgu_b[:, cols], preferred_element_type=_F32)
                              + bgu_ref[0][:, cols])
            wd_b[...] = stage_d[slot].astype(_BF16)
            g, up = halves
        else:
            hgu = jnp.dot(x, wgu_b[...], preferred_element_type=_F32) + bgu_ref[0]
            g, up = hgu[:, :D_FF], hgu[:, D_FF:]
        g = jnp.minimum(g, SWIGLU_LIMIT)
        up = jnp.clip(up, -SWIGLU_LIMIT, SWIGLU_LIMIT)
        act = (up + 1.0) * (g * _sigmoid(SWIGLU_ALPHA * g))
        y = jnp.dot(act.astype(_BF16), wd_b[...], preferred_element_type=_F32) + bd_ref[0]
        y_ref[:n_rows, :] = _pack_bf16_halves(y)
        if n_rows < MOE_BLOCK:
            y_ref[n_rows:, :] = jnp.zeros((MOE_BLOCK - n_rows, y_ref.shape[1]), y_ref.dtype)

    for height in range(FFN_ROW_STEP, MOE_BLOCK + 1, FFN_ROW_STEP):
        in_height = (nr_ref[b] > height - FFN_ROW_STEP) & (nr_ref[b] <= height)
        for round_weights in (True, False):
            @pl.when(in_height & (first if round_weights else jnp.logical_not(first)))
            def _():
                ffn_rows(height, round_weights)

    @pl.when(jnp.logical_not(live))
    def _():
        y_ref[...] = jnp.zeros_like(y_ref)


def _expert_ffn(block_e, block_rows, block_slot, block_next, x_rows, w_gate_up, b_gate_up, w_down,
                b_down):
    n_rows = x_rows.shape[0]
    n_blocks = n_rows // MOE_BLOCK

    def bias_map(b, be, nr, sl, nx):
        return (be[b], 0, 0)

    def row_map(b, be, nr, sl, nx):
        return (b, 0)

    grid_spec = pltpu.PrefetchScalarGridSpec(
        num_scalar_prefetch=4,
        grid=(n_blocks,),
        in_specs=[pl.BlockSpec((MOE_BLOCK, D_MODEL // 2), row_map),
                  pl.BlockSpec(memory_space=pl.ANY),
                  pl.BlockSpec((1, 1, 2 * D_FF), bias_map),
                  pl.BlockSpec(memory_space=pl.ANY),
                  pl.BlockSpec((1, 1, D_MODEL), bias_map)],
        out_specs=pl.BlockSpec((MOE_BLOCK, D_MODEL // 2), row_map),
        scratch_shapes=[pltpu.VMEM((2, D_MODEL, 2 * D_FF), _F32),
                        pltpu.VMEM((2, D_FF, D_MODEL), _F32),
                        pltpu.VMEM((D_MODEL, 2 * D_FF), _BF16),
                        pltpu.VMEM((D_FF, D_MODEL), _BF16),
                        pltpu.SemaphoreType.DMA((2,)),
                        pltpu.SemaphoreType.DMA((2,))],
    )
    return pl.pallas_call(
        _expert_ffn_kernel,
        out_shape=jax.ShapeDtypeStruct((n_rows, D_MODEL // 2), jnp.uint32),
        grid_spec=grid_spec,
        compiler_params=pltpu.CompilerParams(
            dimension_semantics=("arbitrary",), vmem_limit_bytes=VMEM_LIMIT_BYTES),
        name="expert_ffn",
    )(block_e, block_rows, block_slot, block_next, x_rows, w_gate_up, b_gate_up, w_down, b_down)


def _sc_workers():
    info = plsc.get_sparse_core_info()
    return info.num_cores, info.num_cores * info.num_subcores


def _dispatch(h_packed, dest_flat, n_rows):
    t, width = h_packed.shape
    n_cores, n_workers = _sc_workers()
    n_chunks = t // (n_workers * SC_ROWS)
    chunks_per_k = t // SC_ROWS
    assert n_chunks % 2 == 0

    @functools.partial(
        pl.kernel, mesh=plsc.VectorSubcoreMesh(core_axis_name="c", subcore_axis_name="s"),
        out_type=jax.ShapeDtypeStruct((n_rows, width), h_packed.dtype),
        scratch_types=[pltpu.VMEM((TOP_K, n_chunks, SC_ROWS), jnp.int32),
                       pltpu.VMEM((2, SC_ROWS, width), h_packed.dtype),
                       pltpu.SemaphoreType.DMA((2,)),
                       pltpu.SemaphoreType.DMA((2,))])
    def scatter_rows(h_hbm, dest_hbm, out_hbm, idx_v, buf, lsem, ssem):
        wid = lax.axis_index("s") * n_cores + lax.axis_index("c")
        c0 = wid * n_chunks
        for k in range(TOP_K):
            pltpu.sync_copy(dest_hbm.at[pl.ds(k * chunks_per_k + c0, n_chunks)], idx_v.at[k])

        def load(c, b):
            return pltpu.make_async_copy(h_hbm.at[pl.ds((c0 + c) * SC_ROWS, SC_ROWS)], buf.at[b],
                                         lsem.at[b])

        def scatters(c, b):
            return [pltpu.make_async_copy(buf.at[b], out_hbm.at[idx_v.at[k, c]], ssem.at[b])
                    for k in range(TOP_K)]

        load(0, 0).start()

        @pl.loop(0, n_chunks, step=2)
        def _(ci):
            for b in range(2):
                c = ci + b

                @pl.when(c >= 1)
                def _():
                    for cp in scatters(c - 1, 1 - b):
                        cp.wait()

                @pl.when(c + 1 < n_chunks)
                def _():
                    load(c + 1, 1 - b).start()

                load(c, b).wait()
                for cp in scatters(c, b):
                    cp.start()

        for cp in scatters(n_chunks - 1, 1):
            cp.wait()

    return scatter_rows(h_packed, dest_flat.reshape(TOP_K * chunks_per_k, SC_ROWS))


def _collect(y_rows, dest_flat):
    n_idx = dest_flat.shape[0]
    width = y_rows.shape[1]
    rows, depth = COLLECT_ROWS, COLLECT_RING
    n_cores, n_workers = _sc_workers()
    n_chunks = n_idx // (n_workers * rows)
    assert n_chunks % depth == 0

    @functools.partial(
        pl.kernel, mesh=plsc.VectorSubcoreMesh(core_axis_name="c", subcore_axis_name="s"),
        out_type=jax.ShapeDtypeStruct((n_idx, width), y_rows.dtype),
        scratch_types=[pltpu.VMEM((n_chunks, rows), jnp.int32),
                       pltpu.VMEM((depth, rows, width), y_rows.dtype),
                       pltpu.SemaphoreType.DMA((depth,)),
                       pltpu.SemaphoreType.DMA((depth,))])
    def gather_rows(y_hbm, dest_hbm, out_hbm, idx_v, buf, gsem, wsem):
        wid = lax.axis_index("s") * n_cores + lax.axis_index("c")
        c0 = wid * n_chunks
        pltpu.sync_copy(dest_hbm.at[pl.ds(c0, n_chunks)], idx_v)

        def gather(c, b):
            return pltpu.make_async_copy(y_hbm.at[idx_v.at[c]], buf.at[b], gsem.at[b])

        def write(c, b):
            return pltpu.make_async_copy(buf.at[b], out_hbm.at[pl.ds((c0 + c) * rows, rows)],
                                         wsem.at[b])

        for c in range(depth - 1):
            gather(c, c).start()

        @pl.loop(0, n_chunks, step=depth)
        def _(ci):
            for b in range(depth):
                c = ci + b
                gather(c, b).wait()
                write(c, b).start()
                prev = (b - 1) % depth

                @pl.when(c >= 1)
                def _():
                    write(c - 1, prev).wait()

                @pl.when(c + depth - 1 < n_chunks)
                def _():
                    gather(c + depth - 1, prev).start()

        write(n_chunks - 1, (n_chunks - 1) % depth).wait()

    return gather_rows(y_rows, dest_flat.reshape(n_idx // rows, rows))


def _combine_kernel(x1_ref, ya_ref, tg_ref, g_ref, o_ref):
    acc = x1_ref[...]
    tg = jnp.transpose(tg_ref[...])
    for k in range(TOP_K):
        acc = acc + tg[:, k:k + 1] * _unpack_bf16_halves(ya_ref[k])
    o_ref[...] = _rmsnorm(acc, g_ref[...])


def _combine(x1, y_assign, top_g, g_final):
    t = x1.shape[0]
    tm = TOKEN_TILE
    deep = pl.Buffered(COMBINE_BUFFERS)

    def stream(*refs):
        pltpu.emit_pipeline(
            _combine_kernel,
            grid=(t // tm,),
            in_specs=[pl.BlockSpec((tm, D_MODEL), lambda i: (i, 0), pipeline_mode=deep),
                      pl.BlockSpec((TOP_K, tm, D_MODEL // 2), lambda i: (0, i, 0), pipeline_mode=deep),
                      pl.BlockSpec((2 * TOP_K, tm), lambda i: (0, i)),
                      pl.BlockSpec((1, D_MODEL), lambda i: (0, 0))],
            out_specs=[pl.BlockSpec((tm, D_MODEL), lambda i: (i, 0))],
        )(*refs)

    return pl.pallas_call(
        stream,
        out_shape=jax.ShapeDtypeStruct((t, D_MODEL), _F32),
        in_specs=[pl.BlockSpec(memory_space=pl.ANY)] * 4,
        out_specs=pl.BlockSpec(memory_space=pl.ANY),
        compiler_params=pltpu.CompilerParams(vmem_limit_bytes=VMEM_LIMIT_BYTES),
        name="combine",
    )(x1, y_assign, top_g, g_final)


def _block_plan(counts, n_blocks):
    padded = ((counts + MOE_BLOCK - 1) // MOE_BLOCK) * MOE_BLOCK
    pad_end = jnp.cumsum(padded)
    pad_start = pad_end - padded
    block_start = (jnp.arange(n_blocks, dtype=jnp.int32) * MOE_BLOCK)[:, None]
    eidx = jnp.arange(N_EXPERTS, dtype=jnp.int32)
    owns = (pad_start[None, :] <= block_start) & (block_start < pad_end[None, :])
    has_blocks = (padded > 0).astype(jnp.int32)
    ordinal = jnp.cumsum(has_blocks) - has_blocks
    later = (eidx[None, :] > eidx[:, None]) & (padded[None, :] > 0)
    next_expert = jnp.min(jnp.where(later, eidx[None, :], N_EXPERTS), axis=1)

    def per_block(per_expert):
        return jnp.sum(jnp.where(owns, per_expert, 0), axis=1).astype(jnp.int32)

    block_e = per_block(eidx[None, :])
    block_rows = per_block(jnp.clip((pad_start + counts)[None, :] - block_start, 0, MOE_BLOCK))
    block_slot = per_block((ordinal % 2)[None, :])
    block_next = per_block(next_expert[None, :])
    return pad_start, (block_e, block_rows, block_slot, block_next)


def kernel(x, norm_mix_g, w_in, conv_w, w_conv_out, ssm_lam_re, ssm_lam_im, ssm_log_dt, ssm_b_re, ssm_b_im, ssm_c_re, ssm_c_im, ssm_d, w_glu, w_out, norm_ffn_g, w_router, b_router, w_gate_up, b_gate_up, w_down, b_down, norm_f_g):
    bsz, seq, d = x.shape
    t = bsz * seq
    x2 = x.reshape(t, d)
    assert seq % TOKEN_TILE == 0 and seq % SSM_TIME_TILE == 0 and w_in.shape[0] == 1

    w_in_b = w_in[0].astype(_BF16)
    n_bcvu = 3 * D_CONV + D_SSM
    g_mix = norm_mix_g[0].reshape(1, d)

    bz, u = _in_proj(x2, g_mix, w_in_b[:, :n_bcvu], conv_w[0], seq)

    tables = _ssm_tables(ssm_lam_re[0], ssm_lam_im[0], ssm_log_dt[0], ssm_b_re[0], ssm_b_im[0],
                         ssm_c_re[0], ssm_c_im[0], ssm_d[0])
    yg = _ssm(u.reshape(bsz, seq, D_SSM), tables).reshape(t, D_SSM)

    x1, h_packed, top_i, top_g, rank, counts = _mix_route(
        x2, bz, yg, g_mix, w_in_b[:, n_bcvu:], w_conv_out[0].astype(_BF16),
        w_glu[0].astype(_BF16), w_out[0].astype(_BF16), norm_ffn_g[0].reshape(1, d),
        jnp.pad(w_router[0], ((0, 0), (0, LANES - N_EXPERTS))).astype(_BF16),
        b_router[0].reshape(N_EXPERTS, 1))

    n_rows = t * TOP_K + N_EXPERTS * MOE_BLOCK
    pad_start, block_plan = _block_plan(counts[:, 0], n_rows // MOE_BLOCK)
    expert_ids = jnp.arange(N_EXPERTS, dtype=jnp.int32)[:, None, None]
    row_start = jnp.sum(jnp.where(top_i[None] == expert_ids, pad_start[:, None, None], 0), axis=0)
    dest = (row_start + rank).reshape(TOP_K * t)

    x_rows = _dispatch(h_packed, dest, n_rows)
    y_rows = _expert_ffn(*block_plan, x_rows, w_gate_up[0],
                         b_gate_up[0].reshape(N_EXPERTS, 1, 2 * D_FF), w_down[0],
                         b_down[0].reshape(N_EXPERTS, 1, D_MODEL))
    y_assign = _collect(y_rows, dest).reshape(TOP_K, t, D_MODEL // 2)
    out = _combine(x1, y_assign, top_g, norm_f_g.reshape(1, d))
    return out.reshape(bsz, seq, d)
```

```python
import functools

import jax
import jax.numpy as jnp
from jax import lax
from jax.experimental import pallas as pl
from jax.experimental.pallas import tpu as pltpu
from jax.experimental.pallas import tpu_sc as plsc

D_MODEL = 1024
D_CONV = 512
CONV_WIDTH = 3
D_SSM = 512
SSM_GROUP = 16
N_SSM_GROUPS = 32
SSM_STATE = 64
N_EXPERTS = 32
TOP_K = 4
D_FF = 1024
SWIGLU_LIMIT = 7.0
SWIGLU_ALPHA = 1.702
RMS_EPS = 1e-6

LANES = 128
SUBLANES = 8
MXU_DIM = 256
CHUNK = 8
SLAB_GROUPS = LANES // SSM_GROUP
N_SLABS = N_SSM_GROUPS // SLAB_GROUPS
SLAB_STATE = SLAB_GROUPS * SSM_STATE
FLAT = CHUNK * LANES
SSM_TIME_TILE = 128
TOKEN_TILE = 1024
IN_TILE = 1024
IN_CHAIN = 512
MIX_TILE = 1024
ROW_CHAIN = 256
MOE_BLOCK = 1024
FFN_ROW_STEP = 256
SC_ROWS = 64
COLLECT_ROWS = 32
COLLECT_RING = 4
VMEM_LIMIT_BYTES = 56 * 1024 * 1024

_BF16 = jnp.bfloat16
_F32 = jnp.float32


def _rmsnorm(xf, g):
    return xf * lax.rsqrt(jnp.mean(xf * xf, axis=-1, keepdims=True) + RMS_EPS) * g


def _sigmoid(v):
    return 0.5 * jnp.tanh(0.5 * v) + 0.5


def _pack_bf16_halves(v):
    n = v.shape[1] // 2
    bits = pltpu.bitcast(v.astype(_BF16).astype(_F32), jnp.uint32)
    return (bits[:, :n] >> 16) | (bits[:, n:] & jnp.uint32(0xFFFF0000))


def _unpack_bf16_halves(w):
    return jnp.concatenate([pltpu.bitcast(w << 16, _F32),
                            pltpu.bitcast(w & jnp.uint32(0xFFFF0000), _F32)], axis=1)


def _in_proj_kernel(tiles_per_seq, x_ref, g_ref, w_ref, cw_ref, bz_ref, u_ref, hbuf):
    tm = x_ref.shape[0]
    halo = SUBLANES

    @pl.when(pl.program_id(0) % tiles_per_seq == 0)
    def _():
        hbuf[0:halo, :] = jnp.zeros((halo, D_CONV), _F32)

    cw = cw_ref[...]
    for r0 in range(0, tm, IN_CHAIN):
        rows = slice(r0, r0 + IN_CHAIN)
        xn = _rmsnorm(x_ref[rows, :], g_ref[...]).astype(_BF16)
        cv = jnp.dot(xn, w_ref[:, D_CONV:3 * D_CONV], preferred_element_type=_F32)
        hbuf[halo + r0:halo + r0 + IN_CHAIN, :] = cv[:, :D_CONV] * cv[:, D_CONV:]
        u_ref[rows, :] = jnp.dot(xn, w_ref[:, 3 * D_CONV:], preferred_element_type=_F32)
        z = cw[CONV_WIDTH - 1:CONV_WIDTH, :] * hbuf[halo + r0:halo + r0 + IN_CHAIN, :]
        for lag in range(1, CONV_WIDTH):
            z = z + (cw[CONV_WIDTH - 1 - lag:CONV_WIDTH - lag, :]
                     * hbuf[halo + r0 - lag:halo + r0 - lag + IN_CHAIN, :])
        b_gate = jnp.dot(xn, w_ref[:, :D_CONV], preferred_element_type=_F32)
        bz_ref[rows, :] = (b_gate * z).astype(_BF16)
    hbuf[0:halo, :] = hbuf[tm:tm + halo, :]


def _in_proj(x2, g, w_bcvu, conv_w, seq):
    t = x2.shape[0]
    tm = IN_TILE
    assert SUBLANES >= CONV_WIDTH - 1 and seq % tm == 0
    return pl.pallas_call(
        functools.partial(_in_proj_kernel, seq // tm),
        out_shape=(jax.ShapeDtypeStruct((t, D_CONV), _BF16),
                   jax.ShapeDtypeStruct((t, D_SSM), _F32)),
        grid=(t // tm,),
        in_specs=[pl.BlockSpec((tm, D_MODEL), lambda i: (i, 0)),
                  pl.BlockSpec((1, D_MODEL), lambda i: (0, 0)),
                  pl.BlockSpec((D_MODEL, 3 * D_CONV + D_SSM), lambda i: (0, 0)),
                  pl.BlockSpec((CONV_WIDTH, D_CONV), lambda i: (0, 0))],
        out_specs=(pl.BlockSpec((tm, D_CONV), lambda i: (i, 0)),
                   pl.BlockSpec((tm, D_SSM), lambda i: (i, 0))),
        scratch_shapes=[pltpu.VMEM((tm + SUBLANES, D_CONV), _F32)],
        compiler_params=pltpu.CompilerParams(
            dimension_semantics=("arbitrary",), vmem_limit_bytes=VMEM_LIMIT_BYTES),
        name="in_proj",
    )(x2, g, w_bcvu, conv_w)


def _ssm_prep_kernel(lr_ref, lc_ref, bm_ref, cm_ref, d_ref, toep_ref, bst_ref, cst_ref, a_ref):
    def discretise(lre, lim, log_dt):
        dt = jnp.exp(log_dt)
        mag = jnp.exp(lre * dt)
        return mag * jnp.cos(lim * dt), mag * jnp.sin(lim * dt)

    def powers(are, aim):
        pre, pim = [jnp.ones_like(are)], [jnp.zeros_like(are)]
        for _ in range(CHUNK):
            pre, pim = (pre + [pre[-1] * are - pim[-1] * aim],
                        pim + [pre[-1] * aim + pim[-1] * are])
        return pre, pim

    lr = lr_ref[0]
    lre, lim = lr[0:1, :], lr[1:2, :]
    are, aim = discretise(lre, lim, lr[2:3, :])
    pre, pim = powers(are, aim)
    den = lre * lre + lim * lim
    q_re = ((are - 1.0) * lre + aim * lim) / den
    q_im = (aim * lre - (are - 1.0) * lim) / den
    bb_re = q_re * bm_ref[0, 0] - q_im * bm_ref[0, 1]
    bb_im = q_re * bm_ref[0, 1] + q_im * bm_ref[0, 0]
    cm_re, cm_im = cm_ref[0, 0], cm_ref[0, 1]

    def split_bf16(v):
        v_hi = v.astype(_BF16)
        return v_hi, (v - v_hi.astype(_F32)).astype(_BF16)

    c_hi, c_lo = split_bf16(jnp.concatenate([cm_re, -cm_im], axis=0))
    kblk = []
    for k in range(CHUNK):
        ab_re = bb_re * pre[k] - bb_im * pim[k]
        ab_im = bb_re * pim[k] + bb_im * pre[k]
        rows = slice((CHUNK - 1 - k) * LANES, (CHUNK - k) * LANES)
        bst_ref[0, rows, :SLAB_STATE] = ab_re.astype(_BF16)
        bst_ref[0, rows, SLAB_STATE:] = ab_im.astype(_BF16)
        ab_hi, ab_lo = split_bf16(jnp.concatenate([ab_re, ab_im], axis=1))
        kblk.append(jnp.dot(ab_hi, c_hi, preferred_element_type=_F32)
                    + (jnp.dot(ab_lo, c_hi, preferred_element_type=_F32)
                       + jnp.dot(ab_hi, c_lo, preferred_element_type=_F32)))
    r = lax.broadcasted_iota(jnp.int32, (LANES, LANES), 0)
    c = lax.broadcasted_iota(jnp.int32, (LANES, LANES), 1)
    kblk[0] = kblk[0] + jnp.where(r == c, jnp.broadcast_to(d_ref[0], (LANES, LANES)), 0.0)
    kblk = [kb.astype(_BF16) for kb in kblk]
    zeros = jnp.zeros((LANES, LANES), _BF16)
    for sp in range(CHUNK):
        for s in range(CHUNK):
            toep_ref[0, sp * LANES:(sp + 1) * LANES, s * LANES:(s + 1) * LANES] = (
                kblk[s - sp] if s >= sp else zeros)

    lc = lc_ref[0]
    cre, cim = discretise(lc[:, 0:1], lc[:, 1:2], lc[:, 2:3])
    qre, qim = powers(cre, cim)
    for s in range(CHUNK):
        cols = slice(s * LANES, (s + 1) * LANES)
        cst_ref[0, :SLAB_STATE, cols] = (cm_re * qre[s + 1] - cm_im * qim[s + 1]).astype(_BF16)
        cst_ref[0, SLAB_STATE:, cols] = (-(cm_re * qim[s + 1] + cm_im * qre[s + 1])).astype(_BF16)
    a_ref[0, 0:1, :] = pre[CHUNK]
    a_ref[0, 1:2, :] = pim[CHUNK]


def _ssm_tables(lam_re, lam_im, log_dt, b_re, b_im, c_re, c_im, d_skip):
    sg = (N_SLABS, SLAB_GROUPS)
    eye = jnp.eye(SLAB_GROUPS, dtype=_F32)
    lam = jnp.stack([lam_re, lam_im, jnp.broadcast_to(log_dt[:, None], lam_re.shape)], axis=0)
    lam_row = lam.reshape(3, N_SLABS, SLAB_STATE).transpose(1, 0, 2)
    lam_col = lam_row.transpose(0, 2, 1)

    def b_blockdiag(b):
        bt = b.reshape(*sg, SSM_STATE, SSM_GROUP).transpose(0, 1, 3, 2)
        return (bt[:, :, :, None, :] * eye[None, :, None, :, None]).reshape(N_SLABS, LANES, SLAB_STATE)

    def c_blockdiag(c):
        ct = c.reshape(*sg, SSM_GROUP, SSM_STATE).transpose(0, 1, 3, 2)
        return (ct[:, :, :, None, :] * eye[None, :, None, :, None]).reshape(N_SLABS, SLAB_STATE, LANES)

    bm = jnp.stack([b_blockdiag(b_re), b_blockdiag(b_im)], axis=1)
    cm = jnp.stack([c_blockdiag(c_re), c_blockdiag(c_im)], axis=1)
    d = d_skip.reshape(N_SLABS, 1, LANES)
    slab3 = lambda sl: (sl, 0, 0)
    slab4 = lambda sl: (sl, 0, 0, 0)
    return pl.pallas_call(
        _ssm_prep_kernel,
        out_shape=(jax.ShapeDtypeStruct((N_SLABS, FLAT, FLAT), _BF16),
                   jax.ShapeDtypeStruct((N_SLABS, FLAT, 2 * SLAB_STATE), _BF16),
                   jax.ShapeDtypeStruct((N_SLABS, 2 * SLAB_STATE, FLAT), _BF16),
                   jax.ShapeDtypeStruct((N_SLABS, 2, SLAB_STATE), _F32)),
        grid=(N_SLABS,),
        in_specs=[pl.BlockSpec((1, 3, SLAB_STATE), slab3),
                  pl.BlockSpec((1, SLAB_STATE, 3), slab3),
                  pl.BlockSpec((1, 2, LANES, SLAB_STATE), slab4),
                  pl.BlockSpec((1, 2, SLAB_STATE, LANES), slab4),
                  pl.BlockSpec((1, 1, LANES), slab3)],
        out_specs=(pl.BlockSpec((1, FLAT, FLAT), slab3),
                   pl.BlockSpec((1, FLAT, 2 * SLAB_STATE), slab3),
                   pl.BlockSpec((1, 2 * SLAB_STATE, FLAT), slab3),
                   pl.BlockSpec((1, 2, SLAB_STATE), slab3)),
        compiler_params=pltpu.CompilerParams(
            dimension_semantics=("arbitrary",), vmem_limit_bytes=VMEM_LIMIT_BYTES),
        name="ssm_prep",
    )(lam_row, lam_col, bm, cm, d)


def _ssm_kernel(u_ref, toep_ref, bst_ref, cst_ref, a_ref, y_ref, uflat, s_scr, xc_scr, carry, ytoep):
    nb, tt, _ = u_ref.shape
    nch = tt // CHUNK
    n = nb * nch

    @pl.when(pl.program_id(1) == 0)
    def _():
        carry[...] = jnp.zeros_like(carry)

    for s in range(CHUNK):
        part = u_ref[:, pl.ds(s, nch, stride=CHUNK), :]
        uflat[:, s * LANES:(s + 1) * LANES] = part.reshape(n, LANES).astype(_BF16)

    n_cb = FLAT // MXU_DIM

    def toeplitz(cb):
        kk = (cb + 1) * MXU_DIM
        cols = slice(cb * MXU_DIM, kk)
        ytoep[:, cols] = jnp.dot(uflat[:, :kk], toep_ref[0, :kk, cols], preferred_element_type=_F32)

    for cb in range(n_cb // 2):
        toeplitz(cb)

    nblk = SLAB_STATE // LANES
    loc_all = jnp.dot(uflat[...], bst_ref[0], preferred_element_type=_F32)
    for cb in range(n_cb // 2, n_cb):
        toeplitz(cb)
    for kb in range(2 * nblk):
        s_scr[kb] = loc_all[:, kb * LANES:(kb + 1) * LANES]

    a = a_ref[0]
    are = [jnp.broadcast_to(a[0:1, kb * LANES:(kb + 1) * LANES], (nb, LANES)) for kb in range(nblk)]
    aim = [jnp.broadcast_to(a[1:2, kb * LANES:(kb + 1) * LANES], (nb, LANES)) for kb in range(nblk)]
    xr = [carry[kb] for kb in range(nblk)]
    xi = [carry[nblk + kb] for kb in range(nblk)]
    for j in range(nch):
        rows = pl.ds(j, nb, stride=nch)
        for kb in range(nblk):
            xc_scr[kb, rows, :] = xr[kb]
            xc_scr[nblk + kb, rows, :] = xi[kb]
            nr = are[kb] * xr[kb] - aim[kb] * xi[kb] + s_scr[kb, rows, :]
            ni = are[kb] * xi[kb] + aim[kb] * xr[kb] + s_scr[nblk + kb, rows, :]
            xr[kb], xi[kb] = nr, ni
    for kb in range(nblk):
        carry[kb] = xr[kb]
        carry[nblk + kb] = xi[kb]

    xc = jnp.concatenate([xc_scr[kb] for kb in range(2 * nblk)], axis=1).astype(_BF16)
    for cb in range(n_cb):
        cols = slice(cb * MXU_DIM, (cb + 1) * MXU_DIM)
        y = ytoep[:, cols] + jnp.dot(xc, cst_ref[0, :, cols], preferred_element_type=_F32)
        y = jax.nn.gelu(y)
        for h in range(MXU_DIM // LANES):
            s = cb * (MXU_DIM // LANES) + h
            y_ref[:, pl.ds(s, nch, stride=CHUNK), :] = (
                y[:, h * LANES:(h + 1) * LANES].reshape(nb, nch, LANES))


def _ssm(u3, tables):
    toep, bst, cst, a_chunk = tables
    nb, seq, _ = u3.shape
    tt = SSM_TIME_TILE
    n = nb * (tt // CHUNK)
    return pl.pallas_call(
        _ssm_kernel,
        out_shape=jax.ShapeDtypeStruct(u3.shape, _F32),
        grid=(N_SLABS, seq // tt),
        in_specs=[pl.BlockSpec((nb, tt, LANES), lambda sl, ti: (0, ti, sl)),
                  pl.BlockSpec((1, FLAT, FLAT), lambda sl, ti: (sl, 0, 0)),
                  pl.BlockSpec((1, FLAT, 2 * SLAB_STATE), lambda sl, ti: (sl, 0, 0)),
                  pl.BlockSpec((1, 2 * SLAB_STATE, FLAT), lambda sl, ti: (sl, 0, 0)),
                  pl.BlockSpec((1, 2, SLAB_STATE), lambda sl, ti: (sl, 0, 0))],
        out_specs=pl.BlockSpec((nb, tt, LANES), lambda sl, ti: (0, ti, sl)),
        scratch_shapes=[pltpu.VMEM((n, FLAT), _BF16),
                        pltpu.VMEM((2 * SLAB_STATE // LANES, n, LANES), _F32),
                        pltpu.VMEM((2 * SLAB_STATE // LANES, n, LANES), _F32),
                        pltpu.VMEM((2 * SLAB_STATE // LANES, nb, LANES), _F32),
                        pltpu.VMEM((n, FLAT), _F32)],
        compiler_params=pltpu.CompilerParams(
            dimension_semantics=("arbitrary", "arbitrary"), vmem_limit_bytes=VMEM_LIMIT_BYTES),
        name="ssm",
    )(u3, toep, bst, cst, a_chunk)


def _mix_route_kernel(x_ref, bz_ref, yg_ref, gm_ref, wg_ref, wco_ref, wglu_ref, wout_ref,
                      gf_ref, wr_ref, br_ref,
                      x1_ref, h_ref, ti_ref, tg_ref, rk_ref, cnt_ref, base, merged):
    tm = x_ref.shape[0]

    @pl.when(pl.program_id(0) == 0)
    def _():
        base[...] = jnp.zeros_like(base)

    chains = [slice(r0, r0 + ROW_CHAIN) for r0 in range(0, tm, ROW_CHAIN)]
    hs, picks = [], []

    def route(j):
        logits_tok = jnp.dot(hs[j], wr_ref[...], preferred_element_type=_F32)
        picks.append(_top_k_rows(chains[j], logits_tok, br_ref, ti_ref, tg_ref))

    def rank(j):
        _rank_rows(chains[j], *picks[j], rk_ref, base)

    for j, rows in enumerate(chains):
        hs.append(_mix_rows(rows, x_ref, bz_ref, yg_ref, gm_ref, wg_ref, wco_ref, wglu_ref,
                            wout_ref, gf_ref, x1_ref, h_ref, merged))
        if j >= 1:
            route(j - 1)
        if j >= 2:
            rank(j - 2)
    last = len(chains) - 1
    route(last)
    for j in range(max(last - 1, 0), last + 1):
        rank(j)
    cnt_ref[...] = base[...].astype(jnp.int32)


def _mix_rows(rows, x_ref, bz_ref, yg_ref, gm_ref, wg_ref, wco_ref, wglu_ref, wout_ref,
              gf_ref, x1_ref, h_ref, merged):
    x = x_ref[rows, :]
    xn = _rmsnorm(x, gm_ref[...]).astype(_BF16)
    bz = bz_ref[rows, :]
    yg = yg_ref[rows, :].astype(_BF16)
    for c in range(D_MODEL // MXU_DIM):
        lo = slice(c * MXU_DIM, (c + 1) * MXU_DIM)
        hi = slice(D_MODEL + c * MXU_DIM, D_MODEL + (c + 1) * MXU_DIM)
        gate_a = jnp.dot(xn, wg_ref[:, lo], preferred_element_type=_F32)
        gate_b = jnp.dot(xn, wg_ref[:, hi], preferred_element_type=_F32)
        y_a = jnp.dot(bz, wco_ref[:, lo], preferred_element_type=_F32)
        val = jnp.dot(yg, wglu_ref[:, lo], preferred_element_type=_F32)
        glu_gate = jnp.dot(yg, wglu_ref[:, hi], preferred_element_type=_F32)
        y_b = val * _sigmoid(glu_gate)
        merged[rows, lo] = (_sigmoid(gate_a) * y_a + _sigmoid(gate_b) * y_b).astype(_BF16)
    x1 = x + jnp.dot(merged[rows, :], wout_ref[...], preferred_element_type=_F32)
    x1_ref[rows, :] = x1
    h = _rmsnorm(x1, gf_ref[...])
    h_ref[rows, :] = _pack_bf16_halves(h)
    return h.astype(_BF16)


def _top_k_rows(rows, logits_tok, br_ref, ti_ref, tg_ref):
    tm = rows.stop - rows.start
    logits = jnp.transpose(logits_tok)[:N_EXPERTS, :] + br_ref[...]
    erow = lax.broadcasted_iota(jnp.int32, (N_EXPERTS, tm), 0).astype(_F32)
    neg_inf = jnp.float32(-jnp.inf)
    work = logits
    vals, idxs = [], []
    for _ in range(TOP_K):
        m = jnp.max(work, axis=0, keepdims=True)
        idx = jnp.min(jnp.where(work == m, erow, float(N_EXPERTS)), axis=0, keepdims=True)
        vals.append(m)
        idxs.append(idx)
        work = jnp.where(erow == idx, neg_inf, work)
    exps = [jnp.exp(v - vals[0]) for v in vals]
    denom = exps[0] + exps[1] + exps[2] + exps[3]
    sel = jnp.zeros((N_EXPERTS, tm), _F32)
    for k in range(TOP_K):
        ti_ref[k:k + 1, rows] = idxs[k].astype(jnp.int32)
        tg_ref[k:k + 1, rows] = exps[k] / denom
        tg_ref[TOP_K + k:TOP_K + k + 1, rows] = jnp.zeros((1, tm), _F32)
        sel = sel + (erow == idxs[k]).astype(_F32)
    return sel, idxs


def _rank_rows(rows, sel, idxs, rk_ref, base):
    tm = rows.stop - rows.start
    erow = lax.broadcasted_iota(jnp.int32, (N_EXPERTS, tm), 0).astype(_F32)
    row = lax.broadcasted_iota(jnp.int32, (tm, tm), 0)
    col = lax.broadcasted_iota(jnp.int32, (tm, tm), 1)
    earlier = (row < col).astype(_BF16)
    before = jnp.dot(sel.astype(_BF16), earlier, preferred_element_type=_F32) + base[...]
    for k in range(TOP_K):
        rk = jnp.sum(jnp.where(erow == idxs[k], before, 0.0), axis=0, keepdims=True)
        rk_ref[k:k + 1, rows] = rk.astype(jnp.int32)
    base[...] = base[...] + jnp.sum(sel, axis=1, keepdims=True)


def _mix_route(x2, bz, yg, g_mix, w_gates, w_conv_out, w_glu, w_out, g_ffn, w_router, b_router):
    t = x2.shape[0]
    tm = MIX_TILE
    tok = lambda i: (i, 0)
    tok_lanes = lambda i: (0, i)
    fixed = lambda i: (0, 0)
    weight = lambda shape: pl.BlockSpec(shape, fixed, pipeline_mode=pl.Buffered(1))
    return pl.pallas_call(
        _mix_route_kernel,
        out_shape=(jax.ShapeDtypeStruct((t, D_MODEL), _F32),
                   jax.ShapeDtypeStruct((t, D_MODEL // 2), jnp.uint32),
                   jax.ShapeDtypeStruct((TOP_K, t), jnp.int32),
                   jax.ShapeDtypeStruct((2 * TOP_K, t), _F32),
                   jax.ShapeDtypeStruct((TOP_K, t), jnp.int32),
                   jax.ShapeDtypeStruct((N_EXPERTS, 1), jnp.int32)),
        grid=(t // tm,),
        in_specs=[pl.BlockSpec((tm, D_MODEL), tok),
                  pl.BlockSpec((tm, D_CONV), tok),
                  pl.BlockSpec((tm, D_SSM), tok),
                  pl.BlockSpec((1, D_MODEL), fixed),
                  weight((D_MODEL, 2 * D_MODEL)),
                  weight((D_CONV, D_MODEL)),
                  weight((D_SSM, 2 * D_MODEL)),
                  weight((D_MODEL, D_MODEL)),
                  pl.BlockSpec((1, D_MODEL), fixed),
                  weight((D_MODEL, LANES)),
                  pl.BlockSpec((N_EXPERTS, 1), fixed)],
        out_specs=(pl.BlockSpec((tm, D_MODEL), tok),
                   pl.BlockSpec((tm, D_MODEL // 2), tok),
                   pl.BlockSpec((TOP_K, tm), tok_lanes),
                   pl.BlockSpec((2 * TOP_K, tm), tok_lanes),
                   pl.BlockSpec((TOP_K, tm), tok_lanes),
                   pl.BlockSpec((N_EXPERTS, 1), fixed)),
        scratch_shapes=[pltpu.VMEM((N_EXPERTS, 1), _F32),
                        pltpu.VMEM((tm, D_MODEL), _BF16)],
        compiler_params=pltpu.CompilerParams(
            dimension_semantics=("arbitrary",), vmem_limit_bytes=VMEM_LIMIT_BYTES),
        name="mix_route",
    )(x2, bz, yg, g_mix, w_gates, w_conv_out, w_glu, w_out, g_ffn, w_router, b_router)


def _expert_ffn_kernel(be_ref, nr_ref, slot_ref, next_ref, x_ref, wgu_hbm, bgu_ref, wd_hbm, bd_ref,
                       y_ref, stage_gu, stage_d, wgu_b, wd_b, sem_gu, sem_d):
    b = pl.program_id(0)
    expert = be_ref[b]
    live = nr_ref[b] > 0

    def weight_copies(e, slot):
        return (pltpu.make_async_copy(wgu_hbm.at[e], stage_gu.at[slot], sem_gu.at[slot]),
                pltpu.make_async_copy(wd_hbm.at[e], stage_d.at[slot], sem_d.at[slot]))

    first = live & ((b == 0) | (be_ref[jnp.maximum(b - 1, 0)] != expert))
    slot = slot_ref[b]

    @pl.when(first)
    def _():
        @pl.when(b == 0)
        def _():
            for cp in weight_copies(expert, slot):
                cp.start()

        for cp in weight_copies(expert, slot):
            cp.wait()

        @pl.when(next_ref[b] < N_EXPERTS)
        def _():
            for cp in weight_copies(next_ref[b], 1 - slot):
                cp.start(priority=1)

    def ffn_rows(n_rows, round_weights):
        xw = x_ref[:n_rows, :]
        valid = lax.broadcasted_iota(jnp.int32, xw.shape, 0) < nr_ref[b]
        x = _unpack_bf16_halves(jnp.where(valid, xw, jnp.uint32(0))).astype(_BF16)
        if round_weights:
            halves = []
            for cols in (slice(0, D_FF), slice(D_FF, 2 * D_FF)):
                wgu_b[:, cols] = stage_gu[slot, :, cols].astype(_BF16)
                halves.append(jnp.dot(x, wgu_b[:, cols], preferred_element_type=_F32)
                              + bgu_ref[0][:, cols])
            wd_b[...] = stage_d[slot].astype(_BF16)
            g, up = halves
        else:
            hgu = jnp.dot(x, wgu_b[...], preferred_element_type=_F32) + bgu_ref[0]
            g, up = hgu[:, :D_FF], hgu[:, D_FF:]
        g = jnp.minimum(g, SWIGLU_LIMIT)
        up = jnp.clip(up, -SWIGLU_LIMIT, SWIGLU_LIMIT)
        act = (up + 1.0) * (g * _sigmoid(SWIGLU_ALPHA * g))
        y = jnp.dot(act.astype(_BF16), wd_b[...], preferred_element_type=_F32) + bd_ref[0]
        y_ref[:n_rows, :] = _pack_bf16_halves(y)
        if n_rows < MOE_BLOCK:
            y_ref[n_rows:, :] = jnp.zeros((MOE_BLOCK - n_rows, y_ref.shape[1]), y_ref.dtype)

    for height in range(FFN_ROW_STEP, MOE_BLOCK + 1, FFN_ROW_STEP):
        in_height = (nr_ref[b] > height - FFN_ROW_STEP) & (nr_ref[b] <= height)
        for round_weights in (True, False):
            @pl.when(in_height & (first if round_weights else jnp.logical_not(first)))
            def _():
                ffn_rows(height, round_weights)

    @pl.when(jnp.logical_not(live))
    def _():
        y_ref[...] = jnp.zeros_like(y_ref)


def _expert_ffn(block_e, block_rows, block_slot, block_next, x_rows, w_gate_up, b_gate_up, w_down,
                b_down):
    n_rows = x_rows.shape[0]
    n_blocks = n_rows // MOE_BLOCK

    def bias_map(b, be, nr, sl, nx):
        return (be[b], 0, 0)

    def row_map(b, be, nr, sl, nx):
        return (b, 0)

    grid_spec = pltpu.PrefetchScalarGridSpec(
        num_scalar_prefetch=4,
        grid=(n_blocks,),
        in_specs=[pl.BlockSpec((MOE_BLOCK, D_MODEL // 2), row_map),
                  pl.BlockSpec(memory_space=pl.ANY),
                  pl.BlockSpec((1, 1, 2 * D_FF), bias_map),
                  pl.BlockSpec(memory_space=pl.ANY),
                  pl.BlockSpec((1, 1, D_MODEL), bias_map)],
        out_specs=pl.BlockSpec((MOE_BLOCK, D_MODEL // 2), row_map),
        scratch_shapes=[pltpu.VMEM((2, D_MODEL, 2 * D_FF), _F32),
                        pltpu.VMEM((2, D_FF, D_MODEL), _F32),
                        pltpu.VMEM((D_MODEL, 2 * D_FF), _BF16),
                        pltpu.VMEM((D_FF, D_MODEL), _BF16),
                        pltpu.SemaphoreType.DMA((2,)),
                        pltpu.SemaphoreType.DMA((2,))],
    )
    return pl.pallas_call(
        _expert_ffn_kernel,
        out_shape=jax.ShapeDtypeStruct((n_rows, D_MODEL // 2), jnp.uint32),
        grid_spec=grid_spec,
        compiler_params=pltpu.CompilerParams(
            dimension_semantics=("arbitrary",), vmem_limit_bytes=VMEM_LIMIT_BYTES),
        name="expert_ffn",
    )(block_e, block_rows, block_slot, block_next, x_rows, w_gate_up, b_gate_up, w_down, b_down)


def _sc_workers():
    info = plsc.get_sparse_core_info()
    return info.num_cores, info.num_cores * info.num_subcores


def _dispatch(h_packed, dest_flat, n_rows):
    t, width = h_packed.shape
    n_cores, n_workers = _sc_workers()
    n_chunks = t // (n_workers * SC_ROWS)
    chunks_per_k = t // SC_ROWS
    assert n_chunks % 2 == 0

    @functools.partial(
        pl.kernel, mesh=plsc.VectorSubcoreMesh(core_axis_name="c", subcore_axis_name="s"),
        out_type=jax.ShapeDtypeStruct((n_rows, width), h_packed.dtype),
        scratch_types=[pltpu.VMEM((TOP_K, n_chunks, SC_ROWS), jnp.int32),
                       pltpu.VMEM((2, SC_ROWS, width), h_packed.dtype),
                       pltpu.SemaphoreType.DMA((2,)),
                       pltpu.SemaphoreType.DMA((2,))])
    def scatter_rows(h_hbm, dest_hbm, out_hbm, idx_v, buf, lsem, ssem):
        wid = lax.axis_index("s") * n_cores + lax.axis_index("c")
        c0 = wid * n_chunks
        for k in range(TOP_K):
            pltpu.sync_copy(dest_hbm.at[pl.ds(k * chunks_per_k + c0, n_chunks)], idx_v.at[k])

        def load(c, b):
            return pltpu.make_async_copy(h_hbm.at[pl.ds((c0 + c) * SC_ROWS, SC_ROWS)], buf.at[b],
                                         lsem.at[b])

        def scatters(c, b):
            return [pltpu.make_async_copy(buf.at[b], out_hbm.at[idx_v.at[k, c]], ssem.at[b])
                    for k in range(TOP_K)]

        load(0, 0).start()

        @pl.loop(0, n_chunks, step=2)
        def _(ci):
            for b in range(2):
                c = ci + b

                @pl.when(c >= 1)
                def _():
                    for cp in scatters(c - 1, 1 - b):
                        cp.wait()

                @pl.when(c + 1 < n_chunks)
                def _():
                    load(c + 1, 1 - b).start()

                load(c, b).wait()
                for cp in scatters(c, b):
                    cp.start()

        for cp in scatters(n_chunks - 1, 1):
            cp.wait()

    return scatter_rows(h_packed, dest_flat.reshape(TOP_K * chunks_per_k, SC_ROWS))


def _collect(y_rows, dest_flat):
    n_idx = dest_flat.shape[0]
    width = y_rows.shape[1]
    rows, depth = COLLECT_ROWS, COLLECT_RING
    n_cores, n_workers = _sc_workers()
    n_chunks = n_idx // (n_workers * rows)
    assert n_chunks % depth == 0

    @functools.partial(
        pl.kernel, mesh=plsc.VectorSubcoreMesh(core_axis_name="c", subcore_axis_name="s"),
        out_type=jax.ShapeDtypeStruct((n_idx, width), y_rows.dtype),
        scratch_types=[pltpu.VMEM((n_chunks, rows), jnp.int32),
                       pltpu.VMEM((depth, rows, width), y_rows.dtype),
                       pltpu.SemaphoreType.DMA((depth,)),
                       pltpu.SemaphoreType.DMA((depth,))])
    def gather_rows(y_hbm, dest_hbm, out_hbm, idx_v, buf, gsem, wsem):
        wid = lax.axis_index("s") * n_cores + lax.axis_index("c")
        c0 = wid * n_chunks
        pltpu.sync_copy(dest_hbm.at[pl.ds(c0, n_chunks)], idx_v)

        def gather(c, b):
            return pltpu.make_async_copy(y_hbm.at[idx_v.at[c]], buf.at[b], gsem.at[b])

        def write(c, b):
            return pltpu.make_async_copy(buf.at[b], out_hbm.at[pl.ds((c0 + c) * rows, rows)],
                                         wsem.at[b])

        for c in range(depth - 1):
            gather(c, c).start()

        @pl.loop(0, n_chunks, step=depth)
        def _(ci):
            for b in range(depth):
                c = ci + b
                gather(c, b).wait()
                write(c, b).start()
                prev = (b - 1) % depth

                @pl.when(c >= 1)
                def _():
                    write(c - 1, prev).wait()

                @pl.when(c + depth - 1 < n_chunks)
                def _():
                    gather(c + depth - 1, prev).start()

        write(n_chunks - 1, (n_chunks - 1) % depth).wait()

    return gather_rows(y_rows, dest_flat.reshape(n_idx // rows, rows))


def _combine_kernel(x1_ref, ya_ref, tg_ref, g_ref, o_ref):
    acc = x1_ref[...]
    tg = jnp.transpose(tg_ref[...])
    for k in range(TOP_K):
        acc = acc + tg[:, k:k + 1] * _unpack_bf16_halves(ya_ref[k])
    o_ref[...] = _rmsnorm(acc, g_ref[...])


def _combine(x1, y_assign, top_g, g_final):
    t = x1.shape[0]
    tm = TOKEN_TILE
    return pl.pallas_call(
        _combine_kernel,
        out_shape=jax.ShapeDtypeStruct((t, D_MODEL), _F32),
        grid=(t // tm,),
        in_specs=[pl.BlockSpec((tm, D_MODEL), lambda i: (i, 0)),
                  pl.BlockSpec((TOP_K, tm, D_MODEL // 2), lambda i: (0, i, 0)),
                  pl.BlockSpec((2 * TOP_K, tm), lambda i: (0, i)),
                  pl.BlockSpec((1, D_MODEL), lambda i: (0, 0))],
        out_specs=pl.BlockSpec((tm, D_MODEL), lambda i: (i, 0)),
        compiler_params=pltpu.CompilerParams(
            dimension_semantics=("arbitrary",), vmem_limit_bytes=VMEM_LIMIT_BYTES),
        name="combine",
    )(x1, y_assign, top_g, g_final)


def _block_plan(counts, n_blocks):
    padded = ((counts + MOE_BLOCK - 1) // MOE_BLOCK) * MOE_BLOCK
    pad_end = jnp.cumsum(padded)
    pad_start = pad_end - padded
    block_start = (jnp.arange(n_blocks, dtype=jnp.int32) * MOE_BLOCK)[:, None]
    eidx = jnp.arange(N_EXPERTS, dtype=jnp.int32)
    owns = (pad_start[None, :] <= block_start) & (block_start < pad_end[None, :])
    has_blocks = (padded > 0).astype(jnp.int32)
    ordinal = jnp.cumsum(has_blocks) - has_blocks
    later = (eidx[None, :] > eidx[:, None]) & (padded[None, :] > 0)
    next_expert = jnp.min(jnp.where(later, eidx[None, :], N_EXPERTS), axis=1)

    def per_block(per_expert):
        return jnp.sum(jnp.where(owns, per_expert, 0), axis=1).astype(jnp.int32)

    block_e = per_block(eidx[None, :])
    block_rows = per_block(jnp.clip((pad_start + counts)[None, :] - block_start, 0, MOE_BLOCK))
    block_slot = per_block((ordinal % 2)[None, :])
    block_next = per_block(next_expert[None, :])
    return pad_start, (block_e, block_rows, block_slot, block_next)


def kernel(x, norm_mix_g, w_in, conv_w, w_conv_out, ssm_lam_re, ssm_lam_im, ssm_log_dt, ssm_b_re, ssm_b_im, ssm_c_re, ssm_c_im, ssm_d, w_glu, w_out, norm_ffn_g, w_router, b_router, w_gate_up, b_gate_up, w_down, b_down, norm_f_g):
    bsz, seq, d = x.shape
    t = bsz * seq
    x2 = x.reshape(t, d)
    assert seq % TOKEN_TILE == 0 and seq % SSM_TIME_TILE == 0 and w_in.shape[0] == 1

    w_in_b = w_in[0].astype(_BF16)
    n_bcvu = 3 * D_CONV + D_SSM
    g_mix = norm_mix_g[0].reshape(1, d)

    bz, u = _in_proj(x2, g_mix, w_in_b[:, :n_bcvu], conv_w[0], seq)

    tables = _ssm_tables(ssm_lam_re[0], ssm_lam_im[0], ssm_log_dt[0], ssm_b_re[0], ssm_b_im[0],
                         ssm_c_re[0], ssm_c_im[0], ssm_d[0])
    yg = _ssm(u.reshape(bsz, seq, D_SSM), tables).reshape(t, D_SSM)

    x1, h_packed, top_i, top_g, rank, counts = _mix_route(
        x2, bz, yg, g_mix, w_in_b[:, n_bcvu:], w_conv_out[0].astype(_BF16),
        w_glu[0].astype(_BF16), w_out[0].astype(_BF16), norm_ffn_g[0].reshape(1, d),
        jnp.pad(w_router[0], ((0, 0), (0, LANES - N_EXPERTS))).astype(_BF16),
        b_router[0].reshape(N_EXPERTS, 1))

    n_rows = t * TOP_K + N_EXPERTS * MOE_BLOCK
    pad_start, block_plan = _block_plan(counts[:, 0], n_rows // MOE_BLOCK)
    expert_ids = jnp.arange(N_EXPERTS, dtype=jnp.int32)[:, None, None]
    row_start = jnp.sum(jnp.where(top_i[None] == expert_ids, pad_start[:, None, None], 0), axis=0)
    dest = (row_start + rank).reshape(TOP_K * t)

    x_rows = _dispatch(h_packed, dest, n_rows)
    y_rows = _expert_ffn(*block_plan, x_rows, w_gate_up[0],
                         b_gate_up[0].reshape(N_EXPERTS, 1, 2 * D_FF), w_down[0],
                         b_down[0].reshape(N_EXPERTS, 1, D_MODEL))
    y_assign = _collect(y_rows, dest).reshape(TOP_K, t, D_MODEL // 2)
    out = _combine(x1, y_assign, top_g, norm_f_g.reshape(1, d))
    return out.reshape(bsz, seq, d)
```
